```python
import jax, jax.numpy as jnp
from jax import lax
import numpy as np

D_MODEL = 2048
BATCH = 4
SEQ = 4096
DEPTH = 1

PLE_DIM = 256
M_HEADS = 4
M_QK_DIM = 128
M_V_DIM = 256
M_QK = M_HEADS * M_QK_DIM
M_V = M_HEADS * M_V_DIM
CONV_W = 4
CHUNK = 128
POOL_WINDOWS = (2, 4, 8, 16)
POOL_GROUPS = 4
POOL_GROUP_DIM = 256
POOL_W = POOL_GROUPS * POOL_GROUP_DIM
IN_SPLITS = (2 * M_QK, M_V, M_V, M_HEADS, M_HEADS, POOL_W, D_MODEL, D_MODEL)
IN_DIM = sum(IN_SPLITS)
IN_OFFSETS = tuple(int(s) for s in np.cumsum(IN_SPLITS)[:-1])
F_GATE_OFF = 2 * M_QK + 2 * M_V + M_HEADS
N_GROUPS = 4
EXPERTS_PER_GROUP = 8
N_EXPERTS = N_GROUPS * EXPERTS_PER_GROUP
TOP_K = 2
D_EXPERT = 768
MOE_BLOCK = 128
ALPHA = (2 * DEPTH) ** 0.25
BETA = (8 * DEPTH) ** -0.25
LN_EPS = 1e-5

kernel_name = 'hybrid_mlstm_pool_hmoe_deepnorm'


def layer_norm(x, g, b):
    xf = x.astype(jnp.float32)
    mu = xf.mean(-1, keepdims=True)
    var = jnp.square(xf - mu).mean(-1, keepdims=True)
    return ((xf - mu) * lax.rsqrt(var + LN_EPS) * g + b).astype(x.dtype)


def causal_dwconv(u, w, b):
    S = u.shape[1]
    up = jnp.pad(u, ((0, 0), (CONV_W - 1, 0), (0, 0)))
    out = b + up[:, 0:S] * w[0]
    for tap in range(1, CONV_W):
        out = out + up[:, tap:tap + S] * w[tap]
    return out


def mlstm_chunkwise(q, k, v, ig, fg):
    B, S, H, _ = q.shape
    nc = S // CHUNK
    f32 = jnp.float32

    def to_chunks(a):
        a = a.astype(f32).reshape((B, nc, CHUNK, H) + a.shape[3:])
        return jnp.moveaxis(a, (1, 3), (0, 2))

    qc = to_chunks(q)
    kc = to_chunks(k) * (M_QK_DIM ** -0.5)
    vc = to_chunks(v)
    lfc = jax.nn.log_sigmoid(to_chunks(fg))
    igc = to_chunks(ig)
    causal = jnp.tril(jnp.ones((CHUNK, CHUNK), bool))

    def step(carry, inp):
        C, n, m = carry
        qb, kb, vb, lf, ib = inp
        b = jnp.cumsum(lf, axis=-1)
        dmat = jnp.where(causal, b[..., :, None] - b[..., None, :] + ib[..., None, :], -jnp.inf)
        inter = b + m[..., None]
        m_row = jnp.maximum(inter, dmat.max(-1))
        s = jnp.einsum('bhjd,bhld->bhjl', qb, kb) * jnp.exp(dmat - m_row[..., None])
        scale_prev = jnp.exp(inter - m_row)
        num = jnp.einsum('bhjl,bhlv->bhjv', s, vb) + scale_prev[..., None] * jnp.einsum('bhvd,bhjd->bhjv', C, qb)
        den = s.sum(-1) + scale_prev * jnp.einsum('bhd,bhjd->bhj', n, qb)
        h = num / jnp.maximum(jnp.abs(den), jnp.exp(-m_row))[..., None]
        b_last = b[..., -1]
        g = b_last[..., None] - b + ib
        m_new = jnp.maximum(b_last + m, g.max(-1))
        w = jnp.exp(g - m_new[..., None])
        decay = jnp.exp(b_last + m - m_new)
        C = decay[..., None, None] * C + jnp.einsum('bhl,bhlv,bhld->bhvd', w, vb, kb)
        n = decay[..., None] * n + jnp.einsum('bhl,bhld->bhd', w, kb)
        return (C, n, m_new), h

    init = (jnp.zeros((B, H, M_V_DIM, M_QK_DIM), f32), jnp.zeros((B, H, M_QK_DIM), f32), jnp.zeros((B, H), f32))
    _, hs = lax.scan(step, init, (qc, kc, vc, lfc, igc))
    return jnp.moveaxis(hs, (0, 2), (1, 3)).reshape(B, S, H, M_V_DIM)


def multiscale_pool(u):
    S = u.shape[1]
    uf = u.astype(jnp.float32)
    cs = jnp.pad(jnp.cumsum(uf, axis=1), ((0, 0), (1, 0), (0, 0)))
    t = jnp.arange(S)
    outs = []
    for g, win in enumerate(POOL_WINDOWS):
        sl = slice(g * POOL_GROUP_DIM, (g + 1) * POOL_GROUP_DIM)
        c = cs[:, :, sl]
        lo = jnp.maximum(t + 1 - win, 0)
        cnt = jnp.minimum(t + 1, win).astype(jnp.float32)
        outs.append((c[:, 1:] - c[:, lo]) / cnt[None, :, None] - uf[:, :, sl])
    return jnp.concatenate(outs, axis=-1).astype(u.dtype)


def token_mixer(x, w_in, b_in, conv_w, conv_b, mh_g, w_pool, pool_scale, w_m_br, w_p_br, w_out):
    B, S, _ = x.shape
    z = x @ w_in + b_in
    qk_pre, v, o, ig, fg, u, gm, gp = jnp.split(z, IN_OFFSETS, axis=-1)
    qk = jax.nn.silu(causal_dwconv(qk_pre, conv_w, conv_b))
    q, k = jnp.split(qk, [M_QK], axis=-1)
    h = mlstm_chunkwise(q.reshape(B, S, M_HEADS, M_QK_DIM), k.reshape(B, S, M_HEADS, M_QK_DIM),
                        v.reshape(B, S, M_HEADS, M_V_DIM), ig, fg)
    mu = h.mean(-1, keepdims=True)
    var = jnp.square(h - mu).mean(-1, keepdims=True)
    h = (h - mu) * lax.rsqrt(var + LN_EPS) * mh_g.reshape(M_HEADS, M_V_DIM)
    h = h.reshape(B, S, M_V).astype(x.dtype) * jax.nn.sigmoid(o)
    a = h @ w_m_br
    y = multiscale_pool(u).reshape(B, S, POOL_GROUPS, POOL_GROUP_DIM)
    y = jnp.einsum('bsgc,gcd->bsgd', y, w_pool).reshape(B, S, POOL_W) * pool_scale
    pb = y @ w_p_br
    merged = jax.nn.sigmoid(gm) * a + jax.nn.sigmoid(gp) * pb
    return merged @ w_out


def hier_moe(x, w_rg, b_rg, w_re, b_re, w_gate, w_up, w_down):
    B, S, D = x.shape
    T = B * S
    xf = x.reshape(T, D)
    pg = jax.nn.softmax((xf @ w_rg + b_rg).astype(jnp.float32), axis=-1)
    pg_top, g_idx = lax.top_k(pg, 1)
    le = (xf @ w_re + b_re).astype(jnp.float32).reshape(T, N_GROUPS, EXPERTS_PER_GROUP)
    le_sel = jnp.take_along_axis(le, g_idx[:, :, None], axis=1)[:, 0]
    pe = jax.nn.softmax(le_sel, axis=-1)
    pe_top, e_local = lax.top_k(pe, TOP_K)
    gate = pg_top * pe_top / pe_top.sum(-1, keepdims=True)
    e_idx = g_idx * EXPERTS_PER_GROUP + e_local
    A = T * TOP_K
    flat_e = e_idx.reshape(A)
    flat_tok = jnp.repeat(jnp.arange(T, dtype=jnp.int32), TOP_K)
    flat_w = gate.reshape(A)
    order = jnp.argsort(flat_e)
    se, stok, sw = flat_e[order], flat_tok[order], flat_w[order]
    counts = jnp.bincount(flat_e, length=N_EXPERTS)
    starts = jnp.cumsum(counts) - counts
    pcounts = (counts + MOE_BLOCK - 1) // MOE_BLOCK * MOE_BLOCK
    pends = jnp.cumsum(pcounts)
    pstarts = pends - pcounts
    dest = pstarts[se] + jnp.arange(A) - starts[se]
    P = A + N_EXPERTS * MOE_BLOCK
    NB = P // MOE_BLOCK
    slot_tok = jnp.zeros((P,), jnp.int32).at[dest].set(stok)
    slot_w = jnp.zeros((P,), jnp.float32).at[dest].set(sw)
    block_e = jnp.minimum(jnp.searchsorted(pends, jnp.arange(NB) * MOE_BLOCK, side='right'), N_EXPERTS - 1)
    xb = xf[slot_tok].reshape(NB, MOE_BLOCK, D)

    def expert_block(args):
        xblk, e = args
        hid = jax.nn.silu(xblk @ w_gate[e]) * (xblk @ w_up[e])
        return hid @ w_down[e]

    yb = lax.map(expert_block, (xb, block_e)).reshape(P, D)
    yb = yb * slot_w[:, None].astype(yb.dtype)
    out = jnp.zeros((T, D), yb.dtype).at[slot_tok].add(yb)
    return out.reshape(B, S, D)


def setup_inputs(seed: int = 0) -> dict:
    key = jax.random.key(seed)
    ks = jax.random.split(key, 32)

    def nrm(k, shape, scale):
        return scale * jax.random.normal(k, shape, jnp.float32)

    sd = D_MODEL ** -0.5
    x = nrm(ks[0], (BATCH, SEQ, D_MODEL), 1.0)
    p = nrm(ks[1], (DEPTH, BATCH, SEQ, PLE_DIM), 1.0)
    w_in = jnp.concatenate([
        nrm(ks[2], (DEPTH, D_MODEL, 2 * M_QK), sd),
        nrm(ks[3], (DEPTH, D_MODEL, M_V), sd * BETA),
        nrm(ks[4], (DEPTH, D_MODEL, M_V), sd),
        nrm(ks[5], (DEPTH, D_MODEL, 2 * M_HEADS), sd),
        nrm(ks[6], (DEPTH, D_MODEL, POOL_W), sd),
        nrm(ks[7], (DEPTH, D_MODEL, 2 * D_MODEL), sd)], axis=-1)
    f_bias = jnp.broadcast_to(jnp.linspace(3.0, 6.0, M_HEADS, dtype=jnp.float32), (DEPTH, M_HEADS))
    b_in = nrm(ks[8], (DEPTH, IN_DIM), 0.02).at[:, F_GATE_OFF:F_GATE_OFF + M_HEADS].add(f_bias)
    conv_w = nrm(ks[9], (DEPTH, CONV_W, 2 * M_QK), CONV_W ** -0.5)
    conv_b = nrm(ks[10], (DEPTH, 2 * M_QK), 0.02)
    mh_g = 1.0 + nrm(ks[11], (DEPTH, M_V), 0.02)
    w_pool = nrm(ks[12], (DEPTH, POOL_GROUPS, POOL_GROUP_DIM, POOL_GROUP_DIM), POOL_GROUP_DIM ** -0.5)
    pool_scale = 1.0 + nrm(ks[13], (DEPTH, POOL_W), 0.1)
    w_m_br = nrm(ks[14], (DEPTH, M_V, D_MODEL), M_V ** -0.5)
    w_p_br = nrm(ks[15], (DEPTH, POOL_W, D_MODEL), POOL_W ** -0.5)
    w_out = nrm(ks[16], (DEPTH, D_MODEL, D_MODEL), sd * BETA)
    ln1_g = 1.0 + nrm(ks[17], (DEPTH, D_MODEL), 0.02)
    ln1_b = nrm(ks[18], (DEPTH, D_MODEL), 0.02)
    w_rg = nrm(ks[19], (DEPTH, D_MODEL, N_GROUPS), sd)
    b_rg = nrm(ks[20], (DEPTH, N_GROUPS), 0.01)
    w_re = nrm(ks[21], (DEPTH, D_MODEL, N_EXPERTS), sd)
    b_re = nrm(ks[22], (DEPTH, N_EXPERTS), 0.01)
    w_gate = nrm(ks[23], (DEPTH, N_EXPERTS, D_MODEL, D_EXPERT), sd)
    w_up = nrm(ks[24], (DEPTH, N_EXPERTS, D_MODEL, D_EXPERT), sd)
    w_down = nrm(ks[25], (DEPTH, N_EXPERTS, D_EXPERT, D_MODEL), (D_EXPERT ** -0.5) * BETA)
    ln2_g = 1.0 + nrm(ks[26], (DEPTH, D_MODEL), 0.02)
    ln2_b = nrm(ks[27], (DEPTH, D_MODEL), 0.02)
    w_ple_gate = nrm(ks[28], (DEPTH, D_MODEL, D_MODEL), sd)
    b_ple_gate = nrm(ks[29], (DEPTH, D_MODEL), 0.02)
    w_ple_proj = nrm(ks[30], (DEPTH, PLE_DIM, D_MODEL), (PLE_DIM ** -0.5) * BETA)
    return {'x': x, 'p': p, 'w_in': w_in, 'b_in': b_in, 'conv_w': conv_w, 'conv_b': conv_b,
            'mh_g': mh_g, 'w_pool': w_pool, 'pool_scale': pool_scale, 'w_m_br': w_m_br,
            'w_p_br': w_p_br, 'w_out': w_out, 'ln1_g': ln1_g, 'ln1_b': ln1_b, 'w_rg': w_rg,
            'b_rg': b_rg, 'w_re': w_re, 'b_re': b_re, 'w_gate': w_gate, 'w_up': w_up,
            'w_down': w_down, 'ln2_g': ln2_g, 'ln2_b': ln2_b, 'w_ple_gate': w_ple_gate,
            'b_ple_gate': b_ple_gate, 'w_ple_proj': w_ple_proj}


def reference(x, p, w_in, b_in, conv_w, conv_b, mh_g, w_pool, pool_scale, w_m_br, w_p_br, w_out,
              ln1_g, ln1_b, w_rg, b_rg, w_re, b_re, w_gate, w_up, w_down, ln2_g, ln2_b,
              w_ple_gate, b_ple_gate, w_ple_proj):
    for i in range(DEPTH):
        h = token_mixer(x, w_in[i], b_in[i], conv_w[i], conv_b[i], mh_g[i], w_pool[i], pool_scale[i],
                        w_m_br[i], w_p_br[i], w_out[i])
        x = layer_norm(ALPHA * x + h, ln1_g[i], ln1_b[i])
        h = hier_moe(x, w_rg[i], b_rg[i], w_re[i], b_re[i], w_gate[i], w_up[i], w_down[i])
        x = layer_norm(ALPHA * x + h, ln2_g[i], ln2_b[i])
        x = x + jax.nn.sigmoid(x @ w_ple_gate[i] + b_ple_gate[i]) * (p[i] @ w_ple_proj[i])
    return x
```

```python
import functools

import jax
import jax.numpy as jnp
from jax import lax
from jax.experimental import pallas as pl
from jax.experimental.pallas import tpu as pltpu

F32 = jnp.float32
BF16 = jnp.bfloat16

M_HEADS = 4
M_QK_DIM = 128
M_V_DIM = 256
M_QK = M_HEADS * M_QK_DIM
M_V = M_HEADS * M_V_DIM
CONV_W = 4
CHUNK = 128
POOL_WINDOWS = (2, 4, 8, 16)
POOL_GROUP_DIM = 256
POOL_W = len(POOL_WINDOWS) * POOL_GROUP_DIM
N_GROUPS = 4
EXPERTS_PER_GROUP = 8
N_EXPERTS = N_GROUPS * EXPERTS_PER_GROUP
TOP_K = 2
LN_EPS = 1e-5

LANES = 128
SUBLANES = 8
VMEM_LIMIT = 56 * 1024 * 1024
POOL_HALO = 32
EXPERT_TILE = 256


def _dot(a, b):
    return jnp.dot(a, b, preferred_element_type=F32)


def _params(n_grid):
    return pltpu.CompilerParams(dimension_semantics=("arbitrary",) * n_grid,
                                vmem_limit_bytes=VMEM_LIMIT)


def _log_sigmoid(x):
    return -(jnp.maximum(-x, 0.0) + jnp.log1p(jnp.exp(-jnp.abs(x))))


def _layer_norm(r, g, b):
    mu = jnp.mean(r, axis=-1, keepdims=True)
    d = r - mu
    var = jnp.mean(d * d, axis=-1, keepdims=True)
    return d * lax.rsqrt(var + LN_EPS) * g + b


def _inproj_kernel(x_ref, w_ref, b_ref, wif_ref, bif_ref, z_ref, zif_ref, xb_ref):
    @pl.when(pl.program_id(1) == 0)
    def _():
        xb = x_ref[...].astype(BF16)
        xb_ref[...] = xb
        zif_ref[...] = _dot(xb, wif_ref[...]) + bif_ref[...]

    z_ref[...] = (_dot(xb_ref[...], w_ref[...]) + b_ref[...]).astype(BF16)


def _inproj(x2d, w_main, b_main, w_if, b_if, tm, tn):
    t, d = x2d.shape
    n = w_main.shape[1]
    return pl.pallas_call(
        _inproj_kernel,
        grid=(t // tm, n // tn),
        in_specs=[pl.BlockSpec((tm, d), lambda i, j: (i, 0)),
                  pl.BlockSpec((d, tn), lambda i, j: (0, j)),
                  pl.BlockSpec((1, tn), lambda i, j: (0, j)),
                  pl.BlockSpec((d, LANES), lambda i, j: (0, 0)),
                  pl.BlockSpec((1, LANES), lambda i, j: (0, 0))],
        out_specs=[pl.BlockSpec((tm, tn), lambda i, j: (i, j)),
                   pl.BlockSpec((tm, LANES), lambda i, j: (i, 0))],
        out_shape=[jax.ShapeDtypeStruct((t, n), BF16),
                   jax.ShapeDtypeStruct((t, LANES), F32)],
        scratch_shapes=[pltpu.VMEM((tm, d), BF16)],
        compiler_params=_params(2),
        name="inproj",
    )(x2d, w_main, b_main, w_if, b_if)


def _mlstm_kernel(qk_ref, v_ref, o_ref, g_ref, cw_ref, cb_ref, mhg_ref, tri_ref, out_ref,
                  cbuf, ct_ref, n_ref, m_ref, *, nb):
    L = CHUNK
    halo = SUBLANES

    @pl.when(pl.program_id(0) == 0)
    def _():
        cbuf[:, 0:halo, :] = jnp.zeros((nb, halo, 2 * M_QK), F32)
        ct_ref[...] = jnp.zeros_like(ct_ref)
        n_ref[...] = jnp.zeros_like(n_ref)
        m_ref[...] = jnp.zeros_like(m_ref)

    cbuf[:, halo:halo + L, :] = qk_ref[...].astype(F32)
    scale = M_QK_DIM ** -0.5
    row = lax.broadcasted_iota(jnp.int32, (L, L), 0)
    col = lax.broadcasted_iota(jnp.int32, (L, L), 1)
    causal = col <= row
    tri = tri_ref[...]

    def per_batch(b, carry):
        acc = cb_ref[...] + cbuf[b, pl.ds(halo - (CONV_W - 1), L), :] * cw_ref[0:1, :]
        for tap in range(1, CONV_W):
            acc = acc + cbuf[b, pl.ds(halo - (CONV_W - 1) + tap, L), :] * cw_ref[tap:tap + 1, :]
        qk = acc * jax.nn.sigmoid(acc)
        cbuf[b, 0:halo, :] = cbuf[b, L:L + halo, :]

        gcol = g_ref[b]
        lf = _log_sigmoid(gcol)
        lf_hi = lf.astype(BF16)
        lf_lo = (lf - lf_hi.astype(F32)).astype(BF16)
        b_col = _dot(tri, lf_hi) + _dot(tri, lf_lo)
        g_row = gcol.T
        b_row = b_col.T

        for h in range(M_HEADS):
            idx = b * M_HEADS + h
            qf = qk[:, h * M_QK_DIM:(h + 1) * M_QK_DIM]
            q = qf.astype(BF16)
            kf = qk[:, M_QK + h * M_QK_DIM:M_QK + (h + 1) * M_QK_DIM] * scale
            k = kf.astype(BF16)
            v = v_ref[b, :, h * M_V_DIM:(h + 1) * M_V_DIM]
            bj = b_col[:, M_HEADS + h:M_HEADS + h + 1]
            bl = b_row[M_HEADS + h:M_HEADS + h + 1, :]
            il = g_row[h:h + 1, :]
            ij = gcol[:, h:h + 1]
            m_prev = m_ref[idx][0:1, 0:1]

            dmat = jnp.where(causal, bj - bl + il, -jnp.inf)
            inter = bj + m_prev
            m_row = jnp.maximum(inter, jnp.max(dmat, axis=1, keepdims=True))
            pmat = jnp.exp(dmat - m_row)
            s = lax.dot_general(q, k, (((1,), (1,)), ((), ())), preferred_element_type=F32) * pmat
            scale_prev = jnp.exp(inter - m_row)
            ct = ct_ref[idx]
            nvec = n_ref[idx][0:1, :]
            num = _dot(s.astype(BF16), v) + scale_prev * _dot(q, ct.astype(BF16))
            den = (jnp.sum(s, axis=1, keepdims=True)
                   + scale_prev * jnp.sum(qf * nvec, axis=1, keepdims=True))
            hh = num / jnp.maximum(jnp.abs(den), jnp.exp(-m_row))

            b_last = bj[L - 1:L, :]
            gvec = b_last - bj + ij
            m_new = jnp.maximum(b_last + m_prev, jnp.max(gvec, axis=0, keepdims=True))
            wvec = jnp.exp(gvec - m_new)
            decay = jnp.exp(b_last + m_prev - m_new)
            kw = kf * wvec
            ct_ref[idx] = decay * ct + lax.dot_general(
                kw.astype(BF16), v, (((0,), (0,)), ((), ())), preferred_element_type=F32)
            n_new = decay * nvec + jnp.sum(kw, axis=0, keepdims=True)
            n_ref[idx] = jnp.broadcast_to(n_new, (SUBLANES, M_QK_DIM))
            m_ref[idx] = jnp.broadcast_to(m_new, (SUBLANES, LANES))

            mu = jnp.mean(hh, axis=-1, keepdims=True)
            dlt = hh - mu
            var = jnp.mean(dlt * dlt, axis=-1, keepdims=True)
            hn = dlt * lax.rsqrt(var + LN_EPS) * mhg_ref[:, h * M_V_DIM:(h + 1) * M_V_DIM]
            og = jax.nn.sigmoid(o_ref[b, :, h * M_V_DIM:(h + 1) * M_V_DIM].astype(F32))
            out_ref[b, :, h * M_V_DIM:(h + 1) * M_V_DIM] = (hn * og).astype(BF16)
        return carry

    lax.fori_loop(0, nb, per_batch, 0)


def _mlstm(z3, zif3, conv_w, conv_b, mh_g, tri):
    nb, s, _ = z3.shape
    L = CHUNK
    blk = lambda colblk: pl.BlockSpec((nb, L, M_V), lambda c: (0, c, colblk))
    const = lambda shape: pl.BlockSpec(shape, lambda c: (0,) * len(shape))
    return pl.pallas_call(
        functools.partial(_mlstm_kernel, nb=nb),
        grid=(s // L,),
        in_specs=[blk(0), blk(1), blk(2),
                  pl.BlockSpec((nb, L, LANES), lambda c: (0, c, 0)),
                  const((CONV_W, 2 * M_QK)), const((1, 2 * M_QK)), const((1, M_V)), const((L, L))],
        out_specs=pl.BlockSpec((nb, L, M_V), lambda c: (0, c, 0)),
        out_shape=jax.ShapeDtypeStruct((nb, s, M_V), BF16),
        scratch_shapes=[pltpu.VMEM((nb, L + SUBLANES, 2 * M_QK), F32),
                        pltpu.VMEM((nb * M_HEADS, M_QK_DIM, M_V_DIM), F32),
                        pltpu.VMEM((nb * M_HEADS, SUBLANES, M_QK_DIM), F32),
                        pltpu.VMEM((nb * M_HEADS, SUBLANES, LANES), F32)],
        compiler_params=_params(1),
        name="mlstm",
    )(z3, z3, z3, zif3, conv_w, conv_b, mh_g, tri)


def _branch_kernel(hg_ref, u_ref, uh_ref, gm_ref, gp_ref, wpool_ref, ps_ref, wm_ref, wp_ref,
                   out_ref, xa, xb, yp_ref, *, tm, seq):
    H = POOL_HALO
    G = POOL_GROUP_DIM
    t0 = lax.rem(pl.program_id(0) * tm, seq)
    u = u_ref[...].astype(F32)
    xa[H:H + tm, :] = u
    xa[0:H, :] = jnp.where(t0 == 0, 0.0, uh_ref[...].astype(F32))
    n = tm + H - 8
    xb[8:8 + n, :] = xa[8:8 + n, :] + xa[7:7 + n, :]
    n = tm + H - 16
    xa[16:16 + n, G:] = xb[16:16 + n, G:] + xb[14:14 + n, G:]
    n = tm + H - 24
    xb[24:24 + n, 2 * G:] = xa[24:24 + n, 2 * G:] + xa[20:20 + n, 2 * G:]
    xa[H:H + tm, 3 * G:] = xb[H:H + tm, 3 * G:] + xb[H - 8:H - 8 + tm, 3 * G:]
    tpos = t0 + lax.broadcasted_iota(jnp.int32, (tm, 1), 0)
    for g, win in enumerate(POOL_WINDOWS):
        src = (xb, xa, xb, xa)[g]
        cols = slice(g * G, (g + 1) * G)
        cnt = jnp.minimum(tpos + 1, win).astype(F32)
        y = src[H:H + tm, cols] / cnt - u[:, cols]
        yp = _dot(y.astype(BF16), wpool_ref[g]) * ps_ref[:, cols]
        yp_ref[:, cols] = yp.astype(BF16)
    pb = _dot(yp_ref[...], wp_ref[...])
    a = _dot(hg_ref[...], wm_ref[...])
    merged = (jax.nn.sigmoid(gm_ref[...].astype(F32)) * a
              + jax.nn.sigmoid(gp_ref[...].astype(F32)) * pb)
    out_ref[...] = merged.astype(BF16)


def _branch(hg2d, z_main, w_pool, pool_scale, w_m_br, w_p_br, seq, tm):
    t = hg2d.shape[0]
    d = w_m_br.shape[1]
    hb = tm // POOL_HALO
    const = lambda shape: pl.BlockSpec(shape, lambda i: (0,) * len(shape))
    return pl.pallas_call(
        functools.partial(_branch_kernel, tm=tm, seq=seq),
        grid=(t // tm,),
        in_specs=[pl.BlockSpec((tm, M_V), lambda i: (i, 0)),
                  pl.BlockSpec((tm, POOL_W), lambda i: (i, 3)),
                  pl.BlockSpec((POOL_HALO, POOL_W), lambda i: (jnp.maximum(i * hb - 1, 0), 3)),
                  pl.BlockSpec((tm, d), lambda i: (i, 2)),
                  pl.BlockSpec((tm, d), lambda i: (i, 3)),
                  const(w_pool.shape), const((1, POOL_W)), const(w_m_br.shape), const(w_p_br.shape)],
        out_specs=pl.BlockSpec((tm, d), lambda i: (i, 0)),
        out_shape=jax.ShapeDtypeStruct((t, d), BF16),
        scratch_shapes=[pltpu.VMEM((tm + POOL_HALO, POOL_W), F32),
                        pltpu.VMEM((tm + POOL_HALO, POOL_W), F32),
                        pltpu.VMEM((tm, POOL_W), BF16)],
        compiler_params=_params(1),
        name="branch",
    )(hg2d, z_main, z_main, z_main, z_main, w_pool, pool_scale, w_m_br, w_p_br)


def _route_kernel(mg_ref, x_ref, wout_ref, g_ref, b_ref, wr_ref, br_ref, tri_ref,
                  x1_ref, route_ref, gate_ref, cnt_ref, carry, *, alpha, tm):
    @pl.when(pl.program_id(0) == 0)
    def _():
        carry[...] = jnp.zeros_like(carry)

    r = alpha * x_ref[...] + _dot(mg_ref[...], wout_ref[...])
    x1 = _layer_norm(r, g_ref[...], b_ref[...])
    x1_ref[...] = x1

    hi = x1.astype(BF16)
    lo = (x1 - hi.astype(F32)).astype(BF16)
    p1 = _dot(hi, wr_ref[...])
    p2 = _dot(lo, wr_ref[:, 0:LANES])
    logits = p1[:, 0:LANES] + p1[:, LANES:2 * LANES] + p2 + br_ref[...]

    lane = lax.broadcasted_iota(jnp.int32, (tm, LANES), 1)
    lanef = lane.astype(F32)
    big = float(LANES)

    def softmax_masked(mask):
        z = jnp.where(mask, logits, -jnp.inf)
        e = jnp.exp(z - jnp.max(z, axis=1, keepdims=True))
        return e / jnp.sum(e, axis=1, keepdims=True)

    def top1(vals):
        top = jnp.max(vals, axis=1, keepdims=True)
        idx = jnp.min(jnp.where(vals == top, lanef, big), axis=1, keepdims=True)
        return top, idx

    is_grp = lane < N_GROUPS
    pg = jnp.where(is_grp, softmax_masked(is_grp), -1.0)
    pg_top, g_idx = top1(pg)
    e_lo = N_GROUPS + EXPERTS_PER_GROUP * g_idx.astype(jnp.int32)
    in_grp = (lane >= e_lo) & (lane < e_lo + EXPERTS_PER_GROUP)
    pe = jnp.where(in_grp, softmax_masked(in_grp), -1.0)
    pe1, i1 = top1(pe)
    pe2, i2 = top1(jnp.where(lanef == i1, -1.0, pe))
    den = pe1 + pe2
    gate1 = pg_top * pe1 / den
    gate2 = pg_top * pe2 / den
    e1 = i1 - float(N_GROUPS)
    e2 = i2 - float(N_GROUPS)

    hit1 = lanef == e1
    hit2 = lanef == e2
    onehot = jnp.where(hit1 | hit2, 1.0, 0.0)
    incl = _dot(tri_ref[...], onehot.astype(BF16))
    excl = incl - onehot + carry[0:1, :]
    r1 = jnp.sum(jnp.where(hit1, excl, 0.0), axis=1, keepdims=True)
    r2 = jnp.sum(jnp.where(hit2, excl, 0.0), axis=1, keepdims=True)
    carry[...] = carry[...] + incl[tm - 1:tm, :]
    cnt_ref[...] = carry[...]

    route = jnp.where(lane == 0, e1, jnp.where(lane == 1, e2, jnp.where(lane == 2, r1, jnp.where(lane == 3, r2, 0.0))))
    route_ref[...] = route.astype(jnp.int32)
    gate_ref[...] = jnp.where(lane == 0, gate1, jnp.where(lane == 1, gate2, 0.0))


def _route(merged, x2d, w_out, ln_g, ln_b, w_r, b_r, tri, alpha, tm):
    t, d = x2d.shape
    const = lambda shape: pl.BlockSpec(shape, lambda i: (0,) * len(shape))
    rows = lambda width: pl.BlockSpec((tm, width), lambda i: (i, 0))
    return pl.pallas_call(
        functools.partial(_route_kernel, alpha=alpha, tm=tm),
        grid=(t // tm,),
        in_specs=[rows(d), rows(d), const(w_out.shape), const((1, d)), const((1, d)),
                  const(w_r.shape), const((1, LANES)), const((tm, tm))],
        out_specs=[rows(d), rows(LANES), rows(LANES), const((SUBLANES, LANES))],
        out_shape=[jax.ShapeDtypeStruct((t, d), F32),
                   jax.ShapeDtypeStruct((t, LANES), jnp.int32),
                   jax.ShapeDtypeStruct((t, LANES), F32),
                   jax.ShapeDtypeStruct((SUBLANES, LANES), F32)],
        scratch_shapes=[pltpu.VMEM((SUBLANES, LANES), F32)],
        compiler_params=_params(1),
        name="route",
    )(merged, x2d, w_out, ln_g, ln_b, w_r, b_r, tri)


def _row_copy(src, src_row, dst, dst_row, sem, n=1):
    return pltpu.make_async_copy(src.at[pl.ds(src_row, n)], dst.at[pl.ds(dst_row, n)], sem)


def _dispatch_kernel(pad_start_ref, pad_len_ref, nu_ref, dest_ref, x_hbm, xs_hbm, sem, *, tb, te):
    i = pl.program_id(0)

    @pl.when(i == 0)
    def _():
        def per_expert(e, total):
            start = pad_start_ref[e]
            npad = pad_len_ref[e]

            def fill(r, c):
                _row_copy(x_hbm, 0, xs_hbm, start + r, sem).start()
                return c

            lax.fori_loop(0, npad, fill, 0)
            return total + npad

        total = lax.fori_loop(0, N_EXPERTS, per_expert, 0)

        def wait_rows(n):
            def body(r, c):
                _row_copy(x_hbm, 0, xs_hbm, 0, sem, n).wait()
                return c
            return body

        lax.fori_loop(0, total // SUBLANES, wait_rows(SUBLANES), 0)
        lax.fori_loop(0, total % SUBLANES, wait_rows(1), 0)

        def fill_tile(j, c):
            _row_copy(x_hbm, 0, xs_hbm, pl.multiple_of(j * te, te), sem, te).start()
            return c

        n_tiles = xs_hbm.shape[0] // te
        lax.fori_loop(nu_ref[0], n_tiles, fill_tile, 0)
        lax.fori_loop(nu_ref[0], n_tiles, wait_rows(te), 0)

    def scatter(t, c):
        row = i * tb + t
        _row_copy(x_hbm, row, xs_hbm, dest_ref[TOP_K * t], sem).start()
        _row_copy(x_hbm, row, xs_hbm, dest_ref[TOP_K * t + 1], sem).start()
        return c

    lax.fori_loop(0, tb, scatter, 0)
    _row_copy(x_hbm, 0, xs_hbm, 0, sem, TOP_K * tb).wait()


def _dispatch(x1, dest_flat, pad_start, pad_len, n_used, n_slots, tb, te):
    t, d = x1.shape
    grid_spec = pltpu.PrefetchScalarGridSpec(
        num_scalar_prefetch=3,
        grid=(t // tb,),
        in_specs=[pl.BlockSpec((TOP_K * tb,), lambda i, ps, pn, nu: (i,), memory_space=pltpu.SMEM),
                  pl.BlockSpec(memory_space=pl.ANY)],
        out_specs=pl.BlockSpec(memory_space=pl.ANY),
        scratch_shapes=[pltpu.SemaphoreType.DMA(())],
    )
    return pl.pallas_call(
        functools.partial(_dispatch_kernel, tb=tb, te=te),
        grid_spec=grid_spec,
        out_shape=jax.ShapeDtypeStruct((n_slots, d), x1.dtype),
        compiler_params=_params(1),
        name="dispatch",
    )(pad_start, pad_len, n_used, dest_flat, x1)


def _expert_kernel(be_ref, nu_ref, x_ref, wg_ref, wu_ref, wd_ref, y_ref):
    i = pl.program_id(0)

    @pl.when(i < nu_ref[0])
    def _():
        xb = x_ref[...].astype(BF16)
        gt = _dot(xb, wg_ref[0])
        up = _dot(xb, wu_ref[0])
        hid = gt * jax.nn.sigmoid(gt) * up
        y_ref[...] = _dot(hid.astype(BF16), wd_ref[0])

    @pl.when(i >= nu_ref[0])
    def _():
        y_ref[...] = jnp.zeros_like(y_ref)


def _experts(xs, block_e, n_used, w_gate, w_up, w_down, te):
    n_slots, d = xs.shape
    de = w_gate.shape[2]
    grid_spec = pltpu.PrefetchScalarGridSpec(
        num_scalar_prefetch=2,
        grid=(n_slots // te,),
        in_specs=[pl.BlockSpec((te, d), lambda i, be, nu: (jnp.minimum(i, nu[0] - 1), 0)),
                  pl.BlockSpec((1, d, de), lambda i, be, nu: (be[i], 0, 0)),
                  pl.BlockSpec((1, d, de), lambda i, be, nu: (be[i], 0, 0)),
                  pl.BlockSpec((1, de, d), lambda i, be, nu: (be[i], 0, 0))],
        out_specs=pl.BlockSpec((te, d), lambda i, be, nu: (i, 0)),
    )
    return pl.pallas_call(
        _expert_kernel,
        grid_spec=grid_spec,
        out_shape=jax.ShapeDtypeStruct((n_slots, d), F32),
        compiler_params=_params(1),
        name="experts",
    )(block_e, n_used, xs, w_gate, w_up, w_down)


def _final_kernel(dest_ref, x1_ref, gate_ref, p_ref, ys_hbm, g_ref, b_ref, wg_ref, bg_ref, wp_ref,
                  out_ref, ybuf, sem, *, alpha, tm):
    def gather(t, c):
        for k in range(TOP_K):
            pltpu.make_async_copy(ys_hbm.at[pl.ds(dest_ref[TOP_K * t + k], 1)],
                                  ybuf.at[k, pl.ds(t, 1)], sem).start()
        return c

    lax.fori_loop(0, tm, gather, 0)
    pp = _dot(p_ref[...].astype(BF16), wp_ref[...])
    for k in range(TOP_K):
        pltpu.make_async_copy(ys_hbm.at[pl.ds(0, tm)], ybuf.at[k], sem).wait()

    gate = gate_ref[...]
    y = gate[:, 0:1] * ybuf[0] + gate[:, 1:2] * ybuf[1]
    x2 = _layer_norm(alpha * x1_ref[...] + y, g_ref[...], b_ref[...])
    gl = _dot(x2.astype(BF16), wg_ref[...]) + bg_ref[...]
    out_ref[...] = x2 + jax.nn.sigmoid(gl) * pp


def _final(dest_flat, x1, gate, p2d, ys, ln_g, ln_b, w_pg, b_pg, w_pp, alpha, tm):
    t, d = x1.shape
    const = lambda shape: pl.BlockSpec(shape, lambda i: (0,) * len(shape))
    rows = lambda width: pl.BlockSpec((tm, width), lambda i: (i, 0))
    return pl.pallas_call(
        functools.partial(_final_kernel, alpha=alpha, tm=tm),
        grid=(t // tm,),
        in_specs=[pl.BlockSpec((TOP_K * tm,), lambda i: (i,), memory_space=pltpu.SMEM),
                  rows(d), rows(LANES), rows(p2d.shape[1]),
                  pl.BlockSpec(memory_space=pl.ANY),
                  const((1, d)), const((1, d)), const(w_pg.shape), const((1, d)), const(w_pp.shape)],
        out_specs=rows(d),
        out_shape=jax.ShapeDtypeStruct((t, d), F32),
        scratch_shapes=[pltpu.VMEM((TOP_K, tm, d), F32), pltpu.SemaphoreType.DMA(())],
        compiler_params=_params(1),
        name="final",
    )(dest_flat, x1, gate, p2d, ys, ln_g, ln_b, w_pg, b_pg, w_pp)


def _pad_cols(a, width):
    return jnp.pad(a, ((0, 0), (0, width - a.shape[1])))


def _tri(n):
    return jnp.tril(jnp.ones((n, n), BF16))


def _layer(x, p, w_in, b_in, conv_w, conv_b, mh_g, w_pool, pool_scale, w_m_br, w_p_br, w_out,
           ln1_g, ln1_b, w_rg, b_rg, w_re, b_re, w_gate, w_up, w_down, ln2_g, ln2_b,
           w_ple_gate, b_ple_gate, w_ple_proj, alpha):
    nb, seq, d = x.shape
    t = nb * seq
    x2d = x.reshape(t, d)
    row = lambda a: a.reshape(1, -1)

    c_if = 2 * M_QK + 2 * M_V
    w_main = jnp.concatenate([w_in[:, :c_if], w_in[:, c_if + 2 * M_HEADS:]], axis=1).astype(BF16)
    b_main = row(jnp.concatenate([b_in[:c_if], b_in[c_if + 2 * M_HEADS:]]))
    w_if = _pad_cols(w_in[:, c_if:c_if + 2 * M_HEADS], LANES).astype(BF16)
    b_if = _pad_cols(row(b_in[c_if:c_if + 2 * M_HEADS]), LANES)

    tm_in = min(1024, t)
    z_main, z_if = _inproj(x2d, w_main, b_main, w_if, b_if, tm_in, 1024)

    hg = _mlstm(z_main.reshape(nb, seq, -1), z_if.reshape(nb, seq, LANES),
                conv_w, row(conv_b), row(mh_g), _tri(CHUNK))

    tm = min(512, seq)
    merged = _branch(hg.reshape(t, M_V), z_main, w_pool.astype(BF16), row(pool_scale),
                     w_m_br.astype(BF16), w_p_br.astype(BF16), seq, tm)

    w_r = _pad_cols(jnp.concatenate([w_rg, w_re], axis=1), LANES)
    w_r_hi = w_r.astype(BF16)
    w_r_lo = (w_r - w_r_hi.astype(F32)).astype(BF16)
    b_r = _pad_cols(row(jnp.concatenate([b_rg, b_re])), LANES)
    x1, route, gate, cnt = _route(merged, x2d, w_out.astype(BF16), row(ln1_g), row(ln1_b),
                                  jnp.concatenate([w_r_hi, w_r_lo], axis=1), b_r, _tri(tm), alpha, tm)

    te = EXPERT_TILE
    counts = cnt[0, :N_EXPERTS].astype(jnp.int32)
    pcounts = (counts + te - 1) // te * te
    pends = jnp.cumsum(pcounts)
    pstarts = pends - pcounts
    dest = (pstarts[route[:, 0:TOP_K]] + route[:, TOP_K:2 * TOP_K]).reshape(-1)
    n_slots = t * TOP_K + N_EXPERTS * te
    n_tiles = n_slots // te
    n_used = (pends[-1] // te).reshape(1)
    tile_row = jnp.minimum(jnp.arange(n_tiles, dtype=jnp.int32), n_used - 1) * te
    block_e = jnp.minimum(jnp.sum((tile_row[:, None] >= pends[None, :]).astype(jnp.int32), axis=1), N_EXPERTS - 1)

    xs = _dispatch(x1, dest, pstarts + counts, pcounts - counts, n_used, n_slots, min(512, t), te)
    ys = _experts(xs, block_e, n_used, w_gate.astype(BF16), w_up.astype(BF16), w_down.astype(BF16), te)
    return _final(dest, x1, gate, p.reshape(t, -1), ys, row(ln2_g), row(ln2_b),
                  w_ple_gate.astype(BF16), row(b_ple_gate), w_ple_proj.astype(BF16), alpha,
                  min(256, t)).reshape(nb, seq, d)


def kernel(x, p, w_in, b_in, conv_w, conv_b, mh_g, w_pool, pool_scale, w_m_br, w_p_br, w_out, ln1_g, ln1_b, w_rg, b_rg, w_re, b_re, w_gate, w_up, w_down, ln2_g, ln2_b, w_ple_gate, b_ple_gate, w_ple_proj):
    depth = w_in.shape[0]
    alpha = (2 * depth) ** 0.25
    for i in range(depth):
        x = _layer(x, p[i], w_in[i], b_in[i], conv_w[i], conv_b[i], mh_g[i], w_pool[i], pool_scale[i],
                   w_m_br[i], w_p_br[i], w_out[i], ln1_g[i], ln1_b[i], w_rg[i], b_rg[i], w_re[i], b_re[i],
                   w_gate[i], w_up[i], w_down[i], ln2_g[i], ln2_b[i], w_ple_gate[i], b_ple_gate[i],
                   w_ple_proj[i], alpha)
    return x
```

```python
import functools

import jax
import jax.numpy as jnp
from jax import lax
from jax.experimental import pallas as pl
from jax.experimental.pallas import tpu as pltpu

F32 = jnp.float32
BF16 = jnp.bfloat16

M_HEADS = 4
M_QK_DIM = 128
M_V_DIM = 256
M_QK = M_HEADS * M_QK_DIM
M_V = M_HEADS * M_V_DIM
CONV_W = 4
CHUNK = 128
POOL_WINDOWS = (2, 4, 8, 16)
POOL_GROUP_DIM = 256
POOL_W = len(POOL_WINDOWS) * POOL_GROUP_DIM
N_GROUPS = 4
EXPERTS_PER_GROUP = 8
N_EXPERTS = N_GROUPS * EXPERTS_PER_GROUP
TOP_K = 2
LN_EPS = 1e-5

LANES = 128
SUBLANES = 8
VMEM_LIMIT = 56 * 1024 * 1024
POOL_HALO = 32
EXPERT_TILE = 256


def _dot(a, b):
    return jnp.dot(a, b, preferred_element_type=F32)


def _params(n_grid):
    return pltpu.CompilerParams(dimension_semantics=("arbitrary",) * n_grid,
                                vmem_limit_bytes=VMEM_LIMIT)


def _log_sigmoid(x):
    return -(jnp.maximum(-x, 0.0) + jnp.log1p(jnp.exp(-jnp.abs(x))))


def _layer_norm(r, g, b):
    mu = jnp.mean(r, axis=-1, keepdims=True)
    d = r - mu
    var = jnp.mean(d * d, axis=-1, keepdims=True)
    return d * lax.rsqrt(var + LN_EPS) * g + b


def _inproj_kernel(x_ref, w_ref, b_ref, wif_ref, bif_ref, z_ref, zif_ref, xb_ref):
    @pl.when(pl.program_id(1) == 0)
    def _():
        xb = x_ref[...].astype(BF16)
        xb_ref[...] = xb
        zif_ref[...] = _dot(xb, wif_ref[...]) + bif_ref[...]

    z_ref[...] = (_dot(xb_ref[...], w_ref[...]) + b_ref[...]).astype(BF16)


def _inproj(x2d, w_main, b_main, w_if, b_if, tm, tn):
    t, d = x2d.shape
    n = w_main.shape[1]
    return pl.pallas_call(
        _inproj_kernel,
        grid=(t // tm, n // tn),
        in_specs=[pl.BlockSpec((tm, d), lambda i, j: (i, 0)),
                  pl.BlockSpec((d, tn), lambda i, j: (0, j)),
                  pl.BlockSpec((1, tn), lambda i, j: (0, j)),
                  pl.BlockSpec((d, LANES), lambda i, j: (0, 0)),
                  pl.BlockSpec((1, LANES), lambda i, j: (0, 0))],
        out_specs=[pl.BlockSpec((tm, tn), lambda i, j: (i, j)),
                   pl.BlockSpec((tm, LANES), lambda i, j: (i, 0))],
        out_shape=[jax.ShapeDtypeStruct((t, n), BF16),
                   jax.ShapeDtypeStruct((t, LANES), F32)],
        scratch_shapes=[pltpu.VMEM((tm, d), BF16)],
        compiler_params=_params(2),
        name="inproj",
    )(x2d, w_main, b_main, w_if, b_if)


def _mlstm_kernel(qk_ref, v_ref, o_ref, g_ref, cw_ref, cb_ref, mhg_ref, tri_ref, out_ref,
                  cbuf, ct_ref, n_ref, m_ref, *, nb):
    L = CHUNK
    halo = SUBLANES

    @pl.when(pl.program_id(0) == 0)
    def _():
        cbuf[:, 0:halo, :] = jnp.zeros((nb, halo, 2 * M_QK), F32)
        ct_ref[...] = jnp.zeros_like(ct_ref)
        n_ref[...] = jnp.zeros_like(n_ref)
        m_ref[...] = jnp.zeros_like(m_ref)

    cbuf[:, halo:halo + L, :] = qk_ref[...].astype(F32)
    scale = M_QK_DIM ** -0.5
    row = lax.broadcasted_iota(jnp.int32, (L, L), 0)
    col = lax.broadcasted_iota(jnp.int32, (L, L), 1)
    causal = col <= row
    tri = tri_ref[...]

    def per_batch(b, carry):
        acc = cb_ref[...] + cbuf[b, pl.ds(halo - (CONV_W - 1), L), :] * cw_ref[0:1, :]
        for tap in range(1, CONV_W):
            acc = acc + cbuf[b, pl.ds(halo - (CONV_W - 1) + tap, L), :] * cw_ref[tap:tap + 1, :]
        qk = acc * jax.nn.sigmoid(acc)
        cbuf[b, 0:halo, :] = cbuf[b, L:L + halo, :]

        gcol = g_ref[b]
        lf = _log_sigmoid(gcol)
        lf_hi = lf.astype(BF16)
        lf_lo = (lf - lf_hi.astype(F32)).astype(BF16)
        b_col = _dot(tri, lf_hi) + _dot(tri, lf_lo)
        g_row = gcol.T
        b_row = b_col.T

        for h in range(M_HEADS):
            idx = b * M_HEADS + h
            qf = qk[:, h * M_QK_DIM:(h + 1) * M_QK_DIM]
            q = qf.astype(BF16)
            kf = qk[:, M_QK + h * M_QK_DIM:M_QK + (h + 1) * M_QK_DIM] * scale
            k = kf.astype(BF16)
            v = v_ref[b, :, h * M_V_DIM:(h + 1) * M_V_DIM]
            bj = b_col[:, M_HEADS + h:M_HEADS + h + 1]
            bl = b_row[M_HEADS + h:M_HEADS + h + 1, :]
            il = g_row[h:h + 1, :]
            ij = gcol[:, h:h + 1]
            m_prev = m_ref[idx][0:1, 0:1]

            dmat = jnp.where(causal, bj - bl + il, -jnp.inf)
            inter = bj + m_prev
            m_row = jnp.maximum(inter, jnp.max(dmat, axis=1, keepdims=True))
            pmat = jnp.exp(dmat - m_row)
            s = lax.dot_general(q, k, (((1,), (1,)), ((), ())), preferred_element_type=F32) * pmat
            scale_prev = jnp.exp(inter - m_row)
            ct = ct_ref[idx]
            nvec = n_ref[idx][0:1, :]
            num = _dot(s.astype(BF16), v) + scale_prev * _dot(q, ct.astype(BF16))
            den = (jnp.sum(s, axis=1, keepdims=True)
                   + scale_prev * jnp.sum(qf * nvec, axis=1, keepdims=True))
            hh = num / jnp.maximum(jnp.abs(den), jnp.exp(-m_row))

            b_last = bj[L - 1:L, :]
            gvec = b_last - bj + ij
            m_new = jnp.maximum(b_last + m_prev, jnp.max(gvec, axis=0, keepdims=True))
            wvec = jnp.exp(gvec - m_new)
            decay = jnp.exp(b_last + m_prev - m_new)
            kw = kf * wvec
            ct_ref[idx] = decay * ct + lax.dot_general(
                kw.astype(BF16), v, (((0,), (0,)), ((), ())), preferred_element_type=F32)
            n_new = decay * nvec + jnp.sum(kw, axis=0, keepdims=True)
            n_ref[idx] = jnp.broadcast_to(n_new, (SUBLANES, M_QK_DIM))
            m_ref[idx] = jnp.broadcast_to(m_new, (SUBLANES, LANES))

            mu = jnp.mean(hh, axis=-1, keepdims=True)
            dlt = hh - mu
            var = jnp.mean(dlt * dlt, axis=-1, keepdims=True)
            hn = dlt * lax.rsqrt(var + LN_EPS) * mhg_ref[:, h * M_V_DIM:(h + 1) * M_V_DIM]
            og = jax.nn.sigmoid(o_ref[b, :, h * M_V_DIM:(h + 1) * M_V_DIM].astype(F32))
            out_ref[b, :, h * M_V_DIM:(h + 1) * M_V_DIM] = (hn * og).astype(BF16)
        return carry

    lax.fori_loop(0, nb, per_batch, 0)


def _mlstm(z3, zif3, conv_w, conv_b, mh_g, tri):
    nb, s, _ = z3.shape
    L = CHUNK
    blk = lambda colblk: pl.BlockSpec((nb, L, M_V), lambda c: (0, c, colblk))
    const = lambda shape: pl.BlockSpec(shape, lambda c: (0,) * len(shape))
    return pl.pallas_call(
        functools.partial(_mlstm_kernel, nb=nb),
        grid=(s // L,),
        in_specs=[blk(0), blk(1), blk(2),
                  pl.BlockSpec((nb, L, LANES), lambda c: (0, c, 0)),
                  const((CONV_W, 2 * M_QK)), const((1, 2 * M_QK)), const((1, M_V)), const((L, L))],
        out_specs=pl.BlockSpec((nb, L, M_V), lambda c: (0, c, 0)),
        out_shape=jax.ShapeDtypeStruct((nb, s, M_V), BF16),
        scratch_shapes=[pltpu.VMEM((nb, L + SUBLANES, 2 * M_QK), F32),
                        pltpu.VMEM((nb * M_HEADS, M_QK_DIM, M_V_DIM), F32),
                        pltpu.VMEM((nb * M_HEADS, SUBLANES, M_QK_DIM), F32),
                        pltpu.VMEM((nb * M_HEADS, SUBLANES, LANES), F32)],
        compiler_params=_params(1),
        name="mlstm",
    )(z3, z3, z3, zif3, conv_w, conv_b, mh_g, tri)


def _branch_kernel(hg_ref, u_ref, uh_ref, gm_ref, gp_ref, wpool_ref, ps_ref, wm_ref, wp_ref,
                   out_ref, xa, xb, yp_ref, *, tm, seq):
    H = POOL_HALO
    G = POOL_GROUP_DIM
    t0 = lax.rem(pl.program_id(0) * tm, seq)
    u = u_ref[...].astype(F32)
    xa[H:H + tm, :] = u
    xa[0:H, :] = jnp.where(t0 == 0, 0.0, uh_ref[...].astype(F32))
    n = tm + H - 8
    xb[8:8 + n, :] = xa[8:8 + n, :] + xa[7:7 + n, :]
    n = tm + H - 16
    xa[16:16 + n, G:] = xb[16:16 + n, G:] + xb[14:14 + n, G:]
    n = tm + H - 24
    xb[24:24 + n, 2 * G:] = xa[24:24 + n, 2 * G:] + xa[20:20 + n, 2 * G:]
    xa[H:H + tm, 3 * G:] = xb[H:H + tm, 3 * G:] + xb[H - 8:H - 8 + tm, 3 * G:]
    tpos = t0 + lax.broadcasted_iota(jnp.int32, (tm, 1), 0)
    for g, win in enumerate(POOL_WINDOWS):
        src = (xb, xa, xb, xa)[g]
        cols = slice(g * G, (g + 1) * G)
        cnt = jnp.minimum(tpos + 1, win).astype(F32)
        y = src[H:H + tm, cols] / cnt - u[:, cols]
        yp = _dot(y.astype(BF16), wpool_ref[g]) * ps_ref[:, cols]
        yp_ref[:, cols] = yp.astype(BF16)
    pb = _dot(yp_ref[...], wp_ref[...])
    a = _dot(hg_ref[...], wm_ref[...])
    merged = (jax.nn.sigmoid(gm_ref[...].astype(F32)) * a
              + jax.nn.sigmoid(gp_ref[...].astype(F32)) * pb)
    out_ref[...] = merged.astype(BF16)


def _branch(hg2d, z_main, w_pool, pool_scale, w_m_br, w_p_br, seq, tm):
    t = hg2d.shape[0]
    d = w_m_br.shape[1]
    hb = tm // POOL_HALO
    const = lambda shape: pl.BlockSpec(shape, lambda i: (0,) * len(shape))
    return pl.pallas_call(
        functools.partial(_branch_kernel, tm=tm, seq=seq),
        grid=(t // tm,),
        in_specs=[pl.BlockSpec((tm, M_V), lambda i: (i, 0)),
                  pl.BlockSpec((tm, POOL_W), lambda i: (i, 3)),
                  pl.BlockSpec((POOL_HALO, POOL_W), lambda i: (jnp.maximum(i * hb - 1, 0), 3)),
                  pl.BlockSpec((tm, d), lambda i: (i, 2)),
                  pl.BlockSpec((tm, d), lambda i: (i, 3)),
                  const(w_pool.shape), const((1, POOL_W)), const(w_m_br.shape), const(w_p_br.shape)],
        out_specs=pl.BlockSpec((tm, d), lambda i: (i, 0)),
        out_shape=jax.ShapeDtypeStruct((t, d), BF16),
        scratch_shapes=[pltpu.VMEM((tm + POOL_HALO, POOL_W), F32),
                        pltpu.VMEM((tm + POOL_HALO, POOL_W), F32),
                        pltpu.VMEM((tm, POOL_W), BF16)],
        compiler_params=_params(1),
        name="branch",
    )(hg2d, z_main, z_main, z_main, z_main, w_pool, pool_scale, w_m_br, w_p_br)


def _route_kernel(mg_ref, x_ref, wout_ref, g_ref, b_ref, wr_ref, br_ref, tri_ref,
                  x1_ref, route_ref, gate_ref, cnt_ref, carry, *, alpha, tm):
    @pl.when(pl.program_id(0) == 0)
    def _():
        carry[...] = jnp.zeros_like(carry)

    r = alpha * x_ref[...] + _dot(mg_ref[...], wout_ref[...])
    x1 = _layer_norm(r, g_ref[...], b_ref[...])
    x1_ref[...] = x1

    hi = x1.astype(BF16)
    lo = (x1 - hi.astype(F32)).astype(BF16)
    p1 = _dot(hi, wr_ref[...])
    p2 = _dot(lo, wr_ref[:, 0:LANES])
    logits = p1[:, 0:LANES] + p1[:, LANES:2 * LANES] + p2 + br_ref[...]

    lane = lax.broadcasted_iota(jnp.int32, (tm, LANES), 1)
    lanef = lane.astype(F32)
    big = float(LANES)

    def softmax_masked(mask):
        z = jnp.where(mask, logits, -jnp.inf)
        e = jnp.exp(z - jnp.max(z, axis=1, keepdims=True))
        return e / jnp.sum(e, axis=1, keepdims=True)

    def top1(vals):
        top = jnp.max(vals, axis=1, keepdims=True)
        idx = jnp.min(jnp.where(vals == top, lanef, big), axis=1, keepdims=True)
        return top, idx

    is_grp = lane < N_GROUPS
    pg = jnp.where(is_grp, softmax_masked(is_grp), -1.0)
    pg_top, g_idx = top1(pg)
    e_lo = N_GROUPS + EXPERTS_PER_GROUP * g_idx.astype(jnp.int32)
    in_grp = (lane >= e_lo) & (lane < e_lo + EXPERTS_PER_GROUP)
    pe = jnp.where(in_grp, softmax_masked(in_grp), -1.0)
    pe1, i1 = top1(pe)
    pe2, i2 = top1(jnp.where(lanef == i1, -1.0, pe))
    den = pe1 + pe2
    gate1 = pg_top * pe1 / den
    gate2 = pg_top * pe2 / den
    e1 = i1 - float(N_GROUPS)
    e2 = i2 - float(N_GROUPS)

    hit1 = lanef == e1
    hit2 = lanef == e2
    onehot = jnp.where(hit1 | hit2, 1.0, 0.0)
    incl = _dot(tri_ref[...], onehot.astype(BF16))
    excl = incl - onehot + carry[0:1, :]
    r1 = jnp.sum(jnp.where(hit1, excl, 0.0), axis=1, keepdims=True)
    r2 = jnp.sum(jnp.where(hit2, excl, 0.0), axis=1, keepdims=True)
    carry[...] = carry[...] + incl[tm - 1:tm, :]
    cnt_ref[...] = carry[...]

    route = jnp.where(lane == 0, e1, jnp.where(lane == 1, e2, jnp.where(lane == 2, r1, jnp.where(lane == 3, r2, 0.0))))
    route_ref[...] = route.astype(jnp.int32)
    gate_ref[...] = jnp.where(lane == 0, gate1, jnp.where(lane == 1, gate2, 0.0))


def _route(merged, x2d, w_out, ln_g, ln_b, w_r, b_r, tri, alpha, tm):
    t, d = x2d.shape
    const = lambda shape: pl.BlockSpec(shape, lambda i: (0,) * len(shape))
    rows = lambda width: pl.BlockSpec((tm, width), lambda i: (i, 0))
    return pl.pallas_call(
        functools.partial(_route_kernel, alpha=alpha, tm=tm),
        grid=(t // tm,),
        in_specs=[rows(d), rows(d), const(w_out.shape), const((1, d)), const((1, d)),
                  const(w_r.shape), const((1, LANES)), const((tm, tm))],
        out_specs=[rows(d), rows(LANES), rows(LANES), const((SUBLANES, LANES))],
        out_shape=[jax.ShapeDtypeStruct((t, d), F32),
                   jax.ShapeDtypeStruct((t, LANES), jnp.int32),
                   jax.ShapeDtypeStruct((t, LANES), F32),
                   jax.ShapeDtypeStruct((SUBLANES, LANES), F32)],
        scratch_shapes=[pltpu.VMEM((SUBLANES, LANES), F32)],
        compiler_params=_params(1),
        name="route",
    )(merged, x2d, w_out, ln_g, ln_b, w_r, b_r, tri)


def _row_copy(src, src_row, dst, dst_row, sem, n=1):
    return pltpu.make_async_copy(src.at[pl.ds(src_row, n)], dst.at[pl.ds(dst_row, n)], sem)


def _dispatch_kernel(pad_start_ref, pad_len_ref, nu_ref, dest_ref, x_ref, xs_hbm, sem, *, tb, te):
    @pl.when(pl.program_id(0) == 0)
    def _():
        def per_expert(e, total):
            start = pad_start_ref[e]
            npad = pad_len_ref[e]

            def fill(r, c):
                _row_copy(x_ref, 0, xs_hbm, start + r, sem).start()
                return c

            lax.fori_loop(0, npad, fill, 0)
            return total + npad

        total = lax.fori_loop(0, N_EXPERTS, per_expert, 0)

        def wait_rows(n):
            def body(r, c):
                _row_copy(x_ref, 0, xs_hbm, 0, sem, n).wait()
                return c
            return body

        lax.fori_loop(0, total // SUBLANES, wait_rows(SUBLANES), 0)
        lax.fori_loop(0, total % SUBLANES, wait_rows(1), 0)

        def fill_tile(j, c):
            _row_copy(x_ref, 0, xs_hbm, pl.multiple_of(j * te, te), sem, te).start()
            return c

        n_tiles = xs_hbm.shape[0] // te
        lax.fori_loop(nu_ref[0], n_tiles, fill_tile, 0)
        lax.fori_loop(nu_ref[0], n_tiles, wait_rows(te), 0)

    def scatter(t, c):
        for k in range(TOP_K):
            _row_copy(x_ref, t, xs_hbm, dest_ref[TOP_K * t + k], sem).start()
        return c

    lax.fori_loop(0, tb, scatter, 0)
    for k in range(TOP_K):
        _row_copy(x_ref, 0, xs_hbm, 0, sem, tb).wait()


def _dispatch(x1, dest_flat, pad_start, pad_len, n_used, n_slots, tb, te):
    t, d = x1.shape
    grid_spec = pltpu.PrefetchScalarGridSpec(
        num_scalar_prefetch=3,
        grid=(t // tb,),
        in_specs=[pl.BlockSpec((TOP_K * tb,), lambda i, ps, pn, nu: (i,), memory_space=pltpu.SMEM),
                  pl.BlockSpec((tb, d), lambda i, ps, pn, nu: (i, 0))],
        out_specs=pl.BlockSpec(memory_space=pl.ANY),
        scratch_shapes=[pltpu.SemaphoreType.DMA(())],
    )
    return pl.pallas_call(
        functools.partial(_dispatch_kernel, tb=tb, te=te),
        grid_spec=grid_spec,
        out_shape=jax.ShapeDtypeStruct((n_slots, d), x1.dtype),
        compiler_params=_params(1),
        name="dispatch",
    )(pad_start, pad_len, n_used, dest_flat, x1)


def _expert_kernel(be_ref, nu_ref, x_ref, wg_ref, wu_ref, wd_ref, y_ref):
    i = pl.program_id(0)

    @pl.when(i < nu_ref[0])
    def _():
        xb = x_ref[...].astype(BF16)
        gt = _dot(xb, wg_ref[0])
        up = _dot(xb, wu_ref[0])
        hid = gt * jax.nn.sigmoid(gt) * up
        y_ref[...] = _dot(hid.astype(BF16), wd_ref[0])

    @pl.when(i >= nu_ref[0])
    def _():
        y_ref[...] = jnp.zeros_like(y_ref)


def _experts(xs, block_e, n_used, w_gate, w_up, w_down, te):
    n_slots, d = xs.shape
    de = w_gate.shape[2]
    grid_spec = pltpu.PrefetchScalarGridSpec(
        num_scalar_prefetch=2,
        grid=(n_slots // te,),
        in_specs=[pl.BlockSpec((te, d), lambda i, be, nu: (jnp.minimum(i, nu[0] - 1), 0)),
                  pl.BlockSpec((1, d, de), lambda i, be, nu: (be[i], 0, 0)),
                  pl.BlockSpec((1, d, de), lambda i, be, nu: (be[i], 0, 0)),
                  pl.BlockSpec((1, de, d), lambda i, be, nu: (be[i], 0, 0))],
        out_specs=pl.BlockSpec((te, d), lambda i, be, nu: (i, 0)),
    )
    return pl.pallas_call(
        _expert_kernel,
        grid_spec=grid_spec,
        out_shape=jax.ShapeDtypeStruct((n_slots, d), F32),
        compiler_params=_params(1),
        name="experts",
    )(block_e, n_used, xs, w_gate, w_up, w_down)


def _final_kernel(dest_ref, x1_ref, gate_ref, p_ref, ys_hbm, g_ref, b_ref, wg_ref, bg_ref, wp_ref,
                  out_ref, ybuf, sem, *, alpha, tm):
    def gather(t, c):
        for k in range(TOP_K):
            pltpu.make_async_copy(ys_hbm.at[pl.ds(dest_ref[TOP_K * t + k], 1)],
                                  ybuf.at[k, pl.ds(t, 1)], sem).start()
        return c

    lax.fori_loop(0, tm, gather, 0)
    pp = _dot(p_ref[...].astype(BF16), wp_ref[...])
    for k in range(TOP_K):
        pltpu.make_async_copy(ys_hbm.at[pl.ds(0, tm)], ybuf.at[k], sem).wait()

    gate = gate_ref[...]
    y = gate[:, 0:1] * ybuf[0] + gate[:, 1:2] * ybuf[1]
    x2 = _layer_norm(alpha * x1_ref[...] + y, g_ref[...], b_ref[...])
    gl = _dot(x2.astype(BF16), wg_ref[...]) + bg_ref[...]
    out_ref[...] = x2 + jax.nn.sigmoid(gl) * pp


def _final(dest_flat, x1, gate, p2d, ys, ln_g, ln_b, w_pg, b_pg, w_pp, alpha, tm):
    t, d = x1.shape
    const = lambda shape: pl.BlockSpec(shape, lambda i: (0,) * len(shape))
    rows = lambda width: pl.BlockSpec((tm, width), lambda i: (i, 0))
    return pl.pallas_call(
        functools.partial(_final_kernel, alpha=alpha, tm=tm),
        grid=(t // tm,),
        in_specs=[pl.BlockSpec((TOP_K * tm,), lambda i: (i,), memory_space=pltpu.SMEM),
                  rows(d), rows(LANES), rows(p2d.shape[1]),
                  pl.BlockSpec(memory_space=pl.ANY),
                  const((1, d)), const((1, d)), const(w_pg.shape), const((1, d)), const(w_pp.shape)],
        out_specs=rows(d),
        out_shape=jax.ShapeDtypeStruct((t, d), F32),
        scratch_shapes=[pltpu.VMEM((TOP_K, tm, d), F32), pltpu.SemaphoreType.DMA(())],
        compiler_params=_params(1),
        name="final",
    )(dest_flat, x1, gate, p2d, ys, ln_g, ln_b, w_pg, b_pg, w_pp)


def _pad_cols(a, width):
    return jnp.pad(a, ((0, 0), (0, width - a.shape[1])))


def _tri(n):
    return jnp.tril(jnp.ones((n, n), BF16))


def _layer(x, p, w_in, b_in, conv_w, conv_b, mh_g, w_pool, pool_scale, w_m_br, w_p_br, w_out,
           ln1_g, ln1_b, w_rg, b_rg, w_re, b_re, w_gate, w_up, w_down, ln2_g, ln2_b,
           w_ple_gate, b_ple_gate, w_ple_proj, alpha):
    nb, seq, d = x.shape
    t = nb * seq
    x2d = x.reshape(t, d)
    row = lambda a: a.reshape(1, -1)

    c_if = 2 * M_QK + 2 * M_V
    w_main = jnp.concatenate([w_in[:, :c_if], w_in[:, c_if + 2 * M_HEADS:]], axis=1).astype(BF16)
    b_main = row(jnp.concatenate([b_in[:c_if], b_in[c_if + 2 * M_HEADS:]]))
    w_if = _pad_cols(w_in[:, c_if:c_if + 2 * M_HEADS], LANES).astype(BF16)
    b_if = _pad_cols(row(b_in[c_if:c_if + 2 * M_HEADS]), LANES)

    tm_in = min(1024, t)
    z_main, z_if = _inproj(x2d, w_main, b_main, w_if, b_if, tm_in, 1024)

    hg = _mlstm(z_main.reshape(nb, seq, -1), z_if.reshape(nb, seq, LANES),
                conv_w, row(conv_b), row(mh_g), _tri(CHUNK))

    tm = min(512, seq)
    merged = _branch(hg.reshape(t, M_V), z_main, w_pool.astype(BF16), row(pool_scale),
                     w_m_br.astype(BF16), w_p_br.astype(BF16), seq, tm)

    w_r = _pad_cols(jnp.concatenate([w_rg, w_re], axis=1), LANES)
    w_r_hi = w_r.astype(BF16)
    w_r_lo = (w_r - w_r_hi.astype(F32)).astype(BF16)
    b_r = _pad_cols(row(jnp.concatenate([b_rg, b_re])), LANES)
    x1, route, gate, cnt = _route(merged, x2d, w_out.astype(BF16), row(ln1_g), row(ln1_b),
                                  jnp.concatenate([w_r_hi, w_r_lo], axis=1), b_r, _tri(tm), alpha, tm)

    te = EXPERT_TILE
    counts = cnt[0, :N_EXPERTS].astype(jnp.int32)
    pcounts = (counts + te - 1) // te * te
    pends = jnp.cumsum(pcounts)
    pstarts = pends - pcounts
    dest = (pstarts[route[:, 0:TOP_K]] + route[:, TOP_K:2 * TOP_K]).reshape(-1)
    n_slots = t * TOP_K + N_EXPERTS * te
    n_tiles = n_slots // te
    n_used = (pends[-1] // te).reshape(1)
    tile_row = jnp.minimum(jnp.arange(n_tiles, dtype=jnp.int32), n_used - 1) * te
    block_e = jnp.minimum(jnp.sum((tile_row[:, None] >= pends[None, :]).astype(jnp.int32), axis=1), N_EXPERTS - 1)

    xs = _dispatch(x1, dest, pstarts + counts, pcounts - counts, n_used, n_slots, min(512, t), te)
    ys = _experts(xs, block_e, n_used, w_gate.astype(BF16), w_up.astype(BF16), w_down.astype(BF16), te)
    return _final(dest, x1, gate, p.reshape(t, -1), ys, row(ln2_g), row(ln2_b),
                  w_ple_gate.astype(BF16), row(b_ple_gate), w_ple_proj.astype(BF16), alpha,
                  min(256, t)).reshape(nb, seq, d)


def kernel(x, p, w_in, b_in, conv_w, conv_b, mh_g, w_pool, pool_scale, w_m_br, w_p_br, w_out, ln1_g, ln1_b, w_rg, b_rg, w_re, b_re, w_gate, w_up, w_down, ln2_g, ln2_b, w_ple_gate, b_ple_gate, w_ple_proj):
    depth = w_in.shape[0]
    alpha = (2 * depth) ** 0.25
    for i in range(depth):
        x = _layer(x, p[i], w_in[i], b_in[i], conv_w[i], conv_b[i], mh_g[i], w_pool[i], pool_scale[i],
                   w_m_br[i], w_p_br[i], w_out[i], ln1_g[i], ln1_b[i], w_rg[i], b_rg[i], w_re[i], b_re[i],
                   w_gate[i], w_up[i], w_down[i], ln2_g[i], ln2_b[i], w_ple_gate[i], b_ple_gate[i],
                   w_ple_proj[i], alpha)
    return x
```

```python
import functools

import jax
import jax.numpy as jnp
from jax import lax
from jax.experimental import pallas as pl
from jax.experimental.pallas import tpu as pltpu

F32 = jnp.float32
BF16 = jnp.bfloat16

M_HEADS = 4
M_QK_DIM = 128
M_V_DIM = 256
M_QK = M_HEADS * M_QK_DIM
M_V = M_HEADS * M_V_DIM
CONV_W = 4
CHUNK = 128
POOL_WINDOWS = (2, 4, 8, 16)
POOL_GROUP_DIM = 256
POOL_W = len(POOL_WINDOWS) * POOL_GROUP_DIM
N_GROUPS = 4
EXPERTS_PER_GROUP = 8
N_EXPERTS = N_GROUPS * EXPERTS_PER_GROUP
TOP_K = 2
LN_EPS = 1e-5

LANES = 128
SUBLANES = 8
VMEM_LIMIT = 56 * 1024 * 1024
POOL_HALO = 32
EXPERT_TILE = 256


def _dot(a, b):
    return jnp.dot(a, b, preferred_element_type=F32)


def _params(n_grid):
    return pltpu.CompilerParams(dimension_semantics=("arbitrary",) * n_grid,
                                vmem_limit_bytes=VMEM_LIMIT)


def _log_sigmoid(x):
    return -(jnp.maximum(-x, 0.0) + jnp.log1p(jnp.exp(-jnp.abs(x))))


def _layer_norm(r, g, b):
    mu = jnp.mean(r, axis=-1, keepdims=True)
    d = r - mu
    var = jnp.mean(d * d, axis=-1, keepdims=True)
    return d * lax.rsqrt(var + LN_EPS) * g + b


def _store_token_major(ref, val):
    n, d = val.shape
    pitch = d // LANES
    for c in range(pitch):
        ref[pl.ds(c, n, stride=pitch), :] = val[:, c * LANES:(c + 1) * LANES]


def _load_token_major(ref, n, d):
    pitch = d // LANES
    return jnp.concatenate([ref[pl.ds(c, n, stride=pitch), :] for c in range(pitch)], axis=1)


def _token_rows(ref, tok, pitch, n=1):
    return ref.at[pl.ds(pl.multiple_of(tok * pitch, pitch), n * pitch)]


def _inproj_kernel(x_ref, w_ref, b_ref, wif_ref, bif_ref, z_ref, zif_ref, xb_ref):
    @pl.when(pl.program_id(1) == 0)
    def _():
        xb = x_ref[...].astype(BF16)
        xb_ref[...] = xb
        zif_ref[...] = _dot(xb, wif_ref[...]) + bif_ref[...]

    z_ref[...] = (_dot(xb_ref[...], w_ref[...]) + b_ref[...]).astype(BF16)


def _inproj(x2d, w_main, b_main, w_if, b_if, tm, tn):
    t, d = x2d.shape
    n = w_main.shape[1]
    return pl.pallas_call(
        _inproj_kernel,
        grid=(t // tm, n // tn),
        in_specs=[pl.BlockSpec((tm, d), lambda i, j: (i, 0)),
                  pl.BlockSpec((d, tn), lambda i, j: (0, j)),
                  pl.BlockSpec((1, tn), lambda i, j: (0, j)),
                  pl.BlockSpec((d, LANES), lambda i, j: (0, 0)),
                  pl.BlockSpec((1, LANES), lambda i, j: (0, 0))],
        out_specs=[pl.BlockSpec((tm, tn), lambda i, j: (i, j)),
                   pl.BlockSpec((tm, LANES), lambda i, j: (i, 0))],
        out_shape=[jax.ShapeDtypeStruct((t, n), BF16),
                   jax.ShapeDtypeStruct((t, LANES), F32)],
        scratch_shapes=[pltpu.VMEM((tm, d), BF16)],
        compiler_params=_params(2),
        name="inproj",
    )(x2d, w_main, b_main, w_if, b_if)


def _mlstm_kernel(qk_ref, v_ref, o_ref, g_ref, cw_ref, cb_ref, mhg_ref, tri_ref, out_ref,
                  cbuf, ct_ref, n_ref, m_ref, *, nb):
    L = CHUNK
    halo = SUBLANES

    @pl.when(pl.program_id(0) == 0)
    def _():
        cbuf[:, 0:halo, :] = jnp.zeros((nb, halo, 2 * M_QK), F32)
        ct_ref[...] = jnp.zeros_like(ct_ref)
        n_ref[...] = jnp.zeros_like(n_ref)
        m_ref[...] = jnp.zeros_like(m_ref)

    cbuf[:, halo:halo + L, :] = qk_ref[...].astype(F32)
    scale = M_QK_DIM ** -0.5
    row = lax.broadcasted_iota(jnp.int32, (L, L), 0)
    col = lax.broadcasted_iota(jnp.int32, (L, L), 1)
    causal = col <= row
    tri = tri_ref[...]

    def per_batch(b, carry):
        acc = cb_ref[...] + cbuf[b, pl.ds(halo - (CONV_W - 1), L), :] * cw_ref[0:1, :]
        for tap in range(1, CONV_W):
            acc = acc + cbuf[b, pl.ds(halo - (CONV_W - 1) + tap, L), :] * cw_ref[tap:tap + 1, :]
        qk = acc * jax.nn.sigmoid(acc)
        cbuf[b, 0:halo, :] = cbuf[b, L:L + halo, :]

        gcol = g_ref[b]
        lf = _log_sigmoid(gcol)
        lf_hi = lf.astype(BF16)
        lf_lo = (lf - lf_hi.astype(F32)).astype(BF16)
        b_col = _dot(tri, lf_hi) + _dot(tri, lf_lo)
        g_row = gcol.T
        b_row = b_col.T

        for h in range(M_HEADS):
            idx = b * M_HEADS + h
            qf = qk[:, h * M_QK_DIM:(h + 1) * M_QK_DIM]
            q = qf.astype(BF16)
            kf = qk[:, M_QK + h * M_QK_DIM:M_QK + (h + 1) * M_QK_DIM] * scale
            k = kf.astype(BF16)
            v = v_ref[b, :, h * M_V_DIM:(h + 1) * M_V_DIM]
            bj = b_col[:, M_HEADS + h:M_HEADS + h + 1]
            bl = b_row[M_HEADS + h:M_HEADS + h + 1, :]
            il = g_row[h:h + 1, :]
            ij = gcol[:, h:h + 1]
            m_prev = m_ref[idx][0:1, 0:1]

            dmat = jnp.where(causal, bj - bl + il, -jnp.inf)
            inter = bj + m_prev
            m_row = jnp.maximum(inter, jnp.max(dmat, axis=1, keepdims=True))
            pmat = jnp.exp(dmat - m_row)
            s = lax.dot_general(q, k, (((1,), (1,)), ((), ())), preferred_element_type=F32) * pmat
            scale_prev = jnp.exp(inter - m_row)
            ct = ct_ref[idx]
            nvec = n_ref[idx][0:1, :]
            num = _dot(s.astype(BF16), v) + scale_prev * _dot(q, ct.astype(BF16))
            den = (jnp.sum(s, axis=1, keepdims=True)
                   + scale_prev * jnp.sum(qf * nvec, axis=1, keepdims=True))
            hh = num / jnp.maximum(jnp.abs(den), jnp.exp(-m_row))

            b_last = bj[L - 1:L, :]
            gvec = b_last - bj + ij
            m_new = jnp.maximum(b_last + m_prev, jnp.max(gvec, axis=0, keepdims=True))
            wvec = jnp.exp(gvec - m_new)
            decay = jnp.exp(b_last + m_prev - m_new)
            kw = kf * wvec
            ct_ref[idx] = decay * ct + lax.dot_general(
                kw.astype(BF16), v, (((0,), (0,)), ((), ())), preferred_element_type=F32)
            n_new = decay * nvec + jnp.sum(kw, axis=0, keepdims=True)
            n_ref[idx] = jnp.broadcast_to(n_new, (SUBLANES, M_QK_DIM))
            m_ref[idx] = jnp.broadcast_to(m_new, (SUBLANES, LANES))

            mu = jnp.mean(hh, axis=-1, keepdims=True)
            dlt = hh - mu
            var = jnp.mean(dlt * dlt, axis=-1, keepdims=True)
            hn = dlt * lax.rsqrt(var + LN_EPS) * mhg_ref[:, h * M_V_DIM:(h + 1) * M_V_DIM]
            og = jax.nn.sigmoid(o_ref[b, :, h * M_V_DIM:(h + 1) * M_V_DIM].astype(F32))
            out_ref[b, :, h * M_V_DIM:(h + 1) * M_V_DIM] = (hn * og).astype(BF16)
        return carry

    lax.fori_loop(0, nb, per_batch, 0)


def _mlstm(z3, zif3, conv_w, conv_b, mh_g, tri):
    nb, s, _ = z3.shape
    L = CHUNK
    blk = lambda colblk: pl.BlockSpec((nb, L, M_V), lambda c: (0, c, colblk))
    const = lambda shape: pl.BlockSpec(shape, lambda c: (0,) * len(shape))
    return pl.pallas_call(
        functools.partial(_mlstm_kernel, nb=nb),
        grid=(s // L,),
        in_specs=[blk(0), blk(1), blk(2),
                  pl.BlockSpec((nb, L, LANES), lambda c: (0, c, 0)),
                  const((CONV_W, 2 * M_QK)), const((1, 2 * M_QK)), const((1, M_V)), const((L, L))],
        out_specs=pl.BlockSpec((nb, L, M_V), lambda c: (0, c, 0)),
        out_shape=jax.ShapeDtypeStruct((nb, s, M_V), BF16),
        scratch_shapes=[pltpu.VMEM((nb, L + SUBLANES, 2 * M_QK), F32),
                        pltpu.VMEM((nb * M_HEADS, M_QK_DIM, M_V_DIM), F32),
                        pltpu.VMEM((nb * M_HEADS, SUBLANES, M_QK_DIM), F32),
                        pltpu.VMEM((nb * M_HEADS, SUBLANES, LANES), F32)],
        compiler_params=_params(1),
        name="mlstm",
    )(z3, z3, z3, zif3, conv_w, conv_b, mh_g, tri)


def _branch_kernel(hg_ref, u_ref, uh_ref, gm_ref, gp_ref, wpool_ref, ps_ref, wm_ref, wp_ref,
                   out_ref, xa, xb, yp_ref, *, tm, seq):
    H = POOL_HALO
    G = POOL_GROUP_DIM
    t0 = lax.rem(pl.program_id(0) * tm, seq)
    u = u_ref[...].astype(F32)
    xa[H:H + tm, :] = u
    xa[0:H, :] = jnp.where(t0 == 0, 0.0, uh_ref[...].astype(F32))
    n = tm + H - 8
    xb[8:8 + n, :] = xa[8:8 + n, :] + xa[7:7 + n, :]
    n = tm + H - 16
    xa[16:16 + n, G:] = xb[16:16 + n, G:] + xb[14:14 + n, G:]
    n = tm + H - 24
    xb[24:24 + n, 2 * G:] = xa[24:24 + n, 2 * G:] + xa[20:20 + n, 2 * G:]
    xa[H:H + tm, 3 * G:] = xb[H:H + tm, 3 * G:] + xb[H - 8:H - 8 + tm, 3 * G:]
    tpos = t0 + lax.broadcasted_iota(jnp.int32, (tm, 1), 0)
    for g, win in enumerate(POOL_WINDOWS):
        src = (xb, xa, xb, xa)[g]
        cols = slice(g * G, (g + 1) * G)
        cnt = jnp.minimum(tpos + 1, win).astype(F32)
        y = src[H:H + tm, cols] / cnt - u[:, cols]
        yp = _dot(y.astype(BF16), wpool_ref[g]) * ps_ref[:, cols]
        yp_ref[:, cols] = yp.astype(BF16)
    pb = _dot(yp_ref[...], wp_ref[...])
    a = _dot(hg_ref[...], wm_ref[...])
    merged = (jax.nn.sigmoid(gm_ref[...].astype(F32)) * a
              + jax.nn.sigmoid(gp_ref[...].astype(F32)) * pb)
    out_ref[...] = merged.astype(BF16)


def _branch(hg2d, z_main, w_pool, pool_scale, w_m_br, w_p_br, seq, tm):
    t = hg2d.shape[0]
    d = w_m_br.shape[1]
    hb = tm // POOL_HALO
    const = lambda shape: pl.BlockSpec(shape, lambda i: (0,) * len(shape))
    return pl.pallas_call(
        functools.partial(_branch_kernel, tm=tm, seq=seq),
        grid=(t // tm,),
        in_specs=[pl.BlockSpec((tm, M_V), lambda i: (i, 0)),
                  pl.BlockSpec((tm, POOL_W), lambda i: (i, 3)),
                  pl.BlockSpec((POOL_HALO, POOL_W), lambda i: (jnp.maximum(i * hb - 1, 0), 3)),
                  pl.BlockSpec((tm, d), lambda i: (i, 2)),
                  pl.BlockSpec((tm, d), lambda i: (i, 3)),
                  const(w_pool.shape), const((1, POOL_W)), const(w_m_br.shape), const(w_p_br.shape)],
        out_specs=pl.BlockSpec((tm, d), lambda i: (i, 0)),
        out_shape=jax.ShapeDtypeStruct((t, d), BF16),
        scratch_shapes=[pltpu.VMEM((tm + POOL_HALO, POOL_W), F32),
                        pltpu.VMEM((tm + POOL_HALO, POOL_W), F32),
                        pltpu.VMEM((tm, POOL_W), BF16)],
        compiler_params=_params(1),
        name="branch",
    )(hg2d, z_main, z_main, z_main, z_main, w_pool, pool_scale, w_m_br, w_p_br)


def _route_kernel(mg_ref, x_ref, wout_ref, g_ref, b_ref, wr_ref, br_ref, tri_ref,
                  x1_ref, route_ref, gate_ref, cnt_ref, carry, *, alpha, tm):
    @pl.when(pl.program_id(0) == 0)
    def _():
        carry[...] = jnp.zeros_like(carry)

    r = alpha * x_ref[...] + _dot(mg_ref[...], wout_ref[...])
    x1 = _layer_norm(r, g_ref[...], b_ref[...])
    _store_token_major(x1_ref, x1)

    hi = x1.astype(BF16)
    lo = (x1 - hi.astype(F32)).astype(BF16)
    p1 = _dot(hi, wr_ref[...])
    p2 = _dot(lo, wr_ref[:, 0:LANES])
    logits = p1[:, 0:LANES] + p1[:, LANES:2 * LANES] + p2 + br_ref[...]

    lane = lax.broadcasted_iota(jnp.int32, (tm, LANES), 1)
    lanef = lane.astype(F32)
    big = float(LANES)

    def softmax_masked(mask):
        z = jnp.where(mask, logits, -jnp.inf)
        e = jnp.exp(z - jnp.max(z, axis=1, keepdims=True))
        return e / jnp.sum(e, axis=1, keepdims=True)

    def top1(vals):
        top = jnp.max(vals, axis=1, keepdims=True)
        idx = jnp.min(jnp.where(vals == top, lanef, big), axis=1, keepdims=True)
        return top, idx

    is_grp = lane < N_GROUPS
    pg = jnp.where(is_grp, softmax_masked(is_grp), -1.0)
    pg_top, g_idx = top1(pg)
    e_lo = N_GROUPS + EXPERTS_PER_GROUP * g_idx.astype(jnp.int32)
    in_grp = (lane >= e_lo) & (lane < e_lo + EXPERTS_PER_GROUP)
    pe = jnp.where(in_grp, softmax_masked(in_grp), -1.0)
    pe1, i1 = top1(pe)
    pe2, i2 = top1(jnp.where(lanef == i1, -1.0, pe))
    den = pe1 + pe2
    gate1 = pg_top * pe1 / den
    gate2 = pg_top * pe2 / den
    e1 = i1 - float(N_GROUPS)
    e2 = i2 - float(N_GROUPS)

    hit1 = lanef == e1
    hit2 = lanef == e2
    onehot = jnp.where(hit1 | hit2, 1.0, 0.0)
    incl = _dot(tri_ref[...], onehot.astype(BF16))
    excl = incl - onehot + carry[0:1, :]
    r1 = jnp.sum(jnp.where(hit1, excl, 0.0), axis=1, keepdims=True)
    r2 = jnp.sum(jnp.where(hit2, excl, 0.0), axis=1, keepdims=True)
    carry[...] = carry[...] + incl[tm - 1:tm, :]
    cnt_ref[...] = carry[...]

    route = jnp.where(lane == 0, e1, jnp.where(lane == 1, e2, jnp.where(lane == 2, r1, jnp.where(lane == 3, r2, 0.0))))
    route_ref[...] = route.astype(jnp.int32)
    gate_ref[...] = jnp.where(lane == 0, gate1, jnp.where(lane == 1, gate2, 0.0))


def _route(merged, x2d, w_out, ln_g, ln_b, w_r, b_r, tri, alpha, tm):
    t, d = x2d.shape
    const = lambda shape: pl.BlockSpec(shape, lambda i: (0,) * len(shape))
    rows = lambda width: pl.BlockSpec((tm, width), lambda i: (i, 0))
    return pl.pallas_call(
        functools.partial(_route_kernel, alpha=alpha, tm=tm),
        grid=(t // tm,),
        in_specs=[rows(d), rows(d), const(w_out.shape), const((1, d)), const((1, d)),
                  const(w_r.shape), const((1, LANES)), const((tm, tm))],
        out_specs=[pl.BlockSpec((tm * (d // LANES), LANES), lambda i: (i, 0)),
                   rows(LANES), rows(LANES), const((SUBLANES, LANES))],
        out_shape=[jax.ShapeDtypeStruct((t * (d // LANES), LANES), F32),
                   jax.ShapeDtypeStruct((t, LANES), jnp.int32),
                   jax.ShapeDtypeStruct((t, LANES), F32),
                   jax.ShapeDtypeStruct((SUBLANES, LANES), F32)],
        scratch_shapes=[pltpu.VMEM((SUBLANES, LANES), F32)],
        compiler_params=_params(1),
        name="route",
    )(merged, x2d, w_out, ln_g, ln_b, w_r, b_r, tri)


def _dispatch_kernel(pad_start_ref, pad_len_ref, nu_ref, dest_ref, x_ref, xs_hbm, sem, *, tb, te, pitch):
    def copy(src_tok, dst_tok, n=1):
        return pltpu.make_async_copy(_token_rows(x_ref, src_tok, pitch, n),
                                     _token_rows(xs_hbm, dst_tok, pitch, n), sem)

    def wait_tokens(n):
        def body(r, c):
            copy(0, 0, n).wait()
            return c
        return body

    @pl.when(pl.program_id(0) == 0)
    def _():
        def per_expert(e, total):
            start = pad_start_ref[e]
            npad = pad_len_ref[e]

            def fill(r, c):
                copy(0, start + r).start()
                return c

            lax.fori_loop(0, npad, fill, 0)
            return total + npad

        total = lax.fori_loop(0, N_EXPERTS, per_expert, 0)
        lax.fori_loop(0, total // SUBLANES, wait_tokens(SUBLANES), 0)
        lax.fori_loop(0, total % SUBLANES, wait_tokens(1), 0)

        def fill_tile(j, c):
            copy(0, j * te, te).start()
            return c

        n_tiles = xs_hbm.shape[0] // (te * pitch)
        lax.fori_loop(nu_ref[0], n_tiles, fill_tile, 0)
        lax.fori_loop(nu_ref[0], n_tiles, wait_tokens(te), 0)

    def scatter(t, c):
        for k in range(TOP_K):
            copy(t, dest_ref[TOP_K * t + k]).start()
        return c

    lax.fori_loop(0, tb, scatter, 0)
    for k in range(TOP_K):
        copy(0, 0, tb).wait()


def _dispatch(x1t, dest_flat, pad_start, pad_len, n_used, n_slots, tb, te, pitch):
    t = x1t.shape[0] // pitch
    grid_spec = pltpu.PrefetchScalarGridSpec(
        num_scalar_prefetch=3,
        grid=(t // tb,),
        in_specs=[pl.BlockSpec((TOP_K * tb,), lambda i, ps, pn, nu: (i,), memory_space=pltpu.SMEM),
                  pl.BlockSpec((tb * pitch, LANES), lambda i, ps, pn, nu: (i, 0))],
        out_specs=pl.BlockSpec(memory_space=pl.ANY),
        scratch_shapes=[pltpu.SemaphoreType.DMA(())],
    )
    return pl.pallas_call(
        functools.partial(_dispatch_kernel, tb=tb, te=te, pitch=pitch),
        grid_spec=grid_spec,
        out_shape=jax.ShapeDtypeStruct((n_slots * pitch, LANES), x1t.dtype),
        compiler_params=_params(1),
        name="dispatch",
    )(pad_start, pad_len, n_used, dest_flat, x1t)


def _expert_kernel(be_ref, nu_ref, x_ref, wg_ref, wu_ref, wd_ref, y_ref, *, te, d):
    i = pl.program_id(0)

    @pl.when(i < nu_ref[0])
    def _():
        xb = _load_token_major(x_ref, te, d).astype(BF16)
        gt = _dot(xb, wg_ref[0])
        up = _dot(xb, wu_ref[0])
        hid = gt * jax.nn.sigmoid(gt) * up
        _store_token_major(y_ref, _dot(hid.astype(BF16), wd_ref[0]))

    @pl.when(i >= nu_ref[0])
    def _():
        y_ref[...] = jnp.zeros_like(y_ref)


def _experts(xs, block_e, n_used, w_gate, w_up, w_down, te):
    _, d, de = w_gate.shape
    pitch = d // LANES
    n_tiles = xs.shape[0] // (te * pitch)
    grid_spec = pltpu.PrefetchScalarGridSpec(
        num_scalar_prefetch=2,
        grid=(n_tiles,),
        in_specs=[pl.BlockSpec((te * pitch, LANES), lambda i, be, nu: (jnp.minimum(i, nu[0] - 1), 0)),
                  pl.BlockSpec((1, d, de), lambda i, be, nu: (be[i], 0, 0)),
                  pl.BlockSpec((1, d, de), lambda i, be, nu: (be[i], 0, 0)),
                  pl.BlockSpec((1, de, d), lambda i, be, nu: (be[i], 0, 0))],
        out_specs=pl.BlockSpec((te * pitch, LANES), lambda i, be, nu: (i, 0)),
    )
    return pl.pallas_call(
        functools.partial(_expert_kernel, te=te, d=d),
        grid_spec=grid_spec,
        out_shape=jax.ShapeDtypeStruct(xs.shape, F32),
        compiler_params=_params(1),
        name="experts",
    )(block_e, n_used, xs, w_gate, w_up, w_down)


def _final_kernel(dcur_ref, dnxt_ref, x1_ref, gate_ref, p_ref, ys_hbm, g_ref, b_ref, wg_ref, bg_ref, wp_ref,
                  out_ref, ybuf, sem, *, alpha, tm, d):
    i = pl.program_id(0)
    pitch = d // LANES
    slot = lax.rem(i, 2)

    def issue(dest_ref, s):
        def body(t, c):
            for k in range(TOP_K):
                pltpu.make_async_copy(_token_rows(ys_hbm, dest_ref[TOP_K * t + k], pitch),
                                      _token_rows(ybuf.at[s, k], t, pitch), sem.at[s]).start()
            return c
        lax.fori_loop(0, tm, body, 0)

    @pl.when(i == 0)
    def _():
        issue(dcur_ref, 0)

    @pl.when(i + 1 < pl.num_programs(0))
    def _():
        issue(dnxt_ref, 1 - slot)

    pp = _dot(p_ref[...].astype(BF16), wp_ref[...])
    x1 = _load_token_major(x1_ref, tm, d)
    for k in range(TOP_K):
        pltpu.make_async_copy(_token_rows(ys_hbm, 0, pitch, tm), ybuf.at[slot, k], sem.at[slot]).wait()

    gate = gate_ref[...]
    y = (gate[:, 0:1] * _load_token_major(ybuf.at[slot, 0], tm, d)
         + gate[:, 1:2] * _load_token_major(ybuf.at[slot, 1], tm, d))
    x2 = _layer_norm(alpha * x1 + y, g_ref[...], b_ref[...])
    gl = _dot(x2.astype(BF16), wg_ref[...]) + bg_ref[...]
    out_ref[...] = x2 + jax.nn.sigmoid(gl) * pp


def _final(dest_flat, x1t, gate, p2d, ys, ln_g, ln_b, w_pg, b_pg, w_pp, alpha, tm):
    d = w_pg.shape[0]
    pitch = d // LANES
    t = x1t.shape[0] // pitch
    n = t // tm
    const = lambda shape: pl.BlockSpec(shape, lambda i: (0,) * len(shape))
    rows = lambda width: pl.BlockSpec((tm, width), lambda i: (i, 0))
    return pl.pallas_call(
        functools.partial(_final_kernel, alpha=alpha, tm=tm, d=d),
        grid=(n,),
        in_specs=[pl.BlockSpec((TOP_K * tm,), lambda i: (i,), memory_space=pltpu.SMEM),
                  pl.BlockSpec((TOP_K * tm,), lambda i: (jnp.minimum(i + 1, n - 1),), memory_space=pltpu.SMEM),
                  pl.BlockSpec((tm * pitch, LANES), lambda i: (i, 0)),
                  rows(LANES), rows(p2d.shape[1]),
                  pl.BlockSpec(memory_space=pl.ANY),
                  const((1, d)), const((1, d)), const(w_pg.shape), const((1, d)), const(w_pp.shape)],
        out_specs=rows(d),
        out_shape=jax.ShapeDtypeStruct((t, d), F32),
        scratch_shapes=[pltpu.VMEM((2, TOP_K, tm * pitch, LANES), F32), pltpu.SemaphoreType.DMA((2,))],
        compiler_params=_params(1),
        name="final",
    )(dest_flat, dest_flat, x1t, gate, p2d, ys, ln_g, ln_b, w_pg, b_pg, w_pp)


def _pad_cols(a, width):
    return jnp.pad(a, ((0, 0), (0, width - a.shape[1])))


def _tri(n):
    return jnp.tril(jnp.ones((n, n), BF16))


def _layer(x, p, w_in, b_in, conv_w, conv_b, mh_g, w_pool, pool_scale, w_m_br, w_p_br, w_out,
           ln1_g, ln1_b, w_rg, b_rg, w_re, b_re, w_gate, w_up, w_down, ln2_g, ln2_b,
           w_ple_gate, b_ple_gate, w_ple_proj, alpha):
    nb, seq, d = x.shape
    t = nb * seq
    x2d = x.reshape(t, d)
    row = lambda a: a.reshape(1, -1)

    c_if = 2 * M_QK + 2 * M_V
    w_main = jnp.concatenate([w_in[:, :c_if], w_in[:, c_if + 2 * M_HEADS:]], axis=1).astype(BF16)
    b_main = row(jnp.concatenate([b_in[:c_if], b_in[c_if + 2 * M_HEADS:]]))
    w_if = _pad_cols(w_in[:, c_if:c_if + 2 * M_HEADS], LANES).astype(BF16)
    b_if = _pad_cols(row(b_in[c_if:c_if + 2 * M_HEADS]), LANES)

    tm_in = min(1024, t)
    z_main, z_if = _inproj(x2d, w_main, b_main, w_if, b_if, tm_in, 1024)

    hg = _mlstm(z_main.reshape(nb, seq, -1), z_if.reshape(nb, seq, LANES),
                conv_w, row(conv_b), row(mh_g), _tri(CHUNK))

    tm = min(512, seq)
    merged = _branch(hg.reshape(t, M_V), z_main, w_pool.astype(BF16), row(pool_scale),
                     w_m_br.astype(BF16), w_p_br.astype(BF16), seq, tm)

    w_r = _pad_cols(jnp.concatenate([w_rg, w_re], axis=1), LANES)
    w_r_hi = w_r.astype(BF16)
    w_r_lo = (w_r - w_r_hi.astype(F32)).astype(BF16)
    b_r = _pad_cols(row(jnp.concatenate([b_rg, b_re])), LANES)
    x1, route, gate, cnt = _route(merged, x2d, w_out.astype(BF16), row(ln1_g), row(ln1_b),
                                  jnp.concatenate([w_r_hi, w_r_lo], axis=1), b_r, _tri(tm), alpha, tm)

    te = EXPERT_TILE
    counts = cnt[0, :N_EXPERTS].astype(jnp.int32)
    pcounts = (counts + te - 1) // te * te
    pends = jnp.cumsum(pcounts)
    pstarts = pends - pcounts
    e_sel = route[:, 0:TOP_K, None] == jnp.arange(N_EXPERTS, dtype=jnp.int32)
    dest = (jnp.sum(jnp.where(e_sel, pstarts, 0), axis=-1) + route[:, TOP_K:2 * TOP_K]).reshape(-1)
    n_slots = t * TOP_K + N_EXPERTS * te
    n_tiles = n_slots // te
    n_used = (pends[-1] // te).reshape(1)
    tile_row = jnp.minimum(jnp.arange(n_tiles, dtype=jnp.int32), n_used - 1) * te
    block_e = jnp.minimum(jnp.sum((tile_row[:, None] >= pends[None, :]).astype(jnp.int32), axis=1), N_EXPERTS - 1)

    xs = _dispatch(x1, dest, pstarts + counts, pcounts - counts, n_used, n_slots, min(512, t), te, d // LANES)
    ys = _experts(xs, block_e, n_used, w_gate.astype(BF16), w_up.astype(BF16), w_down.astype(BF16), te)
    return _final(dest, x1, gate, p.reshape(t, -1), ys, row(ln2_g), row(ln2_b),
                  w_ple_gate.astype(BF16), row(b_ple_gate), w_ple_proj.astype(BF16), alpha,
                  min(256, t)).reshape(nb, seq, d)


def kernel(x, p, w_in, b_in, conv_w, conv_b, mh_g, w_pool, pool_scale, w_m_br, w_p_br, w_out, ln1_g, ln1_b, w_rg, b_rg, w_re, b_re, w_gate, w_up, w_down, ln2_g, ln2_b, w_ple_gate, b_ple_gate, w_ple_proj):
    depth = w_in.shape[0]
    alpha = (2 * depth) ** 0.25
    for i in range(depth):
        x = _layer(x, p[i], w_in[i], b_in[i], conv_w[i], conv_b[i], mh_g[i], w_pool[i], pool_scale[i],
                   w_m_br[i], w_p_br[i], w_out[i], ln1_g[i], ln1_b[i], w_rg[i], b_rg[i], w_re[i], b_re[i],
                   w_gate[i], w_up[i], w_down[i], ln2_g[i], ln2_b[i], w_ple_gate[i], b_ple_gate[i],
                   w_ple_proj[i], alpha)
    return x
```

```python
import functools

import jax
import jax.numpy as jnp
from jax import lax
from jax.experimental import pallas as pl
from jax.experimental.pallas import tpu as pltpu

F32 = jnp.float32
BF16 = jnp.bfloat16

M_HEADS = 4
M_QK_DIM = 128
M_V_DIM = 256
M_QK = M_HEADS * M_QK_DIM
M_V = M_HEADS * M_V_DIM
CONV_W = 4
CHUNK = 128
POOL_WINDOWS = (2, 4, 8, 16)
POOL_GROUP_DIM = 256
POOL_W = len(POOL_WINDOWS) * POOL_GROUP_DIM
N_GROUPS = 4
EXPERTS_PER_GROUP = 8
N_EXPERTS = N_GROUPS * EXPERTS_PER_GROUP
TOP_K = 2
LN_EPS = 1e-5

LANES = 128
SUBLANES = 8
VMEM_LIMIT = 56 * 1024 * 1024
POOL_HALO = 32
EXPERT_TILE = 256


def _dot(a, b):
    return jnp.dot(a, b, preferred_element_type=F32)


def _params(n_grid):
    return pltpu.CompilerParams(dimension_semantics=("arbitrary",) * n_grid,
                                vmem_limit_bytes=VMEM_LIMIT)


def _log_sigmoid(x):
    return -(jnp.maximum(-x, 0.0) + jnp.log1p(jnp.exp(-jnp.abs(x))))


def _layer_norm(r, g, b):
    mu = jnp.mean(r, axis=-1, keepdims=True)
    d = r - mu
    var = jnp.mean(d * d, axis=-1, keepdims=True)
    return d * lax.rsqrt(var + LN_EPS) * g + b


def _pitch(d):
    return d // LANES + 1


def _store_token_major(ref, val):
    n, d = val.shape
    for c in range(d // LANES):
        ref[pl.ds(c, n, stride=_pitch(d)), :] = val[:, c * LANES:(c + 1) * LANES]
    ref[pl.ds(d // LANES, n, stride=_pitch(d)), :] = jnp.zeros((n, LANES), val.dtype)


def _load_token_major(ref, n, d):
    return jnp.concatenate([ref[pl.ds(c, n, stride=_pitch(d)), :] for c in range(d // LANES)], axis=1)


def _tokens(ref, tok, d, n=1):
    return ref.at[pl.ds(tok * _pitch(d), n * _pitch(d))]


def _inproj_kernel(x_ref, w_ref, b_ref, wif_ref, bif_ref, z_ref, zif_ref, xb_ref):
    @pl.when(pl.program_id(1) == 0)
    def _():
        xb = x_ref[...].astype(BF16)
        xb_ref[...] = xb
        zif_ref[...] = _dot(xb, wif_ref[...]) + bif_ref[...]

    z_ref[...] = (_dot(xb_ref[...], w_ref[...]) + b_ref[...]).astype(BF16)


def _inproj(x2d, w_main, b_main, w_if, b_if, tm, tn):
    t, d = x2d.shape
    n = w_main.shape[1]
    return pl.pallas_call(
        _inproj_kernel,
        grid=(t // tm, n // tn),
        in_specs=[pl.BlockSpec((tm, d), lambda i, j: (i, 0)),
                  pl.BlockSpec((d, tn), lambda i, j: (0, j)),
                  pl.BlockSpec((1, tn), lambda i, j: (0, j)),
                  pl.BlockSpec((d, LANES), lambda i, j: (0, 0)),
                  pl.BlockSpec((1, LANES), lambda i, j: (0, 0))],
        out_specs=[pl.BlockSpec((tm, tn), lambda i, j: (i, j)),
                   pl.BlockSpec((tm, LANES), lambda i, j: (i, 0))],
        out_shape=[jax.ShapeDtypeStruct((t, n), BF16),
                   jax.ShapeDtypeStruct((t, LANES), F32)],
        scratch_shapes=[pltpu.VMEM((tm, d), BF16)],
        compiler_params=_params(2),
        name="inproj",
    )(x2d, w_main, b_main, w_if, b_if)


def _mlstm_kernel(qk_ref, v_ref, o_ref, g_ref, cw_ref, cb_ref, mhg_ref, tri_ref, out_ref,
                  cbuf, ct_ref, n_ref, m_ref, *, nb):
    L = CHUNK
    halo = SUBLANES

    @pl.when(pl.program_id(0) == 0)
    def _():
        cbuf[:, 0:halo, :] = jnp.zeros((nb, halo, 2 * M_QK), F32)
        ct_ref[...] = jnp.zeros_like(ct_ref)
        n_ref[...] = jnp.zeros_like(n_ref)
        m_ref[...] = jnp.zeros_like(m_ref)

    cbuf[:, halo:halo + L, :] = qk_ref[...].astype(F32)
    scale = M_QK_DIM ** -0.5
    row = lax.broadcasted_iota(jnp.int32, (L, L), 0)
    col = lax.broadcasted_iota(jnp.int32, (L, L), 1)
    causal = col <= row
    tri = tri_ref[...]

    def per_batch(b, carry):
        acc = cb_ref[...] + cbuf[b, pl.ds(halo - (CONV_W - 1), L), :] * cw_ref[0:1, :]
        for tap in range(1, CONV_W):
            acc = acc + cbuf[b, pl.ds(halo - (CONV_W - 1) + tap, L), :] * cw_ref[tap:tap + 1, :]
        qk = acc * jax.nn.sigmoid(acc)
        cbuf[b, 0:halo, :] = cbuf[b, L:L + halo, :]

        gcol = g_ref[b]
        lf = _log_sigmoid(gcol)
        lf_hi = lf.astype(BF16)
        lf_lo = (lf - lf_hi.astype(F32)).astype(BF16)
        b_col = _dot(tri, lf_hi) + _dot(tri, lf_lo)
        g_row = gcol.T
        b_row = b_col.T

        for h in range(M_HEADS):
            idx = b * M_HEADS + h
            qf = qk[:, h * M_QK_DIM:(h + 1) * M_QK_DIM]
            q = qf.astype(BF16)
            kf = qk[:, M_QK + h * M_QK_DIM:M_QK + (h + 1) * M_QK_DIM] * scale
            k = kf.astype(BF16)
            v = v_ref[b, :, h * M_V_DIM:(h + 1) * M_V_DIM]
            bj = b_col[:, M_HEADS + h:M_HEADS + h + 1]
            bl = b_row[M_HEADS + h:M_HEADS + h + 1, :]
            il = g_row[h:h + 1, :]
            ij = gcol[:, h:h + 1]
            m_prev = m_ref[idx][0:1, 0:1]

            dmat = jnp.where(causal, bj - bl + il, -jnp.inf)
            inter = bj + m_prev
            m_row = jnp.maximum(inter, jnp.max(dmat, axis=1, keepdims=True))
            pmat = jnp.exp(dmat - m_row)
            s = lax.dot_general(q, k, (((1,), (1,)), ((), ())), preferred_element_type=F32) * pmat
            scale_prev = jnp.exp(inter - m_row)
            ct = ct_ref[idx]
            nvec = n_ref[idx][0:1, :]
            num = _dot(s.astype(BF16), v) + scale_prev * _dot(q, ct.astype(BF16))
            den = (jnp.sum(s, axis=1, keepdims=True)
                   + scale_prev * jnp.sum(qf * nvec, axis=1, keepdims=True))
            hh = num / jnp.maximum(jnp.abs(den), jnp.exp(-m_row))

            b_last = bj[L - 1:L, :]
            gvec = b_last - bj + ij
            m_new = jnp.maximum(b_last + m_prev, jnp.max(gvec, axis=0, keepdims=True))
            wvec = jnp.exp(gvec - m_new)
            decay = jnp.exp(b_last + m_prev - m_new)
            kw = kf * wvec
            ct_ref[idx] = decay * ct + lax.dot_general(
                kw.astype(BF16), v, (((0,), (0,)), ((), ())), preferred_element_type=F32)
            n_new = decay * nvec + jnp.sum(kw, axis=0, keepdims=True)
            n_ref[idx] = jnp.broadcast_to(n_new, (SUBLANES, M_QK_DIM))
            m_ref[idx] = jnp.broadcast_to(m_new, (SUBLANES, LANES))

            mu = jnp.mean(hh, axis=-1, keepdims=True)
            dlt = hh - mu
            var = jnp.mean(dlt * dlt, axis=-1, keepdims=True)
            hn = dlt * lax.rsqrt(var + LN_EPS) * mhg_ref[:, h * M_V_DIM:(h + 1) * M_V_DIM]
            og = jax.nn.sigmoid(o_ref[b, :, h * M_V_DIM:(h + 1) * M_V_DIM].astype(F32))
            out_ref[b, :, h * M_V_DIM:(h + 1) * M_V_DIM] = (hn * og).astype(BF16)
        return carry

    lax.fori_loop(0, nb, per_batch, 0)


def _mlstm(z3, zif3, conv_w, conv_b, mh_g, tri):
    nb, s, _ = z3.shape
    L = CHUNK
    blk = lambda colblk: pl.BlockSpec((nb, L, M_V), lambda c: (0, c, colblk))
    const = lambda shape: pl.BlockSpec(shape, lambda c: (0,) * len(shape))
    return pl.pallas_call(
        functools.partial(_mlstm_kernel, nb=nb),
        grid=(s // L,),
        in_specs=[blk(0), blk(1), blk(2),
                  pl.BlockSpec((nb, L, LANES), lambda c: (0, c, 0)),
                  const((CONV_W, 2 * M_QK)), const((1, 2 * M_QK)), const((1, M_V)), const((L, L))],
        out_specs=pl.BlockSpec((nb, L, M_V), lambda c: (0, c, 0)),
        out_shape=jax.ShapeDtypeStruct((nb, s, M_V), BF16),
        scratch_shapes=[pltpu.VMEM((nb, L + SUBLANES, 2 * M_QK), F32),
                        pltpu.VMEM((nb * M_HEADS, M_QK_DIM, M_V_DIM), F32),
                        pltpu.VMEM((nb * M_HEADS, SUBLANES, M_QK_DIM), F32),
                        pltpu.VMEM((nb * M_HEADS, SUBLANES, LANES), F32)],
        compiler_params=_params(1),
        name="mlstm",
    )(z3, z3, z3, zif3, conv_w, conv_b, mh_g, tri)


def _branch_kernel(hg_ref, u_ref, uh_ref, gm_ref, gp_ref, wpool_ref, ps_ref, wm_ref, wp_ref,
                   out_ref, xa, xb, yp_ref, *, tm, seq):
    H = POOL_HALO
    G = POOL_GROUP_DIM
    t0 = lax.rem(pl.program_id(0) * tm, seq)
    u = u_ref[...].astype(F32)
    xa[H:H + tm, :] = u
    xa[0:H, :] = jnp.where(t0 == 0, 0.0, uh_ref[...].astype(F32))
    n = tm + H - 8
    xb[8:8 + n, :] = xa[8:8 + n, :] + xa[7:7 + n, :]
    n = tm + H - 16
    xa[16:16 + n, G:] = xb[16:16 + n, G:] + xb[14:14 + n, G:]
    n = tm + H - 24
    xb[24:24 + n, 2 * G:] = xa[24:24 + n, 2 * G:] + xa[20:20 + n, 2 * G:]
    xa[H:H + tm, 3 * G:] = xb[H:H + tm, 3 * G:] + xb[H - 8:H - 8 + tm, 3 * G:]
    tpos = t0 + lax.broadcasted_iota(jnp.int32, (tm, 1), 0)
    for g, win in enumerate(POOL_WINDOWS):
        src = (xb, xa, xb, xa)[g]
        cols = slice(g * G, (g + 1) * G)
        cnt = jnp.minimum(tpos + 1, win).astype(F32)
        y = src[H:H + tm, cols] / cnt - u[:, cols]
        yp = _dot(y.astype(BF16), wpool_ref[g]) * ps_ref[:, cols]
        yp_ref[:, cols] = yp.astype(BF16)
    pb = _dot(yp_ref[...], wp_ref[...])
    a = _dot(hg_ref[...], wm_ref[...])
    merged = (jax.nn.sigmoid(gm_ref[...].astype(F32)) * a
              + jax.nn.sigmoid(gp_ref[...].astype(F32)) * pb)
    out_ref[...] = merged.astype(BF16)


def _branch(hg2d, z_main, w_pool, pool_scale, w_m_br, w_p_br, seq, tm):
    t = hg2d.shape[0]
    d = w_m_br.shape[1]
    hb = tm // POOL_HALO
    const = lambda shape: pl.BlockSpec(shape, lambda i: (0,) * len(shape))
    return pl.pallas_call(
        functools.partial(_branch_kernel, tm=tm, seq=seq),
        grid=(t // tm,),
        in_specs=[pl.BlockSpec((tm, M_V), lambda i: (i, 0)),
                  pl.BlockSpec((tm, POOL_W), lambda i: (i, 3)),
                  pl.BlockSpec((POOL_HALO, POOL_W), lambda i: (jnp.maximum(i * hb - 1, 0), 3)),
                  pl.BlockSpec((tm, d), lambda i: (i, 2)),
                  pl.BlockSpec((tm, d), lambda i: (i, 3)),
                  const(w_pool.shape), const((1, POOL_W)), const(w_m_br.shape), const(w_p_br.shape)],
        out_specs=pl.BlockSpec((tm, d), lambda i: (i, 0)),
        out_shape=jax.ShapeDtypeStruct((t, d), BF16),
        scratch_shapes=[pltpu.VMEM((tm + POOL_HALO, POOL_W), F32),
                        pltpu.VMEM((tm + POOL_HALO, POOL_W), F32),
                        pltpu.VMEM((tm, POOL_W), BF16)],
        compiler_params=_params(1),
        name="branch",
    )(hg2d, z_main, z_main, z_main, z_main, w_pool, pool_scale, w_m_br, w_p_br)


def _route_kernel(mg_ref, x_ref, wout_ref, g_ref, b_ref, wr_ref, br_ref, tri_ref,
                  x1_ref, route_ref, gate_ref, cnt_ref, carry, *, alpha, tm):
    @pl.when(pl.program_id(0) == 0)
    def _():
        carry[...] = jnp.zeros_like(carry)

    r = alpha * x_ref[...] + _dot(mg_ref[...], wout_ref[...])
    x1 = _layer_norm(r, g_ref[...], b_ref[...])
    _store_token_major(x1_ref, x1)

    hi = x1.astype(BF16)
    lo = (x1 - hi.astype(F32)).astype(BF16)
    p1 = _dot(hi, wr_ref[...])
    p2 = _dot(lo, wr_ref[:, 0:LANES])
    logits = p1[:, 0:LANES] + p1[:, LANES:2 * LANES] + p2 + br_ref[...]

    lane = lax.broadcasted_iota(jnp.int32, (tm, LANES), 1)
    lanef = lane.astype(F32)
    big = float(LANES)

    def softmax_masked(mask):
        z = jnp.where(mask, logits, -jnp.inf)
        e = jnp.exp(z - jnp.max(z, axis=1, keepdims=True))
        return e / jnp.sum(e, axis=1, keepdims=True)

    def top1(vals):
        top = jnp.max(vals, axis=1, keepdims=True)
        idx = jnp.min(jnp.where(vals == top, lanef, big), axis=1, keepdims=True)
        return top, idx

    is_grp = lane < N_GROUPS
    pg = jnp.where(is_grp, softmax_masked(is_grp), -1.0)
    pg_top, g_idx = top1(pg)
    e_lo = N_GROUPS + EXPERTS_PER_GROUP * g_idx.astype(jnp.int32)
    in_grp = (lane >= e_lo) & (lane < e_lo + EXPERTS_PER_GROUP)
    pe = jnp.where(in_grp, softmax_masked(in_grp), -1.0)
    pe1, i1 = top1(pe)
    pe2, i2 = top1(jnp.where(lanef == i1, -1.0, pe))
    den = pe1 + pe2
    gate1 = pg_top * pe1 / den
    gate2 = pg_top * pe2 / den
    e1 = i1 - float(N_GROUPS)
    e2 = i2 - float(N_GROUPS)

    hit1 = lanef == e1
    hit2 = lanef == e2
    onehot = jnp.where(hit1 | hit2, 1.0, 0.0)
    incl = _dot(tri_ref[...], onehot.astype(BF16))
    excl = incl - onehot + carry[0:1, :]
    r1 = jnp.sum(jnp.where(hit1, excl, 0.0), axis=1, keepdims=True)
    r2 = jnp.sum(jnp.where(hit2, excl, 0.0), axis=1, keepdims=True)
    carry[...] = carry[...] + incl[tm - 1:tm, :]
    cnt_ref[...] = carry[...]

    route = jnp.where(lane == 0, e1, jnp.where(lane == 1, e2, jnp.where(lane == 2, r1, jnp.where(lane == 3, r2, 0.0))))
    route_ref[...] = route.astype(jnp.int32)
    gate_ref[...] = jnp.where(lane == 0, gate1, jnp.where(lane == 1, gate2, 0.0))


def _route(merged, x2d, w_out, ln_g, ln_b, w_r, b_r, tri, alpha, tm):
    t, d = x2d.shape
    const = lambda shape: pl.BlockSpec(shape, lambda i: (0,) * len(shape))
    rows = lambda width: pl.BlockSpec((tm, width), lambda i: (i, 0))
    return pl.pallas_call(
        functools.partial(_route_kernel, alpha=alpha, tm=tm),
        grid=(t // tm,),
        in_specs=[rows(d), rows(d), const(w_out.shape), const((1, d)), const((1, d)),
                  const(w_r.shape), const((1, LANES)), const((tm, tm))],
        out_specs=[pl.BlockSpec((tm * _pitch(d), LANES), lambda i: (i, 0)),
                   rows(LANES), rows(LANES), const((SUBLANES, LANES))],
        out_shape=[jax.ShapeDtypeStruct((t * _pitch(d), LANES), F32),
                   jax.ShapeDtypeStruct((t, LANES), jnp.int32),
                   jax.ShapeDtypeStruct((t, LANES), F32),
                   jax.ShapeDtypeStruct((SUBLANES, LANES), F32)],
        scratch_shapes=[pltpu.VMEM((SUBLANES, LANES), F32)],
        compiler_params=_params(1),
        name="route",
    )(merged, x2d, w_out, ln_g, ln_b, w_r, b_r, tri)


def _dispatch_kernel(pad_start_ref, pad_len_ref, nu_ref, dest_ref, x_ref, xs_hbm, sem, *, tb, te, d):
    def copy(src_tok, dst_tok, n=1):
        return pltpu.make_async_copy(_tokens(x_ref, src_tok, d, n), _tokens(xs_hbm, dst_tok, d, n), sem)

    def wait_tokens(n):
        def body(r, c):
            copy(0, 0, n).wait()
            return c
        return body

    @pl.when(pl.program_id(0) == 0)
    def _():
        def per_expert(e, total):
            start = pad_start_ref[e]
            npad = pad_len_ref[e]

            def fill(r, c):
                copy(0, start + r).start()
                return c

            lax.fori_loop(0, npad, fill, 0)
            return total + npad

        total = lax.fori_loop(0, N_EXPERTS, per_expert, 0)
        lax.fori_loop(0, total // SUBLANES, wait_tokens(SUBLANES), 0)
        lax.fori_loop(0, total % SUBLANES, wait_tokens(1), 0)

        def fill_tile(j, c):
            copy(0, j * te, te).start()
            return c

        n_tiles = xs_hbm.shape[0] // (te * _pitch(d))
        lax.fori_loop(nu_ref[0], n_tiles, fill_tile, 0)
        lax.fori_loop(nu_ref[0], n_tiles, wait_tokens(te), 0)

    def scatter(t, c):
        for k in range(TOP_K):
            copy(t, dest_ref[TOP_K * t + k]).start()
        return c

    lax.fori_loop(0, tb, scatter, 0)
    for k in range(TOP_K):
        copy(0, 0, tb).wait()


def _dispatch(x1t, dest_flat, pad_start, pad_len, n_used, n_slots, tb, te, d):
    pitch = _pitch(d)
    t = x1t.shape[0] // pitch
    grid_spec = pltpu.PrefetchScalarGridSpec(
        num_scalar_prefetch=3,
        grid=(t // tb,),
        in_specs=[pl.BlockSpec((TOP_K * tb,), lambda i, ps, pn, nu: (i,), memory_space=pltpu.SMEM),
                  pl.BlockSpec((tb * pitch, LANES), lambda i, ps, pn, nu: (i, 0))],
        out_specs=pl.BlockSpec(memory_space=pl.ANY),
        scratch_shapes=[pltpu.SemaphoreType.DMA(())],
    )
    return pl.pallas_call(
        functools.partial(_dispatch_kernel, tb=tb, te=te, d=d),
        grid_spec=grid_spec,
        out_shape=jax.ShapeDtypeStruct((n_slots * pitch, LANES), x1t.dtype),
        compiler_params=_params(1),
        name="dispatch",
    )(pad_start, pad_len, n_used, dest_flat, x1t)


def _expert_kernel(be_ref, nu_ref, x_ref, wg_ref, wu_ref, wd_ref, y_ref, *, te, d):
    i = pl.program_id(0)

    @pl.when(i < nu_ref[0])
    def _():
        xb = _load_token_major(x_ref, te, d).astype(BF16)
        gt = _dot(xb, wg_ref[0])
        up = _dot(xb, wu_ref[0])
        hid = gt * jax.nn.sigmoid(gt) * up
        _store_token_major(y_ref, _dot(hid.astype(BF16), wd_ref[0]))

    @pl.when(i >= nu_ref[0])
    def _():
        y_ref[...] = jnp.zeros_like(y_ref)


def _experts(xs, block_e, n_used, w_gate, w_up, w_down, te):
    _, d, de = w_gate.shape
    pitch = _pitch(d)
    n_tiles = xs.shape[0] // (te * pitch)
    grid_spec = pltpu.PrefetchScalarGridSpec(
        num_scalar_prefetch=2,
        grid=(n_tiles,),
        in_specs=[pl.BlockSpec((te * pitch, LANES), lambda i, be, nu: (jnp.minimum(i, nu[0] - 1), 0)),
                  pl.BlockSpec((1, d, de), lambda i, be, nu: (be[i], 0, 0)),
                  pl.BlockSpec((1, d, de), lambda i, be, nu: (be[i], 0, 0)),
                  pl.BlockSpec((1, de, d), lambda i, be, nu: (be[i], 0, 0))],
        out_specs=pl.BlockSpec((te * pitch, LANES), lambda i, be, nu: (i, 0)),
    )
    return pl.pallas_call(
        functools.partial(_expert_kernel, te=te, d=d),
        grid_spec=grid_spec,
        out_shape=jax.ShapeDtypeStruct(xs.shape, F32),
        compiler_params=_params(1),
        name="experts",
    )(block_e, n_used, xs, w_gate, w_up, w_down)


def _final_kernel(dcur_ref, dnxt_ref, x1_ref, gate_ref, p_ref, ys_hbm, g_ref, b_ref, wg_ref, bg_ref, wp_ref,
                  out_ref, ybuf, sem, *, alpha, tm, d):
    i = pl.program_id(0)
    slot = lax.rem(i, 2)

    def issue(dest_ref, s):
        def body(t, c):
            for k in range(TOP_K):
                pltpu.make_async_copy(_tokens(ys_hbm, dest_ref[TOP_K * t + k], d),
                                      _tokens(ybuf.at[s, k], t, d), sem.at[s]).start()
            return c
        lax.fori_loop(0, tm, body, 0)

    @pl.when(i == 0)
    def _():
        issue(dcur_ref, 0)

    @pl.when(i + 1 < pl.num_programs(0))
    def _():
        issue(dnxt_ref, 1 - slot)

    pp = _dot(p_ref[...].astype(BF16), wp_ref[...])
    x1 = _load_token_major(x1_ref, tm, d)
    for k in range(TOP_K):
        pltpu.make_async_copy(_tokens(ys_hbm, 0, d, tm), ybuf.at[slot, k], sem.at[slot]).wait()

    gate = gate_ref[...]
    y = (gate[:, 0:1] * _load_token_major(ybuf.at[slot, 0], tm, d)
         + gate[:, 1:2] * _load_token_major(ybuf.at[slot, 1], tm, d))
    x2 = _layer_norm(alpha * x1 + y, g_ref[...], b_ref[...])
    gl = _dot(x2.astype(BF16), wg_ref[...]) + bg_ref[...]
    out_ref[...] = x2 + jax.nn.sigmoid(gl) * pp


def _final(dest_flat, x1t, gate, p2d, ys, ln_g, ln_b, w_pg, b_pg, w_pp, alpha, tm):
    d = w_pg.shape[0]
    pitch = _pitch(d)
    t = x1t.shape[0] // pitch
    n = t // tm
    const = lambda shape: pl.BlockSpec(shape, lambda i: (0,) * len(shape))
    rows = lambda width: pl.BlockSpec((tm, width), lambda i: (i, 0))
    return pl.pallas_call(
        functools.partial(_final_kernel, alpha=alpha, tm=tm, d=d),
        grid=(n,),
        in_specs=[pl.BlockSpec((TOP_K * tm,), lambda i: (i,), memory_space=pltpu.SMEM),
                  pl.BlockSpec((TOP_K * tm,), lambda i: (jnp.minimum(i + 1, n - 1),), memory_space=pltpu.SMEM),
                  pl.BlockSpec((tm * pitch, LANES), lambda i: (i, 0)),
                  rows(LANES), rows(p2d.shape[1]),
                  pl.BlockSpec(memory_space=pl.ANY),
                  const((1, d)), const((1, d)), const(w_pg.shape), const((1, d)), const(w_pp.shape)],
        out_specs=rows(d),
        out_shape=jax.ShapeDtypeStruct((t, d), F32),
        scratch_shapes=[pltpu.VMEM((2, TOP_K, tm * pitch, LANES), F32), pltpu.SemaphoreType.DMA((2,))],
        compiler_params=_params(1),
        name="final",
    )(dest_flat, dest_flat, x1t, gate, p2d, ys, ln_g, ln_b, w_pg, b_pg, w_pp)


def _pad_cols(a, width):
    return jnp.pad(a, ((0, 0), (0, width - a.shape[1])))


def _tri(n):
    return jnp.tril(jnp.ones((n, n), BF16))


def _layer(x, p, w_in, b_in, conv_w, conv_b, mh_g, w_pool, pool_scale, w_m_br, w_p_br, w_out,
           ln1_g, ln1_b, w_rg, b_rg, w_re, b_re, w_gate, w_up, w_down, ln2_g, ln2_b,
           w_ple_gate, b_ple_gate, w_ple_proj, alpha):
    nb, seq, d = x.shape
    t = nb * seq
    x2d = x.reshape(t, d)
    row = lambda a: a.reshape(1, -1)

    c_if = 2 * M_QK + 2 * M_V
    w_main = jnp.concatenate([w_in[:, :c_if], w_in[:, c_if + 2 * M_HEADS:]], axis=1).astype(BF16)
    b_main = row(jnp.concatenate([b_in[:c_if], b_in[c_if + 2 * M_HEADS:]]))
    w_if = _pad_cols(w_in[:, c_if:c_if + 2 * M_HEADS], LANES).astype(BF16)
    b_if = _pad_cols(row(b_in[c_if:c_if + 2 * M_HEADS]), LANES)

    tm_in = min(1024, t)
    z_main, z_if = _inproj(x2d, w_main, b_main, w_if, b_if, tm_in, 1024)

    hg = _mlstm(z_main.reshape(nb, seq, -1), z_if.reshape(nb, seq, LANES),
                conv_w, row(conv_b), row(mh_g), _tri(CHUNK))

    tm = min(512, seq)
    merged = _branch(hg.reshape(t, M_V), z_main, w_pool.astype(BF16), row(pool_scale),
                     w_m_br.astype(BF16), w_p_br.astype(BF16), seq, tm)

    w_r = _pad_cols(jnp.concatenate([w_rg, w_re], axis=1), LANES)
    w_r_hi = w_r.astype(BF16)
    w_r_lo = (w_r - w_r_hi.astype(F32)).astype(BF16)
    b_r = _pad_cols(row(jnp.concatenate([b_rg, b_re])), LANES)
    x1, route, gate, cnt = _route(merged, x2d, w_out.astype(BF16), row(ln1_g), row(ln1_b),
                                  jnp.concatenate([w_r_hi, w_r_lo], axis=1), b_r, _tri(tm), alpha, tm)

    te = EXPERT_TILE
    counts = cnt[0, :N_EXPERTS].astype(jnp.int32)
    pcounts = (counts + te - 1) // te * te
    pends = jnp.cumsum(pcounts)
    pstarts = pends - pcounts
    e_sel = route[:, 0:TOP_K, None] == jnp.arange(N_EXPERTS, dtype=jnp.int32)
    dest = (jnp.sum(jnp.where(e_sel, pstarts, 0), axis=-1) + route[:, TOP_K:2 * TOP_K]).reshape(-1)
    n_slots = t * TOP_K + N_EXPERTS * te
    n_tiles = n_slots // te
    n_used = (pends[-1] // te).reshape(1)
    tile_row = jnp.minimum(jnp.arange(n_tiles, dtype=jnp.int32), n_used - 1) * te
    block_e = jnp.minimum(jnp.sum((tile_row[:, None] >= pends[None, :]).astype(jnp.int32), axis=1), N_EXPERTS - 1)

    xs = _dispatch(x1, dest, pstarts + counts, pcounts - counts, n_used, n_slots, min(512, t), te, d)
    ys = _experts(xs, block_e, n_used, w_gate.astype(BF16), w_up.astype(BF16), w_down.astype(BF16), te)
    return _final(dest, x1, gate, p.reshape(t, -1), ys, row(ln2_g), row(ln2_b),
                  w_ple_gate.astype(BF16), row(b_ple_gate), w_ple_proj.astype(BF16), alpha,
                  min(256, t)).reshape(nb, seq, d)


def kernel(x, p, w_in, b_in, conv_w, conv_b, mh_g, w_pool, pool_scale, w_m_br, w_p_br, w_out, ln1_g, ln1_b, w_rg, b_rg, w_re, b_re, w_gate, w_up, w_down, ln2_g, ln2_b, w_ple_gate, b_ple_gate, w_ple_proj):
    depth = w_in.shape[0]
    alpha = (2 * depth) ** 0.25
    for i in range(depth):
        x = _layer(x, p[i], w_in[i], b_in[i], conv_w[i], conv_b[i], mh_g[i], w_pool[i], pool_scale[i],
                   w_m_br[i], w_p_br[i], w_out[i], ln1_g[i], ln1_b[i], w_rg[i], b_rg[i], w_re[i], b_re[i],
                   w_gate[i], w_up[i], w_down[i], ln2_g[i], ln2_b[i], w_ple_gate[i], b_ple_gate[i],
                   w_ple_proj[i], alpha)
    return x
```

```python
import functools

import jax
import jax.numpy as jnp
from jax import lax
from jax.experimental import pallas as pl
from jax.experimental.pallas import tpu as pltpu

F32 = jnp.float32
BF16 = jnp.bfloat16

M_HEADS = 4
M_QK_DIM = 128
M_V_DIM = 256
M_QK = M_HEADS * M_QK_DIM
M_V = M_HEADS * M_V_DIM
CONV_W = 4
CHUNK = 128
POOL_WINDOWS = (2, 4, 8, 16)
POOL_GROUP_DIM = 256
POOL_W = len(POOL_WINDOWS) * POOL_GROUP_DIM
N_GROUPS = 4
EXPERTS_PER_GROUP = 8
N_EXPERTS = N_GROUPS * EXPERTS_PER_GROUP
TOP_K = 2
LN_EPS = 1e-5

LANES = 128
SUBLANES = 8
VMEM_LIMIT = 56 * 1024 * 1024
BF16_SUBLANES = 16
CONV_HALO = BF16_SUBLANES
POOL_HALO = 32
EXPERT_TILE = 256


def _dot(a, b):
    return jnp.dot(a, b, preferred_element_type=F32)


def _params(n_grid):
    return pltpu.CompilerParams(dimension_semantics=("arbitrary",) * n_grid,
                                vmem_limit_bytes=VMEM_LIMIT)


def _log_sigmoid(x):
    return -(jnp.maximum(-x, 0.0) + jnp.log1p(jnp.exp(-jnp.abs(x))))


def _layer_norm(r, g, b):
    mu = jnp.mean(r, axis=-1, keepdims=True)
    d = r - mu
    var = jnp.mean(d * d, axis=-1, keepdims=True)
    return d * lax.rsqrt(var + LN_EPS) * g + b


def _pitch(d):
    return d // LANES + 1


def _store_token_major(ref, val):
    n, d = val.shape
    for c in range(d // LANES):
        ref[pl.ds(c, n, stride=_pitch(d)), :] = val[:, c * LANES:(c + 1) * LANES]
    ref[pl.ds(d // LANES, n, stride=_pitch(d)), :] = jnp.zeros((n, LANES), val.dtype)


def _load_token_major(ref, n, d):
    return jnp.concatenate([ref[pl.ds(c, n, stride=_pitch(d)), :] for c in range(d // LANES)], axis=1)


def _tokens(ref, tok, d, n=1):
    return ref.at[pl.ds(tok * _pitch(d), n * _pitch(d))]


def _inproj_kernel(*refs, n_side):
    x_ref, w_ref, b_ref, wif_ref, bif_ref = refs[:5]
    side_in = refs[5:5 + n_side]
    z_ref, zif_ref = refs[5 + n_side:7 + n_side]
    side_out = refs[7 + n_side:7 + 2 * n_side]
    xb_ref = refs[7 + 2 * n_side]

    @pl.when(pl.program_id(1) == 0)
    def _():
        xb = x_ref[...].astype(BF16)
        xb_ref[...] = xb
        zif_ref[...] = _dot(xb, wif_ref[...]) + bif_ref[...]

    z_ref[...] = (_dot(xb_ref[...], w_ref[...]) + b_ref[...]).astype(BF16)
    for src, dst in zip(side_in, side_out):
        dst[...] = src[...].astype(BF16)


def _inproj(x2d, w_main, b_main, w_if, b_if, tm, tn, side):
    t, d = x2d.shape
    n = w_main.shape[1]
    nj = n // tn
    steps = (t // tm) * nj
    side_spec = lambda a: pl.BlockSpec((a.shape[0] // steps, a.shape[1]), lambda i, j: (i * nj + j, 0))
    return pl.pallas_call(
        functools.partial(_inproj_kernel, n_side=len(side)),
        grid=(t // tm, nj),
        in_specs=[pl.BlockSpec((tm, d), lambda i, j: (i, 0)),
                  pl.BlockSpec((d, tn), lambda i, j: (0, j)),
                  pl.BlockSpec((1, tn), lambda i, j: (0, j)),
                  pl.BlockSpec((d, LANES), lambda i, j: (0, 0)),
                  pl.BlockSpec((1, LANES), lambda i, j: (0, 0))] + [side_spec(a) for a in side],
        out_specs=[pl.BlockSpec((tm, tn), lambda i, j: (i, j)),
                   pl.BlockSpec((tm, LANES), lambda i, j: (i, 0))] + [side_spec(a) for a in side],
        out_shape=[jax.ShapeDtypeStruct((t, n), BF16),
                   jax.ShapeDtypeStruct((t, LANES), F32)] + [jax.ShapeDtypeStruct(a.shape, BF16) for a in side],
        scratch_shapes=[pltpu.VMEM((tm, d), BF16)],
        compiler_params=_params(2),
        name="inproj",
    )(x2d, w_main, b_main, w_if, b_if, *side)


def _mlstm_kernel(qk_ref, v_ref, o_ref, g_ref, cw_ref, cb_ref, mhg_ref, out_ref, cbuf, *state, nb):
    L = CHUNK
    halo = CONV_HALO
    ct_refs, nm_refs = state[:M_HEADS], state[M_HEADS:]

    @pl.when(pl.program_id(0) == 0)
    def _():
        cbuf[:, 0:halo, :] = jnp.zeros((nb, halo, 2 * M_QK), BF16)
        for ref in state:
            ref[...] = jnp.zeros_like(ref)

    cbuf[:, halo:halo + L, :] = qk_ref[...]
    scale = M_QK_DIM ** -0.5
    row = lax.broadcasted_iota(jnp.int32, (L, L), 0)
    col = lax.broadcasted_iota(jnp.int32, (L, L), 1)
    causal = col <= row
    upper = jnp.where(row <= col, 1.0, 0.0).astype(BF16)
    wrow = lax.broadcasted_iota(jnp.int32, (L, L + halo), 0)
    wcol = lax.broadcasted_iota(jnp.int32, (L, L + halo), 1)
    shifts = [jnp.where(wcol == wrow + (halo - (CONV_W - 1) + tap), 1.0, 0.0).astype(BF16)
              for tap in range(CONV_W - 1)]
    sub = lax.broadcasted_iota(jnp.int32, (SUBLANES, LANES), 0)

    def per_batch(b, carry):
        window = cbuf[b]
        acc = cb_ref[...] + _dot(shifts[0], window) * cw_ref[0:1, :]
        for tap in range(1, CONV_W - 1):
            acc = acc + _dot(shifts[tap], window) * cw_ref[tap:tap + 1, :]
        acc = acc + qk_ref[b].astype(F32) * cw_ref[CONV_W - 1:CONV_W, :]
        qk = acc * jax.nn.sigmoid(acc)
        cbuf[b, 0:halo, :] = cbuf[b, L:L + halo, :]

        gcol = g_ref[b]
        g_row = gcol.T[0:SUBLANES, :]
        lf = _log_sigmoid(g_row)
        lf_hi = lf.astype(BF16)
        lf_lo = (lf - lf_hi.astype(F32)).astype(BF16)
        cs = _dot(jnp.concatenate([lf_hi, lf_lo], axis=0), upper)
        b_row = cs[0:SUBLANES, :] + cs[SUBLANES:2 * SUBLANES, :]
        b_col = jnp.concatenate([b_row, jnp.zeros((L - SUBLANES, L), F32)], axis=0).T

        for h in range(M_HEADS):
            qf = qk[:, h * M_QK_DIM:(h + 1) * M_QK_DIM]
            q = qf.astype(BF16)
            kf = qk[:, M_QK + h * M_QK_DIM:M_QK + (h + 1) * M_QK_DIM] * scale
            k = kf.astype(BF16)
            v = v_ref[b, :, h * M_V_DIM:(h + 1) * M_V_DIM]
            bj = b_col[:, M_HEADS + h:M_HEADS + h + 1]
            bl = b_row[M_HEADS + h:M_HEADS + h + 1, :]
            il = g_row[h:h + 1, :]
            ij = gcol[:, h:h + 1]
            ct = ct_refs[h][b]
            nm = nm_refs[h][b]
            nvec = nm[0:1, :]
            m_prev = nm[1:2, 0:1]

            dmat = jnp.where(causal, bj - bl + il, -jnp.inf)
            inter = bj + m_prev
            m_row = jnp.maximum(inter, jnp.max(dmat, axis=1, keepdims=True))
            pmat = jnp.exp(dmat - m_row)
            s = lax.dot_general(q, k, (((1,), (1,)), ((), ())), preferred_element_type=F32) * pmat
            scale_prev = jnp.exp(inter - m_row)
            num = _dot(s.astype(BF16), v) + scale_prev * _dot(q, ct.astype(BF16))
            den = (jnp.sum(s, axis=1, keepdims=True)
                   + scale_prev * jnp.sum(qf * nvec, axis=1, keepdims=True))
            hh = num / jnp.maximum(jnp.abs(den), jnp.exp(-m_row))

            b_last = bj[L - 1:L, :]
            gvec = b_last - bj + ij
            m_new = jnp.maximum(b_last + m_prev, jnp.max(gvec, axis=0, keepdims=True))
            wvec = jnp.exp(gvec - m_new)
            decay = jnp.exp(b_last + m_prev - m_new)
            kw = kf * wvec
            ct_refs[h][b] = decay * ct + lax.dot_general(
                kw.astype(BF16), v, (((0,), (0,)), ((), ())), preferred_element_type=F32)
            n_new = decay * nvec + jnp.sum(kw, axis=0, keepdims=True)
            nm_refs[h][b] = jnp.where(sub == 0, n_new, m_new)

            mu = jnp.mean(hh, axis=-1, keepdims=True)
            dlt = hh - mu
            var = jnp.mean(dlt * dlt, axis=-1, keepdims=True)
            hn = dlt * lax.rsqrt(var + LN_EPS) * mhg_ref[:, h * M_V_DIM:(h + 1) * M_V_DIM]
            og = jax.nn.sigmoid(o_ref[b, :, h * M_V_DIM:(h + 1) * M_V_DIM].astype(F32))
            out_ref[b, :, h * M_V_DIM:(h + 1) * M_V_DIM] = (hn * og).astype(BF16)
        return carry

    lax.fori_loop(0, nb, per_batch, 0)


def _mlstm(z3, zif3, conv_w, conv_b, mh_g):
    nb, s, _ = z3.shape
    L = CHUNK
    blk = lambda colblk: pl.BlockSpec((nb, L, M_V), lambda c: (0, c, colblk))
    const = lambda shape: pl.BlockSpec(shape, lambda c: (0,) * len(shape))
    return pl.pallas_call(
        functools.partial(_mlstm_kernel, nb=nb),
        grid=(s // L,),
        in_specs=[blk(0), blk(1), blk(2),
                  pl.BlockSpec((nb, L, LANES), lambda c: (0, c, 0)),
                  const((CONV_W, 2 * M_QK)), const((1, 2 * M_QK)), const((1, M_V))],
        out_specs=pl.BlockSpec((nb, L, M_V), lambda c: (0, c, 0)),
        out_shape=jax.ShapeDtypeStruct((nb, s, M_V), BF16),
        scratch_shapes=([pltpu.VMEM((nb, L + CONV_HALO, 2 * M_QK), BF16)]
                        + [pltpu.VMEM((nb, M_QK_DIM, M_V_DIM), F32)] * M_HEADS
                        + [pltpu.VMEM((nb, SUBLANES, M_QK_DIM), F32)] * M_HEADS),
        compiler_params=_params(1),
        name="mlstm",
    )(z3, z3, z3, zif3, conv_w, conv_b, mh_g)


def _branch_kernel(hg_ref, u_ref, uh_ref, gm_ref, gp_ref, wpool_ref, ps_ref, wm_ref, wp_ref,
                   out_ref, xa, xb, yp_ref, *, tm, seq):
    H = POOL_HALO
    G = POOL_GROUP_DIM
    t0 = lax.rem(pl.program_id(0) * tm, seq)
    u = u_ref[...].astype(F32)
    xa[H:H + tm, :] = u
    xa[0:H, :] = jnp.where(t0 == 0, 0.0, uh_ref[...].astype(F32))
    n = tm + H - 8
    xb[8:8 + n, :] = xa[8:8 + n, :] + xa[7:7 + n, :]
    n = tm + H - 16
    xa[16:16 + n, G:] = xb[16:16 + n, G:] + xb[14:14 + n, G:]
    n = tm + H - 24
    xb[24:24 + n, 2 * G:] = xa[24:24 + n, 2 * G:] + xa[20:20 + n, 2 * G:]
    xa[H:H + tm, 3 * G:] = xb[H:H + tm, 3 * G:] + xb[H - 8:H - 8 + tm, 3 * G:]
    tpos = t0 + lax.broadcasted_iota(jnp.int32, (tm, 1), 0)
    for g, win in enumerate(POOL_WINDOWS):
        src = (xb, xa, xb, xa)[g]
        cols = slice(g * G, (g + 1) * G)
        cnt = jnp.minimum(tpos + 1, win).astype(F32)
        y = src[H:H + tm, cols] / cnt - u[:, cols]
        yp = _dot(y.astype(BF16), wpool_ref[g]) * ps_ref[:, cols]
        yp_ref[:, cols] = yp.astype(BF16)
    pb = _dot(yp_ref[...], wp_ref[...])
    a = _dot(hg_ref[...], wm_ref[...])
    merged = (jax.nn.sigmoid(gm_ref[...].astype(F32)) * a
              + jax.nn.sigmoid(gp_ref[...].astype(F32)) * pb)
    out_ref[...] = merged.astype(BF16)


def _branch(hg2d, z_main, w_pool, pool_scale, w_m_br, w_p_br, seq, tm):
    t = hg2d.shape[0]
    d = w_m_br.shape[1]
    hb = tm // POOL_HALO
    const = lambda shape: pl.BlockSpec(shape, lambda i: (0,) * len(shape))
    return pl.pallas_call(
        functools.partial(_branch_kernel, tm=tm, seq=seq),
        grid=(t // tm,),
        in_specs=[pl.BlockSpec((tm, M_V), lambda i: (i, 0)),
                  pl.BlockSpec((tm, POOL_W), lambda i: (i, 3)),
                  pl.BlockSpec((POOL_HALO, POOL_W), lambda i: (jnp.maximum(i * hb - 1, 0), 3)),
                  pl.BlockSpec((tm, d), lambda i: (i, 2)),
                  pl.BlockSpec((tm, d), lambda i: (i, 3)),
                  const(w_pool.shape), const((1, POOL_W)), const(w_m_br.shape), const(w_p_br.shape)],
        out_specs=pl.BlockSpec((tm, d), lambda i: (i, 0)),
        out_shape=jax.ShapeDtypeStruct((t, d), BF16),
        scratch_shapes=[pltpu.VMEM((tm + POOL_HALO, POOL_W), F32),
                        pltpu.VMEM((tm + POOL_HALO, POOL_W), F32),
                        pltpu.VMEM((tm, POOL_W), BF16)],
        compiler_params=_params(1),
        name="branch",
    )(hg2d, z_main, z_main, z_main, z_main, w_pool, pool_scale, w_m_br, w_p_br)


def _route_kernel(mg_ref, x_ref, wout_ref, g_ref, b_ref, wr_ref, br_ref, tri_ref,
                  x1_ref, route_ref, gate_ref, cnt_ref, carry, *, alpha, tm):
    @pl.when(pl.program_id(0) == 0)
    def _():
        carry[...] = jnp.zeros_like(carry)

    r = alpha * x_ref[...] + _dot(mg_ref[...], wout_ref[...])
    x1 = _layer_norm(r, g_ref[...], b_ref[...])
    _store_token_major(x1_ref, x1)

    hi = x1.astype(BF16)
    lo = (x1 - hi.astype(F32)).astype(BF16)
    p1 = _dot(hi, wr_ref[...])
    p2 = _dot(lo, wr_ref[:, 0:LANES])
    logits = p1[:, 0:LANES] + p1[:, LANES:2 * LANES] + p2 + br_ref[...]

    lane = lax.broadcasted_iota(jnp.int32, (tm, LANES), 1)
    lanef = lane.astype(F32)
    big = float(LANES)

    def softmax_masked(mask):
        z = jnp.where(mask, logits, -jnp.inf)
        e = jnp.exp(z - jnp.max(z, axis=1, keepdims=True))
        return e / jnp.sum(e, axis=1, keepdims=True)

    def top1(vals):
        top = jnp.max(vals, axis=1, keepdims=True)
        idx = jnp.min(jnp.where(vals == top, lanef, big), axis=1, keepdims=True)
        return top, idx

    is_grp = lane < N_GROUPS
    pg = jnp.where(is_grp, softmax_masked(is_grp), -1.0)
    pg_top, g_idx = top1(pg)
    e_lo = N_GROUPS + EXPERTS_PER_GROUP * g_idx.astype(jnp.int32)
    in_grp = (lane >= e_lo) & (lane < e_lo + EXPERTS_PER_GROUP)
    pe = jnp.where(in_grp, softmax_masked(in_grp), -1.0)
    pe1, i1 = top1(pe)
    pe2, i2 = top1(jnp.where(lanef == i1, -1.0, pe))
    den = pe1 + pe2
    gate1 = pg_top * pe1 / den
    gate2 = pg_top * pe2 / den
    e1 = i1 - float(N_GROUPS)
    e2 = i2 - float(N_GROUPS)

    hit1 = lanef == e1
    hit2 = lanef == e2
    onehot = jnp.where(hit1 | hit2, 1.0, 0.0)
    incl = _dot(tri_ref[...], onehot.astype(BF16))
    excl = incl - onehot + carry[0:1, :]
    r1 = jnp.sum(jnp.where(hit1, excl, 0.0), axis=1, keepdims=True)
    r2 = jnp.sum(jnp.where(hit2, excl, 0.0), axis=1, keepdims=True)
    carry[...] = carry[...] + incl[tm - 1:tm, :]
    cnt_ref[...] = carry[...]

    route = jnp.where(lane == 0, e1, jnp.where(lane == 1, e2, jnp.where(lane == 2, r1, jnp.where(lane == 3, r2, 0.0))))
    route_ref[...] = route.astype(jnp.int32)
    gate_ref[...] = jnp.where(lane == 0, gate1, jnp.where(lane == 1, gate2, 0.0))


def _route(merged, x2d, w_out, ln_g, ln_b, w_r, b_r, tri, alpha, tm):
    t, d = x2d.shape
    const = lambda shape: pl.BlockSpec(shape, lambda i: (0,) * len(shape))
    rows = lambda width: pl.BlockSpec((tm, width), lambda i: (i, 0))
    return pl.pallas_call(
        functools.partial(_route_kernel, alpha=alpha, tm=tm),
        grid=(t // tm,),
        in_specs=[rows(d), rows(d), const(w_out.shape), const((1, d)), const((1, d)),
                  const(w_r.shape), const((1, LANES)), const((tm, tm))],
        out_specs=[pl.BlockSpec((tm * _pitch(d), LANES), lambda i: (i, 0)),
                   rows(LANES), rows(LANES), const((SUBLANES, LANES))],
        out_shape=[jax.ShapeDtypeStruct((t * _pitch(d), LANES), F32),
                   jax.ShapeDtypeStruct((t, LANES), jnp.int32),
                   jax.ShapeDtypeStruct((t, LANES), F32),
                   jax.ShapeDtypeStruct((SUBLANES, LANES), F32)],
        scratch_shapes=[pltpu.VMEM((SUBLANES, LANES), F32)],
        compiler_params=_params(1),
        name="route",
    )(merged, x2d, w_out, ln_g, ln_b, w_r, b_r, tri)


def _dispatch_kernel(pad_start_ref, pad_len_ref, nu_ref, dest_ref, x_ref, xs_hbm, sem, *, tb, te, d):
    def copy(src_tok, dst_tok, n=1):
        return pltpu.make_async_copy(_tokens(x_ref, src_tok, d, n), _tokens(xs_hbm, dst_tok, d, n), sem)

    def wait_tokens(n):
        def body(r, c):
            copy(0, 0, n).wait()
            return c
        return body

    @pl.when(pl.program_id(0) == 0)
    def _():
        def per_expert(e, total):
            start = pad_start_ref[e]
            npad = pad_len_ref[e]

            def fill(r, c):
                copy(0, start + r).start()
                return c

            lax.fori_loop(0, npad, fill, 0)
            return total + npad

        total = lax.fori_loop(0, N_EXPERTS, per_expert, 0)
        lax.fori_loop(0, total // SUBLANES, wait_tokens(SUBLANES), 0)
        lax.fori_loop(0, total % SUBLANES, wait_tokens(1), 0)

        def fill_tile(j, c):
            copy(0, j * te, te).start()
            return c

        n_tiles = xs_hbm.shape[0] // (te * _pitch(d))
        lax.fori_loop(nu_ref[0], n_tiles, fill_tile, 0)
        lax.fori_loop(nu_ref[0], n_tiles, wait_tokens(te), 0)

    def scatter(t, c):
        for k in range(TOP_K):
            copy(t, dest_ref[TOP_K * t + k]).start()
        return c

    lax.fori_loop(0, tb, scatter, 0)
    for k in range(TOP_K):
        copy(0, 0, tb).wait()


def _dispatch(x1t, dest_flat, pad_start, pad_len, n_used, n_slots, tb, te, d):
    pitch = _pitch(d)
    t = x1t.shape[0] // pitch
    grid_spec = pltpu.PrefetchScalarGridSpec(
        num_scalar_prefetch=3,
        grid=(t // tb,),
        in_specs=[pl.BlockSpec((TOP_K * tb,), lambda i, ps, pn, nu: (i,), memory_space=pltpu.SMEM),
                  pl.BlockSpec((tb * pitch, LANES), lambda i, ps, pn, nu: (i, 0))],
        out_specs=pl.BlockSpec(memory_space=pl.ANY),
        scratch_shapes=[pltpu.SemaphoreType.DMA(())],
    )
    return pl.pallas_call(
        functools.partial(_dispatch_kernel, tb=tb, te=te, d=d),
        grid_spec=grid_spec,
        out_shape=jax.ShapeDtypeStruct((n_slots * pitch, LANES), x1t.dtype),
        compiler_params=_params(1),
        name="dispatch",
    )(pad_start, pad_len, n_used, dest_flat, x1t)


def _expert_kernel(be_ref, nu_ref, x_ref, wg_ref, wu_ref, wd_ref, y_ref, *, te, d):
    i = pl.program_id(0)

    @pl.when(i < nu_ref[0])
    def _():
        xb = _load_token_major(x_ref, te, d).astype(BF16)
        gt = _dot(xb, wg_ref[0])
        up = _dot(xb, wu_ref[0])
        hid = gt * jax.nn.sigmoid(gt) * up
        _store_token_major(y_ref, _dot(hid.astype(BF16), wd_ref[0]))

    @pl.when(i >= nu_ref[0])
    def _():
        y_ref[...] = jnp.zeros_like(y_ref)


def _experts(xs, block_e, n_used, w_gate, w_up, w_down, te):
    _, d, de = w_gate.shape
    pitch = _pitch(d)
    n_tiles = xs.shape[0] // (te * pitch)
    grid_spec = pltpu.PrefetchScalarGridSpec(
        num_scalar_prefetch=2,
        grid=(n_tiles,),
        in_specs=[pl.BlockSpec((te * pitch, LANES), lambda i, be, nu: (jnp.minimum(i, nu[0] - 1), 0)),
                  pl.BlockSpec((1, d, de), lambda i, be, nu: (be[i], 0, 0)),
                  pl.BlockSpec((1, d, de), lambda i, be, nu: (be[i], 0, 0)),
                  pl.BlockSpec((1, de, d), lambda i, be, nu: (be[i], 0, 0))],
        out_specs=pl.BlockSpec((te * pitch, LANES), lambda i, be, nu: (i, 0)),
    )
    return pl.pallas_call(
        functools.partial(_expert_kernel, te=te, d=d),
        grid_spec=grid_spec,
        out_shape=jax.ShapeDtypeStruct(xs.shape, F32),
        compiler_params=_params(1),
        name="experts",
    )(block_e, n_used, xs, w_gate, w_up, w_down)


def _final_kernel(dcur_ref, dnxt_ref, x1_ref, gate_ref, p_ref, ys_hbm, g_ref, b_ref, wg_ref, bg_ref, wp_ref,
                  out_ref, ybuf, sem, *, alpha, tm, d):
    i = pl.program_id(0)
    slot = lax.rem(i, 2)

    def issue(dest_ref, s):
        def body(t, c):
            for k in range(TOP_K):
                pltpu.make_async_copy(_tokens(ys_hbm, dest_ref[TOP_K * t + k], d),
                                      _tokens(ybuf.at[s, k], t, d), sem.at[s]).start()
            return c
        lax.fori_loop(0, tm, body, 0)

    @pl.when(i == 0)
    def _():
        issue(dcur_ref, 0)

    @pl.when(i + 1 < pl.num_programs(0))
    def _():
        issue(dnxt_ref, 1 - slot)

    pp = _dot(p_ref[...].astype(BF16), wp_ref[...])
    x1 = _load_token_major(x1_ref, tm, d)
    for k in range(TOP_K):
        pltpu.make_async_copy(_tokens(ys_hbm, 0, d, tm), ybuf.at[slot, k], sem.at[slot]).wait()

    gate = gate_ref[...]
    y = (gate[:, 0:1] * _load_token_major(ybuf.at[slot, 0], tm, d)
         + gate[:, 1:2] * _load_token_major(ybuf.at[slot, 1], tm, d))
    x2 = _layer_norm(alpha * x1 + y, g_ref[...], b_ref[...])
    gl = _dot(x2.astype(BF16), wg_ref[...]) + bg_ref[...]
    out_ref[...] = x2 + jax.nn.sigmoid(gl) * pp


def _final(dest_flat, x1t, gate, p2d, ys, ln_g, ln_b, w_pg, b_pg, w_pp, alpha, tm):
    d = w_pg.shape[0]
    pitch = _pitch(d)
    t = x1t.shape[0] // pitch
    n = t // tm
    const = lambda shape: pl.BlockSpec(shape, lambda i: (0,) * len(shape))
    rows = lambda width: pl.BlockSpec((tm, width), lambda i: (i, 0))
    return pl.pallas_call(
        functools.partial(_final_kernel, alpha=alpha, tm=tm, d=d),
        grid=(n,),
        in_specs=[pl.BlockSpec((TOP_K * tm,), lambda i: (i,), memory_space=pltpu.SMEM),
                  pl.BlockSpec((TOP_K * tm,), lambda i: (jnp.minimum(i + 1, n - 1),), memory_space=pltpu.SMEM),
                  pl.BlockSpec((tm * pitch, LANES), lambda i: (i, 0)),
                  rows(LANES), rows(p2d.shape[1]),
                  pl.BlockSpec(memory_space=pl.ANY),
                  const((1, d)), const((1, d)), const(w_pg.shape), const((1, d)), const(w_pp.shape)],
        out_specs=rows(d),
        out_shape=jax.ShapeDtypeStruct((t, d), F32),
        scratch_shapes=[pltpu.VMEM((2, TOP_K, tm * pitch, LANES), F32), pltpu.SemaphoreType.DMA((2,))],
        compiler_params=_params(1),
        name="final",
    )(dest_flat, dest_flat, x1t, gate, p2d, ys, ln_g, ln_b, w_pg, b_pg, w_pp)


def _pad_cols(a, width):
    return jnp.pad(a, ((0, 0), (0, width - a.shape[1])))


def _tri(n):
    return jnp.tril(jnp.ones((n, n), BF16))


def _layer(x, p, w_in, b_in, conv_w, conv_b, mh_g, w_pool, pool_scale, w_m_br, w_p_br, w_out,
           ln1_g, ln1_b, w_rg, b_rg, w_re, b_re, w_gate, w_up, w_down, ln2_g, ln2_b,
           w_ple_gate, b_ple_gate, w_ple_proj, alpha):
    nb, seq, d = x.shape
    t = nb * seq
    x2d = x.reshape(t, d)
    row = lambda a: a.reshape(1, -1)

    c_if = 2 * M_QK + 2 * M_V
    w_main = jnp.concatenate([w_in[:, :c_if].astype(BF16), w_in[:, c_if + 2 * M_HEADS:].astype(BF16)], axis=1)
    b_main = row(jnp.concatenate([b_in[:c_if], b_in[c_if + 2 * M_HEADS:]]))
    w_if = _pad_cols(w_in[:, c_if:c_if + 2 * M_HEADS], LANES).astype(BF16)
    b_if = _pad_cols(row(b_in[c_if:c_if + 2 * M_HEADS]), LANES)

    n_exp, _, d_exp = w_gate.shape
    expert_w = [w_gate.reshape(n_exp * d, d_exp), w_up.reshape(n_exp * d, d_exp), w_down.reshape(n_exp * d_exp, d)]
    tm_in = min(1024, t)
    z_main, z_if, wg_b, wu_b, wd_b = _inproj(x2d, w_main, b_main, w_if, b_if, tm_in, 1024, expert_w)

    hg = _mlstm(z_main.reshape(nb, seq, -1), z_if.reshape(nb, seq, LANES),
                conv_w, row(conv_b), row(mh_g))

    tm = min(512, seq)
    merged = _branch(hg.reshape(t, M_V), z_main, w_pool.astype(BF16), row(pool_scale),
                     w_m_br.astype(BF16), w_p_br.astype(BF16), seq, tm)

    w_r = _pad_cols(jnp.concatenate([w_rg, w_re], axis=1), LANES)
    w_r_hi = w_r.astype(BF16)
    w_r_lo = (w_r - w_r_hi.astype(F32)).astype(BF16)
    b_r = _pad_cols(row(jnp.concatenate([b_rg, b_re])), LANES)
    x1, route, gate, cnt = _route(merged, x2d, w_out.astype(BF16), row(ln1_g), row(ln1_b),
                                  jnp.concatenate([w_r_hi, w_r_lo], axis=1), b_r, _tri(tm), alpha, tm)

    te = EXPERT_TILE
    counts = cnt[0, :N_EXPERTS].astype(jnp.int32)
    pcounts = (counts + te - 1) // te * te
    pends = jnp.cumsum(pcounts)
    pstarts = pends - pcounts
    e_sel = route[:, 0:TOP_K, None] == jnp.arange(N_EXPERTS, dtype=jnp.int32)
    dest = (jnp.sum(jnp.where(e_sel, pstarts, 0), axis=-1) + route[:, TOP_K:2 * TOP_K]).reshape(-1)
    n_slots = t * TOP_K + N_EXPERTS * te
    n_tiles = n_slots // te
    n_used = (pends[-1] // te).reshape(1)
    tile_row = jnp.minimum(jnp.arange(n_tiles, dtype=jnp.int32), n_used - 1) * te
    block_e = jnp.minimum(jnp.sum((tile_row[:, None] >= pends[None, :]).astype(jnp.int32), axis=1), N_EXPERTS - 1)

    xs = _dispatch(x1, dest, pstarts + counts, pcounts - counts, n_used, n_slots, min(512, t), te, d)
    ys = _experts(xs, block_e, n_used, wg_b.reshape(n_exp, d, d_exp), wu_b.reshape(n_exp, d, d_exp),
                  wd_b.reshape(n_exp, d_exp, d), te)
    return _final(dest, x1, gate, p.reshape(t, -1), ys, row(ln2_g), row(ln2_b),
                  w_ple_gate.astype(BF16), row(b_ple_gate), w_ple_proj.astype(BF16), alpha,
                  min(256, t)).reshape(nb, seq, d)


def kernel(x, p, w_in, b_in, conv_w, conv_b, mh_g, w_pool, pool_scale, w_m_br, w_p_br, w_out, ln1_g, ln1_b, w_rg, b_rg, w_re, b_re, w_gate, w_up, w_down, ln2_g, ln2_b, w_ple_gate, b_ple_gate, w_ple_proj):
    depth = w_in.shape[0]
    alpha = (2 * depth) ** 0.25
    for i in range(depth):
        x = _layer(x, p[i], w_in[i], b_in[i], conv_w[i], conv_b[i], mh_g[i], w_pool[i], pool_scale[i],
                   w_m_br[i], w_p_br[i], w_out[i], ln1_g[i], ln1_b[i], w_rg[i], b_rg[i], w_re[i], b_re[i],
                   w_gate[i], w_up[i], w_down[i], ln2_g[i], ln2_b[i], w_ple_gate[i], b_ple_gate[i],
                   w_ple_proj[i], alpha)
    return x
```

```python
import functools

import jax
import jax.numpy as jnp
from jax import lax
from jax.experimental import pallas as pl
from jax.experimental.pallas import tpu as pltpu

F32 = jnp.float32
BF16 = jnp.bfloat16

M_HEADS = 4
M_QK_DIM = 128
M_V_DIM = 256
M_QK = M_HEADS * M_QK_DIM
M_V = M_HEADS * M_V_DIM
CONV_W = 4
CHUNK = 128
POOL_WINDOWS = (2, 4, 8, 16)
POOL_GROUP_DIM = 256
POOL_W = len(POOL_WINDOWS) * POOL_GROUP_DIM
N_GROUPS = 4
EXPERTS_PER_GROUP = 8
N_EXPERTS = N_GROUPS * EXPERTS_PER_GROUP
TOP_K = 2
LN_EPS = 1e-5

LANES = 128
SUBLANES = 8
VMEM_LIMIT = 56 * 1024 * 1024
BF16_SUBLANES = 16
CONV_HALO = BF16_SUBLANES
POOL_HALO = 32
EXPERT_TILE = 256


def _dot(a, b):
    return jnp.dot(a, b, preferred_element_type=F32)


def _params(n_grid):
    return pltpu.CompilerParams(dimension_semantics=("arbitrary",) * n_grid,
                                vmem_limit_bytes=VMEM_LIMIT)


def _log_sigmoid(x):
    return -(jnp.maximum(-x, 0.0) + jnp.log1p(jnp.exp(-jnp.abs(x))))


def _layer_norm(r, g, b):
    mu = jnp.mean(r, axis=-1, keepdims=True)
    d = r - mu
    var = jnp.mean(d * d, axis=-1, keepdims=True)
    return d * lax.rsqrt(var + LN_EPS) * g + b


def _pitch(d):
    return d // LANES + 1


def _store_token_major(ref, val):
    n, d = val.shape
    for c in range(d // LANES):
        ref[pl.ds(c, n, stride=_pitch(d)), :] = val[:, c * LANES:(c + 1) * LANES]
    ref[pl.ds(d // LANES, n, stride=_pitch(d)), :] = jnp.zeros((n, LANES), val.dtype)


def _load_token_major(ref, n, d):
    return jnp.concatenate([ref[pl.ds(c, n, stride=_pitch(d)), :] for c in range(d // LANES)], axis=1)


def _tokens(ref, tok, d, n=1):
    return ref.at[pl.ds(tok * _pitch(d), n * _pitch(d))]


def _inproj_kernel(*refs, n_side):
    x_ref, w_ref, b_ref, wif_ref, bif_ref = refs[:5]
    side_in = refs[5:5 + n_side]
    z_ref, zif_ref = refs[5 + n_side:7 + n_side]
    side_out = refs[7 + n_side:7 + 2 * n_side]
    xb_ref = refs[7 + 2 * n_side]

    @pl.when(pl.program_id(1) == 0)
    def _():
        xb = x_ref[...].astype(BF16)
        xb_ref[...] = xb
        zif_ref[...] = _dot(xb, wif_ref[...]) + bif_ref[...]

    z_ref[...] = (_dot(xb_ref[...], w_ref[...]) + b_ref[...]).astype(BF16)
    for src, dst in zip(side_in, side_out):
        dst[...] = src[...].astype(BF16)


def _inproj(x2d, w_main, b_main, w_if, b_if, tm, tn, side):
    t, d = x2d.shape
    n = w_main.shape[1]
    nj = n // tn
    steps = (t // tm) * nj
    side_spec = lambda a: pl.BlockSpec((a.shape[0] // steps, a.shape[1]), lambda i, j: (i * nj + j, 0))
    return pl.pallas_call(
        functools.partial(_inproj_kernel, n_side=len(side)),
        grid=(t // tm, nj),
        in_specs=[pl.BlockSpec((tm, d), lambda i, j: (i, 0)),
                  pl.BlockSpec((d, tn), lambda i, j: (0, j)),
                  pl.BlockSpec((1, tn), lambda i, j: (0, j)),
                  pl.BlockSpec((d, LANES), lambda i, j: (0, 0)),
                  pl.BlockSpec((1, LANES), lambda i, j: (0, 0))] + [side_spec(a) for a in side],
        out_specs=[pl.BlockSpec((tm, tn), lambda i, j: (i, j)),
                   pl.BlockSpec((tm, LANES), lambda i, j: (i, 0))] + [side_spec(a) for a in side],
        out_shape=[jax.ShapeDtypeStruct((t, n), BF16),
                   jax.ShapeDtypeStruct((t, LANES), F32)] + [jax.ShapeDtypeStruct(a.shape, BF16) for a in side],
        scratch_shapes=[pltpu.VMEM((tm, d), BF16)],
        compiler_params=_params(2),
        name="inproj",
    )(x2d, w_main, b_main, w_if, b_if, *side)


def _mlstm_kernel(qk_ref, v_ref, o_ref, g_ref, cw_ref, cb_ref, mhg_ref, out_ref, cbuf, *state, nb):
    L = CHUNK
    halo = CONV_HALO
    ct_refs, m_ref = state[:M_HEADS], state[M_HEADS]

    @pl.when(pl.program_id(0) == 0)
    def _():
        cbuf[:, 0:halo, :] = jnp.zeros((nb, halo, 2 * M_QK), BF16)
        for ref in state:
            ref[...] = jnp.zeros_like(ref)

    cbuf[:, halo:halo + L, :] = qk_ref[...]
    scale = M_QK_DIM ** -0.5
    row = lax.broadcasted_iota(jnp.int32, (L, L), 0)
    col = lax.broadcasted_iota(jnp.int32, (L, L), 1)
    causal = col <= row
    upper = jnp.where(row <= col, 1.0, 0.0).astype(BF16)
    ones_blk = jnp.ones((L, LANES), BF16)
    lane8 = lax.broadcasted_iota(jnp.int32, (SUBLANES, L), 1)
    wrow = lax.broadcasted_iota(jnp.int32, (L, L + halo), 0)
    wcol = lax.broadcasted_iota(jnp.int32, (L, L + halo), 1)
    shifts = [jnp.where(wcol == wrow + (halo - (CONV_W - 1) + tap), 1.0, 0.0).astype(BF16)
              for tap in range(CONV_W - 1)]
    sub = lax.broadcasted_iota(jnp.int32, (SUBLANES, LANES), 0)

    def per_batch(b, carry):
        window = cbuf[b]
        acc = cb_ref[...] + _dot(shifts[0], window) * cw_ref[0:1, :]
        for tap in range(1, CONV_W - 1):
            acc = acc + _dot(shifts[tap], window) * cw_ref[tap:tap + 1, :]
        acc = acc + qk_ref[b].astype(F32) * cw_ref[CONV_W - 1:CONV_W, :]
        qk = acc * jax.nn.sigmoid(acc)
        cbuf[b, 0:halo, :] = cbuf[b, L:L + halo, :]

        g_row = g_ref[b].T[0:SUBLANES, :]
        lf = _log_sigmoid(g_row)
        lf_hi = lf.astype(BF16)
        lf_lo = (lf - lf_hi.astype(F32)).astype(BF16)
        cs = _dot(jnp.concatenate([lf_hi, lf_lo], axis=0), upper)
        b_row = pltpu.roll(cs[0:SUBLANES, :] + cs[SUBLANES:2 * SUBLANES, :], M_HEADS, 0)
        a_row = g_row - b_row
        cm = a_row
        for sh in (1, 2, 4, 8, 16, 32, 64):
            cm = jnp.maximum(cm, jnp.where(lane8 >= sh, pltpu.roll(cm, sh, 1), -jnp.inf))
        m_prev = m_ref[b]
        mx = jnp.maximum(m_prev, cm)
        mx_last = jnp.maximum(m_prev, jnp.max(a_row, axis=1, keepdims=True))
        carry_in = jnp.exp(m_prev - mx)
        floor = jnp.exp(-(b_row + mx))
        wts = jnp.exp(a_row - mx_last)
        decay = jnp.exp(m_prev - mx_last)
        m_ref[b] = jnp.where(sub < M_HEADS, b_row[:, L - 1:L] + mx_last, 0.0)
        packed = jnp.concatenate(
            [jnp.where(sub < M_HEADS, mx, pltpu.roll(carry_in, M_HEADS, 0)),
             jnp.where(sub < M_HEADS, floor, pltpu.roll(wts, M_HEADS, 0)),
             jnp.zeros((L - 2 * SUBLANES, L), F32)], axis=0).T

        heads = range(M_HEADS)
        vcols = [slice(h * M_V_DIM, (h + 1) * M_V_DIM) for h in heads]
        q = [qk[:, h * M_QK_DIM:(h + 1) * M_QK_DIM].astype(BF16) for h in heads]
        kf = [qk[:, M_QK + h * M_QK_DIM:M_QK + (h + 1) * M_QK_DIM] * scale for h in heads]
        v_ext = [jnp.concatenate([v_ref[b, :, vcols[h]], ones_blk], axis=1) for h in heads]
        ct = [ct_refs[h][b] for h in heads]
        col_of = lambda i, h: packed[:, i * M_HEADS + h:i * M_HEADS + h + 1]

        raw = [lax.dot_general(q[h], kf[h].astype(BF16), (((1,), (1,)), ((), ())),
                               preferred_element_type=F32) for h in heads]
        prev = [_dot(q[h], ct[h].astype(BF16)) for h in heads]
        pmat = [jnp.exp(jnp.where(causal, a_row[h:h + 1, :] - col_of(0, h), -jnp.inf)) for h in heads]
        s = [(raw[h] * pmat[h]).astype(BF16) for h in heads]
        numden = [_dot(s[h], v_ext[h]) + col_of(1, h) * prev[h] for h in heads]
        upd = [lax.dot_general((kf[h] * col_of(3, h)).astype(BF16), v_ext[h], (((0,), (0,)), ((), ())),
                               preferred_element_type=F32) for h in heads]
        for h in heads:
            dec = jnp.concatenate([decay[h:h + 1, :]] * (ct[h].shape[1] // LANES), axis=1)
            ct_refs[h][b] = dec * ct[h] + upd[h]
        inv = [1.0 / jnp.maximum(jnp.abs(numden[h][:, M_V_DIM:]), col_of(2, h)) for h in heads]
        hh = [numden[h][:, 0:M_V_DIM] * jnp.concatenate([inv[h]] * (M_V_DIM // LANES), axis=1) for h in heads]
        mu = [jnp.mean(hh[h], axis=-1, keepdims=True) for h in heads]
        dlt = [hh[h] - mu[h] for h in heads]
        var = [jnp.mean(dlt[h] * dlt[h], axis=-1, keepdims=True) for h in heads]
        for h in heads:
            hn = dlt[h] * lax.rsqrt(var[h] + LN_EPS) * mhg_ref[:, vcols[h]]
            og = jax.nn.sigmoid(o_ref[b, :, vcols[h]].astype(F32))
            out_ref[b, :, vcols[h]] = (hn * og).astype(BF16)
        return carry

    for b in range(nb):
        per_batch(b, 0)


def _mlstm(z3, zif3, conv_w, conv_b, mh_g):
    nb, s, _ = z3.shape
    L = CHUNK
    blk = lambda colblk: pl.BlockSpec((nb, L, M_V), lambda c: (0, c, colblk))
    const = lambda shape: pl.BlockSpec(shape, lambda c: (0,) * len(shape))
    return pl.pallas_call(
        functools.partial(_mlstm_kernel, nb=nb),
        grid=(s // L,),
        in_specs=[blk(0), blk(1), blk(2),
                  pl.BlockSpec((nb, L, LANES), lambda c: (0, c, 0)),
                  const((CONV_W, 2 * M_QK)), const((1, 2 * M_QK)), const((1, M_V))],
        out_specs=pl.BlockSpec((nb, L, M_V), lambda c: (0, c, 0)),
        out_shape=jax.ShapeDtypeStruct((nb, s, M_V), BF16),
        scratch_shapes=([pltpu.VMEM((nb, L + CONV_HALO, 2 * M_QK), BF16)]
                        + [pltpu.VMEM((nb, M_QK_DIM, M_V_DIM + LANES), F32)] * M_HEADS
                        + [pltpu.VMEM((nb, SUBLANES, L), F32)]),
        compiler_params=_params(1),
        name="mlstm",
    )(z3, z3, z3, zif3, conv_w, conv_b, mh_g)


def _branch_kernel(hg_ref, u_ref, uh_ref, gm_ref, gp_ref, wpool_ref, ps_ref, wm_ref, wp_ref,
                   out_ref, xa, xb, yp_ref, *, tm, seq):
    H = POOL_HALO
    G = POOL_GROUP_DIM
    t0 = lax.rem(pl.program_id(0) * tm, seq)
    u = u_ref[...].astype(F32)
    xa[H:H + tm, :] = u
    xa[0:H, :] = jnp.where(t0 == 0, 0.0, uh_ref[...].astype(F32))
    n = tm + H - 8
    xb[8:8 + n, :] = xa[8:8 + n, :] + xa[7:7 + n, :]
    n = tm + H - 16
    xa[16:16 + n, G:] = xb[16:16 + n, G:] + xb[14:14 + n, G:]
    n = tm + H - 24
    xb[24:24 + n, 2 * G:] = xa[24:24 + n, 2 * G:] + xa[20:20 + n, 2 * G:]
    xa[H:H + tm, 3 * G:] = xb[H:H + tm, 3 * G:] + xb[H - 8:H - 8 + tm, 3 * G:]
    tpos = t0 + lax.broadcasted_iota(jnp.int32, (tm, 1), 0)
    for g, win in enumerate(POOL_WINDOWS):
        src = (xb, xa, xb, xa)[g]
        cols = slice(g * G, (g + 1) * G)
        cnt = jnp.minimum(tpos + 1, win).astype(F32)
        y = src[H:H + tm, cols] / cnt - u[:, cols]
        yp = _dot(y.astype(BF16), wpool_ref[g]) * ps_ref[:, cols]
        yp_ref[:, cols] = yp.astype(BF16)
    pb = _dot(yp_ref[...], wp_ref[...])
    a = _dot(hg_ref[...], wm_ref[...])
    merged = (jax.nn.sigmoid(gm_ref[...].astype(F32)) * a
              + jax.nn.sigmoid(gp_ref[...].astype(F32)) * pb)
    out_ref[...] = merged.astype(BF16)


def _branch(hg2d, z_main, w_pool, pool_scale, w_m_br, w_p_br, seq, tm):
    t = hg2d.shape[0]
    d = w_m_br.shape[1]
    hb = tm // POOL_HALO
    const = lambda shape: pl.BlockSpec(shape, lambda i: (0,) * len(shape))
    return pl.pallas_call(
        functools.partial(_branch_kernel, tm=tm, seq=seq),
        grid=(t // tm,),
        in_specs=[pl.BlockSpec((tm, M_V), lambda i: (i, 0)),
                  pl.BlockSpec((tm, POOL_W), lambda i: (i, 3)),
                  pl.BlockSpec((POOL_HALO, POOL_W), lambda i: (jnp.maximum(i * hb - 1, 0), 3)),
                  pl.BlockSpec((tm, d), lambda i: (i, 2)),
                  pl.BlockSpec((tm, d), lambda i: (i, 3)),
                  const(w_pool.shape), const((1, POOL_W)), const(w_m_br.shape), const(w_p_br.shape)],
        out_specs=pl.BlockSpec((tm, d), lambda i: (i, 0)),
        out_shape=jax.ShapeDtypeStruct((t, d), BF16),
        scratch_shapes=[pltpu.VMEM((tm + POOL_HALO, POOL_W), F32),
                        pltpu.VMEM((tm + POOL_HALO, POOL_W), F32),
                        pltpu.VMEM((tm, POOL_W), BF16)],
        compiler_params=_params(1),
        name="branch",
    )(hg2d, z_main, z_main, z_main, z_main, w_pool, pool_scale, w_m_br, w_p_br)


def _route_kernel(mg_ref, x_ref, wout_ref, g_ref, b_ref, wr_ref, br_ref, tri_ref,
                  x1_ref, route_ref, gate_ref, cnt_ref, carry, *, alpha, tm):
    @pl.when(pl.program_id(0) == 0)
    def _():
        carry[...] = jnp.zeros_like(carry)

    r = alpha * x_ref[...] + _dot(mg_ref[...], wout_ref[...])
    x1 = _layer_norm(r, g_ref[...], b_ref[...])
    _store_token_major(x1_ref, x1)

    hi = x1.astype(BF16)
    lo = (x1 - hi.astype(F32)).astype(BF16)
    p1 = _dot(hi, wr_ref[...])
    p2 = _dot(lo, wr_ref[:, 0:LANES])
    logits = p1[:, 0:LANES] + p1[:, LANES:2 * LANES] + p2 + br_ref[...]

    lane = lax.broadcasted_iota(jnp.int32, (tm, LANES), 1)
    lanef = lane.astype(F32)
    big = float(LANES)

    def softmax_masked(mask):
        z = jnp.where(mask, logits, -jnp.inf)
        e = jnp.exp(z - jnp.max(z, axis=1, keepdims=True))
        return e / jnp.sum(e, axis=1, keepdims=True)

    def top1(vals):
        top = jnp.max(vals, axis=1, keepdims=True)
        idx = jnp.min(jnp.where(vals == top, lanef, big), axis=1, keepdims=True)
        return top, idx

    is_grp = lane < N_GROUPS
    pg = jnp.where(is_grp, softmax_masked(is_grp), -1.0)
    pg_top, g_idx = top1(pg)
    e_lo = N_GROUPS + EXPERTS_PER_GROUP * g_idx.astype(jnp.int32)
    in_grp = (lane >= e_lo) & (lane < e_lo + EXPERTS_PER_GROUP)
    pe = jnp.where(in_grp, softmax_masked(in_grp), -1.0)
    pe1, i1 = top1(pe)
    pe2, i2 = top1(jnp.where(lanef == i1, -1.0, pe))
    den = pe1 + pe2
    gate1 = pg_top * pe1 / den
    gate2 = pg_top * pe2 / den
    e1 = i1 - float(N_GROUPS)
    e2 = i2 - float(N_GROUPS)

    hit1 = lanef == e1
    hit2 = lanef == e2
    onehot = jnp.where(hit1 | hit2, 1.0, 0.0)
    incl = _dot(tri_ref[...], onehot.astype(BF16))
    excl = incl - onehot + carry[0:1, :]
    r1 = jnp.sum(jnp.where(hit1, excl, 0.0), axis=1, keepdims=True)
    r2 = jnp.sum(jnp.where(hit2, excl, 0.0), axis=1, keepdims=True)
    carry[...] = carry[...] + incl[tm - 1:tm, :]
    cnt_ref[...] = carry[...]

    route = jnp.where(lane == 0, e1, jnp.where(lane == 1, e2, jnp.where(lane == 2, r1, jnp.where(lane == 3, r2, 0.0))))
    route_ref[...] = route.astype(jnp.int32)
    gate_ref[...] = jnp.where(lane == 0, gate1, jnp.where(lane == 1, gate2, 0.0))


def _route(merged, x2d, w_out, ln_g, ln_b, w_r, b_r, tri, alpha, tm):
    t, d = x2d.shape
    const = lambda shape: pl.BlockSpec(shape, lambda i: (0,) * len(shape))
    rows = lambda width: pl.BlockSpec((tm, width), lambda i: (i, 0))
    return pl.pallas_call(
        functools.partial(_route_kernel, alpha=alpha, tm=tm),
        grid=(t // tm,),
        in_specs=[rows(d), rows(d), const(w_out.shape), const((1, d)), const((1, d)),
                  const(w_r.shape), const((1, LANES)), const((tm, tm))],
        out_specs=[pl.BlockSpec((tm * _pitch(d), LANES), lambda i: (i, 0)),
                   rows(LANES), rows(LANES), const((SUBLANES, LANES))],
        out_shape=[jax.ShapeDtypeStruct((t * _pitch(d), LANES), F32),
                   jax.ShapeDtypeStruct((t, LANES), jnp.int32),
                   jax.ShapeDtypeStruct((t, LANES), F32),
                   jax.ShapeDtypeStruct((SUBLANES, LANES), F32)],
        scratch_shapes=[pltpu.VMEM((SUBLANES, LANES), F32)],
        compiler_params=_params(1),
        name="route",
    )(merged, x2d, w_out, ln_g, ln_b, w_r, b_r, tri)


def _dispatch_kernel(pad_start_ref, pad_len_ref, nu_ref, dest_ref, x_ref, xs_hbm, sem, *, tb, te, d):
    def copy(src_tok, dst_tok, n=1):
        return pltpu.make_async_copy(_tokens(x_ref, src_tok, d, n), _tokens(xs_hbm, dst_tok, d, n), sem)

    def wait_tokens(n):
        def body(r, c):
            copy(0, 0, n).wait()
            return c
        return body

    @pl.when(pl.program_id(0) == 0)
    def _():
        def per_expert(e, total):
            start = pad_start_ref[e]
            npad = pad_len_ref[e]

            def fill(r, c):
                copy(0, start + r).start()
                return c

            lax.fori_loop(0, npad, fill, 0)
            return total + npad

        total = lax.fori_loop(0, N_EXPERTS, per_expert, 0)
        lax.fori_loop(0, total // SUBLANES, wait_tokens(SUBLANES), 0)
        lax.fori_loop(0, total % SUBLANES, wait_tokens(1), 0)

        def fill_tile(j, c):
            copy(0, j * te, te).start()
            return c

        n_tiles = xs_hbm.shape[0] // (te * _pitch(d))
        lax.fori_loop(nu_ref[0], n_tiles, fill_tile, 0)
        lax.fori_loop(nu_ref[0], n_tiles, wait_tokens(te), 0)

    def scatter(t, c):
        for k in range(TOP_K):
            copy(t, dest_ref[TOP_K * t + k]).start()
        return c

    lax.fori_loop(0, tb, scatter, 0)
    for k in range(TOP_K):
        copy(0, 0, tb).wait()


def _dispatch(x1t, dest_flat, pad_start, pad_len, n_used, n_slots, tb, te, d):
    pitch = _pitch(d)
    t = x1t.shape[0] // pitch
    grid_spec = pltpu.PrefetchScalarGridSpec(
        num_scalar_prefetch=3,
        grid=(t // tb,),
        in_specs=[pl.BlockSpec((TOP_K * tb,), lambda i, ps, pn, nu: (i,), memory_space=pltpu.SMEM),
                  pl.BlockSpec((tb * pitch, LANES), lambda i, ps, pn, nu: (i, 0))],
        out_specs=pl.BlockSpec(memory_space=pl.ANY),
        scratch_shapes=[pltpu.SemaphoreType.DMA(())],
    )
    return pl.pallas_call(
        functools.partial(_dispatch_kernel, tb=tb, te=te, d=d),
        grid_spec=grid_spec,
        out_shape=jax.ShapeDtypeStruct((n_slots * pitch, LANES), x1t.dtype),
        compiler_params=_params(1),
        name="dispatch",
    )(pad_start, pad_len, n_used, dest_flat, x1t)


def _expert_kernel(be_ref, nu_ref, x_ref, wg_ref, wu_ref, wd_ref, y_ref, *, te, d):
    i = pl.program_id(0)

    @pl.when(i < nu_ref[0])
    def _():
        xb = _load_token_major(x_ref, te, d).astype(BF16)
        gt = _dot(xb, wg_ref[0])
        up = _dot(xb, wu_ref[0])
        hid = gt * jax.nn.sigmoid(gt) * up
        _store_token_major(y_ref, _dot(hid.astype(BF16), wd_ref[0]))

    @pl.when(i >= nu_ref[0])
    def _():
        y_ref[...] = jnp.zeros_like(y_ref)


def _experts(xs, block_e, n_used, w_gate, w_up, w_down, te):
    _, d, de = w_gate.shape
    pitch = _pitch(d)
    n_tiles = xs.shape[0] // (te * pitch)
    grid_spec = pltpu.PrefetchScalarGridSpec(
        num_scalar_prefetch=2,
        grid=(n_tiles,),
        in_specs=[pl.BlockSpec((te * pitch, LANES), lambda i, be, nu: (jnp.minimum(i, nu[0] - 1), 0)),
                  pl.BlockSpec((1, d, de), lambda i, be, nu: (be[i], 0, 0)),
                  pl.BlockSpec((1, d, de), lambda i, be, nu: (be[i], 0, 0)),
                  pl.BlockSpec((1, de, d), lambda i, be, nu: (be[i], 0, 0))],
        out_specs=pl.BlockSpec((te * pitch, LANES), lambda i, be, nu: (i, 0)),
    )
    return pl.pallas_call(
        functools.partial(_expert_kernel, te=te, d=d),
        grid_spec=grid_spec,
        out_shape=jax.ShapeDtypeStruct(xs.shape, F32),
        compiler_params=_params(1),
        name="experts",
    )(block_e, n_used, xs, w_gate, w_up, w_down)


def _final_kernel(dcur_ref, dnxt_ref, x1_ref, gate_ref, p_ref, ys_hbm, g_ref, b_ref, wg_ref, bg_ref, wp_ref,
                  out_ref, ybuf, sem, *, alpha, tm, d):
    i = pl.program_id(0)
    slot = lax.rem(i, 2)

    def issue(dest_ref, s):
        def body(t, c):
            for k in range(TOP_K):
                pltpu.make_async_copy(_tokens(ys_hbm, dest_ref[TOP_K * t + k], d),
                                      _tokens(ybuf.at[s, k], t, d), sem.at[s]).start()
            return c
        lax.fori_loop(0, tm, body, 0)

    @pl.when(i == 0)
    def _():
        issue(dcur_ref, 0)

    @pl.when(i + 1 < pl.num_programs(0))
    def _():
        issue(dnxt_ref, 1 - slot)

    pp = _dot(p_ref[...].astype(BF16), wp_ref[...])
    x1 = _load_token_major(x1_ref, tm, d)
    for k in range(TOP_K):
        pltpu.make_async_copy(_tokens(ys_hbm, 0, d, tm), ybuf.at[slot, k], sem.at[slot]).wait()

    gate = gate_ref[...]
    y = (gate[:, 0:1] * _load_token_major(ybuf.at[slot, 0], tm, d)
         + gate[:, 1:2] * _load_token_major(ybuf.at[slot, 1], tm, d))
    x2 = _layer_norm(alpha * x1 + y, g_ref[...], b_ref[...])
    gl = _dot(x2.astype(BF16), wg_ref[...]) + bg_ref[...]
    out_ref[...] = x2 + jax.nn.sigmoid(gl) * pp


def _final(dest_flat, x1t, gate, p2d, ys, ln_g, ln_b, w_pg, b_pg, w_pp, alpha, tm):
    d = w_pg.shape[0]
    pitch = _pitch(d)
    t = x1t.shape[0] // pitch
    n = t // tm
    const = lambda shape: pl.BlockSpec(shape, lambda i: (0,) * len(shape))
    rows = lambda width: pl.BlockSpec((tm, width), lambda i: (i, 0))
    return pl.pallas_call(
        functools.partial(_final_kernel, alpha=alpha, tm=tm, d=d),
        grid=(n,),
        in_specs=[pl.BlockSpec((TOP_K * tm,), lambda i: (i,), memory_space=pltpu.SMEM),
                  pl.BlockSpec((TOP_K * tm,), lambda i: (jnp.minimum(i + 1, n - 1),), memory_space=pltpu.SMEM),
                  pl.BlockSpec((tm * pitch, LANES), lambda i: (i, 0)),
                  rows(LANES), rows(p2d.shape[1]),
                  pl.BlockSpec(memory_space=pl.ANY),
                  const((1, d)), const((1, d)), const(w_pg.shape), const((1, d)), const(w_pp.shape)],
        out_specs=rows(d),
        out_shape=jax.ShapeDtypeStruct((t, d), F32),
        scratch_shapes=[pltpu.VMEM((2, TOP_K, tm * pitch, LANES), F32), pltpu.SemaphoreType.DMA((2,))],
        compiler_params=_params(1),
        name="final",
    )(dest_flat, dest_flat, x1t, gate, p2d, ys, ln_g, ln_b, w_pg, b_pg, w_pp)


def _pad_cols(a, width):
    return jnp.pad(a, ((0, 0), (0, width - a.shape[1])))


def _tri(n):
    return jnp.tril(jnp.ones((n, n), BF16))


def _layer(x, p, w_in, b_in, conv_w, conv_b, mh_g, w_pool, pool_scale, w_m_br, w_p_br, w_out,
           ln1_g, ln1_b, w_rg, b_rg, w_re, b_re, w_gate, w_up, w_down, ln2_g, ln2_b,
           w_ple_gate, b_ple_gate, w_ple_proj, alpha):
    nb, seq, d = x.shape
    t = nb * seq
    x2d = x.reshape(t, d)
    row = lambda a: a.reshape(1, -1)

    c_if = 2 * M_QK + 2 * M_V
    w_main = jnp.concatenate([w_in[:, :c_if].astype(BF16), w_in[:, c_if + 2 * M_HEADS:].astype(BF16)], axis=1)
    b_main = row(jnp.concatenate([b_in[:c_if], b_in[c_if + 2 * M_HEADS:]]))
    w_if = _pad_cols(w_in[:, c_if:c_if + 2 * M_HEADS], LANES).astype(BF16)
    b_if = _pad_cols(row(b_in[c_if:c_if + 2 * M_HEADS]), LANES)

    n_exp, _, d_exp = w_gate.shape
    expert_w = [w_gate.reshape(n_exp * d, d_exp), w_up.reshape(n_exp * d, d_exp), w_down.reshape(n_exp * d_exp, d)]
    tm_in = min(1024, t)
    z_main, z_if, wg_b, wu_b, wd_b = _inproj(x2d, w_main, b_main, w_if, b_if, tm_in, 1024, expert_w)

    hg = _mlstm(z_main.reshape(nb, seq, -1), z_if.reshape(nb, seq, LANES),
                conv_w, row(conv_b), row(mh_g))

    tm = min(512, seq)
    merged = _branch(hg.reshape(t, M_V), z_main, w_pool.astype(BF16), row(pool_scale),
                     w_m_br.astype(BF16), w_p_br.astype(BF16), seq, tm)

    w_r = _pad_cols(jnp.concatenate([w_rg, w_re], axis=1), LANES)
    w_r_hi = w_r.astype(BF16)
    w_r_lo = (w_r - w_r_hi.astype(F32)).astype(BF16)
    b_r = _pad_cols(row(jnp.concatenate([b_rg, b_re])), LANES)
    x1, route, gate, cnt = _route(merged, x2d, w_out.astype(BF16), row(ln1_g), row(ln1_b),
                                  jnp.concatenate([w_r_hi, w_r_lo], axis=1), b_r, _tri(tm), alpha, tm)

    te = EXPERT_TILE
    counts = cnt[0, :N_EXPERTS].astype(jnp.int32)
    pcounts = (counts + te - 1) // te * te
    pends = jnp.cumsum(pcounts)
    pstarts = pends - pcounts
    e_sel = route[:, 0:TOP_K, None] == jnp.arange(N_EXPERTS, dtype=jnp.int32)
    dest = (jnp.sum(jnp.where(e_sel, pstarts, 0), axis=-1) + route[:, TOP_K:2 * TOP_K]).reshape(-1)
    n_slots = t * TOP_K + N_EXPERTS * te
    n_tiles = n_slots // te
    n_used = (pends[-1] // te).reshape(1)
    tile_row = jnp.minimum(jnp.arange(n_tiles, dtype=jnp.int32), n_used - 1) * te
    block_e = jnp.minimum(jnp.sum((tile_row[:, None] >= pends[None, :]).astype(jnp.int32), axis=1), N_EXPERTS - 1)

    xs = _dispatch(x1, dest, pstarts + counts, pcounts - counts, n_used, n_slots, min(512, t), te, d)
    ys = _experts(xs, block_e, n_used, wg_b.reshape(n_exp, d, d_exp), wu_b.reshape(n_exp, d, d_exp),
                  wd_b.reshape(n_exp, d_exp, d), te)
    return _final(dest, x1, gate, p.reshape(t, -1), ys, row(ln2_g), row(ln2_b),
                  w_ple_gate.astype(BF16), row(b_ple_gate), w_ple_proj.astype(BF16), alpha,
                  min(256, t)).reshape(nb, seq, d)


def kernel(x, p, w_in, b_in, conv_w, conv_b, mh_g, w_pool, pool_scale, w_m_br, w_p_br, w_out, ln1_g, ln1_b, w_rg, b_rg, w_re, b_re, w_gate, w_up, w_down, ln2_g, ln2_b, w_ple_gate, b_ple_gate, w_ple_proj):
    depth = w_in.shape[0]
    alpha = (2 * depth) ** 0.25
    for i in range(depth):
        x = _layer(x, p[i], w_in[i], b_in[i], conv_w[i], conv_b[i], mh_g[i], w_pool[i], pool_scale[i],
                   w_m_br[i], w_p_br[i], w_out[i], ln1_g[i], ln1_b[i], w_rg[i], b_rg[i], w_re[i], b_re[i],
                   w_gate[i], w_up[i], w_down[i], ln2_g[i], ln2_b[i], w_ple_gate[i], b_ple_gate[i],
                   w_ple_proj[i], alpha)
    return x
```

```python
import functools

import jax
import jax.numpy as jnp
from jax import lax
from jax.experimental import pallas as pl
from jax.experimental.pallas import tpu as pltpu

F32 = jnp.float32
BF16 = jnp.bfloat16

M_HEADS = 4
M_QK_DIM = 128
M_V_DIM = 256
M_QK = M_HEADS * M_QK_DIM
M_V = M_HEADS * M_V_DIM
CONV_W = 4
CHUNK = 128
POOL_WINDOWS = (2, 4, 8, 16)
POOL_GROUP_DIM = 256
POOL_W = len(POOL_WINDOWS) * POOL_GROUP_DIM
N_GROUPS = 4
EXPERTS_PER_GROUP = 8
N_EXPERTS = N_GROUPS * EXPERTS_PER_GROUP
TOP_K = 2
LN_EPS = 1e-5

LANES = 128
SUBLANES = 8
VMEM_LIMIT = 56 * 1024 * 1024
BF16_SUBLANES = 16
CONV_HALO = BF16_SUBLANES
POOL_HALO = 32
EXPERT_TILE = 256
DMA_UNROLL = 8
ROW_SPLIT = 2


def _dot(a, b):
    return jnp.dot(a, b, preferred_element_type=F32)


def _params(n_grid):
    return pltpu.CompilerParams(dimension_semantics=("arbitrary",) * n_grid,
                                vmem_limit_bytes=VMEM_LIMIT)


def _log_sigmoid(x):
    return -(jnp.maximum(-x, 0.0) + jnp.log1p(jnp.exp(-jnp.abs(x))))


def _layer_norm(r, g, b):
    mu = jnp.mean(r, axis=-1, keepdims=True)
    d = r - mu
    var = jnp.mean(d * d, axis=-1, keepdims=True)
    return d * lax.rsqrt(var + LN_EPS) * g + b


def _pitch(d):
    return d // LANES + 1


def _store_token_major(ref, val):
    n, d = val.shape
    for c in range(d // LANES):
        ref[pl.ds(c, n, stride=_pitch(d)), :] = val[:, c * LANES:(c + 1) * LANES]
    ref[pl.ds(d // LANES, n, stride=_pitch(d)), :] = jnp.zeros((n, LANES), val.dtype)


def _load_token_major(ref, n, d):
    return jnp.concatenate([ref[pl.ds(c, n, stride=_pitch(d)), :] for c in range(d // LANES)], axis=1)


def _tokens(ref, tok, d, n=1):
    return ref.at[pl.ds(tok * _pitch(d), n * _pitch(d))]


def _inproj_kernel(*refs, n_side):
    x_ref, w_ref, b_ref, wif_ref, bif_ref = refs[:5]
    side_in = refs[5:5 + n_side]
    z_ref, zif_ref = refs[5 + n_side:7 + n_side]
    side_out = refs[7 + n_side:7 + 2 * n_side]
    xb_ref = refs[7 + 2 * n_side]

    @pl.when(pl.program_id(1) == 0)
    def _():
        xb = x_ref[...].astype(BF16)
        xb_ref[...] = xb
        zif_ref[...] = _dot(xb, wif_ref[...]) + bif_ref[...]

    z_ref[...] = (_dot(xb_ref[...], w_ref[...]) + b_ref[...]).astype(BF16)
    for src, dst in zip(side_in, side_out):
        dst[...] = src[...].astype(BF16)


def _inproj(x2d, w_main, b_main, w_if, b_if, tm, tn, side):
    t, d = x2d.shape
    n = w_main.shape[1]
    nj = n // tn
    steps = (t // tm) * nj
    side_spec = lambda a: pl.BlockSpec((a.shape[0] // steps, a.shape[1]), lambda i, j: (i * nj + j, 0))
    return pl.pallas_call(
        functools.partial(_inproj_kernel, n_side=len(side)),
        grid=(t // tm, nj),
        in_specs=[pl.BlockSpec((tm, d), lambda i, j: (i, 0)),
                  pl.BlockSpec((d, tn), lambda i, j: (0, j)),
                  pl.BlockSpec((1, tn), lambda i, j: (0, j)),
                  pl.BlockSpec((d, LANES), lambda i, j: (0, 0)),
                  pl.BlockSpec((1, LANES), lambda i, j: (0, 0))] + [side_spec(a) for a in side],
        out_specs=[pl.BlockSpec((tm, tn), lambda i, j: (i, j)),
                   pl.BlockSpec((tm, LANES), lambda i, j: (i, 0))] + [side_spec(a) for a in side],
        out_shape=[jax.ShapeDtypeStruct((t, n), BF16),
                   jax.ShapeDtypeStruct((t, LANES), F32)] + [jax.ShapeDtypeStruct(a.shape, BF16) for a in side],
        scratch_shapes=[pltpu.VMEM((tm, d), BF16)],
        compiler_params=_params(2),
        name="inproj",
    )(x2d, w_main, b_main, w_if, b_if, *side)


def _mlstm_kernel(qk_ref, v_ref, o_ref, g_ref, cw_ref, cb_ref, mhg_ref, out_ref, cbuf, *state, nb):
    L = CHUNK
    halo = CONV_HALO
    ct_refs, m_ref = state[:M_HEADS], state[M_HEADS]

    @pl.when(pl.program_id(0) == 0)
    def _():
        cbuf[:, 0:halo, :] = jnp.zeros((nb, halo, 2 * M_QK), BF16)
        for ref in state:
            ref[...] = jnp.zeros_like(ref)

    cbuf[:, halo:halo + L, :] = qk_ref[...]
    scale = M_QK_DIM ** -0.5
    row = lax.broadcasted_iota(jnp.int32, (L, L), 0)
    col = lax.broadcasted_iota(jnp.int32, (L, L), 1)
    causal = col <= row
    upper = jnp.where(row <= col, 1.0, 0.0).astype(BF16)
    ones_blk = jnp.ones((L, LANES), BF16)
    lane8 = lax.broadcasted_iota(jnp.int32, (SUBLANES, L), 1)
    wrow = lax.broadcasted_iota(jnp.int32, (L, L + halo), 0)
    wcol = lax.broadcasted_iota(jnp.int32, (L, L + halo), 1)
    shifts = [jnp.where(wcol == wrow + (halo - (CONV_W - 1) + tap), 1.0, 0.0).astype(BF16)
              for tap in range(CONV_W - 1)]
    sub = lax.broadcasted_iota(jnp.int32, (SUBLANES, LANES), 0)

    def per_batch(b, carry):
        window = cbuf[b]
        acc = cb_ref[...] + _dot(shifts[0], window) * cw_ref[0:1, :]
        for tap in range(1, CONV_W - 1):
            acc = acc + _dot(shifts[tap], window) * cw_ref[tap:tap + 1, :]
        acc = acc + qk_ref[b].astype(F32) * cw_ref[CONV_W - 1:CONV_W, :]
        qk = acc * jax.nn.sigmoid(acc)
        cbuf[b, 0:halo, :] = cbuf[b, L:L + halo, :]

        g_row = g_ref[b].T[0:SUBLANES, :]
        lf = _log_sigmoid(g_row)
        lf_hi = lf.astype(BF16)
        lf_lo = (lf - lf_hi.astype(F32)).astype(BF16)
        cs = _dot(jnp.concatenate([lf_hi, lf_lo], axis=0), upper)
        b_row = pltpu.roll(cs[0:SUBLANES, :] + cs[SUBLANES:2 * SUBLANES, :], M_HEADS, 0)
        a_row = g_row - b_row
        cm = a_row
        for sh in (1, 2, 4, 8, 16, 32, 64):
            cm = jnp.maximum(cm, jnp.where(lane8 >= sh, pltpu.roll(cm, sh, 1), -jnp.inf))
        m_prev = m_ref[b]
        mx = jnp.maximum(m_prev, cm)
        mx_last = jnp.maximum(m_prev, jnp.max(a_row, axis=1, keepdims=True))
        carry_in = jnp.exp(m_prev - mx)
        floor = jnp.exp(-(b_row + mx))
        wts = jnp.exp(a_row - mx_last)
        decay = jnp.exp(m_prev - mx_last)
        m_ref[b] = jnp.where(sub < M_HEADS, b_row[:, L - 1:L] + mx_last, 0.0)
        packed = jnp.concatenate(
            [jnp.where(sub < M_HEADS, mx, pltpu.roll(carry_in, M_HEADS, 0)),
             jnp.where(sub < M_HEADS, floor, pltpu.roll(wts, M_HEADS, 0)),
             jnp.zeros((L - 2 * SUBLANES, L), F32)], axis=0).T

        heads = range(M_HEADS)
        vcols = [slice(h * M_V_DIM, (h + 1) * M_V_DIM) for h in heads]
        q = [qk[:, h * M_QK_DIM:(h + 1) * M_QK_DIM].astype(BF16) for h in heads]
        kf = [qk[:, M_QK + h * M_QK_DIM:M_QK + (h + 1) * M_QK_DIM] * scale for h in heads]
        v_ext = [jnp.concatenate([v_ref[b, :, vcols[h]], ones_blk], axis=1) for h in heads]
        ct = [ct_refs[h][b] for h in heads]
        col_of = lambda i, h: packed[:, i * M_HEADS + h:i * M_HEADS + h + 1]

        raw = [lax.dot_general(q[h], kf[h].astype(BF16), (((1,), (1,)), ((), ())),
                               preferred_element_type=F32) for h in heads]
        prev = [_dot(q[h], ct[h].astype(BF16)) for h in heads]
        pmat = [jnp.exp(jnp.where(causal, a_row[h:h + 1, :] - col_of(0, h), -jnp.inf)) for h in heads]
        s = [(raw[h] * pmat[h]).astype(BF16) for h in heads]
        numden = [_dot(s[h], v_ext[h]) + col_of(1, h) * prev[h] for h in heads]
        upd = [lax.dot_general((kf[h] * col_of(3, h)).astype(BF16), v_ext[h], (((0,), (0,)), ((), ())),
                               preferred_element_type=F32) for h in heads]
        for h in heads:
            dec = jnp.concatenate([decay[h:h + 1, :]] * (ct[h].shape[1] // LANES), axis=1)
            ct_refs[h][b] = dec * ct[h] + upd[h]
        inv = [1.0 / jnp.maximum(jnp.abs(numden[h][:, M_V_DIM:]), col_of(2, h)) for h in heads]
        hh = [numden[h][:, 0:M_V_DIM] * jnp.concatenate([inv[h]] * (M_V_DIM // LANES), axis=1) for h in heads]
        mu = [jnp.mean(hh[h], axis=-1, keepdims=True) for h in heads]
        dlt = [hh[h] - mu[h] for h in heads]
        var = [jnp.mean(dlt[h] * dlt[h], axis=-1, keepdims=True) for h in heads]
        for h in heads:
            hn = dlt[h] * lax.rsqrt(var[h] + LN_EPS) * mhg_ref[:, vcols[h]]
            og = jax.nn.sigmoid(o_ref[b, :, vcols[h]].astype(F32))
            out_ref[b, :, vcols[h]] = (hn * og).astype(BF16)
        return carry

    for b in range(nb):
        per_batch(b, 0)


def _mlstm(z3, zif3, conv_w, conv_b, mh_g):
    nb, s, _ = z3.shape
    L = CHUNK
    blk = lambda colblk: pl.BlockSpec((nb, L, M_V), lambda c: (0, c, colblk))
    const = lambda shape: pl.BlockSpec(shape, lambda c: (0,) * len(shape))
    return pl.pallas_call(
        functools.partial(_mlstm_kernel, nb=nb),
        grid=(s // L,),
        in_specs=[blk(0), blk(1), blk(2),
                  pl.BlockSpec((nb, L, LANES), lambda c: (0, c, 0)),
                  const((CONV_W, 2 * M_QK)), const((1, 2 * M_QK)), const((1, M_V))],
        out_specs=pl.BlockSpec((nb, L, M_V), lambda c: (0, c, 0)),
        out_shape=jax.ShapeDtypeStruct((nb, s, M_V), BF16),
        scratch_shapes=([pltpu.VMEM((nb, L + CONV_HALO, 2 * M_QK), BF16)]
                        + [pltpu.VMEM((nb, M_QK_DIM, M_V_DIM + LANES), F32)] * M_HEADS
                        + [pltpu.VMEM((nb, SUBLANES, L), F32)]),
        compiler_params=_params(1),
        name="mlstm",
    )(z3, z3, z3, zif3, conv_w, conv_b, mh_g)


def _branch_kernel(hg_ref, u_ref, uh_ref, gm_ref, gp_ref, wpool_ref, ps_ref, wm_ref, wp_ref,
                   out_ref, xa, xb, yp_ref, *, tm, seq):
    H = POOL_HALO
    G = POOL_GROUP_DIM
    t0 = lax.rem(pl.program_id(0) * tm, seq)
    u = u_ref[...].astype(F32)
    xa[H:H + tm, :] = u
    xa[0:H, :] = jnp.where(t0 == 0, 0.0, uh_ref[...].astype(F32))
    n = tm + H - 8
    xb[8:8 + n, :] = xa[8:8 + n, :] + xa[7:7 + n, :]
    n = tm + H - 16
    xa[16:16 + n, G:] = xb[16:16 + n, G:] + xb[14:14 + n, G:]
    n = tm + H - 24
    xb[24:24 + n, 2 * G:] = xa[24:24 + n, 2 * G:] + xa[20:20 + n, 2 * G:]
    xa[H:H + tm, 3 * G:] = xb[H:H + tm, 3 * G:] + xb[H - 8:H - 8 + tm, 3 * G:]
    tpos = t0 + lax.broadcasted_iota(jnp.int32, (tm, 1), 0)
    for g, win in enumerate(POOL_WINDOWS):
        src = (xb, xa, xb, xa)[g]
        cols = slice(g * G, (g + 1) * G)
        cnt = jnp.minimum(tpos + 1, win).astype(F32)
        y = src[H:H + tm, cols] / cnt - u[:, cols]
        yp = _dot(y.astype(BF16), wpool_ref[g]) * ps_ref[:, cols]
        yp_ref[:, cols] = yp.astype(BF16)
    hm = tm // ROW_SPLIT
    for r0 in range(0, tm, hm):
        rows = slice(r0, r0 + hm)
        pb = _dot(yp_ref[rows, :], wp_ref[...])
        a = _dot(hg_ref[rows, :], wm_ref[...])
        merged = (jax.nn.sigmoid(gm_ref[rows, :].astype(F32)) * a
                  + jax.nn.sigmoid(gp_ref[rows, :].astype(F32)) * pb)
        out_ref[rows, :] = merged.astype(BF16)


def _branch(hg2d, z_main, w_pool, pool_scale, w_m_br, w_p_br, seq, tm):
    t = hg2d.shape[0]
    d = w_m_br.shape[1]
    hb = tm // POOL_HALO
    const = lambda shape: pl.BlockSpec(shape, lambda i: (0,) * len(shape))
    return pl.pallas_call(
        functools.partial(_branch_kernel, tm=tm, seq=seq),
        grid=(t // tm,),
        in_specs=[pl.BlockSpec((tm, M_V), lambda i: (i, 0)),
                  pl.BlockSpec((tm, POOL_W), lambda i: (i, 3)),
                  pl.BlockSpec((POOL_HALO, POOL_W), lambda i: (jnp.maximum(i * hb - 1, 0), 3)),
                  pl.BlockSpec((tm, d), lambda i: (i, 2)),
                  pl.BlockSpec((tm, d), lambda i: (i, 3)),
                  const(w_pool.shape), const((1, POOL_W)), const(w_m_br.shape), const(w_p_br.shape)],
        out_specs=pl.BlockSpec((tm, d), lambda i: (i, 0)),
        out_shape=jax.ShapeDtypeStruct((t, d), BF16),
        scratch_shapes=[pltpu.VMEM((tm + POOL_HALO, POOL_W), F32),
                        pltpu.VMEM((tm + POOL_HALO, POOL_W), F32),
                        pltpu.VMEM((tm, POOL_W), BF16)],
        compiler_params=_params(1),
        name="branch",
    )(hg2d, z_main, z_main, z_main, z_main, w_pool, pool_scale, w_m_br, w_p_br)


def _route_kernel(mg_ref, x_ref, wout_ref, g_ref, b_ref, wr_ref, br_ref, upper_ref,
                  x1_ref, route_ref, gate_ref, cnt_ref, carry, *, alpha, tm):
    @pl.when(pl.program_id(0) == 0)
    def _():
        carry[...] = jnp.zeros_like(carry)

    hm = tm // ROW_SPLIT
    pitch = _pitch(x_ref.shape[1])
    logits = []
    for r0 in range(0, tm, hm):
        r = alpha * x_ref[r0:r0 + hm, :] + _dot(mg_ref[r0:r0 + hm, :], wout_ref[...])
        x1 = _layer_norm(r, g_ref[...], b_ref[...])
        _store_token_major(x1_ref.at[r0 * pitch:(r0 + hm) * pitch], x1)
        logits.append(_dot(x1.astype(BF16), wr_ref[...]) + br_ref[...])
    lt = jnp.concatenate(logits, axis=0).T

    sub = lax.broadcasted_iota(jnp.int32, (SUBLANES, tm), 0).astype(F32)
    none = float(SUBLANES)

    def softmax(z):
        e = jnp.exp(z - jnp.max(z, axis=0, keepdims=True))
        return e / jnp.sum(e, axis=0, keepdims=True)

    def top1(vals):
        top = jnp.max(vals, axis=0, keepdims=True)
        return top, jnp.min(jnp.where(vals == top, sub, none), axis=0, keepdims=True)

    is_grp = sub < N_GROUPS
    pg = jnp.where(is_grp, softmax(jnp.where(is_grp, lt[0:SUBLANES, :], -jnp.inf)), -1.0)
    pg_top, g_idx = top1(pg)
    le_sel = lt[SUBLANES:2 * SUBLANES, :]
    for g in range(1, N_GROUPS):
        le_sel = jnp.where(g_idx == g, lt[(g + 1) * SUBLANES:(g + 2) * SUBLANES, :], le_sel)
    pe = softmax(le_sel)
    pe1, i1 = top1(pe)
    pe2, i2 = top1(jnp.where(sub == i1, -1.0, pe))
    den = pe1 + pe2
    gate1 = pg_top * pe1 / den
    gate2 = pg_top * pe2 / den
    e1 = g_idx * EXPERTS_PER_GROUP + i1
    e2 = g_idx * EXPERTS_PER_GROUP + i2

    eid = lax.broadcasted_iota(jnp.int32, (N_EXPERTS, tm), 0).astype(F32)
    hit1 = eid == e1
    hit2 = eid == e2
    onehot = jnp.where(hit1 | hit2, 1.0, 0.0)
    incl = _dot(onehot.astype(BF16), upper_ref[...])
    excl = incl - onehot + carry[:, 0:1]
    r1 = jnp.sum(jnp.where(hit1, excl, 0.0), axis=0, keepdims=True)
    r2 = jnp.sum(jnp.where(hit2, excl, 0.0), axis=0, keepdims=True)
    carry[...] = carry[...] + jnp.broadcast_to(incl[:, tm - 1:tm], carry.shape)
    cnt_ref[...] = carry[...]

    route = jnp.where(sub == 0, e1, jnp.where(sub == 1, e2, jnp.where(sub == 2, r1, jnp.where(sub == 3, r2, 0.0))))
    route_ref[...] = route.astype(jnp.int32)
    gate_ref[...] = jnp.where(sub == 0, gate1, jnp.where(sub == 1, gate2, 0.0))


def _route(merged, x2d, w_out, ln_g, ln_b, w_r, b_r, upper, alpha, tm):
    t, d = x2d.shape
    const = lambda shape: pl.BlockSpec(shape, lambda i: (0,) * len(shape))
    rows = lambda width: pl.BlockSpec((tm, width), lambda i: (i, 0))
    cols = pl.BlockSpec((SUBLANES, tm), lambda i: (0, i))
    return pl.pallas_call(
        functools.partial(_route_kernel, alpha=alpha, tm=tm),
        grid=(t // tm,),
        in_specs=[rows(d), rows(d), const(w_out.shape), const((1, d)), const((1, d)),
                  const(w_r.shape), const((1, LANES)), const((tm, tm))],
        out_specs=[pl.BlockSpec((tm * _pitch(d), LANES), lambda i: (i, 0)),
                   cols, cols, const((N_EXPERTS, LANES))],
        out_shape=[jax.ShapeDtypeStruct((t * _pitch(d), LANES), F32),
                   jax.ShapeDtypeStruct((SUBLANES, t), jnp.int32),
                   jax.ShapeDtypeStruct((SUBLANES, t), F32),
                   jax.ShapeDtypeStruct((N_EXPERTS, LANES), F32)],
        scratch_shapes=[pltpu.VMEM((N_EXPERTS, LANES), F32)],
        compiler_params=_params(1),
        name="route",
    )(merged, x2d, w_out, ln_g, ln_b, w_r, b_r, upper)


def _dispatch_kernel(pad_start_ref, pad_len_ref, nu_ref, dest_ref, x_ref, xs_hbm, sem, *, tb, te, d):
    def copy(src_tok, dst_tok, n=1):
        return pltpu.make_async_copy(_tokens(x_ref, src_tok, d, n), _tokens(xs_hbm, dst_tok, d, n), sem)

    def wait_tokens(n):
        def body(r, c):
            copy(0, 0, n).wait()
            return c
        return body

    @pl.when(pl.program_id(0) == 0)
    def _():
        def per_expert(e, total):
            start = pad_start_ref[e]
            npad = pad_len_ref[e]

            def fill(r, c):
                copy(0, start + r).start()
                return c

            lax.fori_loop(0, npad, fill, 0)
            return total + npad

        total = lax.fori_loop(0, N_EXPERTS, per_expert, 0)
        lax.fori_loop(0, total // SUBLANES, wait_tokens(SUBLANES), 0)
        lax.fori_loop(0, total % SUBLANES, wait_tokens(1), 0)

        def fill_tile(j, c):
            copy(0, j * te, te).start()
            return c

        n_tiles = xs_hbm.shape[0] // (te * _pitch(d))
        lax.fori_loop(nu_ref[0], n_tiles, fill_tile, 0)
        lax.fori_loop(nu_ref[0], n_tiles, wait_tokens(te), 0)

    def scatter(t, c):
        for k in range(TOP_K):
            copy(t, dest_ref[TOP_K * t + k]).start()
        return c

    lax.fori_loop(0, tb, scatter, 0, unroll=DMA_UNROLL)
    for k in range(TOP_K):
        copy(0, 0, tb).wait()


def _dispatch(x1t, dest_flat, pad_start, pad_len, n_used, n_slots, tb, te, d):
    pitch = _pitch(d)
    t = x1t.shape[0] // pitch
    grid_spec = pltpu.PrefetchScalarGridSpec(
        num_scalar_prefetch=3,
        grid=(t // tb,),
        in_specs=[pl.BlockSpec((TOP_K * tb,), lambda i, ps, pn, nu: (i,), memory_space=pltpu.SMEM),
                  pl.BlockSpec((tb * pitch, LANES), lambda i, ps, pn, nu: (i, 0))],
        out_specs=pl.BlockSpec(memory_space=pl.ANY),
        scratch_shapes=[pltpu.SemaphoreType.DMA(())],
    )
    return pl.pallas_call(
        functools.partial(_dispatch_kernel, tb=tb, te=te, d=d),
        grid_spec=grid_spec,
        out_shape=jax.ShapeDtypeStruct((n_slots * pitch, LANES), x1t.dtype),
        compiler_params=_params(1),
        name="dispatch",
    )(pad_start, pad_len, n_used, dest_flat, x1t)


def _expert_kernel(be_ref, nu_ref, x_ref, wg_ref, wu_ref, wd_ref, y_ref, *, te, d):
    i = pl.program_id(0)

    @pl.when(i < nu_ref[0])
    def _():
        xb = _load_token_major(x_ref, te, d).astype(BF16)
        gt = _dot(xb, wg_ref[0])
        up = _dot(xb, wu_ref[0])
        hid = gt * jax.nn.sigmoid(gt) * up
        _store_token_major(y_ref, _dot(hid.astype(BF16), wd_ref[0]))

    @pl.when(i >= nu_ref[0])
    def _():
        y_ref[...] = jnp.zeros_like(y_ref)


def _experts(xs, block_e, n_used, w_gate, w_up, w_down, te):
    _, d, de = w_gate.shape
    pitch = _pitch(d)
    n_tiles = xs.shape[0] // (te * pitch)
    grid_spec = pltpu.PrefetchScalarGridSpec(
        num_scalar_prefetch=2,
        grid=(n_tiles,),
        in_specs=[pl.BlockSpec((te * pitch, LANES), lambda i, be, nu: (jnp.minimum(i, nu[0] - 1), 0)),
                  pl.BlockSpec((1, d, de), lambda i, be, nu: (be[i], 0, 0)),
                  pl.BlockSpec((1, d, de), lambda i, be, nu: (be[i], 0, 0)),
                  pl.BlockSpec((1, de, d), lambda i, be, nu: (be[i], 0, 0))],
        out_specs=pl.BlockSpec((te * pitch, LANES), lambda i, be, nu: (i, 0)),
    )
    return pl.pallas_call(
        functools.partial(_expert_kernel, te=te, d=d),
        grid_spec=grid_spec,
        out_shape=jax.ShapeDtypeStruct(xs.shape, F32),
        compiler_params=_params(1),
        name="experts",
    )(block_e, n_used, xs, w_gate, w_up, w_down)


def _final_kernel(dcur_ref, dnxt_ref, x1_ref, gate_ref, p_ref, ys_hbm, g_ref, b_ref, wg_ref, bg_ref, wp_ref,
                  out_ref, ybuf, sem, *, alpha, tm, d):
    i = pl.program_id(0)
    slot = lax.rem(i, 2)

    def issue(dest_ref, s):
        def body(t, c):
            for k in range(TOP_K):
                pltpu.make_async_copy(_tokens(ys_hbm, dest_ref[TOP_K * t + k], d),
                                      _tokens(ybuf.at[s, k], t, d), sem.at[s]).start()
            return c
        lax.fori_loop(0, tm, body, 0, unroll=DMA_UNROLL)

    @pl.when(i == 0)
    def _():
        issue(dcur_ref, 0)

    @pl.when(i + 1 < pl.num_programs(0))
    def _():
        issue(dnxt_ref, 1 - slot)

    for k in range(TOP_K):
        pltpu.make_async_copy(_tokens(ys_hbm, 0, d, tm), ybuf.at[slot, k], sem.at[slot]).wait()

    hm = tm // ROW_SPLIT
    for r0 in range(0, tm, hm):
        rows = slice(r0, r0 + hm)
        trows = pl.ds(r0 * _pitch(d), hm * _pitch(d))
        pp = _dot(p_ref[rows, :].astype(BF16), wp_ref[...])
        gate = gate_ref[rows, :]
        y = (gate[:, 0:1] * _load_token_major(ybuf.at[slot, 0, trows], hm, d)
             + gate[:, 1:2] * _load_token_major(ybuf.at[slot, 1, trows], hm, d))
        x2 = _layer_norm(alpha * _load_token_major(x1_ref.at[trows], hm, d) + y, g_ref[...], b_ref[...])
        gl = _dot(x2.astype(BF16), wg_ref[...]) + bg_ref[...]
        out_ref[rows, :] = x2 + jax.nn.sigmoid(gl) * pp


def _final(dest_flat, x1t, gate, p2d, ys, ln_g, ln_b, w_pg, b_pg, w_pp, alpha, tm):
    d = w_pg.shape[0]
    pitch = _pitch(d)
    t = x1t.shape[0] // pitch
    n = t // tm
    const = lambda shape: pl.BlockSpec(shape, lambda i: (0,) * len(shape))
    resident = lambda shape: pl.BlockSpec(shape, lambda i: (0,) * len(shape), pipeline_mode=pl.Buffered(1))
    rows = lambda width: pl.BlockSpec((tm, width), lambda i: (i, 0))
    return pl.pallas_call(
        functools.partial(_final_kernel, alpha=alpha, tm=tm, d=d),
        grid=(n,),
        in_specs=[pl.BlockSpec((TOP_K * tm,), lambda i: (i,), memory_space=pltpu.SMEM),
                  pl.BlockSpec((TOP_K * tm,), lambda i: (jnp.minimum(i + 1, n - 1),), memory_space=pltpu.SMEM),
                  pl.BlockSpec((tm * pitch, LANES), lambda i: (i, 0)),
                  rows(LANES), rows(p2d.shape[1]),
                  pl.BlockSpec(memory_space=pl.ANY),
                  const((1, d)), const((1, d)), resident(w_pg.shape), const((1, d)), resident(w_pp.shape)],
        out_specs=rows(d),
        out_shape=jax.ShapeDtypeStruct((t, d), F32),
        scratch_shapes=[pltpu.VMEM((2, TOP_K, tm * pitch, LANES), F32), pltpu.SemaphoreType.DMA((2,))],
        compiler_params=_params(1),
        name="final",
    )(dest_flat, dest_flat, x1t, gate, p2d, ys, ln_g, ln_b, w_pg, b_pg, w_pp)


def _pad_cols(a, width):
    return jnp.pad(a, ((0, 0), (0, width - a.shape[1])))


def _tri(n):
    return jnp.tril(jnp.ones((n, n), BF16))


def _layer(x, p, w_in, b_in, conv_w, conv_b, mh_g, w_pool, pool_scale, w_m_br, w_p_br, w_out,
           ln1_g, ln1_b, w_rg, b_rg, w_re, b_re, w_gate, w_up, w_down, ln2_g, ln2_b,
           w_ple_gate, b_ple_gate, w_ple_proj, alpha):
    nb, seq, d = x.shape
    t = nb * seq
    x2d = x.reshape(t, d)
    row = lambda a: a.reshape(1, -1)

    c_if = 2 * M_QK + 2 * M_V
    w_main = jnp.concatenate([w_in[:, :c_if].astype(BF16), w_in[:, c_if + 2 * M_HEADS:].astype(BF16)], axis=1)
    b_main = row(jnp.concatenate([b_in[:c_if], b_in[c_if + 2 * M_HEADS:]]))
    w_if = _pad_cols(w_in[:, c_if:c_if + 2 * M_HEADS], LANES).astype(BF16)
    b_if = _pad_cols(row(b_in[c_if:c_if + 2 * M_HEADS]), LANES)

    n_exp, _, d_exp = w_gate.shape
    expert_w = [w_gate.reshape(n_exp * d, d_exp), w_up.reshape(n_exp * d, d_exp), w_down.reshape(n_exp * d_exp, d)]
    tm_in = min(1024, t)
    z_main, z_if, wg_b, wu_b, wd_b = _inproj(x2d, w_main, b_main, w_if, b_if, tm_in, 1024, expert_w)

    hg = _mlstm(z_main.reshape(nb, seq, -1), z_if.reshape(nb, seq, LANES),
                conv_w, row(conv_b), row(mh_g))

    tm = min(512, seq)
    merged = _branch(hg.reshape(t, M_V), z_main, w_pool.astype(BF16), row(pool_scale),
                     w_m_br.astype(BF16), w_p_br.astype(BF16), seq, tm)

    w_r = _pad_cols(jnp.concatenate([_pad_cols(w_rg, SUBLANES), w_re], axis=1), LANES).astype(BF16)
    b_r = _pad_cols(jnp.concatenate([_pad_cols(row(b_rg), SUBLANES), row(b_re)], axis=1), LANES)
    x1, route, gate_t, cnt = _route(merged, x2d, w_out.astype(BF16), row(ln1_g), row(ln1_b),
                                    w_r, b_r, _tri(tm).T, alpha, tm)
    gate = _pad_cols(gate_t[0:TOP_K].T, LANES)

    te = EXPERT_TILE
    counts = cnt[:, 0].astype(jnp.int32)
    pcounts = (counts + te - 1) // te * te
    pends = jnp.cumsum(pcounts)
    pstarts = pends - pcounts
    e_sel = route[0:TOP_K, :, None] == jnp.arange(N_EXPERTS, dtype=jnp.int32)
    dest = (jnp.sum(jnp.where(e_sel, pstarts, 0), axis=-1) + route[TOP_K:2 * TOP_K]).T.reshape(-1)
    n_slots = t * TOP_K + N_EXPERTS * te
    n_tiles = n_slots // te
    n_used = (pends[-1] // te).reshape(1)
    tile_row = jnp.minimum(jnp.arange(n_tiles, dtype=jnp.int32), n_used - 1) * te
    block_e = jnp.minimum(jnp.sum((tile_row[:, None] >= pends[None, :]).astype(jnp.int32), axis=1), N_EXPERTS - 1)

    xs = _dispatch(x1, dest, pstarts + counts, pcounts - counts, n_used, n_slots, min(512, t), te, d)
    ys = _experts(xs, block_e, n_used, wg_b.reshape(n_exp, d, d_exp), wu_b.reshape(n_exp, d, d_exp),
                  wd_b.reshape(n_exp, d_exp, d), te)
    return _final(dest, x1, gate, p.reshape(t, -1), ys, row(ln2_g), row(ln2_b),
                  w_ple_gate.astype(BF16), row(b_ple_gate), w_ple_proj.astype(BF16), alpha,
                  min(512, t)).reshape(nb, seq, d)


def kernel(x, p, w_in, b_in, conv_w, conv_b, mh_g, w_pool, pool_scale, w_m_br, w_p_br, w_out, ln1_g, ln1_b, w_rg, b_rg, w_re, b_re, w_gate, w_up, w_down, ln2_g, ln2_b, w_ple_gate, b_ple_gate, w_ple_proj):
    depth = w_in.shape[0]
    alpha = (2 * depth) ** 0.25
    for i in range(depth):
        x = _layer(x, p[i], w_in[i], b_in[i], conv_w[i], conv_b[i], mh_g[i], w_pool[i], pool_scale[i],
                   w_m_br[i], w_p_br[i], w_out[i], ln1_g[i], ln1_b[i], w_rg[i], b_rg[i], w_re[i], b_re[i],
                   w_gate[i], w_up[i], w_down[i], ln2_g[i], ln2_b[i], w_ple_gate[i], b_ple_gate[i],
                   w_ple_proj[i], alpha)
    return x
```

```python
import functools

import jax
import jax.numpy as jnp
from jax import lax
from jax.experimental import pallas as pl
from jax.experimental.pallas import tpu as pltpu

F32 = jnp.float32
BF16 = jnp.bfloat16

M_HEADS = 4
M_QK_DIM = 128
M_V_DIM = 256
M_QK = M_HEADS * M_QK_DIM
M_V = M_HEADS * M_V_DIM
CONV_W = 4
CHUNK = 128
POOL_WINDOWS = (2, 4, 8, 16)
POOL_GROUP_DIM = 256
POOL_W = len(POOL_WINDOWS) * POOL_GROUP_DIM
N_GROUPS = 4
EXPERTS_PER_GROUP = 8
N_EXPERTS = N_GROUPS * EXPERTS_PER_GROUP
TOP_K = 2
LN_EPS = 1e-5

LANES = 128
SUBLANES = 8
VMEM_LIMIT = 56 * 1024 * 1024
BF16_SUBLANES = 16
CONV_HALO = BF16_SUBLANES
POOL_HALO = 32
EXPERT_TILE = 256
DMA_UNROLL = 8
ROW_SPLIT = 2


def _dot(a, b):
    return jnp.dot(a, b, preferred_element_type=F32)


def _params(n_grid):
    return pltpu.CompilerParams(dimension_semantics=("arbitrary",) * n_grid,
                                vmem_limit_bytes=VMEM_LIMIT)


def _log_sigmoid(x):
    return -(jnp.maximum(-x, 0.0) + jnp.log1p(jnp.exp(-jnp.abs(x))))


def _layer_norm(r, g, b):
    mu = jnp.mean(r, axis=-1, keepdims=True)
    d = r - mu
    var = jnp.mean(d * d, axis=-1, keepdims=True)
    return d * lax.rsqrt(var + LN_EPS) * g + b


def _pitch(d):
    return d // LANES + 1


def _store_token_major(ref, val):
    n, d = val.shape
    for c in range(d // LANES):
        ref[pl.ds(c, n, stride=_pitch(d)), :] = val[:, c * LANES:(c + 1) * LANES]
    ref[pl.ds(d // LANES, n, stride=_pitch(d)), :] = jnp.zeros((n, LANES), val.dtype)


def _load_token_major(ref, n, d):
    return jnp.concatenate([ref[pl.ds(c, n, stride=_pitch(d)), :] for c in range(d // LANES)], axis=1)


def _tokens(ref, tok, d, n=1):
    return ref.at[pl.ds(tok * _pitch(d), n * _pitch(d))]


def _inproj_kernel(*refs, n_side):
    x_ref, w_ref, b_ref, wif_ref, bif_ref = refs[:5]
    side_in = refs[5:5 + n_side]
    z_ref, zif_ref = refs[5 + n_side:7 + n_side]
    side_out = refs[7 + n_side:7 + 2 * n_side]
    xb_ref = refs[7 + 2 * n_side]

    @pl.when(pl.program_id(1) == 0)
    def _():
        xb = x_ref[...].astype(BF16)
        xb_ref[...] = xb
        zif_ref[...] = _dot(xb, wif_ref[...]) + bif_ref[...]

    z_ref[...] = (_dot(xb_ref[...], w_ref[...]) + b_ref[...]).astype(BF16)
    for src, dst in zip(side_in, side_out):
        dst[...] = src[...].astype(BF16)


def _inproj(x2d, w_main, b_main, w_if, b_if, tm, tn, side):
    t, d = x2d.shape
    n = w_main.shape[1]
    nj = n // tn
    steps = (t // tm) * nj
    side_spec = lambda a: pl.BlockSpec((a.shape[0] // steps, a.shape[1]), lambda i, j: (i * nj + j, 0))
    return pl.pallas_call(
        functools.partial(_inproj_kernel, n_side=len(side)),
        grid=(t // tm, nj),
        in_specs=[pl.BlockSpec((tm, d), lambda i, j: (i, 0)),
                  pl.BlockSpec((d, tn), lambda i, j: (0, j)),
                  pl.BlockSpec((1, tn), lambda i, j: (0, j)),
                  pl.BlockSpec((d, LANES), lambda i, j: (0, 0)),
                  pl.BlockSpec((1, LANES), lambda i, j: (0, 0))] + [side_spec(a) for a in side],
        out_specs=[pl.BlockSpec((tm, tn), lambda i, j: (i, j)),
                   pl.BlockSpec((tm, LANES), lambda i, j: (i, 0))] + [side_spec(a) for a in side],
        out_shape=[jax.ShapeDtypeStruct((t, n), BF16),
                   jax.ShapeDtypeStruct((t, LANES), F32)] + [jax.ShapeDtypeStruct(a.shape, BF16) for a in side],
        scratch_shapes=[pltpu.VMEM((tm, d), BF16)],
        compiler_params=_params(2),
        name="inproj",
    )(x2d, w_main, b_main, w_if, b_if, *side)


def _mlstm_kernel(qk_ref, v_ref, o_ref, g_ref, cw_ref, cb_ref, mhg_ref, out_ref, cbuf, *state, nb):
    L = CHUNK
    halo = CONV_HALO
    ct_refs, m_ref = state[:M_HEADS], state[M_HEADS]

    @pl.when(pl.program_id(0) == 0)
    def _():
        cbuf[:, 0:halo, :] = jnp.zeros((nb, halo, 2 * M_QK), BF16)
        for ref in state:
            ref[...] = jnp.zeros_like(ref)

    cbuf[:, halo:halo + L, :] = qk_ref[...]
    scale = M_QK_DIM ** -0.5
    row = lax.broadcasted_iota(jnp.int32, (L, L), 0)
    col = lax.broadcasted_iota(jnp.int32, (L, L), 1)
    causal = col <= row
    upper = jnp.where(row <= col, 1.0, 0.0).astype(BF16)
    ones_blk = jnp.ones((L, LANES), BF16)
    lane8 = lax.broadcasted_iota(jnp.int32, (SUBLANES, L), 1)
    wrow = lax.broadcasted_iota(jnp.int32, (L, L + halo), 0)
    wcol = lax.broadcasted_iota(jnp.int32, (L, L + halo), 1)
    shifts = [jnp.where(wcol == wrow + (halo - (CONV_W - 1) + tap), 1.0, 0.0).astype(BF16)
              for tap in range(CONV_W - 1)]
    sub = lax.broadcasted_iota(jnp.int32, (SUBLANES, LANES), 0)

    def per_batch(b, carry):
        window = cbuf[b]
        acc = cb_ref[...] + _dot(shifts[0], window) * cw_ref[0:1, :]
        for tap in range(1, CONV_W - 1):
            acc = acc + _dot(shifts[tap], window) * cw_ref[tap:tap + 1, :]
        acc = acc + qk_ref[b].astype(F32) * cw_ref[CONV_W - 1:CONV_W, :]
        qk = acc * jax.nn.sigmoid(acc)
        cbuf[b, 0:halo, :] = cbuf[b, L:L + halo, :]

        g_row = g_ref[b].T[0:SUBLANES, :]
        lf = _log_sigmoid(g_row)
        lf_hi = lf.astype(BF16)
        lf_lo = (lf - lf_hi.astype(F32)).astype(BF16)
        cs = _dot(jnp.concatenate([lf_hi, lf_lo], axis=0), upper)
        b_row = pltpu.roll(cs[0:SUBLANES, :] + cs[SUBLANES:2 * SUBLANES, :], M_HEADS, 0)
        a_row = g_row - b_row
        cm = a_row
        for sh in (1, 2, 4, 8, 16, 32, 64):
            cm = jnp.maximum(cm, jnp.where(lane8 >= sh, pltpu.roll(cm, sh, 1), -jnp.inf))
        m_prev = m_ref[b]
        mx = jnp.maximum(m_prev, cm)
        mx_last = jnp.maximum(m_prev, jnp.max(a_row, axis=1, keepdims=True))
        carry_in = jnp.exp(m_prev - mx)
        floor = jnp.exp(-(b_row + mx))
        wts = jnp.exp(a_row - mx_last)
        decay = jnp.exp(m_prev - mx_last)
        m_ref[b] = jnp.where(sub < M_HEADS, b_row[:, L - 1:L] + mx_last, 0.0)
        packed = jnp.concatenate(
            [jnp.where(sub < M_HEADS, mx, pltpu.roll(carry_in, M_HEADS, 0)),
             jnp.where(sub < M_HEADS, floor, pltpu.roll(wts, M_HEADS, 0)),
             jnp.zeros((L - 2 * SUBLANES, L), F32)], axis=0).T

        heads = range(M_HEADS)
        vcols = [slice(h * M_V_DIM, (h + 1) * M_V_DIM) for h in heads]
        q = [qk[:, h * M_QK_DIM:(h + 1) * M_QK_DIM].astype(BF16) for h in heads]
        kf = [qk[:, M_QK + h * M_QK_DIM:M_QK + (h + 1) * M_QK_DIM] * scale for h in heads]
        v_ext = [jnp.concatenate([v_ref[b, :, vcols[h]], ones_blk], axis=1) for h in heads]
        ct = [ct_refs[h][b] for h in heads]
        col_of = lambda i, h: packed[:, i * M_HEADS + h:i * M_HEADS + h + 1]

        raw = [lax.dot_general(q[h], kf[h].astype(BF16), (((1,), (1,)), ((), ())),
                               preferred_element_type=F32) for h in heads]
        prev = [_dot(q[h], ct[h].astype(BF16)) for h in heads]
        pmat = [jnp.exp(jnp.where(causal, a_row[h:h + 1, :] - col_of(0, h), -jnp.inf)) for h in heads]
        s = [(raw[h] * pmat[h]).astype(BF16) for h in heads]
        numden = [_dot(s[h], v_ext[h]) + col_of(1, h) * prev[h] for h in heads]
        upd = [lax.dot_general((kf[h] * col_of(3, h)).astype(BF16), v_ext[h], (((0,), (0,)), ((), ())),
                               preferred_element_type=F32) for h in heads]
        for h in heads:
            dec = jnp.concatenate([decay[h:h + 1, :]] * (ct[h].shape[1] // LANES), axis=1)
            ct_refs[h][b] = dec * ct[h] + upd[h]
        inv = [1.0 / jnp.maximum(jnp.abs(numden[h][:, M_V_DIM:]), col_of(2, h)) for h in heads]
        hh = [numden[h][:, 0:M_V_DIM] * jnp.concatenate([inv[h]] * (M_V_DIM // LANES), axis=1) for h in heads]
        mu = [jnp.mean(hh[h], axis=-1, keepdims=True) for h in heads]
        dlt = [hh[h] - mu[h] for h in heads]
        var = [jnp.mean(dlt[h] * dlt[h], axis=-1, keepdims=True) for h in heads]
        for h in heads:
            hn = dlt[h] * lax.rsqrt(var[h] + LN_EPS) * mhg_ref[:, vcols[h]]
            og = jax.nn.sigmoid(o_ref[b, :, vcols[h]].astype(F32))
            out_ref[b, :, vcols[h]] = (hn * og).astype(BF16)
        return carry

    for b in range(nb):
        per_batch(b, 0)


def _mlstm(z3, zif3, conv_w, conv_b, mh_g):
    nb, s, _ = z3.shape
    L = CHUNK
    blk = lambda colblk: pl.BlockSpec((nb, L, M_V), lambda c: (0, c, colblk))
    const = lambda shape: pl.BlockSpec(shape, lambda c: (0,) * len(shape))
    return pl.pallas_call(
        functools.partial(_mlstm_kernel, nb=nb),
        grid=(s // L,),
        in_specs=[blk(0), blk(1), blk(2),
                  pl.BlockSpec((nb, L, LANES), lambda c: (0, c, 0)),
                  const((CONV_W, 2 * M_QK)), const((1, 2 * M_QK)), const((1, M_V))],
        out_specs=pl.BlockSpec((nb, L, M_V), lambda c: (0, c, 0)),
        out_shape=jax.ShapeDtypeStruct((nb, s, M_V), BF16),
        scratch_shapes=([pltpu.VMEM((nb, L + CONV_HALO, 2 * M_QK), BF16)]
                        + [pltpu.VMEM((nb, M_QK_DIM, M_V_DIM + LANES), F32)] * M_HEADS
                        + [pltpu.VMEM((nb, SUBLANES, L), F32)]),
        compiler_params=_params(1),
        name="mlstm",
    )(z3, z3, z3, zif3, conv_w, conv_b, mh_g)


def _branch_kernel(hg_ref, u_ref, uh_ref, gm_ref, gp_ref, wpool_ref, ps_ref, wm_ref, wp_ref,
                   out_ref, xa, xb, yp_ref, *, tm, seq):
    H = POOL_HALO
    G = POOL_GROUP_DIM
    t0 = lax.rem(pl.program_id(0) * tm, seq)
    u = u_ref[...].astype(F32)
    xa[H:H + tm, :] = u
    xa[0:H, :] = jnp.where(t0 == 0, 0.0, uh_ref[...].astype(F32))
    n = tm + H - 8
    xb[8:8 + n, :] = xa[8:8 + n, :] + xa[7:7 + n, :]
    n = tm + H - 16
    xa[16:16 + n, G:] = xb[16:16 + n, G:] + xb[14:14 + n, G:]
    n = tm + H - 24
    xb[24:24 + n, 2 * G:] = xa[24:24 + n, 2 * G:] + xa[20:20 + n, 2 * G:]
    xa[H:H + tm, 3 * G:] = xb[H:H + tm, 3 * G:] + xb[H - 8:H - 8 + tm, 3 * G:]
    tpos = t0 + lax.broadcasted_iota(jnp.int32, (tm, 1), 0)
    for g, win in enumerate(POOL_WINDOWS):
        src = (xb, xa, xb, xa)[g]
        cols = slice(g * G, (g + 1) * G)
        cnt = jnp.minimum(tpos + 1, win).astype(F32)
        y = src[H:H + tm, cols] / cnt - u[:, cols]
        yp = _dot(y.astype(BF16), wpool_ref[g]) * ps_ref[:, cols]
        yp_ref[:, cols] = yp.astype(BF16)
    hm = tm // ROW_SPLIT
    for r0 in range(0, tm, hm):
        rows = slice(r0, r0 + hm)
        pb = _dot(yp_ref[rows, :], wp_ref[...])
        a = _dot(hg_ref[rows, :], wm_ref[...])
        merged = (jax.nn.sigmoid(gm_ref[rows, :].astype(F32)) * a
                  + jax.nn.sigmoid(gp_ref[rows, :].astype(F32)) * pb)
        out_ref[rows, :] = merged.astype(BF16)


def _branch(hg2d, z_main, w_pool, pool_scale, w_m_br, w_p_br, seq, tm):
    t = hg2d.shape[0]
    d = w_m_br.shape[1]
    hb = tm // POOL_HALO
    const = lambda shape: pl.BlockSpec(shape, lambda i: (0,) * len(shape))
    return pl.pallas_call(
        functools.partial(_branch_kernel, tm=tm, seq=seq),
        grid=(t // tm,),
        in_specs=[pl.BlockSpec((tm, M_V), lambda i: (i, 0)),
                  pl.BlockSpec((tm, POOL_W), lambda i: (i, 3)),
                  pl.BlockSpec((POOL_HALO, POOL_W), lambda i: (jnp.maximum(i * hb - 1, 0), 3)),
                  pl.BlockSpec((tm, d), lambda i: (i, 2)),
                  pl.BlockSpec((tm, d), lambda i: (i, 3)),
                  const(w_pool.shape), const((1, POOL_W)), const(w_m_br.shape), const(w_p_br.shape)],
        out_specs=pl.BlockSpec((tm, d), lambda i: (i, 0)),
        out_shape=jax.ShapeDtypeStruct((t, d), BF16),
        scratch_shapes=[pltpu.VMEM((tm + POOL_HALO, POOL_W), F32),
                        pltpu.VMEM((tm + POOL_HALO, POOL_W), F32),
                        pltpu.VMEM((tm, POOL_W), BF16)],
        compiler_params=_params(1),
        name="branch",
    )(hg2d, z_main, z_main, z_main, z_main, w_pool, pool_scale, w_m_br, w_p_br)


def _route_kernel(mg_ref, x_ref, wout_ref, g_ref, b_ref, wr_ref, br_ref, upper_ref,
                  x1_ref, route_ref, gate_ref, cnt_ref, carry, *, alpha, tm):
    @pl.when(pl.program_id(0) == 0)
    def _():
        carry[...] = jnp.zeros_like(carry)

    hm = tm // ROW_SPLIT
    pitch = _pitch(x_ref.shape[1])
    logits = []
    for r0 in range(0, tm, hm):
        r = alpha * x_ref[r0:r0 + hm, :] + _dot(mg_ref[r0:r0 + hm, :], wout_ref[...])
        x1 = _layer_norm(r, g_ref[...], b_ref[...])
        _store_token_major(x1_ref.at[r0 * pitch:(r0 + hm) * pitch], x1)
        logits.append(_dot(x1.astype(BF16), wr_ref[...]) + br_ref[...])
    lt = jnp.concatenate(logits, axis=0).T

    sub = lax.broadcasted_iota(jnp.int32, (SUBLANES, tm), 0).astype(F32)
    none = float(SUBLANES)

    def softmax(z):
        e = jnp.exp(z - jnp.max(z, axis=0, keepdims=True))
        return e / jnp.sum(e, axis=0, keepdims=True)

    def top1(vals):
        top = jnp.max(vals, axis=0, keepdims=True)
        return top, jnp.min(jnp.where(vals == top, sub, none), axis=0, keepdims=True)

    is_grp = sub < N_GROUPS
    pg = jnp.where(is_grp, softmax(jnp.where(is_grp, lt[0:SUBLANES, :], -jnp.inf)), -1.0)
    pg_top, g_idx = top1(pg)
    le_sel = lt[SUBLANES:2 * SUBLANES, :]
    for g in range(1, N_GROUPS):
        le_sel = jnp.where(g_idx == g, lt[(g + 1) * SUBLANES:(g + 2) * SUBLANES, :], le_sel)
    pe = softmax(le_sel)
    pe1, i1 = top1(pe)
    pe2, i2 = top1(jnp.where(sub == i1, -1.0, pe))
    den = pe1 + pe2
    gate1 = pg_top * pe1 / den
    gate2 = pg_top * pe2 / den
    e1 = g_idx * EXPERTS_PER_GROUP + i1
    e2 = g_idx * EXPERTS_PER_GROUP + i2

    eid = lax.broadcasted_iota(jnp.int32, (N_EXPERTS, tm), 0).astype(F32)
    hit1 = eid == e1
    hit2 = eid == e2
    onehot = jnp.where(hit1 | hit2, 1.0, 0.0)
    incl = _dot(onehot.astype(BF16), upper_ref[...])
    excl = incl - onehot + carry[:, 0:1]
    r1 = jnp.sum(jnp.where(hit1, excl, 0.0), axis=0, keepdims=True)
    r2 = jnp.sum(jnp.where(hit2, excl, 0.0), axis=0, keepdims=True)
    carry[...] = carry[...] + jnp.broadcast_to(incl[:, tm - 1:tm], carry.shape)
    cnt_ref[...] = carry[...]

    route = jnp.where(sub == 0, e1, jnp.where(sub == 1, e2, jnp.where(sub == 2, r1, jnp.where(sub == 3, r2, 0.0))))
    route_ref[...] = route.astype(jnp.int32)
    gate_ref[...] = jnp.where(sub == 0, gate1, jnp.where(sub == 1, gate2, 0.0))


def _route(merged, x2d, w_out, ln_g, ln_b, w_r, b_r, upper, alpha, tm):
    t, d = x2d.shape
    const = lambda shape: pl.BlockSpec(shape, lambda i: (0,) * len(shape))
    rows = lambda width: pl.BlockSpec((tm, width), lambda i: (i, 0))
    cols = pl.BlockSpec((SUBLANES, tm), lambda i: (0, i))
    return pl.pallas_call(
        functools.partial(_route_kernel, alpha=alpha, tm=tm),
        grid=(t // tm,),
        in_specs=[rows(d), rows(d), const(w_out.shape), const((1, d)), const((1, d)),
                  const(w_r.shape), const((1, LANES)), const((tm, tm))],
        out_specs=[pl.BlockSpec((tm * _pitch(d), LANES), lambda i: (i, 0)),
                   cols, cols, const((N_EXPERTS, LANES))],
        out_shape=[jax.ShapeDtypeStruct((t * _pitch(d), LANES), F32),
                   jax.ShapeDtypeStruct((SUBLANES, t), jnp.int32),
                   jax.ShapeDtypeStruct((SUBLANES, t), F32),
                   jax.ShapeDtypeStruct((N_EXPERTS, LANES), F32)],
        scratch_shapes=[pltpu.VMEM((N_EXPERTS, LANES), F32)],
        compiler_params=_params(1),
        name="route",
    )(merged, x2d, w_out, ln_g, ln_b, w_r, b_r, upper)


def _dispatch_kernel(pad_start_ref, pad_len_ref, nu_ref, dest_ref, x_ref, xs_hbm, sem, *, tb, te, d):
    def copy(src_tok, dst_tok, n=1):
        return pltpu.make_async_copy(_tokens(x_ref, src_tok, d, n), _tokens(xs_hbm, dst_tok, d, n), sem)

    def wait_tokens(n):
        def body(r, c):
            copy(0, 0, n).wait()
            return c
        return body

    @pl.when(pl.program_id(0) == 0)
    def _():
        def per_expert(e, total):
            start = pad_start_ref[e]
            npad = pad_len_ref[e]

            def fill(r, c):
                copy(0, start + r).start()
                return c

            lax.fori_loop(0, npad, fill, 0)
            return total + npad

        total = lax.fori_loop(0, N_EXPERTS, per_expert, 0)
        lax.fori_loop(0, total // SUBLANES, wait_tokens(SUBLANES), 0)
        lax.fori_loop(0, total % SUBLANES, wait_tokens(1), 0)

        def fill_tile(j, c):
            copy(0, j * te, te).start()
            return c

        n_tiles = xs_hbm.shape[0] // (te * _pitch(d))
        lax.fori_loop(nu_ref[0], n_tiles, fill_tile, 0)
        lax.fori_loop(nu_ref[0], n_tiles, wait_tokens(te), 0)

    def scatter(t, c):
        for k in range(TOP_K):
            copy(t, dest_ref[TOP_K * t + k]).start(priority=k % 2)
        return c

    lax.fori_loop(0, tb, scatter, 0, unroll=DMA_UNROLL)
    for k in range(TOP_K):
        copy(0, 0, tb).wait()


def _dispatch(x1t, dest_flat, pad_start, pad_len, n_used, n_slots, tb, te, d):
    pitch = _pitch(d)
    t = x1t.shape[0] // pitch
    grid_spec = pltpu.PrefetchScalarGridSpec(
        num_scalar_prefetch=3,
        grid=(t // tb,),
        in_specs=[pl.BlockSpec((TOP_K * tb,), lambda i, ps, pn, nu: (i,), memory_space=pltpu.SMEM),
                  pl.BlockSpec((tb * pitch, LANES), lambda i, ps, pn, nu: (i, 0))],
        out_specs=pl.BlockSpec(memory_space=pl.ANY),
        scratch_shapes=[pltpu.SemaphoreType.DMA(())],
    )
    return pl.pallas_call(
        functools.partial(_dispatch_kernel, tb=tb, te=te, d=d),
        grid_spec=grid_spec,
        out_shape=jax.ShapeDtypeStruct((n_slots * pitch, LANES), x1t.dtype),
        compiler_params=_params(1),
        name="dispatch",
    )(pad_start, pad_len, n_used, dest_flat, x1t)


def _expert_kernel(be_ref, nu_ref, x_ref, wg_ref, wu_ref, wd_ref, y_ref, *, te, d):
    i = pl.program_id(0)

    @pl.when(i < nu_ref[0])
    def _():
        xb = _load_token_major(x_ref, te, d).astype(BF16)
        gt = _dot(xb, wg_ref[0])
        up = _dot(xb, wu_ref[0])
        hid = gt * jax.nn.sigmoid(gt) * up
        _store_token_major(y_ref, _dot(hid.astype(BF16), wd_ref[0]))

    @pl.when(i >= nu_ref[0])
    def _():
        y_ref[...] = jnp.zeros_like(y_ref)


def _experts(xs, block_e, n_used, w_gate, w_up, w_down, te):
    _, d, de = w_gate.shape
    pitch = _pitch(d)
    n_tiles = xs.shape[0] // (te * pitch)
    grid_spec = pltpu.PrefetchScalarGridSpec(
        num_scalar_prefetch=2,
        grid=(n_tiles,),
        in_specs=[pl.BlockSpec((te * pitch, LANES), lambda i, be, nu: (jnp.minimum(i, nu[0] - 1), 0)),
                  pl.BlockSpec((1, d, de), lambda i, be, nu: (be[i], 0, 0)),
                  pl.BlockSpec((1, d, de), lambda i, be, nu: (be[i], 0, 0)),
                  pl.BlockSpec((1, de, d), lambda i, be, nu: (be[i], 0, 0))],
        out_specs=pl.BlockSpec((te * pitch, LANES), lambda i, be, nu: (i, 0)),
    )
    return pl.pallas_call(
        functools.partial(_expert_kernel, te=te, d=d),
        grid_spec=grid_spec,
        out_shape=jax.ShapeDtypeStruct(xs.shape, F32),
        compiler_params=_params(1),
        name="experts",
    )(block_e, n_used, xs, w_gate, w_up, w_down)


def _final_kernel(dcur_ref, dnxt_ref, x1_ref, gate_ref, p_ref, ys_hbm, g_ref, b_ref, wg_ref, bg_ref, wp_ref,
                  out_ref, ybuf, sem, *, alpha, tm, d):
    i = pl.program_id(0)
    slot = lax.rem(i, 2)

    def issue(dest_ref, s):
        def body(t, c):
            for k in range(TOP_K):
                pltpu.make_async_copy(_tokens(ys_hbm, dest_ref[TOP_K * t + k], d),
                                      _tokens(ybuf.at[s, k], t, d), sem.at[s]).start(priority=k % 2)
            return c
        lax.fori_loop(0, tm, body, 0, unroll=DMA_UNROLL)

    @pl.when(i == 0)
    def _():
        issue(dcur_ref, 0)

    @pl.when(i + 1 < pl.num_programs(0))
    def _():
        issue(dnxt_ref, 1 - slot)

    for k in range(TOP_K):
        pltpu.make_async_copy(_tokens(ys_hbm, 0, d, tm), ybuf.at[slot, k], sem.at[slot]).wait()

    hm = tm // ROW_SPLIT
    for r0 in range(0, tm, hm):
        rows = slice(r0, r0 + hm)
        trows = pl.ds(r0 * _pitch(d), hm * _pitch(d))
        pp = _dot(p_ref[rows, :].astype(BF16), wp_ref[...])
        gate = gate_ref[rows, :]
        y = (gate[:, 0:1] * _load_token_major(ybuf.at[slot, 0, trows], hm, d)
             + gate[:, 1:2] * _load_token_major(ybuf.at[slot, 1, trows], hm, d))
        x2 = _layer_norm(alpha * _load_token_major(x1_ref.at[trows], hm, d) + y, g_ref[...], b_ref[...])
        gl = _dot(x2.astype(BF16), wg_ref[...]) + bg_ref[...]
        out_ref[rows, :] = x2 + jax.nn.sigmoid(gl) * pp


def _final(dest_flat, x1t, gate, p2d, ys, ln_g, ln_b, w_pg, b_pg, w_pp, alpha, tm):
    d = w_pg.shape[0]
    pitch = _pitch(d)
    t = x1t.shape[0] // pitch
    n = t // tm
    const = lambda shape: pl.BlockSpec(shape, lambda i: (0,) * len(shape))
    resident = lambda shape: pl.BlockSpec(shape, lambda i: (0,) * len(shape), pipeline_mode=pl.Buffered(1))
    rows = lambda width: pl.BlockSpec((tm, width), lambda i: (i, 0))
    return pl.pallas_call(
        functools.partial(_final_kernel, alpha=alpha, tm=tm, d=d),
        grid=(n,),
        in_specs=[pl.BlockSpec((TOP_K * tm,), lambda i: (i,), memory_space=pltpu.SMEM),
                  pl.BlockSpec((TOP_K * tm,), lambda i: (jnp.minimum(i + 1, n - 1),), memory_space=pltpu.SMEM),
                  pl.BlockSpec((tm * pitch, LANES), lambda i: (i, 0)),
                  rows(LANES), rows(p2d.shape[1]),
                  pl.BlockSpec(memory_space=pl.ANY),
                  const((1, d)), const((1, d)), resident(w_pg.shape), const((1, d)), resident(w_pp.shape)],
        out_specs=rows(d),
        out_shape=jax.ShapeDtypeStruct((t, d), F32),
        scratch_shapes=[pltpu.VMEM((2, TOP_K, tm * pitch, LANES), F32), pltpu.SemaphoreType.DMA((2,))],
        compiler_params=_params(1),
        name="final",
    )(dest_flat, dest_flat, x1t, gate, p2d, ys, ln_g, ln_b, w_pg, b_pg, w_pp)


def _pad_cols(a, width):
    return jnp.pad(a, ((0, 0), (0, width - a.shape[1])))


def _tri(n):
    return jnp.tril(jnp.ones((n, n), BF16))


def _layer(x, p, w_in, b_in, conv_w, conv_b, mh_g, w_pool, pool_scale, w_m_br, w_p_br, w_out,
           ln1_g, ln1_b, w_rg, b_rg, w_re, b_re, w_gate, w_up, w_down, ln2_g, ln2_b,
           w_ple_gate, b_ple_gate, w_ple_proj, alpha):
    nb, seq, d = x.shape
    t = nb * seq
    x2d = x.reshape(t, d)
    row = lambda a: a.reshape(1, -1)

    c_if = 2 * M_QK + 2 * M_V
    w_main = jnp.concatenate([w_in[:, :c_if].astype(BF16), w_in[:, c_if + 2 * M_HEADS:].astype(BF16)], axis=1)
    b_main = row(jnp.concatenate([b_in[:c_if], b_in[c_if + 2 * M_HEADS:]]))
    w_if = _pad_cols(w_in[:, c_if:c_if + 2 * M_HEADS], LANES).astype(BF16)
    b_if = _pad_cols(row(b_in[c_if:c_if + 2 * M_HEADS]), LANES)

    n_exp, _, d_exp = w_gate.shape
    expert_w = [w_gate.reshape(n_exp * d, d_exp), w_up.reshape(n_exp * d, d_exp), w_down.reshape(n_exp * d_exp, d)]
    tm_in = min(1024, t)
    z_main, z_if, wg_b, wu_b, wd_b = _inproj(x2d, w_main, b_main, w_if, b_if, tm_in, 1024, expert_w)

    hg = _mlstm(z_main.reshape(nb, seq, -1), z_if.reshape(nb, seq, LANES),
                conv_w, row(conv_b), row(mh_g))

    tm = min(512, seq)
    merged = _branch(hg.reshape(t, M_V), z_main, w_pool.astype(BF16), row(pool_scale),
                     w_m_br.astype(BF16), w_p_br.astype(BF16), seq, tm)

    w_r = _pad_cols(jnp.concatenate([_pad_cols(w_rg, SUBLANES), w_re], axis=1), LANES).astype(BF16)
    b_r = _pad_cols(jnp.concatenate([_pad_cols(row(b_rg), SUBLANES), row(b_re)], axis=1), LANES)
    x1, route, gate_t, cnt = _route(merged, x2d, w_out.astype(BF16), row(ln1_g), row(ln1_b),
                                    w_r, b_r, _tri(tm).T, alpha, tm)
    gate = _pad_cols(gate_t[0:TOP_K].T, LANES)

    te = EXPERT_TILE
    counts = cnt[:, 0].astype(jnp.int32)
    pcounts = (counts + te - 1) // te * te
    pends = jnp.cumsum(pcounts)
    pstarts = pends - pcounts
    e_sel = route[0:TOP_K, :, None] == jnp.arange(N_EXPERTS, dtype=jnp.int32)
    dest = (jnp.sum(jnp.where(e_sel, pstarts, 0), axis=-1) + route[TOP_K:2 * TOP_K]).T.reshape(-1)
    n_slots = t * TOP_K + N_EXPERTS * te
    n_tiles = n_slots // te
    n_used = (pends[-1] // te).reshape(1)
    tile_row = jnp.minimum(jnp.arange(n_tiles, dtype=jnp.int32), n_used - 1) * te
    block_e = jnp.minimum(jnp.sum((tile_row[:, None] >= pends[None, :]).astype(jnp.int32), axis=1), N_EXPERTS - 1)

    xs = _dispatch(x1, dest, pstarts + counts, pcounts - counts, n_used, n_slots, min(512, t), te, d)
    ys = _experts(xs, block_e, n_used, wg_b.reshape(n_exp, d, d_exp), wu_b.reshape(n_exp, d, d_exp),
                  wd_b.reshape(n_exp, d_exp, d), te)
    return _final(dest, x1, gate, p.reshape(t, -1), ys, row(ln2_g), row(ln2_b),
                  w_ple_gate.astype(BF16), row(b_ple_gate), w_ple_proj.astype(BF16), alpha,
                  min(512, t)).reshape(nb, seq, d)


def kernel(x, p, w_in, b_in, conv_w, conv_b, mh_g, w_pool, pool_scale, w_m_br, w_p_br, w_out, ln1_g, ln1_b, w_rg, b_rg, w_re, b_re, w_gate, w_up, w_down, ln2_g, ln2_b, w_ple_gate, b_ple_gate, w_ple_proj):
    depth = w_in.shape[0]
    alpha = (2 * depth) ** 0.25
    for i in range(depth):
        x = _layer(x, p[i], w_in[i], b_in[i], conv_w[i], conv_b[i], mh_g[i], w_pool[i], pool_scale[i],
                   w_m_br[i], w_p_br[i], w_out[i], ln1_g[i], ln1_b[i], w_rg[i], b_rg[i], w_re[i], b_re[i],
                   w_gate[i], w_up[i], w_down[i], ln2_g[i], ln2_b[i], w_ple_gate[i], b_ple_gate[i],
                   w_ple_proj[i], alpha)
    return x
```

```python
import functools

import jax
import jax.numpy as jnp
from jax import lax
from jax.experimental import pallas as pl
from jax.experimental.pallas import tpu as pltpu

F32 = jnp.float32
BF16 = jnp.bfloat16

M_HEADS = 4
M_QK_DIM = 128
M_V_DIM = 256
M_QK = M_HEADS * M_QK_DIM
M_V = M_HEADS * M_V_DIM
CONV_W = 4
CHUNK = 128
POOL_WINDOWS = (2, 4, 8, 16)
POOL_GROUP_DIM = 256
POOL_W = len(POOL_WINDOWS) * POOL_GROUP_DIM
N_GROUPS = 4
EXPERTS_PER_GROUP = 8
N_EXPERTS = N_GROUPS * EXPERTS_PER_GROUP
TOP_K = 2
LN_EPS = 1e-5

LANES = 128
SUBLANES = 8
VMEM_LIMIT = 56 * 1024 * 1024
BF16_SUBLANES = 16
CONV_HALO = BF16_SUBLANES
POOL_HALO = 32
EXPERT_TILE = 256
DMA_UNROLL = 8
ROW_SPLIT = 2


def _dot(a, b):
    return jnp.dot(a, b, preferred_element_type=F32)


def _params(n_grid):
    return pltpu.CompilerParams(dimension_semantics=("arbitrary",) * n_grid,
                                vmem_limit_bytes=VMEM_LIMIT)


def _log_sigmoid(x):
    return -(jnp.maximum(-x, 0.0) + jnp.log1p(jnp.exp(-jnp.abs(x))))


def _layer_norm(r, g, b):
    mu = jnp.mean(r, axis=-1, keepdims=True)
    d = r - mu
    var = jnp.mean(d * d, axis=-1, keepdims=True)
    return d * lax.rsqrt(var + LN_EPS) * g + b


TOKEN_WORD = F32


def _pitch(d):
    return d // LANES + 1


def _store_token_major(ref, val):
    n, d = val.shape
    for c in range(d // LANES):
        ref[pl.ds(c, n, stride=_pitch(d)), :] = val[:, c * LANES:(c + 1) * LANES]
    ref[pl.ds(d // LANES, n, stride=_pitch(d)), :] = jnp.zeros((n, LANES), TOKEN_WORD)


def _load_token_major(ref, n, d, first=0, every=1):
    return jnp.concatenate([ref[pl.ds(first * _pitch(d) + c, n, stride=every * _pitch(d)), :]
                            for c in range(d // LANES)], axis=1)


def _tokens(ref, tok, d, n=1):
    return ref.at[pl.ds(tok * _pitch(d), n * _pitch(d))]


def _inproj_kernel(*refs, n_side):
    x_ref, w_ref, b_ref, wif_ref, bif_ref = refs[:5]
    side_in = refs[5:5 + n_side]
    z_ref, zif_ref = refs[5 + n_side:7 + n_side]
    side_out = refs[7 + n_side:7 + 2 * n_side]
    xb_ref = refs[7 + 2 * n_side]

    @pl.when(pl.program_id(1) == 0)
    def _():
        xb = x_ref[...].astype(BF16)
        xb_ref[...] = xb
        zif_ref[...] = _dot(xb, wif_ref[...]) + bif_ref[...]

    z_ref[...] = (_dot(xb_ref[...], w_ref[...]) + b_ref[...]).astype(BF16)
    for src, dst in zip(side_in, side_out):
        dst[...] = src[...].astype(BF16)


def _inproj(x2d, w_main, b_main, w_if, b_if, tm, tn, side):
    t, d = x2d.shape
    n = w_main.shape[1]
    nj = n // tn
    steps = (t // tm) * nj
    side_spec = lambda a: pl.BlockSpec((a.shape[0] // steps, a.shape[1]), lambda i, j: (i * nj + j, 0))
    return pl.pallas_call(
        functools.partial(_inproj_kernel, n_side=len(side)),
        grid=(t // tm, nj),
        in_specs=[pl.BlockSpec((tm, d), lambda i, j: (i, 0)),
                  pl.BlockSpec((d, tn), lambda i, j: (0, j)),
                  pl.BlockSpec((1, tn), lambda i, j: (0, j)),
                  pl.BlockSpec((d, LANES), lambda i, j: (0, 0)),
                  pl.BlockSpec((1, LANES), lambda i, j: (0, 0))] + [side_spec(a) for a in side],
        out_specs=[pl.BlockSpec((tm, tn), lambda i, j: (i, j)),
                   pl.BlockSpec((tm, LANES), lambda i, j: (i, 0))] + [side_spec(a) for a in side],
        out_shape=[jax.ShapeDtypeStruct((t, n), BF16),
                   jax.ShapeDtypeStruct((t, LANES), F32)] + [jax.ShapeDtypeStruct(a.shape, BF16) for a in side],
        scratch_shapes=[pltpu.VMEM((tm, d), BF16)],
        compiler_params=_params(2),
        name="inproj",
    )(x2d, w_main, b_main, w_if, b_if, *side)


def _mlstm_kernel(qk_ref, v_ref, o_ref, g_ref, cw_ref, cb_ref, mhg_ref, out_ref, cbuf, *state, nb):
    L = CHUNK
    halo = CONV_HALO
    ct_refs, m_ref = state[:M_HEADS], state[M_HEADS]

    @pl.when(pl.program_id(0) == 0)
    def _():
        cbuf[:, 0:halo, :] = jnp.zeros((nb, halo, 2 * M_QK), BF16)
        for ref in state:
            ref[...] = jnp.zeros_like(ref)

    cbuf[:, halo:halo + L, :] = qk_ref[...]
    scale = M_QK_DIM ** -0.5
    row = lax.broadcasted_iota(jnp.int32, (L, L), 0)
    col = lax.broadcasted_iota(jnp.int32, (L, L), 1)
    causal = col <= row
    upper = jnp.where(row <= col, 1.0, 0.0).astype(BF16)
    ones_blk = jnp.ones((L, LANES), BF16)
    lane8 = lax.broadcasted_iota(jnp.int32, (SUBLANES, L), 1)
    wrow = lax.broadcasted_iota(jnp.int32, (L, L + halo), 0)
    wcol = lax.broadcasted_iota(jnp.int32, (L, L + halo), 1)
    shifts = [jnp.where(wcol == wrow + (halo - (CONV_W - 1) + tap), 1.0, 0.0).astype(BF16)
              for tap in range(CONV_W - 1)]
    sub = lax.broadcasted_iota(jnp.int32, (SUBLANES, LANES), 0)

    def per_batch(b, carry):
        window = cbuf[b]
        acc = cb_ref[...] + _dot(shifts[0], window) * cw_ref[0:1, :]
        for tap in range(1, CONV_W - 1):
            acc = acc + _dot(shifts[tap], window) * cw_ref[tap:tap + 1, :]
        acc = acc + qk_ref[b].astype(F32) * cw_ref[CONV_W - 1:CONV_W, :]
        qk = acc * jax.nn.sigmoid(acc)
        cbuf[b, 0:halo, :] = cbuf[b, L:L + halo, :]

        g_row = g_ref[b].T[0:SUBLANES, :]
        lf = _log_sigmoid(g_row)
        lf_hi = lf.astype(BF16)
        lf_lo = (lf - lf_hi.astype(F32)).astype(BF16)
        cs = _dot(jnp.concatenate([lf_hi, lf_lo], axis=0), upper)
        b_row = pltpu.roll(cs[0:SUBLANES, :] + cs[SUBLANES:2 * SUBLANES, :], M_HEADS, 0)
        a_row = g_row - b_row
        cm = a_row
        for sh in (1, 2, 4, 8, 16, 32, 64):
            cm = jnp.maximum(cm, jnp.where(lane8 >= sh, pltpu.roll(cm, sh, 1), -jnp.inf))
        m_prev = m_ref[b]
        mx = jnp.maximum(m_prev, cm)
        mx_last = jnp.maximum(m_prev, jnp.max(a_row, axis=1, keepdims=True))
        carry_in = jnp.exp(m_prev - mx)
        floor = jnp.exp(-(b_row + mx))
        wts = jnp.exp(a_row - mx_last)
        decay = jnp.exp(m_prev - mx_last)
        m_ref[b] = jnp.where(sub < M_HEADS, b_row[:, L - 1:L] + mx_last, 0.0)
        packed = jnp.concatenate(
            [jnp.where(sub < M_HEADS, mx, pltpu.roll(carry_in, M_HEADS, 0)),
             jnp.where(sub < M_HEADS, floor, pltpu.roll(wts, M_HEADS, 0)),
             jnp.zeros((L - 2 * SUBLANES, L), F32)], axis=0).T

        heads = range(M_HEADS)
        vcols = [slice(h * M_V_DIM, (h + 1) * M_V_DIM) for h in heads]
        q = [qk[:, h * M_QK_DIM:(h + 1) * M_QK_DIM].astype(BF16) for h in heads]
        kf = [qk[:, M_QK + h * M_QK_DIM:M_QK + (h + 1) * M_QK_DIM] * scale for h in heads]
        v_ext = [jnp.concatenate([v_ref[b, :, vcols[h]], ones_blk], axis=1) for h in heads]
        ct = [ct_refs[h][b] for h in heads]
        col_of = lambda i, h: packed[:, i * M_HEADS + h:i * M_HEADS + h + 1]

        raw = [lax.dot_general(q[h], kf[h].astype(BF16), (((1,), (1,)), ((), ())),
                               preferred_element_type=F32) for h in heads]
        prev = [_dot(q[h], ct[h].astype(BF16)) for h in heads]
        pmat = [jnp.exp(jnp.where(causal, a_row[h:h + 1, :] - col_of(0, h), -jnp.inf)) for h in heads]
        s = [(raw[h] * pmat[h]).astype(BF16) for h in heads]
        numden = [_dot(s[h], v_ext[h]) + col_of(1, h) * prev[h] for h in heads]
        upd = [lax.dot_general((kf[h] * col_of(3, h)).astype(BF16), v_ext[h], (((0,), (0,)), ((), ())),
                               preferred_element_type=F32) for h in heads]
        for h in heads:
            dec = jnp.concatenate([decay[h:h + 1, :]] * (ct[h].shape[1] // LANES), axis=1)
            ct_refs[h][b] = dec * ct[h] + upd[h]
        inv = [1.0 / jnp.maximum(jnp.abs(numden[h][:, M_V_DIM:]), col_of(2, h)) for h in heads]
        hh = [numden[h][:, 0:M_V_DIM] * jnp.concatenate([inv[h]] * (M_V_DIM // LANES), axis=1) for h in heads]
        mu = [jnp.mean(hh[h], axis=-1, keepdims=True) for h in heads]
        dlt = [hh[h] - mu[h] for h in heads]
        var = [jnp.mean(dlt[h] * dlt[h], axis=-1, keepdims=True) for h in heads]
        for h in heads:
            hn = dlt[h] * lax.rsqrt(var[h] + LN_EPS) * mhg_ref[:, vcols[h]]
            og = jax.nn.sigmoid(o_ref[b, :, vcols[h]].astype(F32))
            out_ref[b, :, vcols[h]] = (hn * og).astype(BF16)
        return carry

    for b in range(nb):
        per_batch(b, 0)


def _mlstm(z3, zif3, conv_w, conv_b, mh_g):
    nb, s, _ = z3.shape
    L = CHUNK
    blk = lambda colblk: pl.BlockSpec((nb, L, M_V), lambda c: (0, c, colblk))
    const = lambda shape: pl.BlockSpec(shape, lambda c: (0,) * len(shape))
    return pl.pallas_call(
        functools.partial(_mlstm_kernel, nb=nb),
        grid=(s // L,),
        in_specs=[blk(0), blk(1), blk(2),
                  pl.BlockSpec((nb, L, LANES), lambda c: (0, c, 0)),
                  const((CONV_W, 2 * M_QK)), const((1, 2 * M_QK)), const((1, M_V))],
        out_specs=pl.BlockSpec((nb, L, M_V), lambda c: (0, c, 0)),
        out_shape=jax.ShapeDtypeStruct((nb, s, M_V), BF16),
        scratch_shapes=([pltpu.VMEM((nb, L + CONV_HALO, 2 * M_QK), BF16)]
                        + [pltpu.VMEM((nb, M_QK_DIM, M_V_DIM + LANES), F32)] * M_HEADS
                        + [pltpu.VMEM((nb, SUBLANES, L), F32)]),
        compiler_params=_params(1),
        name="mlstm",
    )(z3, z3, z3, zif3, conv_w, conv_b, mh_g)


def _branch_kernel(hg_ref, u_ref, uh_ref, gm_ref, gp_ref, wpool_ref, ps_ref, wm_ref, wp_ref,
                   out_ref, xa, xb, yp_ref, *, tm, seq):
    H = POOL_HALO
    G = POOL_GROUP_DIM
    t0 = lax.rem(pl.program_id(0) * tm, seq)
    u = u_ref[...].astype(F32)
    xa[H:H + tm, :] = u
    xa[0:H, :] = jnp.where(t0 == 0, 0.0, uh_ref[...].astype(F32))
    n = tm + H - 8
    xb[8:8 + n, :] = xa[8:8 + n, :] + xa[7:7 + n, :]
    n = tm + H - 16
    xa[16:16 + n, G:] = xb[16:16 + n, G:] + xb[14:14 + n, G:]
    n = tm + H - 24
    xb[24:24 + n, 2 * G:] = xa[24:24 + n, 2 * G:] + xa[20:20 + n, 2 * G:]
    xa[H:H + tm, 3 * G:] = xb[H:H + tm, 3 * G:] + xb[H - 8:H - 8 + tm, 3 * G:]
    tpos = t0 + lax.broadcasted_iota(jnp.int32, (tm, 1), 0)
    for g, win in enumerate(POOL_WINDOWS):
        src = (xb, xa, xb, xa)[g]
        cols = slice(g * G, (g + 1) * G)
        cnt = jnp.minimum(tpos + 1, win).astype(F32)
        y = src[H:H + tm, cols] / cnt - u[:, cols]
        yp = _dot(y.astype(BF16), wpool_ref[g]) * ps_ref[:, cols]
        yp_ref[:, cols] = yp.astype(BF16)
    hm = tm // ROW_SPLIT
    for r0 in range(0, tm, hm):
        rows = slice(r0, r0 + hm)
        pb = _dot(yp_ref[rows, :], wp_ref[...])
        a = _dot(hg_ref[rows, :], wm_ref[...])
        merged = (jax.nn.sigmoid(gm_ref[rows, :].astype(F32)) * a
                  + jax.nn.sigmoid(gp_ref[rows, :].astype(F32)) * pb)
        out_ref[rows, :] = merged.astype(BF16)


def _branch(hg2d, z_main, w_pool, pool_scale, w_m_br, w_p_br, seq, tm):
    t = hg2d.shape[0]
    d = w_m_br.shape[1]
    hb = tm // POOL_HALO
    const = lambda shape: pl.BlockSpec(shape, lambda i: (0,) * len(shape))
    return pl.pallas_call(
        functools.partial(_branch_kernel, tm=tm, seq=seq),
        grid=(t // tm,),
        in_specs=[pl.BlockSpec((tm, M_V), lambda i: (i, 0)),
                  pl.BlockSpec((tm, POOL_W), lambda i: (i, 3)),
                  pl.BlockSpec((POOL_HALO, POOL_W), lambda i: (jnp.maximum(i * hb - 1, 0), 3)),
                  pl.BlockSpec((tm, d), lambda i: (i, 2)),
                  pl.BlockSpec((tm, d), lambda i: (i, 3)),
                  const(w_pool.shape), const((1, POOL_W)), const(w_m_br.shape), const(w_p_br.shape)],
        out_specs=pl.BlockSpec((tm, d), lambda i: (i, 0)),
        out_shape=jax.ShapeDtypeStruct((t, d), BF16),
        scratch_shapes=[pltpu.VMEM((tm + POOL_HALO, POOL_W), F32),
                        pltpu.VMEM((tm + POOL_HALO, POOL_W), F32),
                        pltpu.VMEM((tm, POOL_W), BF16)],
        compiler_params=_params(1),
        name="branch",
    )(hg2d, z_main, z_main, z_main, z_main, w_pool, pool_scale, w_m_br, w_p_br)


def _route_kernel(mg_ref, x_ref, wout_ref, g_ref, b_ref, wr_ref, br_ref, upper_ref,
                  x1_ref, x1t_ref, route_ref, gate_ref, cnt_ref, carry, *, alpha, tm):
    @pl.when(pl.program_id(0) == 0)
    def _():
        carry[...] = jnp.zeros_like(carry)

    hm = tm // ROW_SPLIT
    pitch = _pitch(x_ref.shape[1])
    logits = []
    for r0 in range(0, tm, hm):
        r = alpha * x_ref[r0:r0 + hm, :] + _dot(mg_ref[r0:r0 + hm, :], wout_ref[...])
        x1 = _layer_norm(r, g_ref[...], b_ref[...])
        x1_ref[r0:r0 + hm, :] = x1
        _store_token_major(x1t_ref.at[r0 * pitch:(r0 + hm) * pitch], x1)
        logits.append(_dot(x1.astype(BF16), wr_ref[...]) + br_ref[...])
    lt = jnp.concatenate(logits, axis=0).T

    sub = lax.broadcasted_iota(jnp.int32, (SUBLANES, tm), 0).astype(F32)
    none = float(SUBLANES)

    def softmax(z):
        e = jnp.exp(z - jnp.max(z, axis=0, keepdims=True))
        return e / jnp.sum(e, axis=0, keepdims=True)

    def top1(vals):
        top = jnp.max(vals, axis=0, keepdims=True)
        return top, jnp.min(jnp.where(vals == top, sub, none), axis=0, keepdims=True)

    is_grp = sub < N_GROUPS
    pg = jnp.where(is_grp, softmax(jnp.where(is_grp, lt[0:SUBLANES, :], -jnp.inf)), -1.0)
    pg_top, g_idx = top1(pg)
    le_sel = lt[SUBLANES:2 * SUBLANES, :]
    for g in range(1, N_GROUPS):
        le_sel = jnp.where(g_idx == g, lt[(g + 1) * SUBLANES:(g + 2) * SUBLANES, :], le_sel)
    pe = softmax(le_sel)
    pe1, i1 = top1(pe)
    pe2, i2 = top1(jnp.where(sub == i1, -1.0, pe))
    den = pe1 + pe2
    gate1 = pg_top * pe1 / den
    gate2 = pg_top * pe2 / den
    e1 = g_idx * EXPERTS_PER_GROUP + i1
    e2 = g_idx * EXPERTS_PER_GROUP + i2

    eid = lax.broadcasted_iota(jnp.int32, (N_EXPERTS, tm), 0).astype(F32)
    hit1 = eid == e1
    hit2 = eid == e2
    onehot = jnp.where(hit1 | hit2, 1.0, 0.0)
    incl = _dot(onehot.astype(BF16), upper_ref[...])
    excl = incl - onehot + carry[:, 0:1]
    r1 = jnp.sum(jnp.where(hit1, excl, 0.0), axis=0, keepdims=True)
    r2 = jnp.sum(jnp.where(hit2, excl, 0.0), axis=0, keepdims=True)
    carry[...] = carry[...] + jnp.broadcast_to(incl[:, tm - 1:tm], carry.shape)
    cnt_ref[...] = carry[...]

    route = jnp.where(sub == 0, e1, jnp.where(sub == 1, e2, jnp.where(sub == 2, r1, jnp.where(sub == 3, r2, 0.0))))
    route_ref[...] = route.astype(jnp.int32)
    gate_ref[...] = jnp.where(sub == 0, gate1, jnp.where(sub == 1, gate2, 0.0))


def _route(merged, x2d, w_out, ln_g, ln_b, w_r, b_r, upper, alpha, tm):
    t, d = x2d.shape
    const = lambda shape: pl.BlockSpec(shape, lambda i: (0,) * len(shape))
    rows = lambda width: pl.BlockSpec((tm, width), lambda i: (i, 0))
    cols = pl.BlockSpec((SUBLANES, tm), lambda i: (0, i))
    return pl.pallas_call(
        functools.partial(_route_kernel, alpha=alpha, tm=tm),
        grid=(t // tm,),
        in_specs=[rows(d), rows(d), const(w_out.shape), const((1, d)), const((1, d)),
                  const(w_r.shape), const((1, LANES)), const((tm, tm))],
        out_specs=[rows(d), pl.BlockSpec((tm * _pitch(d), LANES), lambda i: (i, 0)),
                   cols, cols, const((N_EXPERTS, LANES))],
        out_shape=[jax.ShapeDtypeStruct((t, d), F32),
                   jax.ShapeDtypeStruct((t * _pitch(d), LANES), TOKEN_WORD),
                   jax.ShapeDtypeStruct((SUBLANES, t), jnp.int32),
                   jax.ShapeDtypeStruct((SUBLANES, t), F32),
                   jax.ShapeDtypeStruct((N_EXPERTS, LANES), F32)],
        scratch_shapes=[pltpu.VMEM((N_EXPERTS, LANES), F32)],
        compiler_params=_params(1),
        name="route",
    )(merged, x2d, w_out, ln_g, ln_b, w_r, b_r, upper)


def _dispatch_kernel(pad_start_ref, pad_len_ref, nu_ref, dest_ref, x_ref, xs_hbm, inv_ref, sem, *, tb, te, d):
    def copy(src_tok, dst_tok, n=1):
        return pltpu.make_async_copy(_tokens(x_ref, src_tok, d, n), _tokens(xs_hbm, dst_tok, d, n), sem)

    def wait_tokens(n):
        def body(r, c):
            copy(0, 0, n).wait()
            return c
        return body

    @pl.when(pl.program_id(0) == 0)
    def _():
        def per_expert(e, total):
            start = pad_start_ref[e]
            npad = pad_len_ref[e]

            def fill(r, c):
                copy(0, start + r).start()
                inv_ref[start + r] = 0
                return c

            lax.fori_loop(0, npad, fill, 0)
            return total + npad

        total = lax.fori_loop(0, N_EXPERTS, per_expert, 0)
        lax.fori_loop(0, total // SUBLANES, wait_tokens(SUBLANES), 0)
        lax.fori_loop(0, total % SUBLANES, wait_tokens(1), 0)

        def fill_tile(j, c):
            copy(0, j * te, te).start()

            def clear(r, cc):
                inv_ref[j * te + r] = 0
                return cc

            return lax.fori_loop(0, te, clear, c)

        n_tiles = xs_hbm.shape[0] // (te * _pitch(d))
        lax.fori_loop(nu_ref[0], n_tiles, fill_tile, 0)
        lax.fori_loop(nu_ref[0], n_tiles, wait_tokens(te), 0)

    first = pl.program_id(0) * tb

    def scatter(t, c):
        for k in range(TOP_K):
            slot = dest_ref[TOP_K * t + k]
            copy(t, slot).start(priority=k % 2)
            inv_ref[slot] = TOP_K * (first + t) + k
        return c

    lax.fori_loop(0, tb, scatter, 0, unroll=DMA_UNROLL)
    for k in range(TOP_K):
        copy(0, 0, tb).wait()


def _dispatch(x1t, dest_flat, pad_start, pad_len, n_used, n_slots, tb, te, d):
    pitch = _pitch(d)
    t = x1t.shape[0] // pitch
    grid_spec = pltpu.PrefetchScalarGridSpec(
        num_scalar_prefetch=3,
        grid=(t // tb,),
        in_specs=[pl.BlockSpec((TOP_K * tb,), lambda i, ps, pn, nu: (i,), memory_space=pltpu.SMEM),
                  pl.BlockSpec((tb * pitch, LANES), lambda i, ps, pn, nu: (i, 0))],
        out_specs=[pl.BlockSpec(memory_space=pl.ANY), pl.BlockSpec(memory_space=pltpu.SMEM)],
        scratch_shapes=[pltpu.SemaphoreType.DMA(())],
    )
    return pl.pallas_call(
        functools.partial(_dispatch_kernel, tb=tb, te=te, d=d),
        grid_spec=grid_spec,
        out_shape=[jax.ShapeDtypeStruct((n_slots * pitch, LANES), x1t.dtype),
                   jax.ShapeDtypeStruct((n_slots,), jnp.int32)],
        compiler_params=_params(1),
        name="dispatch",
    )(pad_start, pad_len, n_used, dest_flat, x1t)


def _expert_kernel(be_ref, nu_ref, valid_ref, inv_ref, x_ref, wg_ref, wu_ref, wd_ref, y_hbm, ybuf, sem, *, te, d):
    i = pl.program_id(0)
    slot = lax.rem(i, 2)

    def drain(s, n_tok):
        def wait(n):
            def body(r, c):
                pltpu.make_async_copy(_tokens(ybuf.at[s], 0, d, n), _tokens(y_hbm, 0, d, n), sem.at[s]).wait()
                return c
            return body
        lax.fori_loop(0, n_tok // SUBLANES, wait(SUBLANES), 0)
        lax.fori_loop(0, n_tok % SUBLANES, wait(1), 0)

    @pl.when(i >= 2)
    def _():
        drain(slot, valid_ref[jnp.maximum(i - 2, 0)])

    n_valid = valid_ref[i]

    @pl.when(n_valid > 0)
    def _():
        xb = _load_token_major(x_ref, te, d).astype(BF16)
        gt = _dot(xb, wg_ref[0])
        up = _dot(xb, wu_ref[0])
        hid = gt * jax.nn.sigmoid(gt) * up
        _store_token_major(ybuf.at[slot], _dot(hid.astype(BF16), wd_ref[0]))

        def scatter(r, c):
            pltpu.make_async_copy(_tokens(ybuf.at[slot], r, d), _tokens(y_hbm, inv_ref[r], d), sem.at[slot]).start()
            return c

        lax.fori_loop(0, n_valid, scatter, 0)

    @pl.when(i == pl.num_programs(0) - 1)
    def _():
        drain(1 - slot, valid_ref[jnp.maximum(i - 1, 0)])
        drain(slot, n_valid)


def _experts(xs, inv, block_e, n_used, valid, w_gate, w_up, w_down, n_assign, te):
    _, d, de = w_gate.shape
    pitch = _pitch(d)
    n_tiles = xs.shape[0] // (te * pitch)
    last = lambda i, nu: jnp.minimum(i, nu[0] - 1)
    grid_spec = pltpu.PrefetchScalarGridSpec(
        num_scalar_prefetch=3,
        grid=(n_tiles,),
        in_specs=[pl.BlockSpec((te,), lambda i, be, nu, va: (last(i, nu),), memory_space=pltpu.SMEM),
                  pl.BlockSpec((te * pitch, LANES), lambda i, be, nu, va: (last(i, nu), 0)),
                  pl.BlockSpec((1, d, de), lambda i, be, nu, va: (be[i], 0, 0)),
                  pl.BlockSpec((1, d, de), lambda i, be, nu, va: (be[i], 0, 0)),
                  pl.BlockSpec((1, de, d), lambda i, be, nu, va: (be[i], 0, 0))],
        out_specs=pl.BlockSpec(memory_space=pl.ANY),
        scratch_shapes=[pltpu.VMEM((2, te * pitch, LANES), TOKEN_WORD), pltpu.SemaphoreType.DMA((2,))],
    )
    return pl.pallas_call(
        functools.partial(_expert_kernel, te=te, d=d),
        grid_spec=grid_spec,
        out_shape=jax.ShapeDtypeStruct((n_assign * pitch, LANES), TOKEN_WORD),
        compiler_params=_params(1),
        name="experts",
    )(block_e, n_used, valid, inv, xs, w_gate, w_up, w_down)


def _final_kernel(x1_ref, gate_ref, p_ref, y_ref, g_ref, b_ref, wg_ref, bg_ref, wp_ref, out_ref, *, alpha, tm, d):
    hm = tm // ROW_SPLIT
    for r0 in range(0, tm, hm):
        rows = slice(r0, r0 + hm)
        pp = _dot(p_ref[rows, :].astype(BF16), wp_ref[...])
        gate = gate_ref[rows, :]
        y = gate[:, 0:1] * _load_token_major(y_ref, hm, d, TOP_K * r0, TOP_K)
        for k in range(1, TOP_K):
            y = y + gate[:, k:k + 1] * _load_token_major(y_ref, hm, d, TOP_K * r0 + k, TOP_K)
        x2 = _layer_norm(alpha * x1_ref[rows, :] + y, g_ref[...], b_ref[...])
        gl = _dot(x2.astype(BF16), wg_ref[...]) + bg_ref[...]
        out_ref[rows, :] = x2 + jax.nn.sigmoid(gl) * pp


def _final(x1, gate, p2d, ys, ln_g, ln_b, w_pg, b_pg, w_pp, alpha, tm):
    t, d = x1.shape
    const = lambda shape: pl.BlockSpec(shape, lambda i: (0,) * len(shape))
    resident = lambda shape: pl.BlockSpec(shape, lambda i: (0,) * len(shape), pipeline_mode=pl.Buffered(1))
    rows = lambda width: pl.BlockSpec((tm, width), lambda i: (i, 0))
    return pl.pallas_call(
        functools.partial(_final_kernel, alpha=alpha, tm=tm, d=d),
        grid=(t // tm,),
        in_specs=[rows(d), rows(LANES), rows(p2d.shape[1]),
                  pl.BlockSpec((TOP_K * tm * _pitch(d), LANES), lambda i: (i, 0)),
                  const((1, d)), const((1, d)), resident(w_pg.shape), const((1, d)), resident(w_pp.shape)],
        out_specs=rows(d),
        out_shape=jax.ShapeDtypeStruct((t, d), F32),
        compiler_params=_params(1),
        name="final",
    )(x1, gate, p2d, ys, ln_g, ln_b, w_pg, b_pg, w_pp)


def _pad_cols(a, width):
    return jnp.pad(a, ((0, 0), (0, width - a.shape[1])))


def _tri(n):
    return jnp.tril(jnp.ones((n, n), BF16))


def _layer(x, p, w_in, b_in, conv_w, conv_b, mh_g, w_pool, pool_scale, w_m_br, w_p_br, w_out,
           ln1_g, ln1_b, w_rg, b_rg, w_re, b_re, w_gate, w_up, w_down, ln2_g, ln2_b,
           w_ple_gate, b_ple_gate, w_ple_proj, alpha):
    nb, seq, d = x.shape
    t = nb * seq
    x2d = x.reshape(t, d)
    row = lambda a: a.reshape(1, -1)

    c_if = 2 * M_QK + 2 * M_V
    w_main = jnp.concatenate([w_in[:, :c_if].astype(BF16), w_in[:, c_if + 2 * M_HEADS:].astype(BF16)], axis=1)
    b_main = row(jnp.concatenate([b_in[:c_if], b_in[c_if + 2 * M_HEADS:]]))
    w_if = _pad_cols(w_in[:, c_if:c_if + 2 * M_HEADS], LANES).astype(BF16)
    b_if = _pad_cols(row(b_in[c_if:c_if + 2 * M_HEADS]), LANES)

    n_exp, _, d_exp = w_gate.shape
    expert_w = [w_gate.reshape(n_exp * d, d_exp), w_up.reshape(n_exp * d, d_exp), w_down.reshape(n_exp * d_exp, d)]
    tm_in = min(1024, t)
    z_main, z_if, wg_b, wu_b, wd_b = _inproj(x2d, w_main, b_main, w_if, b_if, tm_in, 1024, expert_w)

    hg = _mlstm(z_main.reshape(nb, seq, -1), z_if.reshape(nb, seq, LANES),
                conv_w, row(conv_b), row(mh_g))

    tm = min(512, seq)
    merged = _branch(hg.reshape(t, M_V), z_main, w_pool.astype(BF16), row(pool_scale),
                     w_m_br.astype(BF16), w_p_br.astype(BF16), seq, tm)

    w_r = _pad_cols(jnp.concatenate([_pad_cols(w_rg, SUBLANES), w_re], axis=1), LANES).astype(BF16)
    b_r = _pad_cols(jnp.concatenate([_pad_cols(row(b_rg), SUBLANES), row(b_re)], axis=1), LANES)
    x1, x1t, route, gate_t, cnt = _route(merged, x2d, w_out.astype(BF16), row(ln1_g), row(ln1_b),
                                    w_r, b_r, _tri(tm).T, alpha, tm)
    gate = _pad_cols(gate_t[0:TOP_K].T, LANES)

    te = EXPERT_TILE
    counts = cnt[:, 0].astype(jnp.int32)
    pcounts = (counts + te - 1) // te * te
    pends = jnp.cumsum(pcounts)
    pstarts = pends - pcounts
    e_sel = route[0:TOP_K, :, None] == jnp.arange(N_EXPERTS, dtype=jnp.int32)
    dest = (jnp.sum(jnp.where(e_sel, pstarts, 0), axis=-1) + route[TOP_K:2 * TOP_K]).T.reshape(-1)
    n_slots = t * TOP_K + N_EXPERTS * te
    n_tiles = n_slots // te
    n_used = (pends[-1] // te).reshape(1)
    tile_row = jnp.minimum(jnp.arange(n_tiles, dtype=jnp.int32), n_used - 1) * te
    block_e = jnp.minimum(jnp.sum((tile_row[:, None] >= pends[None, :]).astype(jnp.int32), axis=1), N_EXPERTS - 1)
    seg_end = jnp.sum(jnp.where(block_e[:, None] == jnp.arange(N_EXPERTS), pstarts + counts, 0), axis=1)
    tile_first = jnp.arange(n_tiles, dtype=jnp.int32) * te
    valid = jnp.where(tile_first < pends[-1], jnp.clip(seg_end - tile_first, 0, te), 0).astype(jnp.int32)

    xs, inv = _dispatch(x1t, dest, pstarts + counts, pcounts - counts, n_used, n_slots, min(512, t), te, d)
    ys = _experts(xs, inv, block_e, n_used, valid, wg_b.reshape(n_exp, d, d_exp), wu_b.reshape(n_exp, d, d_exp),
                  wd_b.reshape(n_exp, d_exp, d), t * TOP_K, te)
    return _final(x1, gate, p.reshape(t, -1), ys, row(ln2_g), row(ln2_b),
                  w_ple_gate.astype(BF16), row(b_ple_gate), w_ple_proj.astype(BF16), alpha,
                  min(512, t)).reshape(nb, seq, d)


def kernel(x, p, w_in, b_in, conv_w, conv_b, mh_g, w_pool, pool_scale, w_m_br, w_p_br, w_out, ln1_g, ln1_b, w_rg, b_rg, w_re, b_re, w_gate, w_up, w_down, ln2_g, ln2_b, w_ple_gate, b_ple_gate, w_ple_proj):
    depth = w_in.shape[0]
    alpha = (2 * depth) ** 0.25
    for i in range(depth):
        x = _layer(x, p[i], w_in[i], b_in[i], conv_w[i], conv_b[i], mh_g[i], w_pool[i], pool_scale[i],
                   w_m_br[i], w_p_br[i], w_out[i], ln1_g[i], ln1_b[i], w_rg[i], b_rg[i], w_re[i], b_re[i],
                   w_gate[i], w_up[i], w_down[i], ln2_g[i], ln2_b[i], w_ple_gate[i], b_ple_gate[i],
                   w_ple_proj[i], alpha)
    return x
```

```python
import functools

import jax
import jax.numpy as jnp
from jax import lax
from jax.experimental import pallas as pl
from jax.experimental.pallas import tpu as pltpu

F32 = jnp.float32
BF16 = jnp.bfloat16

M_HEADS = 4
M_QK_DIM = 128
M_V_DIM = 256
M_QK = M_HEADS * M_QK_DIM
M_V = M_HEADS * M_V_DIM
CONV_W = 4
CHUNK = 128
POOL_WINDOWS = (2, 4, 8, 16)
POOL_GROUP_DIM = 256
POOL_W = len(POOL_WINDOWS) * POOL_GROUP_DIM
N_GROUPS = 4
EXPERTS_PER_GROUP = 8
N_EXPERTS = N_GROUPS * EXPERTS_PER_GROUP
TOP_K = 2
LN_EPS = 1e-5

LANES = 128
SUBLANES = 8
VMEM_LIMIT = 56 * 1024 * 1024
BF16_SUBLANES = 16
CONV_HALO = BF16_SUBLANES
POOL_HALO = 32
EXPERT_TILE = 256
DMA_UNROLL = 8
ROW_SPLIT = 2


def _dot(a, b):
    return jnp.dot(a, b, preferred_element_type=F32)


def _params(n_grid):
    return pltpu.CompilerParams(dimension_semantics=("arbitrary",) * n_grid,
                                vmem_limit_bytes=VMEM_LIMIT)


def _log_sigmoid(x):
    return -(jnp.maximum(-x, 0.0) + jnp.log1p(jnp.exp(-jnp.abs(x))))


def _layer_norm(r, g, b):
    mu = jnp.mean(r, axis=-1, keepdims=True)
    d = r - mu
    var = jnp.mean(d * d, axis=-1, keepdims=True)
    return d * lax.rsqrt(var + LN_EPS) * g + b


def _pitch(d):
    return d // LANES + 1


def _store_token_major(ref, val):
    n, d = val.shape
    for c in range(d // LANES):
        ref[pl.ds(c, n, stride=_pitch(d)), :] = val[:, c * LANES:(c + 1) * LANES]
    ref[pl.ds(d // LANES, n, stride=_pitch(d)), :] = jnp.zeros((n, LANES), val.dtype)


def _load_token_major(ref, n, d):
    return jnp.concatenate([ref[pl.ds(c, n, stride=_pitch(d)), :] for c in range(d // LANES)], axis=1)


def _tokens(ref, tok, d, n=1):
    return ref.at[pl.ds(tok * _pitch(d), n * _pitch(d))]


def _regroup_kernel(a_ref, b_ref, out_ref, *, n_keep, skip):
    j = pl.program_id(1)

    @pl.when(j < n_keep)
    def _():
        out_ref[...] = a_ref[...].astype(BF16)

    @pl.when(j >= n_keep)
    def _():
        out_ref[...] = jnp.concatenate([a_ref[:, skip:], b_ref[:, 0:skip]], axis=1).astype(BF16)


def _regroup(w, cut, skip, tn, tr):
    rows, cols = w.shape
    n_out = cols - skip
    per = tn // LANES
    return pl.pallas_call(
        functools.partial(_regroup_kernel, n_keep=cut // tn, skip=skip),
        grid=(rows // tr, n_out // tn),
        in_specs=[pl.BlockSpec((tr, tn), lambda i, j: (i, j)),
                  pl.BlockSpec((tr, LANES), lambda i, j: (i, per * (j + 1)))],
        out_specs=pl.BlockSpec((tr, tn), lambda i, j: (i, j)),
        out_shape=jax.ShapeDtypeStruct((rows, n_out), BF16),
        compiler_params=_params(2),
        name="regroup",
    )(w, w)


def _inproj_kernel(*refs, n_side):
    x_ref, w_ref, b_ref, wif_ref, bif_ref = refs[:5]
    side_in = refs[5:5 + n_side]
    z_ref, zif_ref = refs[5 + n_side:7 + n_side]
    side_out = refs[7 + n_side:7 + 2 * n_side]
    xb_ref = refs[7 + 2 * n_side]

    @pl.when(pl.program_id(1) == 0)
    def _():
        xb = x_ref[...].astype(BF16)
        xb_ref[...] = xb
        zif_ref[...] = _dot(xb, wif_ref[...]) + bif_ref[...]

    z_ref[...] = (_dot(xb_ref[...], w_ref[...]) + b_ref[...]).astype(BF16)
    for src, dst in zip(side_in, side_out):
        dst[...] = src[...].astype(BF16)


def _inproj(x2d, w_main, b_main, w_if, b_if, tm, tn, side):
    t, d = x2d.shape
    n = w_main.shape[1]
    nj = n // tn
    steps = (t // tm) * nj
    side_spec = lambda a: pl.BlockSpec((a.shape[0] // steps, a.shape[1]), lambda i, j: (i * nj + j, 0))
    return pl.pallas_call(
        functools.partial(_inproj_kernel, n_side=len(side)),
        grid=(t // tm, nj),
        in_specs=[pl.BlockSpec((tm, d), lambda i, j: (i, 0)),
                  pl.BlockSpec((d, tn), lambda i, j: (0, j)),
                  pl.BlockSpec((1, tn), lambda i, j: (0, j)),
                  pl.BlockSpec((d, LANES), lambda i, j: (0, 0)),
                  pl.BlockSpec((1, LANES), lambda i, j: (0, 0))] + [side_spec(a) for a in side],
        out_specs=[pl.BlockSpec((tm, tn), lambda i, j: (i, j)),
                   pl.BlockSpec((tm, LANES), lambda i, j: (i, 0))] + [side_spec(a) for a in side],
        out_shape=[jax.ShapeDtypeStruct((t, n), BF16),
                   jax.ShapeDtypeStruct((t, LANES), F32)] + [jax.ShapeDtypeStruct(a.shape, BF16) for a in side],
        scratch_shapes=[pltpu.VMEM((tm, d), BF16)],
        compiler_params=_params(2),
        name="inproj",
    )(x2d, w_main, b_main, w_if, b_if, *side)


def _mlstm_kernel(*refs, nb, n_side):
    qk_ref, v_ref, o_ref, g_ref, cw_ref, cb_ref, mhg_ref = refs[:7]
    side_in = refs[7:7 + n_side]
    out_ref = refs[7 + n_side]
    side_out = refs[8 + n_side:8 + 2 * n_side]
    cbuf = refs[8 + 2 * n_side]
    state = refs[9 + 2 * n_side:]
    _mlstm_body(qk_ref, v_ref, o_ref, g_ref, cw_ref, cb_ref, mhg_ref, out_ref, cbuf, state, nb)
    for src, dst in zip(side_in, side_out):
        dst[...] = src[...].astype(BF16)


def _mlstm_body(qk_ref, v_ref, o_ref, g_ref, cw_ref, cb_ref, mhg_ref, out_ref, cbuf, state, nb):
    L = CHUNK
    halo = CONV_HALO
    ct_refs, m_ref = state[:M_HEADS], state[M_HEADS]

    @pl.when(pl.program_id(0) == 0)
    def _():
        cbuf[:, 0:halo, :] = jnp.zeros((nb, halo, 2 * M_QK), BF16)
        for ref in state:
            ref[...] = jnp.zeros_like(ref)

    cbuf[:, halo:halo + L, :] = qk_ref[...]
    scale = M_QK_DIM ** -0.5
    row = lax.broadcasted_iota(jnp.int32, (L, L), 0)
    col = lax.broadcasted_iota(jnp.int32, (L, L), 1)
    causal = col <= row
    upper = jnp.where(row <= col, 1.0, 0.0).astype(BF16)
    ones_blk = jnp.ones((L, LANES), BF16)
    lane8 = lax.broadcasted_iota(jnp.int32, (SUBLANES, L), 1)
    wrow = lax.broadcasted_iota(jnp.int32, (L, L + halo), 0)
    wcol = lax.broadcasted_iota(jnp.int32, (L, L + halo), 1)
    shifts = [jnp.where(wcol == wrow + (halo - (CONV_W - 1) + tap), 1.0, 0.0).astype(BF16)
              for tap in range(CONV_W - 1)]
    sub = lax.broadcasted_iota(jnp.int32, (SUBLANES, LANES), 0)

    def per_batch(b, carry):
        window = cbuf[b]
        acc = cb_ref[...] + _dot(shifts[0], window) * cw_ref[0:1, :]
        for tap in range(1, CONV_W - 1):
            acc = acc + _dot(shifts[tap], window) * cw_ref[tap:tap + 1, :]
        acc = acc + qk_ref[b].astype(F32) * cw_ref[CONV_W - 1:CONV_W, :]
        qk = acc * jax.nn.sigmoid(acc)
        cbuf[b, 0:halo, :] = cbuf[b, L:L + halo, :]

        g_row = g_ref[b].T[0:SUBLANES, :]
        lf = _log_sigmoid(g_row)
        lf_hi = lf.astype(BF16)
        lf_lo = (lf - lf_hi.astype(F32)).astype(BF16)
        cs = _dot(jnp.concatenate([lf_hi, lf_lo], axis=0), upper)
        b_row = pltpu.roll(cs[0:SUBLANES, :] + cs[SUBLANES:2 * SUBLANES, :], M_HEADS, 0)
        a_row = g_row - b_row
        cm = a_row
        for sh in (1, 2, 4, 8, 16, 32, 64):
            cm = jnp.maximum(cm, jnp.where(lane8 >= sh, pltpu.roll(cm, sh, 1), -jnp.inf))
        m_prev = m_ref[b]
        mx = jnp.maximum(m_prev, cm)
        mx_last = jnp.maximum(m_prev, jnp.max(a_row, axis=1, keepdims=True))
        carry_in = jnp.exp(m_prev - mx)
        floor = jnp.exp(-(b_row + mx))
        wts = jnp.exp(a_row - mx_last)
        decay = jnp.exp(m_prev - mx_last)
        m_ref[b] = jnp.where(sub < M_HEADS, b_row[:, L - 1:L] + mx_last, 0.0)
        packed = jnp.concatenate(
            [jnp.where(sub < M_HEADS, mx, pltpu.roll(carry_in, M_HEADS, 0)),
             jnp.where(sub < M_HEADS, floor, pltpu.roll(wts, M_HEADS, 0)),
             jnp.zeros((L - 2 * SUBLANES, L), F32)], axis=0).T

        heads = range(M_HEADS)
        vcols = [slice(h * M_V_DIM, (h + 1) * M_V_DIM) for h in heads]
        q = [qk[:, h * M_QK_DIM:(h + 1) * M_QK_DIM].astype(BF16) for h in heads]
        kf = [qk[:, M_QK + h * M_QK_DIM:M_QK + (h + 1) * M_QK_DIM] * scale for h in heads]
        v_ext = [jnp.concatenate([v_ref[b, :, vcols[h]], ones_blk], axis=1) for h in heads]
        ct = [ct_refs[h][b] for h in heads]
        col_of = lambda i, h: packed[:, i * M_HEADS + h:i * M_HEADS + h + 1]

        raw = [lax.dot_general(q[h], kf[h].astype(BF16), (((1,), (1,)), ((), ())),
                               preferred_element_type=F32) for h in heads]
        prev = [_dot(q[h], ct[h].astype(BF16)) for h in heads]
        pmat = [jnp.exp(jnp.where(causal, a_row[h:h + 1, :] - col_of(0, h), -jnp.inf)) for h in heads]
        s = [(raw[h] * pmat[h]).astype(BF16) for h in heads]
        numden = [_dot(s[h], v_ext[h]) + col_of(1, h) * prev[h] for h in heads]
        upd = [lax.dot_general((kf[h] * col_of(3, h)).astype(BF16), v_ext[h], (((0,), (0,)), ((), ())),
                               preferred_element_type=F32) for h in heads]
        for h in heads:
            dec = jnp.concatenate([decay[h:h + 1, :]] * (ct[h].shape[1] // LANES), axis=1)
            ct_refs[h][b] = dec * ct[h] + upd[h]
        inv = [1.0 / jnp.maximum(jnp.abs(numden[h][:, M_V_DIM:]), col_of(2, h)) for h in heads]
        hh = [numden[h][:, 0:M_V_DIM] * jnp.concatenate([inv[h]] * (M_V_DIM // LANES), axis=1) for h in heads]
        mu = [jnp.mean(hh[h], axis=-1, keepdims=True) for h in heads]
        dlt = [hh[h] - mu[h] for h in heads]
        var = [jnp.mean(dlt[h] * dlt[h], axis=-1, keepdims=True) for h in heads]
        for h in heads:
            hn = dlt[h] * lax.rsqrt(var[h] + LN_EPS) * mhg_ref[:, vcols[h]]
            og = jax.nn.sigmoid(o_ref[b, :, vcols[h]].astype(F32))
            out_ref[b, :, vcols[h]] = (hn * og).astype(BF16)
        return carry

    for b in range(nb):
        per_batch(b, 0)


def _mlstm(z3, zif3, conv_w, conv_b, mh_g, side):
    nb, s, _ = z3.shape
    L = CHUNK
    steps = s // L
    blk = lambda colblk: pl.BlockSpec((nb, L, M_V), lambda c: (0, c, colblk))
    const = lambda shape: pl.BlockSpec(shape, lambda c: (0,) * len(shape))
    side_spec = lambda a: pl.BlockSpec((a.shape[0] // steps, a.shape[1]), lambda c: (c, 0))
    return pl.pallas_call(
        functools.partial(_mlstm_kernel, nb=nb, n_side=len(side)),
        grid=(steps,),
        in_specs=[blk(0), blk(1), blk(2),
                  pl.BlockSpec((nb, L, LANES), lambda c: (0, c, 0)),
                  const((CONV_W, 2 * M_QK)), const((1, 2 * M_QK)), const((1, M_V))] + [side_spec(a) for a in side],
        out_specs=[pl.BlockSpec((nb, L, M_V), lambda c: (0, c, 0))] + [side_spec(a) for a in side],
        out_shape=[jax.ShapeDtypeStruct((nb, s, M_V), BF16)] + [jax.ShapeDtypeStruct(a.shape, BF16) for a in side],
        scratch_shapes=([pltpu.VMEM((nb, L + CONV_HALO, 2 * M_QK), BF16)]
                        + [pltpu.VMEM((nb, M_QK_DIM, M_V_DIM + LANES), F32)] * M_HEADS
                        + [pltpu.VMEM((nb, SUBLANES, L), F32)]),
        compiler_params=_params(1),
        name="mlstm",
    )(z3, z3, z3, zif3, conv_w, conv_b, mh_g, *side)


def _branch_kernel(hg_ref, u_ref, uh_ref, gm_ref, gp_ref, wpool_ref, ps_ref, wm_ref, wp_ref,
                   out_ref, xa, xb, yp_ref, *, tm, seq):
    H = POOL_HALO
    G = POOL_GROUP_DIM
    t0 = lax.rem(pl.program_id(0) * tm, seq)
    u = u_ref[...].astype(F32)
    xa[H:H + tm, :] = u
    xa[0:H, :] = jnp.where(t0 == 0, 0.0, uh_ref[...].astype(F32))
    n = tm + H - 8
    xb[8:8 + n, :] = xa[8:8 + n, :] + xa[7:7 + n, :]
    n = tm + H - 16
    xa[16:16 + n, G:] = xb[16:16 + n, G:] + xb[14:14 + n, G:]
    n = tm + H - 24
    xb[24:24 + n, 2 * G:] = xa[24:24 + n, 2 * G:] + xa[20:20 + n, 2 * G:]
    xa[H:H + tm, 3 * G:] = xb[H:H + tm, 3 * G:] + xb[H - 8:H - 8 + tm, 3 * G:]
    tpos = t0 + lax.broadcasted_iota(jnp.int32, (tm, 1), 0)
    for g, win in enumerate(POOL_WINDOWS):
        src = (xb, xa, xb, xa)[g]
        cols = slice(g * G, (g + 1) * G)
        cnt = jnp.minimum(tpos + 1, win).astype(F32)
        y = src[H:H + tm, cols] / cnt - u[:, cols]
        yp = _dot(y.astype(BF16), wpool_ref[g]) * ps_ref[:, cols]
        yp_ref[:, cols] = yp.astype(BF16)
    hm = tm // ROW_SPLIT
    for r0 in range(0, tm, hm):
        rows = slice(r0, r0 + hm)
        pb = _dot(yp_ref[rows, :], wp_ref[...])
        a = _dot(hg_ref[rows, :], wm_ref[...])
        merged = (jax.nn.sigmoid(gm_ref[rows, :].astype(F32)) * a
                  + jax.nn.sigmoid(gp_ref[rows, :].astype(F32)) * pb)
        out_ref[rows, :] = merged.astype(BF16)


def _branch(hg2d, z_main, w_pool, pool_scale, w_m_br, w_p_br, seq, tm):
    t = hg2d.shape[0]
    d = w_m_br.shape[1]
    hb = tm // POOL_HALO
    const = lambda shape: pl.BlockSpec(shape, lambda i: (0,) * len(shape))
    return pl.pallas_call(
        functools.partial(_branch_kernel, tm=tm, seq=seq),
        grid=(t // tm,),
        in_specs=[pl.BlockSpec((tm, M_V), lambda i: (i, 0)),
                  pl.BlockSpec((tm, POOL_W), lambda i: (i, 3)),
                  pl.BlockSpec((POOL_HALO, POOL_W), lambda i: (jnp.maximum(i * hb - 1, 0), 3)),
                  pl.BlockSpec((tm, d), lambda i: (i, 2)),
                  pl.BlockSpec((tm, d), lambda i: (i, 3)),
                  const(w_pool.shape), const((1, POOL_W)), const(w_m_br.shape), const(w_p_br.shape)],
        out_specs=pl.BlockSpec((tm, d), lambda i: (i, 0)),
        out_shape=jax.ShapeDtypeStruct((t, d), BF16),
        scratch_shapes=[pltpu.VMEM((tm + POOL_HALO, POOL_W), F32),
                        pltpu.VMEM((tm + POOL_HALO, POOL_W), F32),
                        pltpu.VMEM((tm, POOL_W), BF16)],
        compiler_params=_params(1),
        name="branch",
    )(hg2d, z_main, z_main, z_main, z_main, w_pool, pool_scale, w_m_br, w_p_br)


def _route_kernel(mg_ref, x_ref, wout_ref, g_ref, b_ref, wr_ref, br_ref, upper_ref,
                  x1_ref, route_ref, gate_ref, cnt_ref, carry, *, alpha, tm):
    @pl.when(pl.program_id(0) == 0)
    def _():
        carry[...] = jnp.zeros_like(carry)

    hm = tm // ROW_SPLIT
    pitch = _pitch(x_ref.shape[1])
    logits = []
    for r0 in range(0, tm, hm):
        r = alpha * x_ref[r0:r0 + hm, :] + _dot(mg_ref[r0:r0 + hm, :], wout_ref[...])
        x1 = _layer_norm(r, g_ref[...], b_ref[...])
        _store_token_major(x1_ref.at[r0 * pitch:(r0 + hm) * pitch], x1)
        logits.append(_dot(x1.astype(BF16), wr_ref[...]) + br_ref[...])
    lt = jnp.concatenate(logits, axis=0).T

    sub = lax.broadcasted_iota(jnp.int32, (SUBLANES, tm), 0).astype(F32)
    none = float(SUBLANES)

    def softmax(z):
        e = jnp.exp(z - jnp.max(z, axis=0, keepdims=True))
        return e / jnp.sum(e, axis=0, keepdims=True)

    def top1(vals):
        top = jnp.max(vals, axis=0, keepdims=True)
        return top, jnp.min(jnp.where(vals == top, sub, none), axis=0, keepdims=True)

    is_grp = sub < N_GROUPS
    pg = jnp.where(is_grp, softmax(jnp.where(is_grp, lt[0:SUBLANES, :], -jnp.inf)), -1.0)
    pg_top, g_idx = top1(pg)
    le_sel = lt[SUBLANES:2 * SUBLANES, :]
    for g in range(1, N_GROUPS):
        le_sel = jnp.where(g_idx == g, lt[(g + 1) * SUBLANES:(g + 2) * SUBLANES, :], le_sel)
    pe = softmax(le_sel)
    pe1, i1 = top1(pe)
    pe2, i2 = top1(jnp.where(sub == i1, -1.0, pe))
    den = pe1 + pe2
    gate1 = pg_top * pe1 / den
    gate2 = pg_top * pe2 / den
    e1 = g_idx * EXPERTS_PER_GROUP + i1
    e2 = g_idx * EXPERTS_PER_GROUP + i2

    eid = lax.broadcasted_iota(jnp.int32, (N_EXPERTS, tm), 0).astype(F32)
    hit1 = eid == e1
    hit2 = eid == e2
    onehot = jnp.where(hit1 | hit2, 1.0, 0.0)
    incl = _dot(onehot.astype(BF16), upper_ref[...])
    excl = incl - onehot + carry[:, 0:1]
    r1 = jnp.sum(jnp.where(hit1, excl, 0.0), axis=0, keepdims=True)
    r2 = jnp.sum(jnp.where(hit2, excl, 0.0), axis=0, keepdims=True)
    carry[...] = carry[...] + jnp.broadcast_to(incl[:, tm - 1:tm], carry.shape)
    cnt_ref[...] = carry[...]

    route = jnp.where(sub == 0, e1, jnp.where(sub == 1, e2, jnp.where(sub == 2, r1, jnp.where(sub == 3, r2, 0.0))))
    route_ref[...] = route.astype(jnp.int32)
    gate_ref[...] = jnp.where(sub == 0, gate1, jnp.where(sub == 1, gate2, 0.0))


def _route(merged, x2d, w_out, ln_g, ln_b, w_r, b_r, upper, alpha, tm):
    t, d = x2d.shape
    const = lambda shape: pl.BlockSpec(shape, lambda i: (0,) * len(shape))
    rows = lambda width: pl.BlockSpec((tm, width), lambda i: (i, 0))
    cols = pl.BlockSpec((SUBLANES, tm), lambda i: (0, i))
    return pl.pallas_call(
        functools.partial(_route_kernel, alpha=alpha, tm=tm),
        grid=(t // tm,),
        in_specs=[rows(d), rows(d), const(w_out.shape), const((1, d)), const((1, d)),
                  const(w_r.shape), const((1, LANES)), const((tm, tm))],
        out_specs=[pl.BlockSpec((tm * _pitch(d), LANES), lambda i: (i, 0)),
                   cols, cols, const((N_EXPERTS, LANES))],
        out_shape=[jax.ShapeDtypeStruct((t * _pitch(d), LANES), F32),
                   jax.ShapeDtypeStruct((SUBLANES, t), jnp.int32),
                   jax.ShapeDtypeStruct((SUBLANES, t), F32),
                   jax.ShapeDtypeStruct((N_EXPERTS, LANES), F32)],
        scratch_shapes=[pltpu.VMEM((N_EXPERTS, LANES), F32)],
        compiler_params=_params(1),
        name="route",
    )(merged, x2d, w_out, ln_g, ln_b, w_r, b_r, upper)


def _dispatch_kernel(pad_start_ref, pad_len_ref, nu_ref, dest_ref, x_ref, xs_hbm, sem, *, tb, te, d):
    def copy(src_tok, dst_tok, n=1):
        return pltpu.make_async_copy(_tokens(x_ref, src_tok, d, n), _tokens(xs_hbm, dst_tok, d, n), sem)

    def wait_tokens(n):
        def body(r, c):
            copy(0, 0, n).wait()
            return c
        return body

    @pl.when(pl.program_id(0) == 0)
    def _():
        def per_expert(e, total):
            start = pad_start_ref[e]
            npad = pad_len_ref[e]

            def fill(r, c):
                copy(0, start + r).start()
                return c

            lax.fori_loop(0, npad, fill, 0)
            return total + npad

        total = lax.fori_loop(0, N_EXPERTS, per_expert, 0)
        lax.fori_loop(0, total // SUBLANES, wait_tokens(SUBLANES), 0)
        lax.fori_loop(0, total % SUBLANES, wait_tokens(1), 0)

        def fill_tile(j, c):
            copy(0, j * te, te).start()
            return c

        n_tiles = xs_hbm.shape[0] // (te * _pitch(d))
        lax.fori_loop(nu_ref[0], n_tiles, fill_tile, 0)
        lax.fori_loop(nu_ref[0], n_tiles, wait_tokens(te), 0)

    def scatter(t, c):
        for k in range(TOP_K):
            copy(t, dest_ref[TOP_K * t + k]).start(priority=k % 2)
        return c

    lax.fori_loop(0, tb, scatter, 0, unroll=DMA_UNROLL)
    for k in range(TOP_K):
        copy(0, 0, tb).wait()


def _dispatch(x1t, dest_flat, pad_start, pad_len, n_used, n_slots, tb, te, d):
    pitch = _pitch(d)
    t = x1t.shape[0] // pitch
    grid_spec = pltpu.PrefetchScalarGridSpec(
        num_scalar_prefetch=3,
        grid=(t // tb,),
        in_specs=[pl.BlockSpec((TOP_K * tb,), lambda i, ps, pn, nu: (i,), memory_space=pltpu.SMEM),
                  pl.BlockSpec((tb * pitch, LANES), lambda i, ps, pn, nu: (i, 0))],
        out_specs=pl.BlockSpec(memory_space=pl.ANY),
        scratch_shapes=[pltpu.SemaphoreType.DMA(())],
    )
    return pl.pallas_call(
        functools.partial(_dispatch_kernel, tb=tb, te=te, d=d),
        grid_spec=grid_spec,
        out_shape=jax.ShapeDtypeStruct((n_slots * pitch, LANES), x1t.dtype),
        compiler_params=_params(1),
        name="dispatch",
    )(pad_start, pad_len, n_used, dest_flat, x1t)


def _expert_kernel(be_ref, nu_ref, x_ref, wg_ref, wu_ref, wd_ref, y_ref, *, te, d):
    i = pl.program_id(0)

    @pl.when(i < nu_ref[0])
    def _():
        xb = _load_token_major(x_ref, te, d).astype(BF16)
        gt = _dot(xb, wg_ref[0])
        up = _dot(xb, wu_ref[0])
        hid = gt * jax.nn.sigmoid(gt) * up
        _store_token_major(y_ref, _dot(hid.astype(BF16), wd_ref[0]))

    @pl.when(i >= nu_ref[0])
    def _():
        y_ref[...] = jnp.zeros_like(y_ref)


def _experts(xs, block_e, n_used, w_gate, w_up, w_down, te):
    _, d, de = w_gate.shape
    pitch = _pitch(d)
    n_tiles = xs.shape[0] // (te * pitch)
    grid_spec = pltpu.PrefetchScalarGridSpec(
        num_scalar_prefetch=2,
        grid=(n_tiles,),
        in_specs=[pl.BlockSpec((te * pitch, LANES), lambda i, be, nu: (jnp.minimum(i, nu[0] - 1), 0)),
                  pl.BlockSpec((1, d, de), lambda i, be, nu: (be[i], 0, 0)),
                  pl.BlockSpec((1, d, de), lambda i, be, nu: (be[i], 0, 0)),
                  pl.BlockSpec((1, de, d), lambda i, be, nu: (be[i], 0, 0))],
        out_specs=pl.BlockSpec((te * pitch, LANES), lambda i, be, nu: (i, 0)),
    )
    return pl.pallas_call(
        functools.partial(_expert_kernel, te=te, d=d),
        grid_spec=grid_spec,
        out_shape=jax.ShapeDtypeStruct(xs.shape, F32),
        compiler_params=_params(1),
        name="experts",
    )(block_e, n_used, xs, w_gate, w_up, w_down)


def _final_kernel(dcur_ref, dnxt_ref, x1_ref, gate_ref, p_ref, ys_hbm, g_ref, b_ref, wg_ref, bg_ref, wp_ref,
                  out_ref, ybuf, sem, *, alpha, tm, d):
    i = pl.program_id(0)
    slot = lax.rem(i, 2)

    def issue(dest_ref, s):
        def body(t, c):
            for k in range(TOP_K):
                pltpu.make_async_copy(_tokens(ys_hbm, dest_ref[TOP_K * t + k], d),
                                      _tokens(ybuf.at[s, k], t, d), sem.at[s]).start(priority=k % 2)
            return c
        lax.fori_loop(0, tm, body, 0, unroll=DMA_UNROLL)

    @pl.when(i == 0)
    def _():
        issue(dcur_ref, 0)

    @pl.when(i + 1 < pl.num_programs(0))
    def _():
        issue(dnxt_ref, 1 - slot)

    for k in range(TOP_K):
        pltpu.make_async_copy(_tokens(ys_hbm, 0, d, tm), ybuf.at[slot, k], sem.at[slot]).wait()

    hm = tm // ROW_SPLIT
    for r0 in range(0, tm, hm):
        rows = slice(r0, r0 + hm)
        trows = pl.ds(r0 * _pitch(d), hm * _pitch(d))
        pp = _dot(p_ref[rows, :].astype(BF16), wp_ref[...])
        gate = gate_ref[rows, :]
        y = (gate[:, 0:1] * _load_token_major(ybuf.at[slot, 0, trows], hm, d)
             + gate[:, 1:2] * _load_token_major(ybuf.at[slot, 1, trows], hm, d))
        x2 = _layer_norm(alpha * _load_token_major(x1_ref.at[trows], hm, d) + y, g_ref[...], b_ref[...])
        gl = _dot(x2.astype(BF16), wg_ref[...]) + bg_ref[...]
        out_ref[rows, :] = x2 + jax.nn.sigmoid(gl) * pp


def _final(dest_flat, x1t, gate, p2d, ys, ln_g, ln_b, w_pg, b_pg, w_pp, alpha, tm):
    d = w_pg.shape[0]
    pitch = _pitch(d)
    t = x1t.shape[0] // pitch
    n = t // tm
    const = lambda shape: pl.BlockSpec(shape, lambda i: (0,) * len(shape))
    resident = lambda shape: pl.BlockSpec(shape, lambda i: (0,) * len(shape), pipeline_mode=pl.Buffered(1))
    rows = lambda width: pl.BlockSpec((tm, width), lambda i: (i, 0))
    return pl.pallas_call(
        functools.partial(_final_kernel, alpha=alpha, tm=tm, d=d),
        grid=(n,),
        in_specs=[pl.BlockSpec((TOP_K * tm,), lambda i: (i,), memory_space=pltpu.SMEM),
                  pl.BlockSpec((TOP_K * tm,), lambda i: (jnp.minimum(i + 1, n - 1),), memory_space=pltpu.SMEM),
                  pl.BlockSpec((tm * pitch, LANES), lambda i: (i, 0)),
                  rows(LANES), rows(p2d.shape[1]),
                  pl.BlockSpec(memory_space=pl.ANY),
                  const((1, d)), const((1, d)), resident(w_pg.shape), const((1, d)), resident(w_pp.shape)],
        out_specs=rows(d),
        out_shape=jax.ShapeDtypeStruct((t, d), F32),
        scratch_shapes=[pltpu.VMEM((2, TOP_K, tm * pitch, LANES), F32), pltpu.SemaphoreType.DMA((2,))],
        compiler_params=_params(1),
        name="final",
    )(dest_flat, dest_flat, x1t, gate, p2d, ys, ln_g, ln_b, w_pg, b_pg, w_pp)


def _pad_cols(a, width):
    return jnp.pad(a, ((0, 0), (0, width - a.shape[1])))


def _tri(n):
    return jnp.tril(jnp.ones((n, n), BF16))


def _layer(x, p, w_in, b_in, conv_w, conv_b, mh_g, w_pool, pool_scale, w_m_br, w_p_br, w_out,
           ln1_g, ln1_b, w_rg, b_rg, w_re, b_re, w_gate, w_up, w_down, ln2_g, ln2_b,
           w_ple_gate, b_ple_gate, w_ple_proj, alpha):
    nb, seq, d = x.shape
    t = nb * seq
    x2d = x.reshape(t, d)
    row = lambda a: a.reshape(1, -1)

    c_if = 2 * M_QK + 2 * M_V
    w_main = _regroup(w_in, c_if, 2 * M_HEADS, 1024, min(1024, d))
    b_main = row(jnp.concatenate([b_in[:c_if], b_in[c_if + 2 * M_HEADS:]]))
    w_if = _pad_cols(w_in[:, c_if:c_if + 2 * M_HEADS], LANES).astype(BF16)
    b_if = _pad_cols(row(b_in[c_if:c_if + 2 * M_HEADS]), LANES)

    n_exp, _, d_exp = w_gate.shape
    expert_w = [w_gate.reshape(n_exp * d, d_exp), w_up.reshape(n_exp * d, d_exp), w_down.reshape(n_exp * d_exp, d)]
    tm_in = min(1024, t)
    z_main, z_if, wg_b, wu_b, wd_b = _inproj(x2d, w_main, b_main, w_if, b_if, tm_in, 1024, expert_w)

    dense_w = [w_pool, w_m_br, w_p_br, w_out, w_ple_gate, w_ple_proj]
    hg, *dense_b = _mlstm(z_main.reshape(nb, seq, -1), z_if.reshape(nb, seq, LANES),
                          conv_w, row(conv_b), row(mh_g), [a.reshape(d, -1) for a in dense_w])
    w_pool, w_m_br, w_p_br, w_out, w_ple_gate, w_ple_proj = [b.reshape(a.shape) for a, b in zip(dense_w, dense_b)]

    tm = min(512, seq)
    merged = _branch(hg.reshape(t, M_V), z_main, w_pool, row(pool_scale), w_m_br, w_p_br, seq, tm)

    w_r = _pad_cols(jnp.concatenate([_pad_cols(w_rg, SUBLANES), w_re], axis=1), LANES).astype(BF16)
    b_r = _pad_cols(jnp.concatenate([_pad_cols(row(b_rg), SUBLANES), row(b_re)], axis=1), LANES)
    x1, route, gate_t, cnt = _route(merged, x2d, w_out, row(ln1_g), row(ln1_b),
                                    w_r, b_r, _tri(tm).T, alpha, tm)
    gate = _pad_cols(gate_t[0:TOP_K].T, LANES)

    te = EXPERT_TILE
    counts = cnt[:, 0].astype(jnp.int32)
    pcounts = (counts + te - 1) // te * te
    pends = jnp.cumsum(pcounts)
    pstarts = pends - pcounts
    e_sel = route[0:TOP_K, :, None] == jnp.arange(N_EXPERTS, dtype=jnp.int32)
    dest = (jnp.sum(jnp.where(e_sel, pstarts, 0), axis=-1) + route[TOP_K:2 * TOP_K]).T.reshape(-1)
    n_slots = t * TOP_K + N_EXPERTS * te
    n_tiles = n_slots // te
    n_used = (pends[-1] // te).reshape(1)
    tile_row = jnp.minimum(jnp.arange(n_tiles, dtype=jnp.int32), n_used - 1) * te
    block_e = jnp.minimum(jnp.sum((tile_row[:, None] >= pends[None, :]).astype(jnp.int32), axis=1), N_EXPERTS - 1)

    xs = _dispatch(x1, dest, pstarts + counts, pcounts - counts, n_used, n_slots, min(512, t), te, d)
    ys = _experts(xs, block_e, n_used, wg_b.reshape(n_exp, d, d_exp), wu_b.reshape(n_exp, d, d_exp),
                  wd_b.reshape(n_exp, d_exp, d), te)
    return _final(dest, x1, gate, p.reshape(t, -1), ys, row(ln2_g), row(ln2_b),
                  w_ple_gate, row(b_ple_gate), w_ple_proj, alpha,
                  min(512, t)).reshape(nb, seq, d)


def kernel(x, p, w_in, b_in, conv_w, conv_b, mh_g, w_pool, pool_scale, w_m_br, w_p_br, w_out, ln1_g, ln1_b, w_rg, b_rg, w_re, b_re, w_gate, w_up, w_down, ln2_g, ln2_b, w_ple_gate, b_ple_gate, w_ple_proj):
    depth = w_in.shape[0]
    alpha = (2 * depth) ** 0.25
    for i in range(depth):
        x = _layer(x, p[i], w_in[i], b_in[i], conv_w[i], conv_b[i], mh_g[i], w_pool[i], pool_scale[i],
                   w_m_br[i], w_p_br[i], w_out[i], ln1_g[i], ln1_b[i], w_rg[i], b_rg[i], w_re[i], b_re[i],
                   w_gate[i], w_up[i], w_down[i], ln2_g[i], ln2_b[i], w_ple_gate[i], b_ple_gate[i],
                   w_ple_proj[i], alpha)
    return x
```

```python
import functools

import jax
import jax.numpy as jnp
from jax import lax
from jax.experimental import pallas as pl
from jax.experimental.pallas import tpu as pltpu

F32 = jnp.float32
BF16 = jnp.bfloat16

M_HEADS = 4
M_QK_DIM = 128
M_V_DIM = 256
M_QK = M_HEADS * M_QK_DIM
M_V = M_HEADS * M_V_DIM
CONV_W = 4
CHUNK = 128
POOL_WINDOWS = (2, 4, 8, 16)
POOL_GROUP_DIM = 256
POOL_W = len(POOL_WINDOWS) * POOL_GROUP_DIM
N_GROUPS = 4
EXPERTS_PER_GROUP = 8
N_EXPERTS = N_GROUPS * EXPERTS_PER_GROUP
TOP_K = 2
LN_EPS = 1e-5

LANES = 128
SUBLANES = 8
VMEM_LIMIT = 56 * 1024 * 1024
BF16_SUBLANES = 16
CONV_HALO = BF16_SUBLANES
POOL_HALO = 32
EXPERT_TILE = 256
DMA_UNROLL = 8
ROW_SPLIT = 2


def _dot(a, b):
    return jnp.dot(a, b, preferred_element_type=F32)


def _params(n_grid):
    return pltpu.CompilerParams(dimension_semantics=("arbitrary",) * n_grid,
                                vmem_limit_bytes=VMEM_LIMIT)


def _log_sigmoid(x):
    return -(jnp.maximum(-x, 0.0) + jnp.log1p(jnp.exp(-jnp.abs(x))))


def _layer_norm(r, g, b):
    mu = jnp.mean(r, axis=-1, keepdims=True)
    d = r - mu
    var = jnp.mean(d * d, axis=-1, keepdims=True)
    return d * lax.rsqrt(var + LN_EPS) * g + b


def _pitch(d):
    return d // LANES + 1


def _store_token_major(ref, val):
    n, d = val.shape
    for c in range(d // LANES):
        ref[pl.ds(c, n, stride=_pitch(d)), :] = val[:, c * LANES:(c + 1) * LANES]
    ref[pl.ds(d // LANES, n, stride=_pitch(d)), :] = jnp.zeros((n, LANES), val.dtype)


def _load_token_major(ref, n, d):
    return jnp.concatenate([ref[pl.ds(c, n, stride=_pitch(d)), :] for c in range(d // LANES)], axis=1)


def _tokens(ref, tok, d, n=1):
    return ref.at[pl.ds(tok * _pitch(d), n * _pitch(d))]


def _regroup_kernel(a_ref, out_ref):
    out_ref[...] = a_ref[...].T.astype(BF16)


def _regroup(w_t, cut, skip, tn, tk):
    n, k = w_t.shape
    n_keep = cut // tn
    assert skip % SUBLANES == 0
    first_row = lambda j: pl.multiple_of(jnp.where(j < n_keep, j * tn, j * tn + skip), SUBLANES)
    return pl.pallas_call(
        _regroup_kernel,
        grid=((n - skip) // tn, k // tk),
        in_specs=[pl.BlockSpec((pl.Element(tn), pl.Element(tk)), lambda j, kk: (first_row(j), kk * tk))],
        out_specs=pl.BlockSpec((tk, tn), lambda j, kk: (kk, j)),
        out_shape=jax.ShapeDtypeStruct((k, n - skip), BF16),
        compiler_params=_params(2),
        name="regroup",
    )(w_t)


def _inproj_kernel(*refs, n_side):
    x_ref, w_ref, b_ref, wif_ref, bif_ref = refs[:5]
    side_in = refs[5:5 + n_side]
    z_ref, zif_ref = refs[5 + n_side:7 + n_side]
    side_out = refs[7 + n_side:7 + 2 * n_side]
    xb_ref = refs[7 + 2 * n_side]

    @pl.when(pl.program_id(1) == 0)
    def _():
        xb = x_ref[...].astype(BF16)
        xb_ref[...] = xb
        zif_ref[...] = _dot(xb, wif_ref[...].astype(BF16)) + bif_ref[...]

    z_ref[...] = (_dot(xb_ref[...], w_ref[...]) + b_ref[...]).astype(BF16)
    for src, dst in zip(side_in, side_out):
        dst[...] = src[...].astype(BF16)


def _inproj(x2d, w_main, b_main, w_if, b_if, tm, tn, side):
    t, d = x2d.shape
    n = w_main.shape[1]
    nj = n // tn
    steps = (t // tm) * nj
    side_spec = lambda a: pl.BlockSpec((a.shape[0] // steps, a.shape[1]), lambda i, j: (i * nj + j, 0))
    return pl.pallas_call(
        functools.partial(_inproj_kernel, n_side=len(side)),
        grid=(t // tm, nj),
        in_specs=[pl.BlockSpec((tm, d), lambda i, j: (i, 0)),
                  pl.BlockSpec((d, tn), lambda i, j: (0, j)),
                  pl.BlockSpec((1, tn), lambda i, j: (0, j)),
                  pl.BlockSpec((d, LANES), lambda i, j: (0, 0)),
                  pl.BlockSpec((1, LANES), lambda i, j: (0, 0))] + [side_spec(a) for a in side],
        out_specs=[pl.BlockSpec((tm, tn), lambda i, j: (i, j)),
                   pl.BlockSpec((tm, LANES), lambda i, j: (i, 0))] + [side_spec(a) for a in side],
        out_shape=[jax.ShapeDtypeStruct((t, n), BF16),
                   jax.ShapeDtypeStruct((t, LANES), F32)] + [jax.ShapeDtypeStruct(a.shape, BF16) for a in side],
        scratch_shapes=[pltpu.VMEM((tm, d), BF16)],
        compiler_params=_params(2),
        name="inproj",
    )(x2d, w_main, b_main, w_if, b_if, *side)


def _mlstm_kernel(*refs, nb, n_side):
    qk_ref, v_ref, o_ref, g_ref, cw_ref, cb_ref, mhg_ref = refs[:7]
    side_in = refs[7:7 + n_side]
    out_ref = refs[7 + n_side]
    side_out = refs[8 + n_side:8 + 2 * n_side]
    cbuf = refs[8 + 2 * n_side]
    state = refs[9 + 2 * n_side:]
    _mlstm_body(qk_ref, v_ref, o_ref, g_ref, cw_ref, cb_ref, mhg_ref, out_ref, cbuf, state, nb)
    for src, dst in zip(side_in, side_out):
        dst[...] = src[...].astype(BF16)


def _mlstm_body(qk_ref, v_ref, o_ref, g_ref, cw_ref, cb_ref, mhg_ref, out_ref, cbuf, state, nb):
    L = CHUNK
    halo = CONV_HALO
    ct_refs, m_ref = state[:M_HEADS], state[M_HEADS]

    @pl.when(pl.program_id(0) == 0)
    def _():
        cbuf[:, 0:halo, :] = jnp.zeros((nb, halo, 2 * M_QK), BF16)
        for ref in state:
            ref[...] = jnp.zeros_like(ref)

    cbuf[:, halo:halo + L, :] = qk_ref[...]
    scale = M_QK_DIM ** -0.5
    row = lax.broadcasted_iota(jnp.int32, (L, L), 0)
    col = lax.broadcasted_iota(jnp.int32, (L, L), 1)
    causal = col <= row
    upper = jnp.where(row <= col, 1.0, 0.0).astype(BF16)
    ones_blk = jnp.ones((L, LANES), BF16)
    lane8 = lax.broadcasted_iota(jnp.int32, (SUBLANES, L), 1)
    wrow = lax.broadcasted_iota(jnp.int32, (L, L + halo), 0)
    wcol = lax.broadcasted_iota(jnp.int32, (L, L + halo), 1)
    shifts = [jnp.where(wcol == wrow + (halo - (CONV_W - 1) + tap), 1.0, 0.0).astype(BF16)
              for tap in range(CONV_W - 1)]
    sub = lax.broadcasted_iota(jnp.int32, (SUBLANES, LANES), 0)

    def per_batch(b, carry):
        window = cbuf[b]
        acc = cb_ref[...] + _dot(shifts[0], window) * cw_ref[0:1, :]
        for tap in range(1, CONV_W - 1):
            acc = acc + _dot(shifts[tap], window) * cw_ref[tap:tap + 1, :]
        acc = acc + qk_ref[b].astype(F32) * cw_ref[CONV_W - 1:CONV_W, :]
        qk = acc * jax.nn.sigmoid(acc)
        cbuf[b, 0:halo, :] = cbuf[b, L:L + halo, :]

        g_row = g_ref[b].T[0:SUBLANES, :]
        lf = _log_sigmoid(g_row)
        lf_hi = lf.astype(BF16)
        lf_lo = (lf - lf_hi.astype(F32)).astype(BF16)
        cs = _dot(jnp.concatenate([lf_hi, lf_lo], axis=0), upper)
        b_row = pltpu.roll(cs[0:SUBLANES, :] + cs[SUBLANES:2 * SUBLANES, :], M_HEADS, 0)
        a_row = g_row - b_row
        cm = a_row
        for sh in (1, 2, 4, 8, 16, 32, 64):
            cm = jnp.maximum(cm, jnp.where(lane8 >= sh, pltpu.roll(cm, sh, 1), -jnp.inf))
        m_prev = m_ref[b]
        mx = jnp.maximum(m_prev, cm)
        mx_last = jnp.maximum(m_prev, jnp.max(a_row, axis=1, keepdims=True))
        carry_in = jnp.exp(m_prev - mx)
        floor = jnp.exp(-(b_row + mx))
        wts = jnp.exp(a_row - mx_last)
        decay = jnp.exp(m_prev - mx_last)
        m_ref[b] = jnp.where(sub < M_HEADS, b_row[:, L - 1:L] + mx_last, 0.0)
        packed = jnp.concatenate(
            [jnp.where(sub < M_HEADS, mx, pltpu.roll(carry_in, M_HEADS, 0)),
             jnp.where(sub < M_HEADS, floor, pltpu.roll(wts, M_HEADS, 0)),
             jnp.zeros((L - 2 * SUBLANES, L), F32)], axis=0).T

        heads = range(M_HEADS)
        vcols = [slice(h * M_V_DIM, (h + 1) * M_V_DIM) for h in heads]
        q = [qk[:, h * M_QK_DIM:(h + 1) * M_QK_DIM].astype(BF16) for h in heads]
        kf = [qk[:, M_QK + h * M_QK_DIM:M_QK + (h + 1) * M_QK_DIM] * scale for h in heads]
        v_ext = [jnp.concatenate([v_ref[b, :, vcols[h]], ones_blk], axis=1) for h in heads]
        ct = [ct_refs[h][b] for h in heads]
        col_of = lambda i, h: packed[:, i * M_HEADS + h:i * M_HEADS + h + 1]

        raw = [lax.dot_general(q[h], kf[h].astype(BF16), (((1,), (1,)), ((), ())),
                               preferred_element_type=F32) for h in heads]
        prev = [_dot(q[h], ct[h].astype(BF16)) for h in heads]
        pmat = [jnp.exp(jnp.where(causal, a_row[h:h + 1, :] - col_of(0, h), -jnp.inf)) for h in heads]
        s = [(raw[h] * pmat[h]).astype(BF16) for h in heads]
        numden = [_dot(s[h], v_ext[h]) + col_of(1, h) * prev[h] for h in heads]
        upd = [lax.dot_general((kf[h] * col_of(3, h)).astype(BF16), v_ext[h], (((0,), (0,)), ((), ())),
                               preferred_element_type=F32) for h in heads]
        for h in heads:
            dec = jnp.concatenate([decay[h:h + 1, :]] * (ct[h].shape[1] // LANES), axis=1)
            ct_refs[h][b] = dec * ct[h] + upd[h]
        inv = [1.0 / jnp.maximum(jnp.abs(numden[h][:, M_V_DIM:]), col_of(2, h)) for h in heads]
        hh = [numden[h][:, 0:M_V_DIM] * jnp.concatenate([inv[h]] * (M_V_DIM // LANES), axis=1) for h in heads]
        mu = [jnp.mean(hh[h], axis=-1, keepdims=True) for h in heads]
        dlt = [hh[h] - mu[h] for h in heads]
        var = [jnp.mean(dlt[h] * dlt[h], axis=-1, keepdims=True) for h in heads]
        for h in heads:
            hn = dlt[h] * lax.rsqrt(var[h] + LN_EPS) * mhg_ref[:, vcols[h]]
            og = jax.nn.sigmoid(o_ref[b, :, vcols[h]].astype(F32))
            out_ref[b, :, vcols[h]] = (hn * og).astype(BF16)
        return carry

    for b in range(nb):
        per_batch(b, 0)


def _mlstm(z3, zif3, conv_w, conv_b, mh_g, side):
    nb, s, _ = z3.shape
    L = CHUNK
    steps = s // L
    blk = lambda colblk: pl.BlockSpec((nb, L, M_V), lambda c: (0, c, colblk))
    const = lambda shape: pl.BlockSpec(shape, lambda c: (0,) * len(shape))

    def side_spec(a):
        rb = max(BF16_SUBLANES, a.shape[0] // steps)
        return pl.BlockSpec((rb, a.shape[1]), lambda c: (jnp.minimum(c, a.shape[0] // rb - 1), 0))

    return pl.pallas_call(
        functools.partial(_mlstm_kernel, nb=nb, n_side=len(side)),
        grid=(steps,),
        in_specs=[blk(0), blk(1), blk(2),
                  pl.BlockSpec((nb, L, LANES), lambda c: (0, c, 0)),
                  const((CONV_W, 2 * M_QK)), const((1, 2 * M_QK)), const((1, M_V))] + [side_spec(a) for a in side],
        out_specs=[pl.BlockSpec((nb, L, M_V), lambda c: (0, c, 0))] + [side_spec(a) for a in side],
        out_shape=[jax.ShapeDtypeStruct((nb, s, M_V), BF16)] + [jax.ShapeDtypeStruct(a.shape, BF16) for a in side],
        scratch_shapes=([pltpu.VMEM((nb, L + CONV_HALO, 2 * M_QK), BF16)]
                        + [pltpu.VMEM((nb, M_QK_DIM, M_V_DIM + LANES), F32)] * M_HEADS
                        + [pltpu.VMEM((nb, SUBLANES, L), F32)]),
        compiler_params=_params(1),
        name="mlstm",
    )(z3, z3, z3, zif3, conv_w, conv_b, mh_g, *side)


def _branch_kernel(hg_ref, u_ref, uh_ref, gm_ref, gp_ref, wpool_ref, ps_ref, wm_ref, wp_ref,
                   out_ref, xa, xb, yp_ref, *, tm, seq):
    H = POOL_HALO
    G = POOL_GROUP_DIM
    t0 = lax.rem(pl.program_id(0) * tm, seq)
    u = u_ref[...].astype(F32)
    xa[H:H + tm, :] = u
    xa[0:H, :] = jnp.where(t0 == 0, 0.0, uh_ref[...].astype(F32))
    n = tm + H - 8
    xb[8:8 + n, :] = xa[8:8 + n, :] + xa[7:7 + n, :]
    n = tm + H - 16
    xa[16:16 + n, G:] = xb[16:16 + n, G:] + xb[14:14 + n, G:]
    n = tm + H - 24
    xb[24:24 + n, 2 * G:] = xa[24:24 + n, 2 * G:] + xa[20:20 + n, 2 * G:]
    xa[H:H + tm, 3 * G:] = xb[H:H + tm, 3 * G:] + xb[H - 8:H - 8 + tm, 3 * G:]
    tpos = t0 + lax.broadcasted_iota(jnp.int32, (tm, 1), 0)
    for g, win in enumerate(POOL_WINDOWS):
        src = (xb, xa, xb, xa)[g]
        cols = slice(g * G, (g + 1) * G)
        cnt = jnp.minimum(tpos + 1, win).astype(F32)
        y = src[H:H + tm, cols] / cnt - u[:, cols]
        yp = _dot(y.astype(BF16), wpool_ref[g]) * ps_ref[:, cols]
        yp_ref[:, cols] = yp.astype(BF16)
    hm = tm // ROW_SPLIT
    for r0 in range(0, tm, hm):
        rows = slice(r0, r0 + hm)
        pb = _dot(yp_ref[rows, :], wp_ref[...])
        a = _dot(hg_ref[rows, :], wm_ref[...])
        merged = (jax.nn.sigmoid(gm_ref[rows, :].astype(F32)) * a
                  + jax.nn.sigmoid(gp_ref[rows, :].astype(F32)) * pb)
        out_ref[rows, :] = merged.astype(BF16)


def _branch(hg2d, z_main, w_pool, pool_scale, w_m_br, w_p_br, seq, tm):
    t = hg2d.shape[0]
    d = w_m_br.shape[1]
    hb = tm // POOL_HALO
    const = lambda shape: pl.BlockSpec(shape, lambda i: (0,) * len(shape))
    return pl.pallas_call(
        functools.partial(_branch_kernel, tm=tm, seq=seq),
        grid=(t // tm,),
        in_specs=[pl.BlockSpec((tm, M_V), lambda i: (i, 0)),
                  pl.BlockSpec((tm, POOL_W), lambda i: (i, 3)),
                  pl.BlockSpec((POOL_HALO, POOL_W), lambda i: (jnp.maximum(i * hb - 1, 0), 3)),
                  pl.BlockSpec((tm, d), lambda i: (i, 2)),
                  pl.BlockSpec((tm, d), lambda i: (i, 3)),
                  const(w_pool.shape), const((1, POOL_W)), const(w_m_br.shape), const(w_p_br.shape)],
        out_specs=pl.BlockSpec((tm, d), lambda i: (i, 0)),
        out_shape=jax.ShapeDtypeStruct((t, d), BF16),
        scratch_shapes=[pltpu.VMEM((tm + POOL_HALO, POOL_W), F32),
                        pltpu.VMEM((tm + POOL_HALO, POOL_W), F32),
                        pltpu.VMEM((tm, POOL_W), BF16)],
        compiler_params=_params(1),
        name="branch",
    )(hg2d, z_main, z_main, z_main, z_main, w_pool, pool_scale, w_m_br, w_p_br)


def _route_kernel(mg_ref, x_ref, wout_ref, g_ref, b_ref, wr_ref, br_ref, upper_ref,
                  x1_ref, route_ref, gate_ref, cnt_ref, carry, *, alpha, tm):
    @pl.when(pl.program_id(0) == 0)
    def _():
        carry[...] = jnp.zeros_like(carry)

    hm = tm // ROW_SPLIT
    pitch = _pitch(x_ref.shape[1])
    logits = []
    for r0 in range(0, tm, hm):
        r = alpha * x_ref[r0:r0 + hm, :] + _dot(mg_ref[r0:r0 + hm, :], wout_ref[...])
        x1 = _layer_norm(r, g_ref[...], b_ref[...])
        _store_token_major(x1_ref.at[r0 * pitch:(r0 + hm) * pitch], x1)
        logits.append(_dot(x1.astype(BF16), wr_ref[...]) + br_ref[...])
    lt = jnp.concatenate(logits, axis=0).T

    sub = lax.broadcasted_iota(jnp.int32, (SUBLANES, tm), 0).astype(F32)
    none = float(SUBLANES)

    def softmax(z):
        e = jnp.exp(z - jnp.max(z, axis=0, keepdims=True))
        return e / jnp.sum(e, axis=0, keepdims=True)

    def top1(vals):
        top = jnp.max(vals, axis=0, keepdims=True)
        return top, jnp.min(jnp.where(vals == top, sub, none), axis=0, keepdims=True)

    is_grp = sub < N_GROUPS
    pg = jnp.where(is_grp, softmax(jnp.where(is_grp, lt[0:SUBLANES, :], -jnp.inf)), -1.0)
    pg_top, g_idx = top1(pg)
    le_sel = lt[SUBLANES:2 * SUBLANES, :]
    for g in range(1, N_GROUPS):
        le_sel = jnp.where(g_idx == g, lt[(g + 1) * SUBLANES:(g + 2) * SUBLANES, :], le_sel)
    pe = softmax(le_sel)
    pe1, i1 = top1(pe)
    pe2, i2 = top1(jnp.where(sub == i1, -1.0, pe))
    den = pe1 + pe2
    gate1 = pg_top * pe1 / den
    gate2 = pg_top * pe2 / den
    e1 = g_idx * EXPERTS_PER_GROUP + i1
    e2 = g_idx * EXPERTS_PER_GROUP + i2

    eid = lax.broadcasted_iota(jnp.int32, (N_EXPERTS, tm), 0).astype(F32)
    hit1 = eid == e1
    hit2 = eid == e2
    onehot = jnp.where(hit1 | hit2, 1.0, 0.0)
    incl = _dot(onehot.astype(BF16), upper_ref[...])
    excl = incl - onehot + carry[:, 0:1]
    r1 = jnp.sum(jnp.where(hit1, excl, 0.0), axis=0, keepdims=True)
    r2 = jnp.sum(jnp.where(hit2, excl, 0.0), axis=0, keepdims=True)
    carry[...] = carry[...] + jnp.broadcast_to(incl[:, tm - 1:tm], carry.shape)
    cnt_ref[...] = carry[...]

    route = jnp.where(sub == 0, e1, jnp.where(sub == 1, e2, jnp.where(sub == 2, r1, jnp.where(sub == 3, r2, 0.0))))
    route_ref[...] = route.astype(jnp.int32)
    gate_ref[...] = jnp.where(sub == 0, gate1, jnp.where(sub == 1, gate2, 0.0))


def _route(merged, x2d, w_out, ln_g, ln_b, w_r, b_r, upper, alpha, tm):
    t, d = x2d.shape
    const = lambda shape: pl.BlockSpec(shape, lambda i: (0,) * len(shape))
    rows = lambda width: pl.BlockSpec((tm, width), lambda i: (i, 0))
    cols = pl.BlockSpec((SUBLANES, tm), lambda i: (0, i))
    return pl.pallas_call(
        functools.partial(_route_kernel, alpha=alpha, tm=tm),
        grid=(t // tm,),
        in_specs=[rows(d), rows(d), const(w_out.shape), const((1, d)), const((1, d)),
                  const(w_r.shape), const((1, LANES)), const((tm, tm))],
        out_specs=[pl.BlockSpec((tm * _pitch(d), LANES), lambda i: (i, 0)),
                   cols, cols, const((N_EXPERTS, LANES))],
        out_shape=[jax.ShapeDtypeStruct((t * _pitch(d), LANES), F32),
                   jax.ShapeDtypeStruct((SUBLANES, t), jnp.int32),
                   jax.ShapeDtypeStruct((SUBLANES, t), F32),
                   jax.ShapeDtypeStruct((N_EXPERTS, LANES), F32)],
        scratch_shapes=[pltpu.VMEM((N_EXPERTS, LANES), F32)],
        compiler_params=_params(1),
        name="route",
    )(merged, x2d, w_out, ln_g, ln_b, w_r, b_r, upper)


def _dispatch_kernel(pad_start_ref, pad_len_ref, nu_ref, dest_ref, x_ref, xs_hbm, sem, *, tb, te, d):
    def copy(src_tok, dst_tok, n=1):
        return pltpu.make_async_copy(_tokens(x_ref, src_tok, d, n), _tokens(xs_hbm, dst_tok, d, n), sem)

    def wait_tokens(n):
        def body(r, c):
            copy(0, 0, n).wait()
            return c
        return body

    @pl.when(pl.program_id(0) == 0)
    def _():
        def per_expert(e, total):
            start = pad_start_ref[e]
            npad = pad_len_ref[e]

            def fill(r, c):
                copy(0, start + r).start()
                return c

            lax.fori_loop(0, npad, fill, 0)
            return total + npad

        total = lax.fori_loop(0, N_EXPERTS, per_expert, 0)
        lax.fori_loop(0, total // SUBLANES, wait_tokens(SUBLANES), 0)
        lax.fori_loop(0, total % SUBLANES, wait_tokens(1), 0)

        def fill_tile(j, c):
            copy(0, j * te, te).start()
            return c

        n_tiles = xs_hbm.shape[0] // (te * _pitch(d))
        lax.fori_loop(nu_ref[0], n_tiles, fill_tile, 0)
        lax.fori_loop(nu_ref[0], n_tiles, wait_tokens(te), 0)

    def scatter(t, c):
        for k in range(TOP_K):
            copy(t, dest_ref[TOP_K * t + k]).start(priority=k % 2)
        return c

    lax.fori_loop(0, tb, scatter, 0, unroll=DMA_UNROLL)
    for k in range(TOP_K):
        copy(0, 0, tb).wait()


def _dispatch(x1t, dest_flat, pad_start, pad_len, n_used, n_slots, tb, te, d):
    pitch = _pitch(d)
    t = x1t.shape[0] // pitch
    grid_spec = pltpu.PrefetchScalarGridSpec(
        num_scalar_prefetch=3,
        grid=(t // tb,),
        in_specs=[pl.BlockSpec((TOP_K * tb,), lambda i, ps, pn, nu: (i,), memory_space=pltpu.SMEM),
                  pl.BlockSpec((tb * pitch, LANES), lambda i, ps, pn, nu: (i, 0))],
        out_specs=pl.BlockSpec(memory_space=pl.ANY),
        scratch_shapes=[pltpu.SemaphoreType.DMA(())],
    )
    return pl.pallas_call(
        functools.partial(_dispatch_kernel, tb=tb, te=te, d=d),
        grid_spec=grid_spec,
        out_shape=jax.ShapeDtypeStruct((n_slots * pitch, LANES), x1t.dtype),
        compiler_params=_params(1),
        name="dispatch",
    )(pad_start, pad_len, n_used, dest_flat, x1t)


def _expert_kernel(be_ref, nu_ref, x_ref, wg_ref, wu_ref, wd_ref, y_ref, *, te, d):
    i = pl.program_id(0)

    @pl.when(i < nu_ref[0])
    def _():
        xb = _load_token_major(x_ref, te, d).astype(BF16)
        gt = _dot(xb, wg_ref[0])
        up = _dot(xb, wu_ref[0])
        hid = gt * jax.nn.sigmoid(gt) * up
        _store_token_major(y_ref, _dot(hid.astype(BF16), wd_ref[0]))

    @pl.when(i >= nu_ref[0])
    def _():
        y_ref[...] = jnp.zeros_like(y_ref)


def _experts(xs, block_e, n_used, w_gate, w_up, w_down, te):
    _, d, de = w_gate.shape
    pitch = _pitch(d)
    n_tiles = xs.shape[0] // (te * pitch)
    grid_spec = pltpu.PrefetchScalarGridSpec(
        num_scalar_prefetch=2,
        grid=(n_tiles,),
        in_specs=[pl.BlockSpec((te * pitch, LANES), lambda i, be, nu: (jnp.minimum(i, nu[0] - 1), 0)),
                  pl.BlockSpec((1, d, de), lambda i, be, nu: (be[i], 0, 0)),
                  pl.BlockSpec((1, d, de), lambda i, be, nu: (be[i], 0, 0)),
                  pl.BlockSpec((1, de, d), lambda i, be, nu: (be[i], 0, 0))],
        out_specs=pl.BlockSpec((te * pitch, LANES), lambda i, be, nu: (i, 0)),
    )
    return pl.pallas_call(
        functools.partial(_expert_kernel, te=te, d=d),
        grid_spec=grid_spec,
        out_shape=jax.ShapeDtypeStruct(xs.shape, F32),
        compiler_params=_params(1),
        name="experts",
    )(block_e, n_used, xs, w_gate, w_up, w_down)


def _final_kernel(dcur_ref, dnxt_ref, x1_ref, gate_ref, p_ref, ys_hbm, g_ref, b_ref, wg_ref, bg_ref, wp_ref,
                  out_ref, ybuf, sem, *, alpha, tm, d):
    i = pl.program_id(0)
    slot = lax.rem(i, 2)

    def issue(dest_ref, s):
        def body(t, c):
            for k in range(TOP_K):
                pltpu.make_async_copy(_tokens(ys_hbm, dest_ref[TOP_K * t + k], d),
                                      _tokens(ybuf.at[s, k], t, d), sem.at[s]).start(priority=k % 2)
            return c
        lax.fori_loop(0, tm, body, 0, unroll=DMA_UNROLL)

    @pl.when(i == 0)
    def _():
        issue(dcur_ref, 0)

    @pl.when(i + 1 < pl.num_programs(0))
    def _():
        issue(dnxt_ref, 1 - slot)

    for k in range(TOP_K):
        pltpu.make_async_copy(_tokens(ys_hbm, 0, d, tm), ybuf.at[slot, k], sem.at[slot]).wait()

    hm = tm // ROW_SPLIT
    for r0 in range(0, tm, hm):
        rows = slice(r0, r0 + hm)
        trows = pl.ds(r0 * _pitch(d), hm * _pitch(d))
        pp = _dot(p_ref[rows, :].astype(BF16), wp_ref[...])
        gate = gate_ref[rows, :]
        y = (gate[:, 0:1] * _load_token_major(ybuf.at[slot, 0, trows], hm, d)
             + gate[:, 1:2] * _load_token_major(ybuf.at[slot, 1, trows], hm, d))
        x2 = _layer_norm(alpha * _load_token_major(x1_ref.at[trows], hm, d) + y, g_ref[...], b_ref[...])
        gl = _dot(x2.astype(BF16), wg_ref[...]) + bg_ref[...]
        out_ref[rows, :] = x2 + jax.nn.sigmoid(gl) * pp


def _final(dest_flat, x1t, gate, p2d, ys, ln_g, ln_b, w_pg, b_pg, w_pp, alpha, tm):
    d = w_pg.shape[0]
    pitch = _pitch(d)
    t = x1t.shape[0] // pitch
    n = t // tm
    const = lambda shape: pl.BlockSpec(shape, lambda i: (0,) * len(shape))
    resident = lambda shape: pl.BlockSpec(shape, lambda i: (0,) * len(shape), pipeline_mode=pl.Buffered(1))
    rows = lambda width: pl.BlockSpec((tm, width), lambda i: (i, 0))
    return pl.pallas_call(
        functools.partial(_final_kernel, alpha=alpha, tm=tm, d=d),
        grid=(n,),
        in_specs=[pl.BlockSpec((TOP_K * tm,), lambda i: (i,), memory_space=pltpu.SMEM),
                  pl.BlockSpec((TOP_K * tm,), lambda i: (jnp.minimum(i + 1, n - 1),), memory_space=pltpu.SMEM),
                  pl.BlockSpec((tm * pitch, LANES), lambda i: (i, 0)),
                  rows(LANES), rows(p2d.shape[1]),
                  pl.BlockSpec(memory_space=pl.ANY),
                  const((1, d)), const((1, d)), resident(w_pg.shape), const((1, d)), resident(w_pp.shape)],
        out_specs=rows(d),
        out_shape=jax.ShapeDtypeStruct((t, d), F32),
        scratch_shapes=[pltpu.VMEM((2, TOP_K, tm * pitch, LANES), F32), pltpu.SemaphoreType.DMA((2,))],
        compiler_params=_params(1),
        name="final",
    )(dest_flat, dest_flat, x1t, gate, p2d, ys, ln_g, ln_b, w_pg, b_pg, w_pp)


def _pad_cols(a, width):
    return jnp.pad(a, ((0, 0), (0, width - a.shape[1])))


def _tri(n):
    return jnp.tril(jnp.ones((n, n), BF16))


def _layer(x, p, w_in_t, b_in, conv_w, conv_b, mh_g, w_pool, pool_scale, w_m_br, w_p_br, w_out,
           ln1_g, ln1_b, w_rg, b_rg, w_re, b_re, w_gate, w_up, w_down, ln2_g, ln2_b,
           w_ple_gate, b_ple_gate, w_ple_proj, alpha):
    nb, seq, d = x.shape
    t = nb * seq
    x2d = x.reshape(t, d)
    row = lambda a: a.reshape(1, -1)

    c_if = 2 * M_QK + 2 * M_V
    w_main = _regroup(w_in_t, c_if, 2 * M_HEADS, 1024, min(1024, d))
    b_main = row(jnp.concatenate([b_in[:c_if], b_in[c_if + 2 * M_HEADS:]]))
    w_if = _pad_cols(w_in_t[c_if:c_if + 2 * M_HEADS, :].T, LANES)
    b_if = _pad_cols(row(b_in[c_if:c_if + 2 * M_HEADS]), LANES)

    n_exp, _, d_exp = w_gate.shape
    expert_w = [w_gate.reshape(n_exp * d, d_exp), w_up.reshape(n_exp * d, d_exp), w_down.reshape(n_exp * d_exp, d)]
    tm_in = min(1024, t)
    z_main, z_if, wg_b, wu_b, wd_b = _inproj(x2d, w_main, b_main, w_if, b_if, tm_in, 1024, expert_w)

    dense_w = [w_pool, w_m_br, w_p_br, w_out, w_ple_gate, w_ple_proj]
    hg, *dense_b = _mlstm(z_main.reshape(nb, seq, -1), z_if.reshape(nb, seq, LANES),
                          conv_w, row(conv_b), row(mh_g), [a.reshape(-1, a.shape[-1]) for a in dense_w])
    w_pool, w_m_br, w_p_br, w_out, w_ple_gate, w_ple_proj = [b.reshape(a.shape) for a, b in zip(dense_w, dense_b)]

    tm = min(512, seq)
    merged = _branch(hg.reshape(t, M_V), z_main, w_pool, row(pool_scale), w_m_br, w_p_br, seq, tm)

    w_r = _pad_cols(jnp.concatenate([_pad_cols(w_rg, SUBLANES), w_re], axis=1), LANES).astype(BF16)
    b_r = _pad_cols(jnp.concatenate([_pad_cols(row(b_rg), SUBLANES), row(b_re)], axis=1), LANES)
    x1, route, gate_t, cnt = _route(merged, x2d, w_out, row(ln1_g), row(ln1_b),
                                    w_r, b_r, _tri(tm).T, alpha, tm)
    gate = _pad_cols(gate_t[0:TOP_K].T, LANES)

    te = EXPERT_TILE
    counts = cnt[:, 0].astype(jnp.int32)
    pcounts = (counts + te - 1) // te * te
    pends = jnp.cumsum(pcounts)
    pstarts = pends - pcounts
    e_sel = route[0:TOP_K, :, None] == jnp.arange(N_EXPERTS, dtype=jnp.int32)
    dest = (jnp.sum(jnp.where(e_sel, pstarts, 0), axis=-1) + route[TOP_K:2 * TOP_K]).T.reshape(-1)
    n_slots = t * TOP_K + N_EXPERTS * te
    n_tiles = n_slots // te
    n_used = (pends[-1] // te).reshape(1)
    tile_row = jnp.minimum(jnp.arange(n_tiles, dtype=jnp.int32), n_used - 1) * te
    block_e = jnp.minimum(jnp.sum((tile_row[:, None] >= pends[None, :]).astype(jnp.int32), axis=1), N_EXPERTS - 1)

    xs = _dispatch(x1, dest, pstarts + counts, pcounts - counts, n_used, n_slots, min(512, t), te, d)
    ys = _experts(xs, block_e, n_used, wg_b.reshape(n_exp, d, d_exp), wu_b.reshape(n_exp, d, d_exp),
                  wd_b.reshape(n_exp, d_exp, d), te)
    return _final(dest, x1, gate, p.reshape(t, -1), ys, row(ln2_g), row(ln2_b),
                  w_ple_gate, row(b_ple_gate), w_ple_proj, alpha,
                  min(512, t)).reshape(nb, seq, d)


def kernel(x, p, w_in, b_in, conv_w, conv_b, mh_g, w_pool, pool_scale, w_m_br, w_p_br, w_out, ln1_g, ln1_b, w_rg, b_rg, w_re, b_re, w_gate, w_up, w_down, ln2_g, ln2_b, w_ple_gate, b_ple_gate, w_ple_proj):
    depth = w_in.shape[0]
    alpha = (2 * depth) ** 0.25
    for i in range(depth):
        x = _layer(x, p[i], w_in[i].T, b_in[i], conv_w[i], conv_b[i], mh_g[i], w_pool[i], pool_scale[i],
                   w_m_br[i], w_p_br[i], w_out[i], ln1_g[i], ln1_b[i], w_rg[i], b_rg[i], w_re[i], b_re[i],
                   w_gate[i], w_up[i], w_down[i], ln2_g[i], ln2_b[i], w_ple_gate[i], b_ple_gate[i],
                   w_ple_proj[i], alpha)
    return x
```

```python
import functools

import jax
import jax.numpy as jnp
from jax import lax
from jax.experimental import pallas as pl
from jax.experimental.pallas import tpu as pltpu

F32 = jnp.float32
BF16 = jnp.bfloat16

M_HEADS = 4
M_QK_DIM = 128
M_V_DIM = 256
M_QK = M_HEADS * M_QK_DIM
M_V = M_HEADS * M_V_DIM
CONV_W = 4
CHUNK = 128
POOL_WINDOWS = (2, 4, 8, 16)
POOL_GROUP_DIM = 256
POOL_W = len(POOL_WINDOWS) * POOL_GROUP_DIM
N_GROUPS = 4
EXPERTS_PER_GROUP = 8
N_EXPERTS = N_GROUPS * EXPERTS_PER_GROUP
TOP_K = 2
LN_EPS = 1e-5

LANES = 128
SUBLANES = 8
VMEM_LIMIT = 56 * 1024 * 1024
BF16_SUBLANES = 16
CONV_HALO = BF16_SUBLANES
POOL_HALO = 32
EXPERT_TILE = 256
DMA_UNROLL = 8
ROW_SPLIT = 2


def _dot(a, b):
    return jnp.dot(a, b, preferred_element_type=F32)


def _params(n_grid):
    return pltpu.CompilerParams(dimension_semantics=("arbitrary",) * n_grid,
                                vmem_limit_bytes=VMEM_LIMIT)


def _log_sigmoid(x):
    return -(jnp.maximum(-x, 0.0) + jnp.log1p(jnp.exp(-jnp.abs(x))))


def _layer_norm(r, g, b):
    mu = jnp.mean(r, axis=-1, keepdims=True)
    d = r - mu
    var = jnp.mean(d * d, axis=-1, keepdims=True)
    return d * lax.rsqrt(var + LN_EPS) * g + b


def _pitch(d):
    return d // LANES + 1


def _store_token_major(ref, val):
    n, d = val.shape
    for c in range(d // LANES):
        ref[pl.ds(c, n, stride=_pitch(d)), :] = val[:, c * LANES:(c + 1) * LANES]
    ref[pl.ds(d // LANES, n, stride=_pitch(d)), :] = jnp.zeros((n, LANES), val.dtype)


def _load_token_major(ref, n, d):
    return jnp.concatenate([ref[pl.ds(c, n, stride=_pitch(d)), :] for c in range(d // LANES)], axis=1)


def _tokens(ref, tok, d, n=1):
    return ref.at[pl.ds(tok * _pitch(d), n * _pitch(d))]


def _regroup_kernel(a_ref, c_ref, out_ref, cut_ref):
    out_ref[...] = a_ref[...].T.astype(BF16)
    skip, tk = c_ref.shape
    cut_ref[...] = jnp.concatenate([c_ref[...], jnp.zeros((LANES - skip, tk), F32)], axis=0).T


def _regroup(w_t, cut, skip, tn, tk):
    n, k = w_t.shape
    n_keep = cut // tn
    assert skip % SUBLANES == 0
    first_row = lambda j: pl.multiple_of(jnp.where(j < n_keep, j * tn, j * tn + skip), SUBLANES)
    return pl.pallas_call(
        _regroup_kernel,
        grid=(k // tk, (n - skip) // tn),
        in_specs=[pl.BlockSpec((pl.Element(tn), pl.Element(tk)), lambda kk, j: (first_row(j), kk * tk)),
                  pl.BlockSpec((pl.Element(skip), pl.Element(tk)), lambda kk, j: (cut, kk * tk))],
        out_specs=[pl.BlockSpec((tk, tn), lambda kk, j: (kk, j)),
                   pl.BlockSpec((tk, LANES), lambda kk, j: (kk, 0))],
        out_shape=[jax.ShapeDtypeStruct((k, n - skip), BF16), jax.ShapeDtypeStruct((k, LANES), F32)],
        compiler_params=_params(2),
        name="regroup",
    )(w_t, w_t)


def _inproj_kernel(*refs, n_side):
    x_ref, w_ref, b_ref, wif_ref, bif_ref = refs[:5]
    side_in = refs[5:5 + n_side]
    z_ref, zif_ref = refs[5 + n_side:7 + n_side]
    side_out = refs[7 + n_side:7 + 2 * n_side]
    xb_ref = refs[7 + 2 * n_side]

    @pl.when(pl.program_id(1) == 0)
    def _():
        xb = x_ref[...].astype(BF16)
        xb_ref[...] = xb
        zif_ref[...] = _dot(xb, wif_ref[...].astype(BF16)) + bif_ref[...]

    z_ref[...] = (_dot(xb_ref[...], w_ref[...]) + b_ref[...]).astype(BF16)
    for src, dst in zip(side_in, side_out):
        dst[...] = src[...].astype(BF16)


def _inproj(x2d, w_main, b_main, w_if, b_if, tm, tn, side):
    t, d = x2d.shape
    n = w_main.shape[1]
    nj = n // tn
    steps = (t // tm) * nj
    side_spec = lambda a: pl.BlockSpec((a.shape[0] // steps, a.shape[1]), lambda i, j: (i * nj + j, 0))
    return pl.pallas_call(
        functools.partial(_inproj_kernel, n_side=len(side)),
        grid=(t // tm, nj),
        in_specs=[pl.BlockSpec((tm, d), lambda i, j: (i, 0)),
                  pl.BlockSpec((d, tn), lambda i, j: (0, j)),
                  pl.BlockSpec((1, tn), lambda i, j: (0, j)),
                  pl.BlockSpec((d, LANES), lambda i, j: (0, 0)),
                  pl.BlockSpec((1, LANES), lambda i, j: (0, 0))] + [side_spec(a) for a in side],
        out_specs=[pl.BlockSpec((tm, tn), lambda i, j: (i, j)),
                   pl.BlockSpec((tm, LANES), lambda i, j: (i, 0))] + [side_spec(a) for a in side],
        out_shape=[jax.ShapeDtypeStruct((t, n), BF16),
                   jax.ShapeDtypeStruct((t, LANES), F32)] + [jax.ShapeDtypeStruct(a.shape, BF16) for a in side],
        scratch_shapes=[pltpu.VMEM((tm, d), BF16)],
        compiler_params=_params(2),
        name="inproj",
    )(x2d, w_main, b_main, w_if, b_if, *side)


def _mlstm_kernel(*refs, nb, n_side):
    qk_ref, v_ref, o_ref, g_ref, cw_ref, cb_ref, mhg_ref = refs[:7]
    side_in = refs[7:7 + n_side]
    out_ref = refs[7 + n_side]
    side_out = refs[8 + n_side:8 + 2 * n_side]
    cbuf = refs[8 + 2 * n_side]
    state = refs[9 + 2 * n_side:]
    _mlstm_body(qk_ref, v_ref, o_ref, g_ref, cw_ref, cb_ref, mhg_ref, out_ref, cbuf, state, nb)
    for src, dst in zip(side_in, side_out):
        dst[...] = src[...].astype(BF16)


def _mlstm_body(qk_ref, v_ref, o_ref, g_ref, cw_ref, cb_ref, mhg_ref, out_ref, cbuf, state, nb):
    L = CHUNK
    halo = CONV_HALO
    ct_refs, m_ref = state[:M_HEADS], state[M_HEADS]

    @pl.when(pl.program_id(0) == 0)
    def _():
        cbuf[:, 0:halo, :] = jnp.zeros((nb, halo, 2 * M_QK), BF16)
        for ref in state:
            ref[...] = jnp.zeros_like(ref)

    cbuf[:, halo:halo + L, :] = qk_ref[...]
    scale = M_QK_DIM ** -0.5
    row = lax.broadcasted_iota(jnp.int32, (L, L), 0)
    col = lax.broadcasted_iota(jnp.int32, (L, L), 1)
    causal = col <= row
    upper = jnp.where(row <= col, 1.0, 0.0).astype(BF16)
    ones_blk = jnp.ones((L, LANES), BF16)
    lane8 = lax.broadcasted_iota(jnp.int32, (SUBLANES, L), 1)
    wrow = lax.broadcasted_iota(jnp.int32, (L, L + halo), 0)
    wcol = lax.broadcasted_iota(jnp.int32, (L, L + halo), 1)
    shifts = [jnp.where(wcol == wrow + (halo - (CONV_W - 1) + tap), 1.0, 0.0).astype(BF16)
              for tap in range(CONV_W - 1)]
    sub = lax.broadcasted_iota(jnp.int32, (SUBLANES, LANES), 0)

    def per_batch(b, carry):
        window = cbuf[b]
        acc = cb_ref[...] + _dot(shifts[0], window) * cw_ref[0:1, :]
        for tap in range(1, CONV_W - 1):
            acc = acc + _dot(shifts[tap], window) * cw_ref[tap:tap + 1, :]
        acc = acc + qk_ref[b].astype(F32) * cw_ref[CONV_W - 1:CONV_W, :]
        qk = acc * jax.nn.sigmoid(acc)
        cbuf[b, 0:halo, :] = cbuf[b, L:L + halo, :]

        g_row = g_ref[b].T[0:SUBLANES, :]
        lf = _log_sigmoid(g_row)
        lf_hi = lf.astype(BF16)
        lf_lo = (lf - lf_hi.astype(F32)).astype(BF16)
        cs = _dot(jnp.concatenate([lf_hi, lf_lo], axis=0), upper)
        b_row = pltpu.roll(cs[0:SUBLANES, :] + cs[SUBLANES:2 * SUBLANES, :], M_HEADS, 0)
        a_row = g_row - b_row
        cm = a_row
        for sh in (1, 2, 4, 8, 16, 32, 64):
            cm = jnp.maximum(cm, jnp.where(lane8 >= sh, pltpu.roll(cm, sh, 1), -jnp.inf))
        m_prev = m_ref[b]
        mx = jnp.maximum(m_prev, cm)
        mx_last = jnp.maximum(m_prev, jnp.max(a_row, axis=1, keepdims=True))
        carry_in = jnp.exp(m_prev - mx)
        floor = jnp.exp(-(b_row + mx))
        wts = jnp.exp(a_row - mx_last)
        decay = jnp.exp(m_prev - mx_last)
        m_ref[b] = jnp.where(sub < M_HEADS, b_row[:, L - 1:L] + mx_last, 0.0)
        packed = jnp.concatenate(
            [jnp.where(sub < M_HEADS, mx, pltpu.roll(carry_in, M_HEADS, 0)),
             jnp.where(sub < M_HEADS, floor, pltpu.roll(wts, M_HEADS, 0)),
             jnp.zeros((L - 2 * SUBLANES, L), F32)], axis=0).T

        heads = range(M_HEADS)
        vcols = [slice(h * M_V_DIM, (h + 1) * M_V_DIM) for h in heads]
        q = [qk[:, h * M_QK_DIM:(h + 1) * M_QK_DIM].astype(BF16) for h in heads]
        kf = [qk[:, M_QK + h * M_QK_DIM:M_QK + (h + 1) * M_QK_DIM] * scale for h in heads]
        v_ext = [jnp.concatenate([v_ref[b, :, vcols[h]], ones_blk], axis=1) for h in heads]
        ct = [ct_refs[h][b] for h in heads]
        col_of = lambda i, h: packed[:, i * M_HEADS + h:i * M_HEADS + h + 1]

        raw = [lax.dot_general(q[h], kf[h].astype(BF16), (((1,), (1,)), ((), ())),
                               preferred_element_type=F32) for h in heads]
        prev = [_dot(q[h], ct[h].astype(BF16)) for h in heads]
        pmat = [jnp.exp(jnp.where(causal, a_row[h:h + 1, :] - col_of(0, h), -jnp.inf)) for h in heads]
        s = [(raw[h] * pmat[h]).astype(BF16) for h in heads]
        numden = [_dot(s[h], v_ext[h]) + col_of(1, h) * prev[h] for h in heads]
        upd = [lax.dot_general((kf[h] * col_of(3, h)).astype(BF16), v_ext[h], (((0,), (0,)), ((), ())),
                               preferred_element_type=F32) for h in heads]
        for h in heads:
            dec = jnp.concatenate([decay[h:h + 1, :]] * (ct[h].shape[1] // LANES), axis=1)
            ct_refs[h][b] = dec * ct[h] + upd[h]
        inv = [1.0 / jnp.maximum(jnp.abs(numden[h][:, M_V_DIM:]), col_of(2, h)) for h in heads]
        hh = [numden[h][:, 0:M_V_DIM] * jnp.concatenate([inv[h]] * (M_V_DIM // LANES), axis=1) for h in heads]
        mu = [jnp.mean(hh[h], axis=-1, keepdims=True) for h in heads]
        dlt = [hh[h] - mu[h] for h in heads]
        var = [jnp.mean(dlt[h] * dlt[h], axis=-1, keepdims=True) for h in heads]
        for h in heads:
            hn = dlt[h] * lax.rsqrt(var[h] + LN_EPS) * mhg_ref[:, vcols[h]]
            og = jax.nn.sigmoid(o_ref[b, :, vcols[h]].astype(F32))
            out_ref[b, :, vcols[h]] = (hn * og).astype(BF16)
        return carry

    for b in range(nb):
        per_batch(b, 0)


def _mlstm(z3, zif3, conv_w, conv_b, mh_g, side):
    nb, s, _ = z3.shape
    L = CHUNK
    steps = s // L
    blk = lambda colblk: pl.BlockSpec((nb, L, M_V), lambda c: (0, c, colblk))
    const = lambda shape: pl.BlockSpec(shape, lambda c: (0,) * len(shape))

    def side_spec(a):
        rb = max(BF16_SUBLANES, a.shape[0] // steps)
        return pl.BlockSpec((rb, a.shape[1]), lambda c: (jnp.minimum(c, a.shape[0] // rb - 1), 0))

    return pl.pallas_call(
        functools.partial(_mlstm_kernel, nb=nb, n_side=len(side)),
        grid=(steps,),
        in_specs=[blk(0), blk(1), blk(2),
                  pl.BlockSpec((nb, L, LANES), lambda c: (0, c, 0)),
                  const((CONV_W, 2 * M_QK)), const((1, 2 * M_QK)), const((1, M_V))] + [side_spec(a) for a in side],
        out_specs=[pl.BlockSpec((nb, L, M_V), lambda c: (0, c, 0))] + [side_spec(a) for a in side],
        out_shape=[jax.ShapeDtypeStruct((nb, s, M_V), BF16)] + [jax.ShapeDtypeStruct(a.shape, BF16) for a in side],
        scratch_shapes=([pltpu.VMEM((nb, L + CONV_HALO, 2 * M_QK), BF16)]
                        + [pltpu.VMEM((nb, M_QK_DIM, M_V_DIM + LANES), F32)] * M_HEADS
                        + [pltpu.VMEM((nb, SUBLANES, L), F32)]),
        compiler_params=_params(1),
        name="mlstm",
    )(z3, z3, z3, zif3, conv_w, conv_b, mh_g, *side)


def _branch_kernel(hg_ref, u_ref, uh_ref, gm_ref, gp_ref, wpool_ref, ps_ref, wm_ref, wp_ref,
                   out_ref, xa, xb, yp_ref, *, tm, seq):
    H = POOL_HALO
    G = POOL_GROUP_DIM
    t0 = lax.rem(pl.program_id(0) * tm, seq)
    u = u_ref[...].astype(F32)
    xa[H:H + tm, :] = u
    xa[0:H, :] = jnp.where(t0 == 0, 0.0, uh_ref[...].astype(F32))
    n = tm + H - 8
    xb[8:8 + n, :] = xa[8:8 + n, :] + xa[7:7 + n, :]
    n = tm + H - 16
    xa[16:16 + n, G:] = xb[16:16 + n, G:] + xb[14:14 + n, G:]
    n = tm + H - 24
    xb[24:24 + n, 2 * G:] = xa[24:24 + n, 2 * G:] + xa[20:20 + n, 2 * G:]
    xa[H:H + tm, 3 * G:] = xb[H:H + tm, 3 * G:] + xb[H - 8:H - 8 + tm, 3 * G:]
    tpos = t0 + lax.broadcasted_iota(jnp.int32, (tm, 1), 0)
    for g, win in enumerate(POOL_WINDOWS):
        src = (xb, xa, xb, xa)[g]
        cols = slice(g * G, (g + 1) * G)
        cnt = jnp.minimum(tpos + 1, win).astype(F32)
        y = src[H:H + tm, cols] / cnt - u[:, cols]
        yp = _dot(y.astype(BF16), wpool_ref[g]) * ps_ref[:, cols]
        yp_ref[:, cols] = yp.astype(BF16)
    hm = tm // ROW_SPLIT
    for r0 in range(0, tm, hm):
        rows = slice(r0, r0 + hm)
        pb = _dot(yp_ref[rows, :], wp_ref[...])
        a = _dot(hg_ref[rows, :], wm_ref[...])
        merged = (jax.nn.sigmoid(gm_ref[rows, :].astype(F32)) * a
                  + jax.nn.sigmoid(gp_ref[rows, :].astype(F32)) * pb)
        out_ref[rows, :] = merged.astype(BF16)


def _branch(hg2d, z_main, w_pool, pool_scale, w_m_br, w_p_br, seq, tm):
    t = hg2d.shape[0]
    d = w_m_br.shape[1]
    hb = tm // POOL_HALO
    const = lambda shape: pl.BlockSpec(shape, lambda i: (0,) * len(shape))
    return pl.pallas_call(
        functools.partial(_branch_kernel, tm=tm, seq=seq),
        grid=(t // tm,),
        in_specs=[pl.BlockSpec((tm, M_V), lambda i: (i, 0)),
                  pl.BlockSpec((tm, POOL_W), lambda i: (i, 3)),
                  pl.BlockSpec((POOL_HALO, POOL_W), lambda i: (jnp.maximum(i * hb - 1, 0), 3)),
                  pl.BlockSpec((tm, d), lambda i: (i, 2)),
                  pl.BlockSpec((tm, d), lambda i: (i, 3)),
                  const(w_pool.shape), const((1, POOL_W)), const(w_m_br.shape), const(w_p_br.shape)],
        out_specs=pl.BlockSpec((tm, d), lambda i: (i, 0)),
        out_shape=jax.ShapeDtypeStruct((t, d), BF16),
        scratch_shapes=[pltpu.VMEM((tm + POOL_HALO, POOL_W), F32),
                        pltpu.VMEM((tm + POOL_HALO, POOL_W), F32),
                        pltpu.VMEM((tm, POOL_W), BF16)],
        compiler_params=_params(1),
        name="branch",
    )(hg2d, z_main, z_main, z_main, z_main, w_pool, pool_scale, w_m_br, w_p_br)


def _route_kernel(mg_ref, x_ref, wout_ref, g_ref, b_ref, wr_ref, br_ref, upper_ref,
                  x1_ref, route_ref, gate_ref, cnt_ref, carry, *, alpha, tm):
    @pl.when(pl.program_id(0) == 0)
    def _():
        carry[...] = jnp.zeros_like(carry)

    hm = tm // ROW_SPLIT
    pitch = _pitch(x_ref.shape[1])
    logits = []
    for r0 in range(0, tm, hm):
        r = alpha * x_ref[r0:r0 + hm, :] + _dot(mg_ref[r0:r0 + hm, :], wout_ref[...])
        x1 = _layer_norm(r, g_ref[...], b_ref[...])
        _store_token_major(x1_ref.at[r0 * pitch:(r0 + hm) * pitch], x1)
        logits.append(_dot(x1.astype(BF16), wr_ref[...]) + br_ref[...])
    lt = jnp.concatenate(logits, axis=0).T

    sub = lax.broadcasted_iota(jnp.int32, (SUBLANES, tm), 0).astype(F32)
    none = float(SUBLANES)

    def softmax(z):
        e = jnp.exp(z - jnp.max(z, axis=0, keepdims=True))
        return e / jnp.sum(e, axis=0, keepdims=True)

    def top1(vals):
        top = jnp.max(vals, axis=0, keepdims=True)
        return top, jnp.min(jnp.where(vals == top, sub, none), axis=0, keepdims=True)

    is_grp = sub < N_GROUPS
    pg = jnp.where(is_grp, softmax(jnp.where(is_grp, lt[0:SUBLANES, :], -jnp.inf)), -1.0)
    pg_top, g_idx = top1(pg)
    le_sel = lt[SUBLANES:2 * SUBLANES, :]
    for g in range(1, N_GROUPS):
        le_sel = jnp.where(g_idx == g, lt[(g + 1) * SUBLANES:(g + 2) * SUBLANES, :], le_sel)
    pe = softmax(le_sel)
    pe1, i1 = top1(pe)
    pe2, i2 = top1(jnp.where(sub == i1, -1.0, pe))
    den = pe1 + pe2
    gate1 = pg_top * pe1 / den
    gate2 = pg_top * pe2 / den
    e1 = g_idx * EXPERTS_PER_GROUP + i1
    e2 = g_idx * EXPERTS_PER_GROUP + i2

    eid = lax.broadcasted_iota(jnp.int32, (N_EXPERTS, tm), 0).astype(F32)
    hit1 = eid == e1
    hit2 = eid == e2
    onehot = jnp.where(hit1 | hit2, 1.0, 0.0)
    incl = _dot(onehot.astype(BF16), upper_ref[...])
    excl = incl - onehot + carry[:, 0:1]
    r1 = jnp.sum(jnp.where(hit1, excl, 0.0), axis=0, keepdims=True)
    r2 = jnp.sum(jnp.where(hit2, excl, 0.0), axis=0, keepdims=True)
    carry[...] = carry[...] + jnp.broadcast_to(incl[:, tm - 1:tm], carry.shape)
    cnt_ref[...] = carry[...]

    route = jnp.where(sub == 0, e1, jnp.where(sub == 1, e2, jnp.where(sub == 2, r1, jnp.where(sub == 3, r2, 0.0))))
    route_ref[...] = route.astype(jnp.int32)
    gate_ref[...] = jnp.where(sub == 0, gate1, jnp.where(sub == 1, gate2, 0.0))


def _route(merged, x2d, w_out, ln_g, ln_b, w_r, b_r, upper, alpha, tm):
    t, d = x2d.shape
    const = lambda shape: pl.BlockSpec(shape, lambda i: (0,) * len(shape))
    rows = lambda width: pl.BlockSpec((tm, width), lambda i: (i, 0))
    cols = pl.BlockSpec((SUBLANES, tm), lambda i: (0, i))
    return pl.pallas_call(
        functools.partial(_route_kernel, alpha=alpha, tm=tm),
        grid=(t // tm,),
        in_specs=[rows(d), rows(d), const(w_out.shape), const((1, d)), const((1, d)),
                  const(w_r.shape), const((1, LANES)), const((tm, tm))],
        out_specs=[pl.BlockSpec((tm * _pitch(d), LANES), lambda i: (i, 0)),
                   cols, cols, const((N_EXPERTS, LANES))],
        out_shape=[jax.ShapeDtypeStruct((t * _pitch(d), LANES), F32),
                   jax.ShapeDtypeStruct((SUBLANES, t), jnp.int32),
                   jax.ShapeDtypeStruct((SUBLANES, t), F32),
                   jax.ShapeDtypeStruct((N_EXPERTS, LANES), F32)],
        scratch_shapes=[pltpu.VMEM((N_EXPERTS, LANES), F32)],
        compiler_params=_params(1),
        name="route",
    )(merged, x2d, w_out, ln_g, ln_b, w_r, b_r, upper)


def _dispatch_kernel(pad_start_ref, pad_len_ref, nu_ref, dest_ref, x_ref, xs_hbm, sem, *, tb, te, d):
    def copy(src_tok, dst_tok, n=1):
        return pltpu.make_async_copy(_tokens(x_ref, src_tok, d, n), _tokens(xs_hbm, dst_tok, d, n), sem)

    def wait_tokens(n):
        def body(r, c):
            copy(0, 0, n).wait()
            return c
        return body

    @pl.when(pl.program_id(0) == 0)
    def _():
        def per_expert(e, total):
            start = pad_start_ref[e]
            npad = pad_len_ref[e]

            def fill(r, c):
                copy(0, start + r).start()
                return c

            lax.fori_loop(0, npad, fill, 0)
            return total + npad

        total = lax.fori_loop(0, N_EXPERTS, per_expert, 0)
        lax.fori_loop(0, total // SUBLANES, wait_tokens(SUBLANES), 0)
        lax.fori_loop(0, total % SUBLANES, wait_tokens(1), 0)

        def fill_tile(j, c):
            copy(0, j * te, te).start()
            return c

        n_tiles = xs_hbm.shape[0] // (te * _pitch(d))
        lax.fori_loop(nu_ref[0], n_tiles, fill_tile, 0)
        lax.fori_loop(nu_ref[0], n_tiles, wait_tokens(te), 0)

    def scatter(t, c):
        for k in range(TOP_K):
            copy(t, dest_ref[TOP_K * t + k]).start(priority=k % 2)
        return c

    lax.fori_loop(0, tb, scatter, 0, unroll=DMA_UNROLL)
    for k in range(TOP_K):
        copy(0, 0, tb).wait()


def _dispatch(x1t, dest_flat, pad_start, pad_len, n_used, n_slots, tb, te, d):
    pitch = _pitch(d)
    t = x1t.shape[0] // pitch
    grid_spec = pltpu.PrefetchScalarGridSpec(
        num_scalar_prefetch=3,
        grid=(t // tb,),
        in_specs=[pl.BlockSpec((TOP_K * tb,), lambda i, ps, pn, nu: (i,), memory_space=pltpu.SMEM),
                  pl.BlockSpec((tb * pitch, LANES), lambda i, ps, pn, nu: (i, 0))],
        out_specs=pl.BlockSpec(memory_space=pl.ANY),
        scratch_shapes=[pltpu.SemaphoreType.DMA(())],
    )
    return pl.pallas_call(
        functools.partial(_dispatch_kernel, tb=tb, te=te, d=d),
        grid_spec=grid_spec,
        out_shape=jax.ShapeDtypeStruct((n_slots * pitch, LANES), x1t.dtype),
        compiler_params=_params(1),
        name="dispatch",
    )(pad_start, pad_len, n_used, dest_flat, x1t)


def _expert_kernel(be_ref, nu_ref, x_ref, wg_ref, wu_ref, wd_ref, y_ref, *, te, d):
    i = pl.program_id(0)

    @pl.when(i < nu_ref[0])
    def _():
        xb = _load_token_major(x_ref, te, d).astype(BF16)
        gt = _dot(xb, wg_ref[0])
        up = _dot(xb, wu_ref[0])
        hid = gt * jax.nn.sigmoid(gt) * up
        _store_token_major(y_ref, _dot(hid.astype(BF16), wd_ref[0]))

    @pl.when(i >= nu_ref[0])
    def _():
        y_ref[...] = jnp.zeros_like(y_ref)


def _experts(xs, block_e, n_used, w_gate, w_up, w_down, te):
    _, d, de = w_gate.shape
    pitch = _pitch(d)
    n_tiles = xs.shape[0] // (te * pitch)
    grid_spec = pltpu.PrefetchScalarGridSpec(
        num_scalar_prefetch=2,
        grid=(n_tiles,),
        in_specs=[pl.BlockSpec((te * pitch, LANES), lambda i, be, nu: (jnp.minimum(i, nu[0] - 1), 0)),
                  pl.BlockSpec((1, d, de), lambda i, be, nu: (be[i], 0, 0)),
                  pl.BlockSpec((1, d, de), lambda i, be, nu: (be[i], 0, 0)),
                  pl.BlockSpec((1, de, d), lambda i, be, nu: (be[i], 0, 0))],
        out_specs=pl.BlockSpec((te * pitch, LANES), lambda i, be, nu: (i, 0)),
    )
    return pl.pallas_call(
        functools.partial(_expert_kernel, te=te, d=d),
        grid_spec=grid_spec,
        out_shape=jax.ShapeDtypeStruct(xs.shape, F32),
        compiler_params=_params(1),
        name="experts",
    )(block_e, n_used, xs, w_gate, w_up, w_down)


def _final_kernel(dcur_ref, dnxt_ref, x1_ref, gate_ref, p_ref, ys_hbm, g_ref, b_ref, wg_ref, bg_ref, wp_ref,
                  out_ref, ybuf, sem, *, alpha, tm, d):
    i = pl.program_id(0)
    slot = lax.rem(i, 2)

    def issue(dest_ref, s):
        def body(t, c):
            for k in range(TOP_K):
                pltpu.make_async_copy(_tokens(ys_hbm, dest_ref[TOP_K * t + k], d),
                                      _tokens(ybuf.at[s, k], t, d), sem.at[s]).start(priority=k % 2)
            return c
        lax.fori_loop(0, tm, body, 0, unroll=DMA_UNROLL)

    @pl.when(i == 0)
    def _():
        issue(dcur_ref, 0)

    @pl.when(i + 1 < pl.num_programs(0))
    def _():
        issue(dnxt_ref, 1 - slot)

    for k in range(TOP_K):
        pltpu.make_async_copy(_tokens(ys_hbm, 0, d, tm), ybuf.at[slot, k], sem.at[slot]).wait()

    hm = tm // ROW_SPLIT
    for r0 in range(0, tm, hm):
        rows = slice(r0, r0 + hm)
        trows = pl.ds(r0 * _pitch(d), hm * _pitch(d))
        pp = _dot(p_ref[rows, :].astype(BF16), wp_ref[...])
        gate = gate_ref[rows, :]
        y = (gate[:, 0:1] * _load_token_major(ybuf.at[slot, 0, trows], hm, d)
             + gate[:, 1:2] * _load_token_major(ybuf.at[slot, 1, trows], hm, d))
        x2 = _layer_norm(alpha * _load_token_major(x1_ref.at[trows], hm, d) + y, g_ref[...], b_ref[...])
        gl = _dot(x2.astype(BF16), wg_ref[...]) + bg_ref[...]
        out_ref[rows, :] = x2 + jax.nn.sigmoid(gl) * pp


def _final(dest_flat, x1t, gate, p2d, ys, ln_g, ln_b, w_pg, b_pg, w_pp, alpha, tm):
    d = w_pg.shape[0]
    pitch = _pitch(d)
    t = x1t.shape[0] // pitch
    n = t // tm
    const = lambda shape: pl.BlockSpec(shape, lambda i: (0,) * len(shape))
    resident = lambda shape: pl.BlockSpec(shape, lambda i: (0,) * len(shape), pipeline_mode=pl.Buffered(1))
    rows = lambda width: pl.BlockSpec((tm, width), lambda i: (i, 0))
    return pl.pallas_call(
        functools.partial(_final_kernel, alpha=alpha, tm=tm, d=d),
        grid=(n,),
        in_specs=[pl.BlockSpec((TOP_K * tm,), lambda i: (i,), memory_space=pltpu.SMEM),
                  pl.BlockSpec((TOP_K * tm,), lambda i: (jnp.minimum(i + 1, n - 1),), memory_space=pltpu.SMEM),
                  pl.BlockSpec((tm * pitch, LANES), lambda i: (i, 0)),
                  rows(LANES), rows(p2d.shape[1]),
                  pl.BlockSpec(memory_space=pl.ANY),
                  const((1, d)), const((1, d)), resident(w_pg.shape), const((1, d)), resident(w_pp.shape)],
        out_specs=rows(d),
        out_shape=jax.ShapeDtypeStruct((t, d), F32),
        scratch_shapes=[pltpu.VMEM((2, TOP_K, tm * pitch, LANES), F32), pltpu.SemaphoreType.DMA((2,))],
        compiler_params=_params(1),
        name="final",
    )(dest_flat, dest_flat, x1t, gate, p2d, ys, ln_g, ln_b, w_pg, b_pg, w_pp)


def _pad_cols(a, width):
    return jnp.pad(a, ((0, 0), (0, width - a.shape[1])))


def _tri(n):
    return jnp.tril(jnp.ones((n, n), BF16))


def _layer(x, p, w_in_t, b_in, conv_w, conv_b, mh_g, w_pool, pool_scale, w_m_br, w_p_br, w_out,
           ln1_g, ln1_b, w_rg, b_rg, w_re, b_re, w_gate, w_up, w_down, ln2_g, ln2_b,
           w_ple_gate, b_ple_gate, w_ple_proj, alpha):
    nb, seq, d = x.shape
    t = nb * seq
    x2d = x.reshape(t, d)
    row = lambda a: a.reshape(1, -1)

    c_if = 2 * M_QK + 2 * M_V
    w_main, w_if = _regroup(w_in_t, c_if, 2 * M_HEADS, 1024, min(1024, d))
    b_main = row(jnp.concatenate([b_in[:c_if], b_in[c_if + 2 * M_HEADS:]]))
    b_if = _pad_cols(row(b_in[c_if:c_if + 2 * M_HEADS]), LANES)

    n_exp, _, d_exp = w_gate.shape
    expert_w = [w_gate.reshape(n_exp * d, d_exp), w_up.reshape(n_exp * d, d_exp), w_down.reshape(n_exp * d_exp, d)]
    tm_in = min(1024, t)
    z_main, z_if, wg_b, wu_b, wd_b = _inproj(x2d, w_main, b_main, w_if, b_if, tm_in, 1024, expert_w)

    dense_w = [w_pool, w_m_br, w_p_br, w_out, w_ple_gate, w_ple_proj]
    hg, *dense_b = _mlstm(z_main.reshape(nb, seq, -1), z_if.reshape(nb, seq, LANES),
                          conv_w, row(conv_b), row(mh_g), [a.reshape(-1, a.shape[-1]) for a in dense_w])
    w_pool, w_m_br, w_p_br, w_out, w_ple_gate, w_ple_proj = [b.reshape(a.shape) for a, b in zip(dense_w, dense_b)]

    tm = min(512, seq)
    merged = _branch(hg.reshape(t, M_V), z_main, w_pool, row(pool_scale), w_m_br, w_p_br, seq, tm)

    w_r = _pad_cols(jnp.concatenate([_pad_cols(w_rg, SUBLANES), w_re], axis=1), LANES).astype(BF16)
    b_r = _pad_cols(jnp.concatenate([_pad_cols(row(b_rg), SUBLANES), row(b_re)], axis=1), LANES)
    x1, route, gate_t, cnt = _route(merged, x2d, w_out, row(ln1_g), row(ln1_b),
                                    w_r, b_r, _tri(tm).T, alpha, tm)
    gate = _pad_cols(gate_t[0:TOP_K].T, LANES)

    te = EXPERT_TILE
    counts = cnt[:, 0].astype(jnp.int32)
    pcounts = (counts + te - 1) // te * te
    pends = jnp.cumsum(pcounts)
    pstarts = pends - pcounts
    e_sel = route[0:TOP_K, :, None] == jnp.arange(N_EXPERTS, dtype=jnp.int32)
    dest = (jnp.sum(jnp.where(e_sel, pstarts, 0), axis=-1) + route[TOP_K:2 * TOP_K]).T.reshape(-1)
    n_slots = t * TOP_K + N_EXPERTS * te
    n_tiles = n_slots // te
    n_used = (pends[-1] // te).reshape(1)
    tile_row = jnp.minimum(jnp.arange(n_tiles, dtype=jnp.int32), n_used - 1) * te
    block_e = jnp.minimum(jnp.sum((tile_row[:, None] >= pends[None, :]).astype(jnp.int32), axis=1), N_EXPERTS - 1)

    xs = _dispatch(x1, dest, pstarts + counts, pcounts - counts, n_used, n_slots, min(512, t), te, d)
    ys = _experts(xs, block_e, n_used, wg_b.reshape(n_exp, d, d_exp), wu_b.reshape(n_exp, d, d_exp),
                  wd_b.reshape(n_exp, d_exp, d), te)
    return _final(dest, x1, gate, p.reshape(t, -1), ys, row(ln2_g), row(ln2_b),
                  w_ple_gate, row(b_ple_gate), w_ple_proj, alpha,
                  min(512, t)).reshape(nb, seq, d)


def kernel(x, p, w_in, b_in, conv_w, conv_b, mh_g, w_pool, pool_scale, w_m_br, w_p_br, w_out, ln1_g, ln1_b, w_rg, b_rg, w_re, b_re, w_gate, w_up, w_down, ln2_g, ln2_b, w_ple_gate, b_ple_gate, w_ple_proj):
    depth = w_in.shape[0]
    alpha = (2 * depth) ** 0.25
    for i in range(depth):
        x = _layer(x, p[i], w_in[i].T, b_in[i], conv_w[i], conv_b[i], mh_g[i], w_pool[i], pool_scale[i],
                   w_m_br[i], w_p_br[i], w_out[i], ln1_g[i], ln1_b[i], w_rg[i], b_rg[i], w_re[i], b_re[i],
                   w_gate[i], w_up[i], w_down[i], ln2_g[i], ln2_b[i], w_ple_gate[i], b_ple_gate[i],
                   w_ple_proj[i], alpha)
    return x
```

```python
import functools

import jax
import jax.numpy as jnp
from jax import lax
from jax.experimental import pallas as pl
from jax.experimental.pallas import tpu as pltpu

F32 = jnp.float32
BF16 = jnp.bfloat16

M_HEADS = 4
M_QK_DIM = 128
M_V_DIM = 256
M_QK = M_HEADS * M_QK_DIM
M_V = M_HEADS * M_V_DIM
CONV_W = 4
CHUNK = 128
POOL_WINDOWS = (2, 4, 8, 16)
POOL_GROUP_DIM = 256
POOL_W = len(POOL_WINDOWS) * POOL_GROUP_DIM
N_GROUPS = 4
EXPERTS_PER_GROUP = 8
N_EXPERTS = N_GROUPS * EXPERTS_PER_GROUP
TOP_K = 2
LN_EPS = 1e-5

LANES = 128
SUBLANES = 8
VMEM_LIMIT = 56 * 1024 * 1024
BF16_SUBLANES = 16
CONV_HALO = BF16_SUBLANES
POOL_HALO = 32
EXPERT_TILE = 256
DMA_UNROLL = 8
ROW_SPLIT = 2


def _dot(a, b):
    return jnp.dot(a, b, preferred_element_type=F32)


def _params(n_grid):
    return pltpu.CompilerParams(dimension_semantics=("arbitrary",) * n_grid,
                                vmem_limit_bytes=VMEM_LIMIT)


def _log_sigmoid(x):
    return -(jnp.maximum(-x, 0.0) + jnp.log1p(jnp.exp(-jnp.abs(x))))


def _layer_norm(r, g, b):
    mu = jnp.mean(r, axis=-1, keepdims=True)
    d = r - mu
    var = jnp.mean(d * d, axis=-1, keepdims=True)
    return d * lax.rsqrt(var + LN_EPS) * g + b


def _pitch(d):
    return d // LANES + 1


def _store_token_major(ref, val):
    n, d = val.shape
    for c in range(d // LANES):
        ref[pl.ds(c, n, stride=_pitch(d)), :] = val[:, c * LANES:(c + 1) * LANES]
    ref[pl.ds(d // LANES, n, stride=_pitch(d)), :] = jnp.zeros((n, LANES), val.dtype)


def _load_token_major(ref, n, d):
    return jnp.concatenate([ref[pl.ds(c, n, stride=_pitch(d)), :] for c in range(d // LANES)], axis=1)


def _tokens(ref, tok, d, n=1):
    return ref.at[pl.ds(tok * _pitch(d), n * _pitch(d))]


def _side_specs(side, steps, step_of):
    def spec(a):
        rb = max(BF16_SUBLANES, a.shape[0] // steps)
        return pl.BlockSpec((rb, a.shape[1]), lambda *g: (jnp.minimum(step_of(*g), a.shape[0] // rb - 1), 0))
    return [spec(a) for a in side]


def _side_shapes(side):
    return [jax.ShapeDtypeStruct(a.shape, BF16) for a in side]


def _side_cast(side_in, side_out):
    for src, dst in zip(side_in, side_out):
        dst[...] = src[...].astype(BF16)


def _regroup_kernel(a_ref, c_ref, out_ref, cut_ref):
    out_ref[...] = a_ref[...].T.astype(BF16)
    skip, tk = c_ref.shape
    cut_ref[...] = jnp.concatenate([c_ref[...], jnp.zeros((LANES - skip, tk), F32)], axis=0).T


def _regroup(w_t, cut, skip, tn, tk):
    n, k = w_t.shape
    n_keep = cut // tn
    assert skip % SUBLANES == 0
    first_row = lambda j: pl.multiple_of(jnp.where(j < n_keep, j * tn, j * tn + skip), SUBLANES)
    return pl.pallas_call(
        _regroup_kernel,
        grid=(k // tk, (n - skip) // tn),
        in_specs=[pl.BlockSpec((pl.Element(tn), pl.Element(tk)), lambda kk, j: (first_row(j), kk * tk)),
                  pl.BlockSpec((pl.Element(skip), pl.Element(tk)), lambda kk, j: (cut, kk * tk))],
        out_specs=[pl.BlockSpec((tk, tn), lambda kk, j: (kk, j)),
                   pl.BlockSpec((tk, LANES), lambda kk, j: (kk, 0))],
        out_shape=[jax.ShapeDtypeStruct((k, n - skip), BF16), jax.ShapeDtypeStruct((k, LANES), F32)],
        compiler_params=_params(2),
        name="regroup",
    )(w_t, w_t)


def _inproj_kernel(*refs, n_side):
    x_ref, w_ref, b_ref, wif_ref, bif_ref = refs[:5]
    side_in = refs[5:5 + n_side]
    z_ref, zif_ref = refs[5 + n_side:7 + n_side]
    side_out = refs[7 + n_side:7 + 2 * n_side]
    xb_ref = refs[7 + 2 * n_side]

    @pl.when(pl.program_id(1) == 0)
    def _():
        xb = x_ref[...].astype(BF16)
        xb_ref[...] = xb
        zif_ref[...] = _dot(xb, wif_ref[...].astype(BF16)) + bif_ref[...]

    z_ref[...] = (_dot(xb_ref[...], w_ref[...]) + b_ref[...]).astype(BF16)
    _side_cast(side_in, side_out)


def _inproj(x2d, w_main, b_main, w_if, b_if, tm, tn, side):
    t, d = x2d.shape
    n = w_main.shape[1]
    nj = n // tn
    side_specs = _side_specs(side, (t // tm) * nj, lambda i, j: i * nj + j)
    return pl.pallas_call(
        functools.partial(_inproj_kernel, n_side=len(side)),
        grid=(t // tm, nj),
        in_specs=[pl.BlockSpec((tm, d), lambda i, j: (i, 0)),
                  pl.BlockSpec((d, tn), lambda i, j: (0, j)),
                  pl.BlockSpec((1, tn), lambda i, j: (0, j)),
                  pl.BlockSpec((d, LANES), lambda i, j: (0, 0)),
                  pl.BlockSpec((1, LANES), lambda i, j: (0, 0))] + side_specs,
        out_specs=[pl.BlockSpec((tm, tn), lambda i, j: (i, j)),
                   pl.BlockSpec((tm, LANES), lambda i, j: (i, 0))] + side_specs,
        out_shape=[jax.ShapeDtypeStruct((t, n), BF16),
                   jax.ShapeDtypeStruct((t, LANES), F32)] + _side_shapes(side),
        scratch_shapes=[pltpu.VMEM((tm, d), BF16)],
        compiler_params=_params(2),
        name="inproj",
    )(x2d, w_main, b_main, w_if, b_if, *side)


def _mlstm_kernel(*refs, nb, n_side):
    qk_ref, v_ref, o_ref, g_ref, cw_ref, cb_ref, mhg_ref = refs[:7]
    side_in = refs[7:7 + n_side]
    out_ref = refs[7 + n_side]
    side_out = refs[8 + n_side:8 + 2 * n_side]
    cbuf = refs[8 + 2 * n_side]
    state = refs[9 + 2 * n_side:]
    _mlstm_body(qk_ref, v_ref, o_ref, g_ref, cw_ref, cb_ref, mhg_ref, out_ref, cbuf, state, nb)
    _side_cast(side_in, side_out)


def _mlstm_body(qk_ref, v_ref, o_ref, g_ref, cw_ref, cb_ref, mhg_ref, out_ref, cbuf, state, nb):
    L = CHUNK
    halo = CONV_HALO
    ct_refs, m_ref = state[:M_HEADS], state[M_HEADS]

    @pl.when(pl.program_id(0) == 0)
    def _():
        cbuf[:, 0:halo, :] = jnp.zeros((nb, halo, 2 * M_QK), BF16)
        for ref in state:
            ref[...] = jnp.zeros_like(ref)

    cbuf[:, halo:halo + L, :] = qk_ref[...]
    scale = M_QK_DIM ** -0.5
    row = lax.broadcasted_iota(jnp.int32, (L, L), 0)
    col = lax.broadcasted_iota(jnp.int32, (L, L), 1)
    causal = col <= row
    upper = jnp.where(row <= col, 1.0, 0.0).astype(BF16)
    ones_blk = jnp.ones((L, LANES), BF16)
    lane8 = lax.broadcasted_iota(jnp.int32, (SUBLANES, L), 1)
    wrow = lax.broadcasted_iota(jnp.int32, (L, L + halo), 0)
    wcol = lax.broadcasted_iota(jnp.int32, (L, L + halo), 1)
    shifts = [jnp.where(wcol == wrow + (halo - (CONV_W - 1) + tap), 1.0, 0.0).astype(BF16)
              for tap in range(CONV_W - 1)]
    sub = lax.broadcasted_iota(jnp.int32, (SUBLANES, LANES), 0)

    def per_batch(b, carry):
        window = cbuf[b]
        acc = cb_ref[...] + _dot(shifts[0], window) * cw_ref[0:1, :]
        for tap in range(1, CONV_W - 1):
            acc = acc + _dot(shifts[tap], window) * cw_ref[tap:tap + 1, :]
        acc = acc + qk_ref[b].astype(F32) * cw_ref[CONV_W - 1:CONV_W, :]
        qk = acc * jax.nn.sigmoid(acc)
        cbuf[b, 0:halo, :] = cbuf[b, L:L + halo, :]

        g_row = g_ref[b].T[0:SUBLANES, :]
        lf = _log_sigmoid(g_row)
        lf_hi = lf.astype(BF16)
        lf_lo = (lf - lf_hi.astype(F32)).astype(BF16)
        cs = _dot(jnp.concatenate([lf_hi, lf_lo], axis=0), upper)
        b_row = pltpu.roll(cs[0:SUBLANES, :] + cs[SUBLANES:2 * SUBLANES, :], M_HEADS, 0)
        a_row = g_row - b_row
        cm = a_row
        for sh in (1, 2, 4, 8, 16, 32, 64):
            cm = jnp.maximum(cm, jnp.where(lane8 >= sh, pltpu.roll(cm, sh, 1), -jnp.inf))
        m_prev = m_ref[b]
        mx = jnp.maximum(m_prev, cm)
        mx_last = jnp.maximum(m_prev, jnp.max(a_row, axis=1, keepdims=True))
        carry_in = jnp.exp(m_prev - mx)
        floor = jnp.exp(-(b_row + mx))
        wts = jnp.exp(a_row - mx_last)
        decay = jnp.exp(m_prev - mx_last)
        m_ref[b] = jnp.where(sub < M_HEADS, b_row[:, L - 1:L] + mx_last, 0.0)
        packed = jnp.concatenate(
            [jnp.where(sub < M_HEADS, mx, pltpu.roll(carry_in, M_HEADS, 0)),
             jnp.where(sub < M_HEADS, floor, pltpu.roll(wts, M_HEADS, 0)),
             jnp.zeros((L - 2 * SUBLANES, L), F32)], axis=0).T

        heads = range(M_HEADS)
        vcols = [slice(h * M_V_DIM, (h + 1) * M_V_DIM) for h in heads]
        q = [qk[:, h * M_QK_DIM:(h + 1) * M_QK_DIM].astype(BF16) for h in heads]
        kf = [qk[:, M_QK + h * M_QK_DIM:M_QK + (h + 1) * M_QK_DIM] * scale for h in heads]
        v_ext = [jnp.concatenate([v_ref[b, :, vcols[h]], ones_blk], axis=1) for h in heads]
        ct = [ct_refs[h][b] for h in heads]
        col_of = lambda i, h: packed[:, i * M_HEADS + h:i * M_HEADS + h + 1]

        raw = [lax.dot_general(q[h], kf[h].astype(BF16), (((1,), (1,)), ((), ())),
                               preferred_element_type=F32) for h in heads]
        prev = [_dot(q[h], ct[h].astype(BF16)) for h in heads]
        pmat = [jnp.exp(jnp.where(causal, a_row[h:h + 1, :] - col_of(0, h), -jnp.inf)) for h in heads]
        s = [(raw[h] * pmat[h]).astype(BF16) for h in heads]
        numden = [_dot(s[h], v_ext[h]) + col_of(1, h) * prev[h] for h in heads]
        upd = [lax.dot_general((kf[h] * col_of(3, h)).astype(BF16), v_ext[h], (((0,), (0,)), ((), ())),
                               preferred_element_type=F32) for h in heads]
        for h in heads:
            dec = jnp.concatenate([decay[h:h + 1, :]] * (ct[h].shape[1] // LANES), axis=1)
            ct_refs[h][b] = dec * ct[h] + upd[h]
        inv = [1.0 / jnp.maximum(jnp.abs(numden[h][:, M_V_DIM:]), col_of(2, h)) for h in heads]
        hh = [numden[h][:, 0:M_V_DIM] * jnp.concatenate([inv[h]] * (M_V_DIM // LANES), axis=1) for h in heads]
        mu = [jnp.mean(hh[h], axis=-1, keepdims=True) for h in heads]
        dlt = [hh[h] - mu[h] for h in heads]
        var = [jnp.mean(dlt[h] * dlt[h], axis=-1, keepdims=True) for h in heads]
        for h in heads:
            hn = dlt[h] * lax.rsqrt(var[h] + LN_EPS) * mhg_ref[:, vcols[h]]
            og = jax.nn.sigmoid(o_ref[b, :, vcols[h]].astype(F32))
            out_ref[b, :, vcols[h]] = (hn * og).astype(BF16)
        return carry

    for b in range(nb):
        per_batch(b, 0)


def _mlstm(z3, zif3, conv_w, conv_b, mh_g, side):
    nb, s, _ = z3.shape
    L = CHUNK
    blk = lambda colblk: pl.BlockSpec((nb, L, M_V), lambda c: (0, c, colblk))
    const = lambda shape: pl.BlockSpec(shape, lambda c: (0,) * len(shape))
    side_specs = _side_specs(side, s // L, lambda c: c)
    return pl.pallas_call(
        functools.partial(_mlstm_kernel, nb=nb, n_side=len(side)),
        grid=(s // L,),
        in_specs=[blk(0), blk(1), blk(2),
                  pl.BlockSpec((nb, L, LANES), lambda c: (0, c, 0)),
                  const((CONV_W, 2 * M_QK)), const((1, 2 * M_QK)), const((1, M_V))] + side_specs,
        out_specs=[pl.BlockSpec((nb, L, M_V), lambda c: (0, c, 0))] + side_specs,
        out_shape=[jax.ShapeDtypeStruct((nb, s, M_V), BF16)] + _side_shapes(side),
        scratch_shapes=([pltpu.VMEM((nb, L + CONV_HALO, 2 * M_QK), BF16)]
                        + [pltpu.VMEM((nb, M_QK_DIM, M_V_DIM + LANES), F32)] * M_HEADS
                        + [pltpu.VMEM((nb, SUBLANES, L), F32)]),
        compiler_params=_params(1),
        name="mlstm",
    )(z3, z3, z3, zif3, conv_w, conv_b, mh_g, *side)


def _branch_kernel(hg_ref, u_ref, uh_ref, gm_ref, gp_ref, wpool_ref, ps_ref, wm_ref, wp_ref,
                   out_ref, xa, xb, yp_ref, *, tm, seq):
    H = POOL_HALO
    G = POOL_GROUP_DIM
    t0 = lax.rem(pl.program_id(0) * tm, seq)
    u = u_ref[...].astype(F32)
    xa[H:H + tm, :] = u
    xa[0:H, :] = jnp.where(t0 == 0, 0.0, uh_ref[...].astype(F32))
    n = tm + H - 8
    xb[8:8 + n, :] = xa[8:8 + n, :] + xa[7:7 + n, :]
    n = tm + H - 16
    xa[16:16 + n, G:] = xb[16:16 + n, G:] + xb[14:14 + n, G:]
    n = tm + H - 24
    xb[24:24 + n, 2 * G:] = xa[24:24 + n, 2 * G:] + xa[20:20 + n, 2 * G:]
    xa[H:H + tm, 3 * G:] = xb[H:H + tm, 3 * G:] + xb[H - 8:H - 8 + tm, 3 * G:]
    tpos = t0 + lax.broadcasted_iota(jnp.int32, (tm, 1), 0)
    for g, win in enumerate(POOL_WINDOWS):
        src = (xb, xa, xb, xa)[g]
        cols = slice(g * G, (g + 1) * G)
        cnt = jnp.minimum(tpos + 1, win).astype(F32)
        y = src[H:H + tm, cols] / cnt - u[:, cols]
        yp = _dot(y.astype(BF16), wpool_ref[g]) * ps_ref[:, cols]
        yp_ref[:, cols] = yp.astype(BF16)
    hm = tm // ROW_SPLIT
    for r0 in range(0, tm, hm):
        rows = slice(r0, r0 + hm)
        pb = _dot(yp_ref[rows, :], wp_ref[...])
        a = _dot(hg_ref[rows, :], wm_ref[...])
        merged = (jax.nn.sigmoid(gm_ref[rows, :].astype(F32)) * a
                  + jax.nn.sigmoid(gp_ref[rows, :].astype(F32)) * pb)
        out_ref[rows, :] = merged.astype(BF16)


def _branch_side_kernel(*refs, n_side, tm, seq):
    side_in = refs[9:9 + n_side]
    side_out = refs[10 + n_side:10 + 2 * n_side]
    _branch_kernel(*refs[:9], refs[9 + n_side], *refs[10 + 2 * n_side:], tm=tm, seq=seq)
    _side_cast(side_in, side_out)


def _branch(hg2d, z_main, w_pool, pool_scale, w_m_br, w_p_br, seq, tm, side):
    t = hg2d.shape[0]
    d = w_m_br.shape[1]
    hb = tm // POOL_HALO
    const = lambda shape: pl.BlockSpec(shape, lambda i: (0,) * len(shape))
    side_specs = _side_specs(side, t // tm, lambda i: i)
    return pl.pallas_call(
        functools.partial(_branch_side_kernel, n_side=len(side), tm=tm, seq=seq),
        grid=(t // tm,),
        in_specs=[pl.BlockSpec((tm, M_V), lambda i: (i, 0)),
                  pl.BlockSpec((tm, POOL_W), lambda i: (i, 3)),
                  pl.BlockSpec((POOL_HALO, POOL_W), lambda i: (jnp.maximum(i * hb - 1, 0), 3)),
                  pl.BlockSpec((tm, d), lambda i: (i, 2)),
                  pl.BlockSpec((tm, d), lambda i: (i, 3)),
                  const(w_pool.shape), const((1, POOL_W)), const(w_m_br.shape), const(w_p_br.shape)] + side_specs,
        out_specs=[pl.BlockSpec((tm, d), lambda i: (i, 0))] + side_specs,
        out_shape=[jax.ShapeDtypeStruct((t, d), BF16)] + _side_shapes(side),
        scratch_shapes=[pltpu.VMEM((tm + POOL_HALO, POOL_W), F32),
                        pltpu.VMEM((tm + POOL_HALO, POOL_W), F32),
                        pltpu.VMEM((tm, POOL_W), BF16)],
        compiler_params=_params(1),
        name="branch",
    )(hg2d, z_main, z_main, z_main, z_main, w_pool, pool_scale, w_m_br, w_p_br, *side)


def _route_kernel(mg_ref, x_ref, wout_ref, g_ref, b_ref, wr_ref, br_ref, upper_ref,
                  x1_ref, route_ref, gate_ref, cnt_ref, carry, *, alpha, tm):
    @pl.when(pl.program_id(0) == 0)
    def _():
        carry[...] = jnp.zeros_like(carry)

    hm = tm // ROW_SPLIT
    pitch = _pitch(x_ref.shape[1])
    logits = []
    for r0 in range(0, tm, hm):
        r = alpha * x_ref[r0:r0 + hm, :] + _dot(mg_ref[r0:r0 + hm, :], wout_ref[...])
        x1 = _layer_norm(r, g_ref[...], b_ref[...])
        _store_token_major(x1_ref.at[r0 * pitch:(r0 + hm) * pitch], x1)
        logits.append(_dot(x1.astype(BF16), wr_ref[...]) + br_ref[...])
    lt = jnp.concatenate(logits, axis=0).T

    sub = lax.broadcasted_iota(jnp.int32, (SUBLANES, tm), 0).astype(F32)
    none = float(SUBLANES)

    def softmax(z):
        e = jnp.exp(z - jnp.max(z, axis=0, keepdims=True))
        return e / jnp.sum(e, axis=0, keepdims=True)

    def top1(vals):
        top = jnp.max(vals, axis=0, keepdims=True)
        return top, jnp.min(jnp.where(vals == top, sub, none), axis=0, keepdims=True)

    is_grp = sub < N_GROUPS
    pg = jnp.where(is_grp, softmax(jnp.where(is_grp, lt[0:SUBLANES, :], -jnp.inf)), -1.0)
    pg_top, g_idx = top1(pg)
    le_sel = lt[SUBLANES:2 * SUBLANES, :]
    for g in range(1, N_GROUPS):
        le_sel = jnp.where(g_idx == g, lt[(g + 1) * SUBLANES:(g + 2) * SUBLANES, :], le_sel)
    pe = softmax(le_sel)
    pe1, i1 = top1(pe)
    pe2, i2 = top1(jnp.where(sub == i1, -1.0, pe))
    den = pe1 + pe2
    gate1 = pg_top * pe1 / den
    gate2 = pg_top * pe2 / den
    e1 = g_idx * EXPERTS_PER_GROUP + i1
    e2 = g_idx * EXPERTS_PER_GROUP + i2

    eid = lax.broadcasted_iota(jnp.int32, (N_EXPERTS, tm), 0).astype(F32)
    hit1 = eid == e1
    hit2 = eid == e2
    onehot = jnp.where(hit1 | hit2, 1.0, 0.0)
    incl = _dot(onehot.astype(BF16), upper_ref[...])
    excl = incl - onehot + carry[:, 0:1]
    r1 = jnp.sum(jnp.where(hit1, excl, 0.0), axis=0, keepdims=True)
    r2 = jnp.sum(jnp.where(hit2, excl, 0.0), axis=0, keepdims=True)
    carry[...] = carry[...] + jnp.broadcast_to(incl[:, tm - 1:tm], carry.shape)
    cnt_ref[...] = carry[...]

    route = jnp.where(sub == 0, e1, jnp.where(sub == 1, e2, jnp.where(sub == 2, r1, jnp.where(sub == 3, r2, 0.0))))
    route_ref[...] = route.astype(jnp.int32)
    gate_ref[...] = jnp.where(sub == 0, gate1, jnp.where(sub == 1, gate2, 0.0))


def _route_side_kernel(*refs, n_side, alpha, tm):
    side_in = refs[8:8 + n_side]
    side_out = refs[12 + n_side:12 + 2 * n_side]
    _route_kernel(*refs[:8], *refs[8 + n_side:12 + n_side], *refs[12 + 2 * n_side:], alpha=alpha, tm=tm)
    _side_cast(side_in, side_out)


def _route(merged, x2d, w_out, ln_g, ln_b, w_r, b_r, upper, alpha, tm, side):
    t, d = x2d.shape
    const = lambda shape: pl.BlockSpec(shape, lambda i: (0,) * len(shape))
    resident = lambda shape: pl.BlockSpec(shape, lambda i: (0,) * len(shape), pipeline_mode=pl.Buffered(1))
    rows = lambda width: pl.BlockSpec((tm, width), lambda i: (i, 0))
    cols = pl.BlockSpec((SUBLANES, tm), lambda i: (0, i))
    side_specs = _side_specs(side, t // tm, lambda i: i)
    return pl.pallas_call(
        functools.partial(_route_side_kernel, n_side=len(side), alpha=alpha, tm=tm),
        grid=(t // tm,),
        in_specs=[rows(d), rows(d), resident(w_out.shape), const((1, d)), const((1, d)),
                  const(w_r.shape), const((1, LANES)), const((tm, tm))] + side_specs,
        out_specs=[pl.BlockSpec((tm * _pitch(d), LANES), lambda i: (i, 0)),
                   cols, cols, const((N_EXPERTS, LANES))] + side_specs,
        out_shape=[jax.ShapeDtypeStruct((t * _pitch(d), LANES), F32),
                   jax.ShapeDtypeStruct((SUBLANES, t), jnp.int32),
                   jax.ShapeDtypeStruct((SUBLANES, t), F32),
                   jax.ShapeDtypeStruct((N_EXPERTS, LANES), F32)] + _side_shapes(side),
        scratch_shapes=[pltpu.VMEM((N_EXPERTS, LANES), F32)],
        compiler_params=_params(1),
        name="route",
    )(merged, x2d, w_out, ln_g, ln_b, w_r, b_r, upper, *side)


def _dispatch_kernel(pad_start_ref, pad_len_ref, nu_ref, dest_ref, x_ref, xs_hbm, sem, *, tb, te, d):
    def copy(src_tok, dst_tok, n=1):
        return pltpu.make_async_copy(_tokens(x_ref, src_tok, d, n), _tokens(xs_hbm, dst_tok, d, n), sem)

    def wait_tokens(n):
        def body(r, c):
            copy(0, 0, n).wait()
            return c
        return body

    @pl.when(pl.program_id(0) == 0)
    def _():
        def per_expert(e, total):
            start = pad_start_ref[e]
            npad = pad_len_ref[e]

            def fill(r, c):
                copy(0, start + r).start()
                return c

            lax.fori_loop(0, npad, fill, 0)
            return total + npad

        total = lax.fori_loop(0, N_EXPERTS, per_expert, 0)
        lax.fori_loop(0, total // SUBLANES, wait_tokens(SUBLANES), 0)
        lax.fori_loop(0, total % SUBLANES, wait_tokens(1), 0)

        def fill_tile(j, c):
            copy(0, j * te, te).start()
            return c

        n_tiles = xs_hbm.shape[0] // (te * _pitch(d))
        lax.fori_loop(nu_ref[0], n_tiles, fill_tile, 0)
        lax.fori_loop(nu_ref[0], n_tiles, wait_tokens(te), 0)

    def scatter(t, c):
        for k in range(TOP_K):
            copy(t, dest_ref[TOP_K * t + k]).start(priority=k % 2)
        return c

    lax.fori_loop(0, tb, scatter, 0, unroll=DMA_UNROLL)
    for k in range(TOP_K):
        copy(0, 0, tb).wait()


def _dispatch(x1t, dest_flat, pad_start, pad_len, n_used, n_slots, tb, te, d):
    pitch = _pitch(d)
    t = x1t.shape[0] // pitch
    grid_spec = pltpu.PrefetchScalarGridSpec(
        num_scalar_prefetch=3,
        grid=(t // tb,),
        in_specs=[pl.BlockSpec((TOP_K * tb,), lambda i, ps, pn, nu: (i,), memory_space=pltpu.SMEM),
                  pl.BlockSpec((tb * pitch, LANES), lambda i, ps, pn, nu: (i, 0))],
        out_specs=pl.BlockSpec(memory_space=pl.ANY),
        scratch_shapes=[pltpu.SemaphoreType.DMA(())],
    )
    return pl.pallas_call(
        functools.partial(_dispatch_kernel, tb=tb, te=te, d=d),
        grid_spec=grid_spec,
        out_shape=jax.ShapeDtypeStruct((n_slots * pitch, LANES), x1t.dtype),
        compiler_params=_params(1),
        name="dispatch",
    )(pad_start, pad_len, n_used, dest_flat, x1t)


def _expert_kernel(be_ref, nu_ref, x_ref, wg_ref, wu_ref, wd_ref, y_ref, *, te, d):
    i = pl.program_id(0)

    @pl.when(i < nu_ref[0])
    def _():
        xb = _load_token_major(x_ref, te, d).astype(BF16)
        gt = _dot(xb, wg_ref[0])
        up = _dot(xb, wu_ref[0])
        hid = gt * jax.nn.sigmoid(gt) * up
        _store_token_major(y_ref, _dot(hid.astype(BF16), wd_ref[0]))

    @pl.when(i >= nu_ref[0])
    def _():
        y_ref[...] = jnp.zeros_like(y_ref)


def _experts(xs, block_e, n_used, w_gate, w_up, w_down, te):
    _, d, de = w_gate.shape
    pitch = _pitch(d)
    n_tiles = xs.shape[0] // (te * pitch)
    grid_spec = pltpu.PrefetchScalarGridSpec(
        num_scalar_prefetch=2,
        grid=(n_tiles,),
        in_specs=[pl.BlockSpec((te * pitch, LANES), lambda i, be, nu: (jnp.minimum(i, nu[0] - 1), 0)),
                  pl.BlockSpec((1, d, de), lambda i, be, nu: (be[i], 0, 0)),
                  pl.BlockSpec((1, d, de), lambda i, be, nu: (be[i], 0, 0)),
                  pl.BlockSpec((1, de, d), lambda i, be, nu: (be[i], 0, 0))],
        out_specs=pl.BlockSpec((te * pitch, LANES), lambda i, be, nu: (i, 0)),
    )
    return pl.pallas_call(
        functools.partial(_expert_kernel, te=te, d=d),
        grid_spec=grid_spec,
        out_shape=jax.ShapeDtypeStruct(xs.shape, F32),
        compiler_params=_params(1),
        name="experts",
    )(block_e, n_used, xs, w_gate, w_up, w_down)


def _final_kernel(dcur_ref, dnxt_ref, x1_ref, gate_ref, p_ref, ys_hbm, g_ref, b_ref, wg_ref, bg_ref, wp_ref,
                  out_ref, ybuf, sem, *, alpha, tm, d):
    i = pl.program_id(0)
    slot = lax.rem(i, 2)

    def issue(dest_ref, s):
        def body(t, c):
            for k in range(TOP_K):
                pltpu.make_async_copy(_tokens(ys_hbm, dest_ref[TOP_K * t + k], d),
                                      _tokens(ybuf.at[s, k], t, d), sem.at[s]).start(priority=k % 2)
            return c
        lax.fori_loop(0, tm, body, 0, unroll=DMA_UNROLL)

    @pl.when(i == 0)
    def _():
        issue(dcur_ref, 0)

    @pl.when(i + 1 < pl.num_programs(0))
    def _():
        issue(dnxt_ref, 1 - slot)

    for k in range(TOP_K):
        pltpu.make_async_copy(_tokens(ys_hbm, 0, d, tm), ybuf.at[slot, k], sem.at[slot]).wait()

    hm = tm // ROW_SPLIT
    for r0 in range(0, tm, hm):
        rows = slice(r0, r0 + hm)
        trows = pl.ds(r0 * _pitch(d), hm * _pitch(d))
        pp = _dot(p_ref[rows, :].astype(BF16), wp_ref[...])
        gate = gate_ref[rows, :]
        y = (gate[:, 0:1] * _load_token_major(ybuf.at[slot, 0, trows], hm, d)
             + gate[:, 1:2] * _load_token_major(ybuf.at[slot, 1, trows], hm, d))
        x2 = _layer_norm(alpha * _load_token_major(x1_ref.at[trows], hm, d) + y, g_ref[...], b_ref[...])
        gl = _dot(x2.astype(BF16), wg_ref[...]) + bg_ref[...]
        out_ref[rows, :] = x2 + jax.nn.sigmoid(gl) * pp


def _final(dest_flat, x1t, gate, p2d, ys, ln_g, ln_b, w_pg, b_pg, w_pp, alpha, tm):
    d = w_pg.shape[0]
    pitch = _pitch(d)
    t = x1t.shape[0] // pitch
    n = t // tm
    const = lambda shape: pl.BlockSpec(shape, lambda i: (0,) * len(shape))
    resident = lambda shape: pl.BlockSpec(shape, lambda i: (0,) * len(shape), pipeline_mode=pl.Buffered(1))
    rows = lambda width: pl.BlockSpec((tm, width), lambda i: (i, 0))
    return pl.pallas_call(
        functools.partial(_final_kernel, alpha=alpha, tm=tm, d=d),
        grid=(n,),
        in_specs=[pl.BlockSpec((TOP_K * tm,), lambda i: (i,), memory_space=pltpu.SMEM),
                  pl.BlockSpec((TOP_K * tm,), lambda i: (jnp.minimum(i + 1, n - 1),), memory_space=pltpu.SMEM),
                  pl.BlockSpec((tm * pitch, LANES), lambda i: (i, 0)),
                  rows(LANES), rows(p2d.shape[1]),
                  pl.BlockSpec(memory_space=pl.ANY),
                  const((1, d)), const((1, d)), resident(w_pg.shape), const((1, d)), resident(w_pp.shape)],
        out_specs=rows(d),
        out_shape=jax.ShapeDtypeStruct((t, d), F32),
        scratch_shapes=[pltpu.VMEM((2, TOP_K, tm * pitch, LANES), F32), pltpu.SemaphoreType.DMA((2,))],
        compiler_params=_params(1),
        name="final",
    )(dest_flat, dest_flat, x1t, gate, p2d, ys, ln_g, ln_b, w_pg, b_pg, w_pp)


def _pad_cols(a, width):
    return jnp.pad(a, ((0, 0), (0, width - a.shape[1])))


def _tri(n):
    return jnp.tril(jnp.ones((n, n), BF16))


def _layer(x, p, w_in_t, b_in, conv_w, conv_b, mh_g, w_pool, pool_scale, w_m_br, w_p_br, w_out,
           ln1_g, ln1_b, w_rg, b_rg, w_re, b_re, w_gate, w_up, w_down, ln2_g, ln2_b,
           w_ple_gate, b_ple_gate, w_ple_proj, alpha):
    nb, seq, d = x.shape
    t = nb * seq
    x2d = x.reshape(t, d)
    row = lambda a: a.reshape(1, -1)

    c_if = 2 * M_QK + 2 * M_V
    w_main, w_if = _regroup(w_in_t, c_if, 2 * M_HEADS, 1024, min(1024, d))
    b_main = row(jnp.concatenate([b_in[:c_if], b_in[c_if + 2 * M_HEADS:]]))
    b_if = _pad_cols(row(b_in[c_if:c_if + 2 * M_HEADS]), LANES)

    n_exp, _, d_exp = w_gate.shape
    tm_in = min(1024, t)
    z_main, z_if = _inproj(x2d, w_main, b_main, w_if, b_if, tm_in, 2048, [])

    dense_w = [w_pool, w_m_br, w_p_br, w_out, w_ple_gate, w_ple_proj, w_down.reshape(n_exp * d_exp, d)]
    hg, *dense_b = _mlstm(z_main.reshape(nb, seq, -1), z_if.reshape(nb, seq, LANES),
                          conv_w, row(conv_b), row(mh_g), [a.reshape(-1, a.shape[-1]) for a in dense_w])
    w_pool, w_m_br, w_p_br, w_out, w_ple_gate, w_ple_proj, wd_b = [
        b.reshape(a.shape) for a, b in zip(dense_w, dense_b)]

    tm = min(512, seq)
    merged, wu_b = _branch(hg.reshape(t, M_V), z_main, w_pool, row(pool_scale), w_m_br, w_p_br, seq, tm,
                           [w_up.reshape(n_exp * d, d_exp)])

    w_r = _pad_cols(jnp.concatenate([_pad_cols(w_rg, SUBLANES), w_re], axis=1), LANES).astype(BF16)
    b_r = _pad_cols(jnp.concatenate([_pad_cols(row(b_rg), SUBLANES), row(b_re)], axis=1), LANES)
    x1, route, gate_t, cnt, wg_b = _route(merged, x2d, w_out, row(ln1_g), row(ln1_b),
                                          w_r, b_r, _tri(tm).T, alpha, tm, [w_gate.reshape(n_exp * d, d_exp)])
    gate = _pad_cols(gate_t[0:TOP_K].T, LANES)

    te = EXPERT_TILE
    counts = cnt[:, 0].astype(jnp.int32)
    pcounts = (counts + te - 1) // te * te
    pends = jnp.cumsum(pcounts)
    pstarts = pends - pcounts
    e_sel = route[0:TOP_K, :, None] == jnp.arange(N_EXPERTS, dtype=jnp.int32)
    dest = (jnp.sum(jnp.where(e_sel, pstarts, 0), axis=-1) + route[TOP_K:2 * TOP_K]).T.reshape(-1)
    n_slots = t * TOP_K + N_EXPERTS * te
    n_tiles = n_slots // te
    n_used = (pends[-1] // te).reshape(1)
    tile_row = jnp.minimum(jnp.arange(n_tiles, dtype=jnp.int32), n_used - 1) * te
    block_e = jnp.minimum(jnp.sum((tile_row[:, None] >= pends[None, :]).astype(jnp.int32), axis=1), N_EXPERTS - 1)

    xs = _dispatch(x1, dest, pstarts + counts, pcounts - counts, n_used, n_slots, min(512, t), te, d)
    ys = _experts(xs, block_e, n_used, wg_b.reshape(n_exp, d, d_exp), wu_b.reshape(n_exp, d, d_exp),
                  wd_b.reshape(n_exp, d_exp, d), te)
    return _final(dest, x1, gate, p.reshape(t, -1), ys, row(ln2_g), row(ln2_b),
                  w_ple_gate, row(b_ple_gate), w_ple_proj, alpha,
                  min(512, t)).reshape(nb, seq, d)


def kernel(x, p, w_in, b_in, conv_w, conv_b, mh_g, w_pool, pool_scale, w_m_br, w_p_br, w_out, ln1_g, ln1_b, w_rg, b_rg, w_re, b_re, w_gate, w_up, w_down, ln2_g, ln2_b, w_ple_gate, b_ple_gate, w_ple_proj):
    depth = w_in.shape[0]
    alpha = (2 * depth) ** 0.25
    for i in range(depth):
        x = _layer(x, p[i], w_in[i].T, b_in[i], conv_w[i], conv_b[i], mh_g[i], w_pool[i], pool_scale[i],
                   w_m_br[i], w_p_br[i], w_out[i], ln1_g[i], ln1_b[i], w_rg[i], b_rg[i], w_re[i], b_re[i],
                   w_gate[i], w_up[i], w_down[i], ln2_g[i], ln2_b[i], w_ple_gate[i], b_ple_gate[i],
                   w_ple_proj[i], alpha)
    return x
```

```python
import functools

import jax
import jax.numpy as jnp
from jax import lax
from jax.experimental import pallas as pl
from jax.experimental.pallas import tpu as pltpu

F32 = jnp.float32
BF16 = jnp.bfloat16

M_HEADS = 4
M_QK_DIM = 128
M_V_DIM = 256
M_QK = M_HEADS * M_QK_DIM
M_V = M_HEADS * M_V_DIM
CONV_W = 4
CHUNK = 128
POOL_WINDOWS = (2, 4, 8, 16)
POOL_GROUP_DIM = 256
POOL_W = len(POOL_WINDOWS) * POOL_GROUP_DIM
N_GROUPS = 4
EXPERTS_PER_GROUP = 8
N_EXPERTS = N_GROUPS * EXPERTS_PER_GROUP
TOP_K = 2
LN_EPS = 1e-5

LANES = 128
SUBLANES = 8
VMEM_LIMIT = 56 * 1024 * 1024
BF16_SUBLANES = 16
CONV_HALO = BF16_SUBLANES
POOL_HALO = 32
EXPERT_TILE = 512
DMA_UNROLL = 8
ROW_SPLIT = 2


def _dot(a, b):
    return jnp.dot(a, b, preferred_element_type=F32)


def _params(n_grid):
    return pltpu.CompilerParams(dimension_semantics=("arbitrary",) * n_grid,
                                vmem_limit_bytes=VMEM_LIMIT)


def _log_sigmoid(x):
    return -(jnp.maximum(-x, 0.0) + jnp.log1p(jnp.exp(-jnp.abs(x))))


def _layer_norm(r, g, b):
    mu = jnp.mean(r, axis=-1, keepdims=True)
    d = r - mu
    var = jnp.mean(d * d, axis=-1, keepdims=True)
    return d * lax.rsqrt(var + LN_EPS) * g + b


def _pitch(d):
    return d // LANES + 1


def _store_token_major(ref, val):
    n, d = val.shape
    for c in range(d // LANES):
        ref[pl.ds(c, n, stride=_pitch(d)), :] = val[:, c * LANES:(c + 1) * LANES]
    ref[pl.ds(d // LANES, n, stride=_pitch(d)), :] = jnp.zeros((n, LANES), val.dtype)


def _load_token_major(ref, n, d):
    return jnp.concatenate([ref[pl.ds(c, n, stride=_pitch(d)), :] for c in range(d // LANES)], axis=1)


def _tokens(ref, tok, d, n=1):
    return ref.at[pl.ds(tok * _pitch(d), n * _pitch(d))]


def _side_specs(side, steps, step_of):
    def spec(a):
        rb = max(BF16_SUBLANES, a.shape[0] // steps)
        return pl.BlockSpec((rb, a.shape[1]), lambda *g: (jnp.minimum(step_of(*g), a.shape[0] // rb - 1), 0))
    return [spec(a) for a in side]


def _side_shapes(side):
    return [jax.ShapeDtypeStruct(a.shape, BF16) for a in side]


def _side_cast(side_in, side_out):
    for src, dst in zip(side_in, side_out):
        dst[...] = src[...].astype(BF16)


def _regroup_kernel(a_ref, c_ref, out_ref, cut_ref):
    out_ref[...] = a_ref[...].T.astype(BF16)
    skip, tk = c_ref.shape
    cut_ref[...] = jnp.concatenate([c_ref[...], jnp.zeros((LANES - skip, tk), F32)], axis=0).T


def _regroup(w_t, cut, skip, tn, tk):
    n, k = w_t.shape
    n_keep = cut // tn
    assert skip % SUBLANES == 0
    first_row = lambda j: pl.multiple_of(jnp.where(j < n_keep, j * tn, j * tn + skip), SUBLANES)
    return pl.pallas_call(
        _regroup_kernel,
        grid=(k // tk, (n - skip) // tn),
        in_specs=[pl.BlockSpec((pl.Element(tn), pl.Element(tk)), lambda kk, j: (first_row(j), kk * tk)),
                  pl.BlockSpec((pl.Element(skip), pl.Element(tk)), lambda kk, j: (cut, kk * tk))],
        out_specs=[pl.BlockSpec((tk, tn), lambda kk, j: (kk, j)),
                   pl.BlockSpec((tk, LANES), lambda kk, j: (kk, 0))],
        out_shape=[jax.ShapeDtypeStruct((k, n - skip), BF16), jax.ShapeDtypeStruct((k, LANES), F32)],
        compiler_params=_params(2),
        name="regroup",
    )(w_t, w_t)


def _inproj_kernel(*refs, n_side):
    x_ref, w_ref, b_ref, wif_ref, bif_ref = refs[:5]
    side_in = refs[5:5 + n_side]
    z_ref, zif_ref = refs[5 + n_side:7 + n_side]
    side_out = refs[7 + n_side:7 + 2 * n_side]
    xb_ref = refs[7 + 2 * n_side]

    @pl.when(pl.program_id(1) == 0)
    def _():
        xb = x_ref[...].astype(BF16)
        xb_ref[...] = xb
        zif_ref[...] = _dot(xb, wif_ref[...].astype(BF16)) + bif_ref[...]

    z_ref[...] = (_dot(xb_ref[...], w_ref[...]) + b_ref[...]).astype(BF16)
    _side_cast(side_in, side_out)


def _inproj(x2d, w_main, b_main, w_if, b_if, tm, tn, side):
    t, d = x2d.shape
    n = w_main.shape[1]
    nj = n // tn
    side_specs = _side_specs(side, (t // tm) * nj, lambda i, j: i * nj + j)
    return pl.pallas_call(
        functools.partial(_inproj_kernel, n_side=len(side)),
        grid=(t // tm, nj),
        in_specs=[pl.BlockSpec((tm, d), lambda i, j: (i, 0)),
                  pl.BlockSpec((d, tn), lambda i, j: (0, j)),
                  pl.BlockSpec((1, tn), lambda i, j: (0, j)),
                  pl.BlockSpec((d, LANES), lambda i, j: (0, 0)),
                  pl.BlockSpec((1, LANES), lambda i, j: (0, 0))] + side_specs,
        out_specs=[pl.BlockSpec((tm, tn), lambda i, j: (i, j)),
                   pl.BlockSpec((tm, LANES), lambda i, j: (i, 0))] + side_specs,
        out_shape=[jax.ShapeDtypeStruct((t, n), BF16),
                   jax.ShapeDtypeStruct((t, LANES), F32)] + _side_shapes(side),
        scratch_shapes=[pltpu.VMEM((tm, d), BF16)],
        compiler_params=_params(2),
        name="inproj",
    )(x2d, w_main, b_main, w_if, b_if, *side)


def _mlstm_kernel(*refs, nb, n_side):
    qk_ref, v_ref, o_ref, g_ref, cw_ref, cb_ref, mhg_ref = refs[:7]
    side_in = refs[7:7 + n_side]
    out_ref = refs[7 + n_side]
    side_out = refs[8 + n_side:8 + 2 * n_side]
    cbuf = refs[8 + 2 * n_side]
    state = refs[9 + 2 * n_side:]
    _mlstm_body(qk_ref, v_ref, o_ref, g_ref, cw_ref, cb_ref, mhg_ref, out_ref, cbuf, state, nb)
    _side_cast(side_in, side_out)


def _mlstm_body(qk_ref, v_ref, o_ref, g_ref, cw_ref, cb_ref, mhg_ref, out_ref, cbuf, state, nb):
    L = CHUNK
    halo = CONV_HALO
    ct_refs, m_ref = state[:M_HEADS], state[M_HEADS]

    @pl.when(pl.program_id(0) == 0)
    def _():
        cbuf[:, 0:halo, :] = jnp.zeros((nb, halo, 2 * M_QK), BF16)
        for ref in state:
            ref[...] = jnp.zeros_like(ref)

    cbuf[:, halo:halo + L, :] = qk_ref[...]
    scale = M_QK_DIM ** -0.5
    row = lax.broadcasted_iota(jnp.int32, (L, L), 0)
    col = lax.broadcasted_iota(jnp.int32, (L, L), 1)
    causal = col <= row
    upper = jnp.where(row <= col, 1.0, 0.0).astype(BF16)
    ones_blk = jnp.ones((L, LANES), BF16)
    lane8 = lax.broadcasted_iota(jnp.int32, (SUBLANES, L), 1)
    wrow = lax.broadcasted_iota(jnp.int32, (L, L + halo), 0)
    wcol = lax.broadcasted_iota(jnp.int32, (L, L + halo), 1)
    shifts = [jnp.where(wcol == wrow + (halo - (CONV_W - 1) + tap), 1.0, 0.0).astype(BF16)
              for tap in range(CONV_W - 1)]
    sub = lax.broadcasted_iota(jnp.int32, (SUBLANES, LANES), 0)

    def per_batch(b, carry):
        window = cbuf[b]
        acc = cb_ref[...] + _dot(shifts[0], window) * cw_ref[0:1, :]
        for tap in range(1, CONV_W - 1):
            acc = acc + _dot(shifts[tap], window) * cw_ref[tap:tap + 1, :]
        acc = acc + qk_ref[b].astype(F32) * cw_ref[CONV_W - 1:CONV_W, :]
        qk = acc * jax.nn.sigmoid(acc)
        cbuf[b, 0:halo, :] = cbuf[b, L:L + halo, :]

        g_row = g_ref[b].T[0:SUBLANES, :]
        lf = _log_sigmoid(g_row)
        lf_hi = lf.astype(BF16)
        lf_lo = (lf - lf_hi.astype(F32)).astype(BF16)
        cs = _dot(jnp.concatenate([lf_hi, lf_lo], axis=0), upper)
        b_row = pltpu.roll(cs[0:SUBLANES, :] + cs[SUBLANES:2 * SUBLANES, :], M_HEADS, 0)
        a_row = g_row - b_row
        cm = a_row
        for sh in (1, 2, 4, 8, 16, 32, 64):
            cm = jnp.maximum(cm, jnp.where(lane8 >= sh, pltpu.roll(cm, sh, 1), -jnp.inf))
        m_prev = m_ref[b]
        mx = jnp.maximum(m_prev, cm)
        mx_last = jnp.maximum(m_prev, jnp.max(a_row, axis=1, keepdims=True))
        carry_in = jnp.exp(m_prev - mx)
        floor = jnp.exp(-(b_row + mx))
        wts = jnp.exp(a_row - mx_last)
        decay = jnp.exp(m_prev - mx_last)
        m_ref[b] = jnp.where(sub < M_HEADS, b_row[:, L - 1:L] + mx_last, 0.0)
        packed = jnp.concatenate(
            [jnp.where(sub < M_HEADS, mx, pltpu.roll(carry_in, M_HEADS, 0)),
             jnp.where(sub < M_HEADS, floor, pltpu.roll(wts, M_HEADS, 0)),
             jnp.zeros((L - 2 * SUBLANES, L), F32)], axis=0).T

        heads = range(M_HEADS)
        vcols = [slice(h * M_V_DIM, (h + 1) * M_V_DIM) for h in heads]
        q = [qk[:, h * M_QK_DIM:(h + 1) * M_QK_DIM].astype(BF16) for h in heads]
        kf = [qk[:, M_QK + h * M_QK_DIM:M_QK + (h + 1) * M_QK_DIM] * scale for h in heads]
        v_ext = [jnp.concatenate([v_ref[b, :, vcols[h]], ones_blk], axis=1) for h in heads]
        ct = [ct_refs[h][b] for h in heads]
        col_of = lambda i, h: packed[:, i * M_HEADS + h:i * M_HEADS + h + 1]

        raw = [lax.dot_general(q[h], kf[h].astype(BF16), (((1,), (1,)), ((), ())),
                               preferred_element_type=F32) for h in heads]
        prev = [_dot(q[h], ct[h].astype(BF16)) for h in heads]
        pmat = [jnp.exp(jnp.where(causal, a_row[h:h + 1, :] - col_of(0, h), -jnp.inf)) for h in heads]
        s = [(raw[h] * pmat[h]).astype(BF16) for h in heads]
        numden = [_dot(s[h], v_ext[h]) + col_of(1, h) * prev[h] for h in heads]
        upd = [lax.dot_general((kf[h] * col_of(3, h)).astype(BF16), v_ext[h], (((0,), (0,)), ((), ())),
                               preferred_element_type=F32) for h in heads]
        for h in heads:
            dec = jnp.concatenate([decay[h:h + 1, :]] * (ct[h].shape[1] // LANES), axis=1)
            ct_refs[h][b] = dec * ct[h] + upd[h]
        inv = [1.0 / jnp.maximum(jnp.abs(numden[h][:, M_V_DIM:]), col_of(2, h)) for h in heads]
        hh = [numden[h][:, 0:M_V_DIM] * jnp.concatenate([inv[h]] * (M_V_DIM // LANES), axis=1) for h in heads]
        mu = [jnp.mean(hh[h], axis=-1, keepdims=True) for h in heads]
        dlt = [hh[h] - mu[h] for h in heads]
        var = [jnp.mean(dlt[h] * dlt[h], axis=-1, keepdims=True) for h in heads]
        for h in heads:
            hn = dlt[h] * lax.rsqrt(var[h] + LN_EPS) * mhg_ref[:, vcols[h]]
            og = jax.nn.sigmoid(o_ref[b, :, vcols[h]].astype(F32))
            out_ref[b, :, vcols[h]] = (hn * og).astype(BF16)
        return carry

    for b in range(nb):
        per_batch(b, 0)


def _mlstm(z3, zif3, conv_w, conv_b, mh_g, side):
    nb, s, _ = z3.shape
    L = CHUNK
    blk = lambda colblk: pl.BlockSpec((nb, L, M_V), lambda c: (0, c, colblk))
    const = lambda shape: pl.BlockSpec(shape, lambda c: (0,) * len(shape))
    side_specs = _side_specs(side, s // L, lambda c: c)
    return pl.pallas_call(
        functools.partial(_mlstm_kernel, nb=nb, n_side=len(side)),
        grid=(s // L,),
        in_specs=[blk(0), blk(1), blk(2),
                  pl.BlockSpec((nb, L, LANES), lambda c: (0, c, 0)),
                  const((CONV_W, 2 * M_QK)), const((1, 2 * M_QK)), const((1, M_V))] + side_specs,
        out_specs=[pl.BlockSpec((nb, L, M_V), lambda c: (0, c, 0))] + side_specs,
        out_shape=[jax.ShapeDtypeStruct((nb, s, M_V), BF16)] + _side_shapes(side),
        scratch_shapes=([pltpu.VMEM((nb, L + CONV_HALO, 2 * M_QK), BF16)]
                        + [pltpu.VMEM((nb, M_QK_DIM, M_V_DIM + LANES), F32)] * M_HEADS
                        + [pltpu.VMEM((nb, SUBLANES, L), F32)]),
        compiler_params=_params(1),
        name="mlstm",
    )(z3, z3, z3, zif3, conv_w, conv_b, mh_g, *side)


def _branch_kernel(hg_ref, u_ref, uh_ref, gm_ref, gp_ref, wpool_ref, ps_ref, wm_ref, wp_ref,
                   out_ref, xa, xb, yp_ref, *, tm, seq):
    H = POOL_HALO
    G = POOL_GROUP_DIM
    t0 = lax.rem(pl.program_id(0) * tm, seq)
    u = u_ref[...].astype(F32)
    xa[H:H + tm, :] = u
    xa[0:H, :] = jnp.where(t0 == 0, 0.0, uh_ref[...].astype(F32))
    n = tm + H - 8
    xb[8:8 + n, :] = xa[8:8 + n, :] + xa[7:7 + n, :]
    n = tm + H - 16
    xa[16:16 + n, G:] = xb[16:16 + n, G:] + xb[14:14 + n, G:]
    n = tm + H - 24
    xb[24:24 + n, 2 * G:] = xa[24:24 + n, 2 * G:] + xa[20:20 + n, 2 * G:]
    xa[H:H + tm, 3 * G:] = xb[H:H + tm, 3 * G:] + xb[H - 8:H - 8 + tm, 3 * G:]
    tpos = t0 + lax.broadcasted_iota(jnp.int32, (tm, 1), 0)
    for g, win in enumerate(POOL_WINDOWS):
        src = (xb, xa, xb, xa)[g]
        cols = slice(g * G, (g + 1) * G)
        cnt = jnp.minimum(tpos + 1, win).astype(F32)
        y = src[H:H + tm, cols] / cnt - u[:, cols]
        yp = _dot(y.astype(BF16), wpool_ref[g]) * ps_ref[:, cols]
        yp_ref[:, cols] = yp.astype(BF16)
    hm = tm // ROW_SPLIT
    for r0 in range(0, tm, hm):
        rows = slice(r0, r0 + hm)
        pb = _dot(yp_ref[rows, :], wp_ref[...])
        a = _dot(hg_ref[rows, :], wm_ref[...])
        merged = (jax.nn.sigmoid(gm_ref[rows, :].astype(F32)) * a
                  + jax.nn.sigmoid(gp_ref[rows, :].astype(F32)) * pb)
        out_ref[rows, :] = merged.astype(BF16)


def _branch_side_kernel(*refs, n_side, tm, seq):
    side_in = refs[9:9 + n_side]
    side_out = refs[10 + n_side:10 + 2 * n_side]
    _branch_kernel(*refs[:9], refs[9 + n_side], *refs[10 + 2 * n_side:], tm=tm, seq=seq)
    _side_cast(side_in, side_out)


def _branch(hg2d, z_main, w_pool, pool_scale, w_m_br, w_p_br, seq, tm, side):
    t = hg2d.shape[0]
    d = w_m_br.shape[1]
    hb = tm // POOL_HALO
    const = lambda shape: pl.BlockSpec(shape, lambda i: (0,) * len(shape))
    side_specs = _side_specs(side, t // tm, lambda i: i)
    return pl.pallas_call(
        functools.partial(_branch_side_kernel, n_side=len(side), tm=tm, seq=seq),
        grid=(t // tm,),
        in_specs=[pl.BlockSpec((tm, M_V), lambda i: (i, 0)),
                  pl.BlockSpec((tm, POOL_W), lambda i: (i, 3)),
                  pl.BlockSpec((POOL_HALO, POOL_W), lambda i: (jnp.maximum(i * hb - 1, 0), 3)),
                  pl.BlockSpec((tm, d), lambda i: (i, 2)),
                  pl.BlockSpec((tm, d), lambda i: (i, 3)),
                  const(w_pool.shape), const((1, POOL_W)), const(w_m_br.shape), const(w_p_br.shape)] + side_specs,
        out_specs=[pl.BlockSpec((tm, d), lambda i: (i, 0))] + side_specs,
        out_shape=[jax.ShapeDtypeStruct((t, d), BF16)] + _side_shapes(side),
        scratch_shapes=[pltpu.VMEM((tm + POOL_HALO, POOL_W), F32),
                        pltpu.VMEM((tm + POOL_HALO, POOL_W), F32),
                        pltpu.VMEM((tm, POOL_W), BF16)],
        compiler_params=_params(1),
        name="branch",
    )(hg2d, z_main, z_main, z_main, z_main, w_pool, pool_scale, w_m_br, w_p_br, *side)


def _route_kernel(mg_ref, x_ref, wout_ref, g_ref, b_ref, wr_ref, br_ref, upper_ref,
                  x1_ref, route_ref, gate_ref, cnt_ref, carry, *, alpha, tm):
    @pl.when(pl.program_id(0) == 0)
    def _():
        carry[...] = jnp.zeros_like(carry)

    hm = tm // ROW_SPLIT
    pitch = _pitch(x_ref.shape[1])
    logits = []
    for r0 in range(0, tm, hm):
        r = alpha * x_ref[r0:r0 + hm, :] + _dot(mg_ref[r0:r0 + hm, :], wout_ref[...])
        x1 = _layer_norm(r, g_ref[...], b_ref[...])
        _store_token_major(x1_ref.at[r0 * pitch:(r0 + hm) * pitch], x1)
        logits.append(_dot(x1.astype(BF16), wr_ref[...]) + br_ref[...])
    lt = jnp.concatenate(logits, axis=0).T

    sub = lax.broadcasted_iota(jnp.int32, (SUBLANES, tm), 0).astype(F32)
    none = float(SUBLANES)

    def softmax(z):
        e = jnp.exp(z - jnp.max(z, axis=0, keepdims=True))
        return e / jnp.sum(e, axis=0, keepdims=True)

    def top1(vals):
        top = jnp.max(vals, axis=0, keepdims=True)
        return top, jnp.min(jnp.where(vals == top, sub, none), axis=0, keepdims=True)

    is_grp = sub < N_GROUPS
    pg = jnp.where(is_grp, softmax(jnp.where(is_grp, lt[0:SUBLANES, :], -jnp.inf)), -1.0)
    pg_top, g_idx = top1(pg)
    le_sel = lt[SUBLANES:2 * SUBLANES, :]
    for g in range(1, N_GROUPS):
        le_sel = jnp.where(g_idx == g, lt[(g + 1) * SUBLANES:(g + 2) * SUBLANES, :], le_sel)
    pe = softmax(le_sel)
    pe1, i1 = top1(pe)
    pe2, i2 = top1(jnp.where(sub == i1, -1.0, pe))
    den = pe1 + pe2
    gate1 = pg_top * pe1 / den
    gate2 = pg_top * pe2 / den
    e1 = g_idx * EXPERTS_PER_GROUP + i1
    e2 = g_idx * EXPERTS_PER_GROUP + i2

    eid = lax.broadcasted_iota(jnp.int32, (N_EXPERTS, tm), 0).astype(F32)
    hit1 = eid == e1
    hit2 = eid == e2
    onehot = jnp.where(hit1 | hit2, 1.0, 0.0)
    incl = _dot(onehot.astype(BF16), upper_ref[...])
    excl = incl - onehot + carry[:, 0:1]
    r1 = jnp.sum(jnp.where(hit1, excl, 0.0), axis=0, keepdims=True)
    r2 = jnp.sum(jnp.where(hit2, excl, 0.0), axis=0, keepdims=True)
    carry[...] = carry[...] + jnp.broadcast_to(incl[:, tm - 1:tm], carry.shape)
    cnt_ref[...] = carry[...]

    route = jnp.where(sub == 0, e1, jnp.where(sub == 1, e2, jnp.where(sub == 2, r1, jnp.where(sub == 3, r2, 0.0))))
    route_ref[...] = route.astype(jnp.int32)
    gate_ref[...] = jnp.where(sub == 0, gate1, jnp.where(sub == 1, gate2, 0.0))


def _route_side_kernel(*refs, n_side, alpha, tm):
    side_in = refs[8:8 + n_side]
    side_out = refs[12 + n_side:12 + 2 * n_side]
    _route_kernel(*refs[:8], *refs[8 + n_side:12 + n_side], *refs[12 + 2 * n_side:], alpha=alpha, tm=tm)
    _side_cast(side_in, side_out)


def _route(merged, x2d, w_out, ln_g, ln_b, w_r, b_r, upper, alpha, tm, side):
    t, d = x2d.shape
    const = lambda shape: pl.BlockSpec(shape, lambda i: (0,) * len(shape))
    resident = lambda shape: pl.BlockSpec(shape, lambda i: (0,) * len(shape), pipeline_mode=pl.Buffered(1))
    rows = lambda width: pl.BlockSpec((tm, width), lambda i: (i, 0))
    cols = pl.BlockSpec((SUBLANES, tm), lambda i: (0, i))
    side_specs = _side_specs(side, t // tm, lambda i: i)
    return pl.pallas_call(
        functools.partial(_route_side_kernel, n_side=len(side), alpha=alpha, tm=tm),
        grid=(t // tm,),
        in_specs=[rows(d), rows(d), resident(w_out.shape), const((1, d)), const((1, d)),
                  const(w_r.shape), const((1, LANES)), const((tm, tm))] + side_specs,
        out_specs=[pl.BlockSpec((tm * _pitch(d), LANES), lambda i: (i, 0)),
                   cols, cols, const((N_EXPERTS, LANES))] + side_specs,
        out_shape=[jax.ShapeDtypeStruct((t * _pitch(d), LANES), F32),
                   jax.ShapeDtypeStruct((SUBLANES, t), jnp.int32),
                   jax.ShapeDtypeStruct((SUBLANES, t), F32),
                   jax.ShapeDtypeStruct((N_EXPERTS, LANES), F32)] + _side_shapes(side),
        scratch_shapes=[pltpu.VMEM((N_EXPERTS, LANES), F32)],
        compiler_params=_params(1),
        name="route",
    )(merged, x2d, w_out, ln_g, ln_b, w_r, b_r, upper, *side)


def _dispatch_kernel(pad_start_ref, pad_len_ref, nu_ref, dest_ref, x_ref, xs_hbm, sem, *, tb, te, d):
    def copy(src_tok, dst_tok, n=1):
        return pltpu.make_async_copy(_tokens(x_ref, src_tok, d, n), _tokens(xs_hbm, dst_tok, d, n), sem)

    def wait_tokens(n):
        def body(r, c):
            copy(0, 0, n).wait()
            return c
        return body

    @pl.when(pl.program_id(0) == 0)
    def _():
        def per_expert(e, total):
            start = pad_start_ref[e]
            npad = pad_len_ref[e]

            def fill(r, c):
                copy(0, start + r).start()
                return c

            lax.fori_loop(0, npad, fill, 0)
            return total + npad

        total = lax.fori_loop(0, N_EXPERTS, per_expert, 0)
        lax.fori_loop(0, total // SUBLANES, wait_tokens(SUBLANES), 0)
        lax.fori_loop(0, total % SUBLANES, wait_tokens(1), 0)

        def fill_tile(j, c):
            copy(0, j * te, te).start()
            return c

        n_tiles = xs_hbm.shape[0] // (te * _pitch(d))
        lax.fori_loop(nu_ref[0], n_tiles, fill_tile, 0)
        lax.fori_loop(nu_ref[0], n_tiles, wait_tokens(te), 0)

    def scatter(t, c):
        for k in range(TOP_K):
            copy(t, dest_ref[TOP_K * t + k]).start(priority=k % 2)
        return c

    lax.fori_loop(0, tb, scatter, 0, unroll=DMA_UNROLL)
    for k in range(TOP_K):
        copy(0, 0, tb).wait()


def _dispatch(x1t, dest_flat, pad_start, pad_len, n_used, n_slots, tb, te, d):
    pitch = _pitch(d)
    t = x1t.shape[0] // pitch
    grid_spec = pltpu.PrefetchScalarGridSpec(
        num_scalar_prefetch=3,
        grid=(t // tb,),
        in_specs=[pl.BlockSpec((TOP_K * tb,), lambda i, ps, pn, nu: (i,), memory_space=pltpu.SMEM),
                  pl.BlockSpec((tb * pitch, LANES), lambda i, ps, pn, nu: (i, 0))],
        out_specs=pl.BlockSpec(memory_space=pl.ANY),
        scratch_shapes=[pltpu.SemaphoreType.DMA(())],
    )
    return pl.pallas_call(
        functools.partial(_dispatch_kernel, tb=tb, te=te, d=d),
        grid_spec=grid_spec,
        out_shape=jax.ShapeDtypeStruct((n_slots * pitch, LANES), x1t.dtype),
        compiler_params=_params(1),
        name="dispatch",
    )(pad_start, pad_len, n_used, dest_flat, x1t)


def _expert_kernel(be_ref, nu_ref, x_ref, wg_ref, wu_ref, wd_ref, y_ref, *, te, d):
    i = pl.program_id(0)

    @pl.when(i < nu_ref[0])
    def _():
        hm = te // ROW_SPLIT
        for r0 in range(0, te, hm):
            trows = pl.ds(r0 * _pitch(d), hm * _pitch(d))
            xb = _load_token_major(x_ref.at[trows], hm, d).astype(BF16)
            gt = _dot(xb, wg_ref[0])
            up = _dot(xb, wu_ref[0])
            hid = gt * jax.nn.sigmoid(gt) * up
            _store_token_major(y_ref.at[trows], _dot(hid.astype(BF16), wd_ref[0]))

    @pl.when(i >= nu_ref[0])
    def _():
        y_ref[...] = jnp.zeros_like(y_ref)


def _experts(xs, block_e, n_used, w_gate, w_up, w_down, te):
    _, d, de = w_gate.shape
    pitch = _pitch(d)
    n_tiles = xs.shape[0] // (te * pitch)
    grid_spec = pltpu.PrefetchScalarGridSpec(
        num_scalar_prefetch=2,
        grid=(n_tiles,),
        in_specs=[pl.BlockSpec((te * pitch, LANES), lambda i, be, nu: (jnp.minimum(i, nu[0] - 1), 0)),
                  pl.BlockSpec((1, d, de), lambda i, be, nu: (be[i], 0, 0)),
                  pl.BlockSpec((1, d, de), lambda i, be, nu: (be[i], 0, 0)),
                  pl.BlockSpec((1, de, d), lambda i, be, nu: (be[i], 0, 0))],
        out_specs=pl.BlockSpec((te * pitch, LANES), lambda i, be, nu: (i, 0)),
    )
    return pl.pallas_call(
        functools.partial(_expert_kernel, te=te, d=d),
        grid_spec=grid_spec,
        out_shape=jax.ShapeDtypeStruct(xs.shape, F32),
        compiler_params=_params(1),
        name="experts",
    )(block_e, n_used, xs, w_gate, w_up, w_down)


def _final_kernel(dcur_ref, dnxt_ref, x1_ref, gate_ref, p_ref, ys_hbm, g_ref, b_ref, wg_ref, bg_ref, wp_ref,
                  out_ref, ybuf, sem, *, alpha, tm, d):
    i = pl.program_id(0)
    slot = lax.rem(i, 2)

    def issue(dest_ref, s):
        def body(t, c):
            for k in range(TOP_K):
                pltpu.make_async_copy(_tokens(ys_hbm, dest_ref[TOP_K * t + k], d),
                                      _tokens(ybuf.at[s, k], t, d), sem.at[s]).start(priority=k % 2)
            return c
        lax.fori_loop(0, tm, body, 0, unroll=DMA_UNROLL)

    @pl.when(i == 0)
    def _():
        issue(dcur_ref, 0)

    @pl.when(i + 1 < pl.num_programs(0))
    def _():
        issue(dnxt_ref, 1 - slot)

    for k in range(TOP_K):
        pltpu.make_async_copy(_tokens(ys_hbm, 0, d, tm), ybuf.at[slot, k], sem.at[slot]).wait()

    hm = tm // ROW_SPLIT
    for r0 in range(0, tm, hm):
        rows = slice(r0, r0 + hm)
        trows = pl.ds(r0 * _pitch(d), hm * _pitch(d))
        pp = _dot(p_ref[rows, :].astype(BF16), wp_ref[...])
        gate = gate_ref[rows, :]
        y = (gate[:, 0:1] * _load_token_major(ybuf.at[slot, 0, trows], hm, d)
             + gate[:, 1:2] * _load_token_major(ybuf.at[slot, 1, trows], hm, d))
        x2 = _layer_norm(alpha * _load_token_major(x1_ref.at[trows], hm, d) + y, g_ref[...], b_ref[...])
        gl = _dot(x2.astype(BF16), wg_ref[...]) + bg_ref[...]
        out_ref[rows, :] = x2 + jax.nn.sigmoid(gl) * pp


def _final(dest_flat, x1t, gate, p2d, ys, ln_g, ln_b, w_pg, b_pg, w_pp, alpha, tm):
    d = w_pg.shape[0]
    pitch = _pitch(d)
    t = x1t.shape[0] // pitch
    n = t // tm
    const = lambda shape: pl.BlockSpec(shape, lambda i: (0,) * len(shape))
    resident = lambda shape: pl.BlockSpec(shape, lambda i: (0,) * len(shape), pipeline_mode=pl.Buffered(1))
    rows = lambda width: pl.BlockSpec((tm, width), lambda i: (i, 0))
    return pl.pallas_call(
        functools.partial(_final_kernel, alpha=alpha, tm=tm, d=d),
        grid=(n,),
        in_specs=[pl.BlockSpec((TOP_K * tm,), lambda i: (i,), memory_space=pltpu.SMEM),
                  pl.BlockSpec((TOP_K * tm,), lambda i: (jnp.minimum(i + 1, n - 1),), memory_space=pltpu.SMEM),
                  pl.BlockSpec((tm * pitch, LANES), lambda i: (i, 0)),
                  rows(LANES), rows(p2d.shape[1]),
                  pl.BlockSpec(memory_space=pl.ANY),
                  const((1, d)), const((1, d)), resident(w_pg.shape), const((1, d)), resident(w_pp.shape)],
        out_specs=rows(d),
        out_shape=jax.ShapeDtypeStruct((t, d), F32),
        scratch_shapes=[pltpu.VMEM((2, TOP_K, tm * pitch, LANES), F32), pltpu.SemaphoreType.DMA((2,))],
        compiler_params=_params(1),
        name="final",
    )(dest_flat, dest_flat, x1t, gate, p2d, ys, ln_g, ln_b, w_pg, b_pg, w_pp)


def _pad_cols(a, width):
    return jnp.pad(a, ((0, 0), (0, width - a.shape[1])))


def _tri(n):
    return jnp.tril(jnp.ones((n, n), BF16))


def _layer(x, p, w_in_t, b_in, conv_w, conv_b, mh_g, w_pool, pool_scale, w_m_br, w_p_br, w_out,
           ln1_g, ln1_b, w_rg, b_rg, w_re, b_re, w_gate, w_up, w_down, ln2_g, ln2_b,
           w_ple_gate, b_ple_gate, w_ple_proj, alpha):
    nb, seq, d = x.shape
    t = nb * seq
    x2d = x.reshape(t, d)
    row = lambda a: a.reshape(1, -1)

    c_if = 2 * M_QK + 2 * M_V
    w_main, w_if = _regroup(w_in_t, c_if, 2 * M_HEADS, 1024, min(1024, d))
    b_main = row(jnp.concatenate([b_in[:c_if], b_in[c_if + 2 * M_HEADS:]]))
    b_if = _pad_cols(row(b_in[c_if:c_if + 2 * M_HEADS]), LANES)

    n_exp, _, d_exp = w_gate.shape
    tm_in = min(1024, t)
    z_main, z_if = _inproj(x2d, w_main, b_main, w_if, b_if, tm_in, 2048, [])

    dense_w = [w_pool, w_m_br, w_p_br, w_out, w_ple_gate, w_ple_proj, w_down.reshape(n_exp * d_exp, d)]
    hg, *dense_b = _mlstm(z_main.reshape(nb, seq, -1), z_if.reshape(nb, seq, LANES),
                          conv_w, row(conv_b), row(mh_g), [a.reshape(-1, a.shape[-1]) for a in dense_w])
    w_pool, w_m_br, w_p_br, w_out, w_ple_gate, w_ple_proj, wd_b = [
        b.reshape(a.shape) for a, b in zip(dense_w, dense_b)]

    tm = min(512, seq)
    merged, wu_b = _branch(hg.reshape(t, M_V), z_main, w_pool, row(pool_scale), w_m_br, w_p_br, seq, tm,
                           [w_up.reshape(n_exp * d, d_exp)])

    w_r = _pad_cols(jnp.concatenate([_pad_cols(w_rg, SUBLANES), w_re], axis=1), LANES).astype(BF16)
    b_r = _pad_cols(jnp.concatenate([_pad_cols(row(b_rg), SUBLANES), row(b_re)], axis=1), LANES)
    x1, route, gate_t, cnt, wg_b = _route(merged, x2d, w_out, row(ln1_g), row(ln1_b),
                                          w_r, b_r, _tri(tm).T, alpha, tm, [w_gate.reshape(n_exp * d, d_exp)])
    gate = _pad_cols(gate_t[0:TOP_K].T, LANES)

    te = EXPERT_TILE
    counts = cnt[:, 0].astype(jnp.int32)
    pcounts = (counts + te - 1) // te * te
    pends = jnp.cumsum(pcounts)
    pstarts = pends - pcounts
    e_sel = route[0:TOP_K, :, None] == jnp.arange(N_EXPERTS, dtype=jnp.int32)
    dest = (jnp.sum(jnp.where(e_sel, pstarts, 0), axis=-1) + route[TOP_K:2 * TOP_K]).T.reshape(-1)
    n_slots = t * TOP_K + N_EXPERTS * te
    n_tiles = n_slots // te
    n_used = (pends[-1] // te).reshape(1)
    tile_row = jnp.minimum(jnp.arange(n_tiles, dtype=jnp.int32), n_used - 1) * te
    block_e = jnp.minimum(jnp.sum((tile_row[:, None] >= pends[None, :]).astype(jnp.int32), axis=1), N_EXPERTS - 1)

    xs = _dispatch(x1, dest, pstarts + counts, pcounts - counts, n_used, n_slots, min(512, t), te, d)
    ys = _experts(xs, block_e, n_used, wg_b.reshape(n_exp, d, d_exp), wu_b.reshape(n_exp, d, d_exp),
                  wd_b.reshape(n_exp, d_exp, d), te)
    return _final(dest, x1, gate, p.reshape(t, -1), ys, row(ln2_g), row(ln2_b),
                  w_ple_gate, row(b_ple_gate), w_ple_proj, alpha,
                  min(512, t)).reshape(nb, seq, d)


def kernel(x, p, w_in, b_in, conv_w, conv_b, mh_g, w_pool, pool_scale, w_m_br, w_p_br, w_out, ln1_g, ln1_b, w_rg, b_rg, w_re, b_re, w_gate, w_up, w_down, ln2_g, ln2_b, w_ple_gate, b_ple_gate, w_ple_proj):
    depth = w_in.shape[0]
    alpha = (2 * depth) ** 0.25
    for i in range(depth):
        x = _layer(x, p[i], w_in[i].T, b_in[i], conv_w[i], conv_b[i], mh_g[i], w_pool[i], pool_scale[i],
                   w_m_br[i], w_p_br[i], w_out[i], ln1_g[i], ln1_b[i], w_rg[i], b_rg[i], w_re[i], b_re[i],
                   w_gate[i], w_up[i], w_down[i], ln2_g[i], ln2_b[i], w_ple_gate[i], b_ple_gate[i],
                   w_ple_proj[i], alpha)
    return x
```

```python
import functools

import jax
import jax.numpy as jnp
from jax import lax
from jax.experimental import pallas as pl
from jax.experimental.pallas import tpu as pltpu

F32 = jnp.float32
BF16 = jnp.bfloat16

M_HEADS = 4
M_QK_DIM = 128
M_V_DIM = 256
M_QK = M_HEADS * M_QK_DIM
M_V = M_HEADS * M_V_DIM
CONV_W = 4
CHUNK = 128
POOL_WINDOWS = (2, 4, 8, 16)
POOL_GROUP_DIM = 256
POOL_W = len(POOL_WINDOWS) * POOL_GROUP_DIM
N_GROUPS = 4
EXPERTS_PER_GROUP = 8
N_EXPERTS = N_GROUPS * EXPERTS_PER_GROUP
TOP_K = 2
LN_EPS = 1e-5

LANES = 128
SUBLANES = 8
VMEM_LIMIT = 56 * 1024 * 1024
BF16_SUBLANES = 16
CONV_HALO = BF16_SUBLANES
POOL_HALO = 32
EXPERT_TILE = 256
MLSTM_GROUP = 4
DMA_UNROLL = 8
ROW_SPLIT = 2


def _dot(a, b):
    return jnp.dot(a, b, preferred_element_type=F32)


def _params(n_grid):
    return pltpu.CompilerParams(dimension_semantics=("arbitrary",) * n_grid,
                                vmem_limit_bytes=VMEM_LIMIT)


def _log_sigmoid(x):
    return -(jnp.maximum(-x, 0.0) + jnp.log1p(jnp.exp(-jnp.abs(x))))


def _layer_norm(r, g, b):
    mu = jnp.mean(r, axis=-1, keepdims=True)
    d = r - mu
    var = jnp.mean(d * d, axis=-1, keepdims=True)
    return d * lax.rsqrt(var + LN_EPS) * g + b


def _pitch(d):
    return d // LANES + 1


def _store_token_major(ref, val):
    n, d = val.shape
    for c in range(d // LANES):
        ref[pl.ds(c, n, stride=_pitch(d)), :] = val[:, c * LANES:(c + 1) * LANES]
    ref[pl.ds(d // LANES, n, stride=_pitch(d)), :] = jnp.zeros((n, LANES), val.dtype)


def _load_token_major(ref, n, d):
    return jnp.concatenate([ref[pl.ds(c, n, stride=_pitch(d)), :] for c in range(d // LANES)], axis=1)


def _tokens(ref, tok, d, n=1):
    return ref.at[pl.ds(tok * _pitch(d), n * _pitch(d))]


def _side_specs(side, steps, step_of):
    def spec(a):
        rb = max(BF16_SUBLANES, a.shape[0] // steps)
        return pl.BlockSpec((rb, a.shape[1]), lambda *g: (jnp.minimum(step_of(*g), a.shape[0] // rb - 1), 0))
    return [spec(a) for a in side]


def _side_shapes(side):
    return [jax.ShapeDtypeStruct(a.shape, BF16) for a in side]


def _side_cast(side_in, side_out):
    for src, dst in zip(side_in, side_out):
        dst[...] = src[...].astype(BF16)


def _regroup_kernel(a_ref, c_ref, out_ref, cut_ref):
    out_ref[...] = a_ref[...].T.astype(BF16)
    skip, tk = c_ref.shape
    cut_ref[...] = jnp.concatenate([c_ref[...], jnp.zeros((LANES - skip, tk), F32)], axis=0).T


def _regroup(w_t, cut, skip, tn, tk):
    n, k = w_t.shape
    n_keep = cut // tn
    assert skip % SUBLANES == 0
    first_row = lambda j: pl.multiple_of(jnp.where(j < n_keep, j * tn, j * tn + skip), SUBLANES)
    return pl.pallas_call(
        _regroup_kernel,
        grid=(k // tk, (n - skip) // tn),
        in_specs=[pl.BlockSpec((pl.Element(tn), pl.Element(tk)), lambda kk, j: (first_row(j), kk * tk)),
                  pl.BlockSpec((pl.Element(skip), pl.Element(tk)), lambda kk, j: (cut, kk * tk))],
        out_specs=[pl.BlockSpec((tk, tn), lambda kk, j: (kk, j)),
                   pl.BlockSpec((tk, LANES), lambda kk, j: (kk, 0))],
        out_shape=[jax.ShapeDtypeStruct((k, n - skip), BF16), jax.ShapeDtypeStruct((k, LANES), F32)],
        compiler_params=_params(2),
        name="regroup",
    )(w_t, w_t)


def _inproj_kernel(*refs, n_side):
    x_ref, w_ref, b_ref, wif_ref, bif_ref = refs[:5]
    side_in = refs[5:5 + n_side]
    z_ref, zif_ref = refs[5 + n_side:7 + n_side]
    side_out = refs[7 + n_side:7 + 2 * n_side]
    xb_ref = refs[7 + 2 * n_side]

    @pl.when(pl.program_id(1) == 0)
    def _():
        xb = x_ref[...].astype(BF16)
        xb_ref[...] = xb
        zif_ref[...] = _dot(xb, wif_ref[...].astype(BF16)) + bif_ref[...]

    z_ref[...] = (_dot(xb_ref[...], w_ref[...]) + b_ref[...]).astype(BF16)
    _side_cast(side_in, side_out)


def _inproj(x2d, w_main, b_main, w_if, b_if, tm, tn, side):
    t, d = x2d.shape
    n = w_main.shape[1]
    nj = n // tn
    side_specs = _side_specs(side, (t // tm) * nj, lambda i, j: i * nj + j)
    return pl.pallas_call(
        functools.partial(_inproj_kernel, n_side=len(side)),
        grid=(t // tm, nj),
        in_specs=[pl.BlockSpec((tm, d), lambda i, j: (i, 0)),
                  pl.BlockSpec((d, tn), lambda i, j: (0, j)),
                  pl.BlockSpec((1, tn), lambda i, j: (0, j)),
                  pl.BlockSpec((d, LANES), lambda i, j: (0, 0)),
                  pl.BlockSpec((1, LANES), lambda i, j: (0, 0))] + side_specs,
        out_specs=[pl.BlockSpec((tm, tn), lambda i, j: (i, j)),
                   pl.BlockSpec((tm, LANES), lambda i, j: (i, 0))] + side_specs,
        out_shape=[jax.ShapeDtypeStruct((t, n), BF16),
                   jax.ShapeDtypeStruct((t, LANES), F32)] + _side_shapes(side),
        scratch_shapes=[pltpu.VMEM((tm, d), BF16)],
        compiler_params=_params(2),
        name="inproj",
    )(x2d, w_main, b_main, w_if, b_if, *side)


def _mlstm_kernel(*refs, nb, n_side):
    qk_ref, v_ref, o_ref, g_ref, cw_ref, cb_ref, mhg_ref = refs[:7]
    side_in = refs[7:7 + n_side]
    out_ref = refs[7 + n_side]
    side_out = refs[8 + n_side:8 + 2 * n_side]
    cbuf = refs[8 + 2 * n_side]
    state = refs[9 + 2 * n_side:]
    _mlstm_body(qk_ref, v_ref, o_ref, g_ref, cw_ref, cb_ref, mhg_ref, out_ref, cbuf, state, nb)
    _side_cast(side_in, side_out)


def _mlstm_body(qk_ref, v_ref, o_ref, g_ref, cw_ref, cb_ref, mhg_ref, out_ref, cbuf, state, nb):
    L = CHUNK
    halo = CONV_HALO
    ct_refs, m_ref = state[:M_HEADS], state[M_HEADS]

    @pl.when(pl.program_id(0) == 0)
    def _():
        cbuf[:, 0:halo, :] = jnp.zeros((nb, halo, 2 * M_QK), BF16)
        for ref in state:
            ref[...] = jnp.zeros_like(ref)

    cbuf[:, halo:halo + L, :] = qk_ref[...]
    scale = M_QK_DIM ** -0.5
    row = lax.broadcasted_iota(jnp.int32, (L, L), 0)
    col = lax.broadcasted_iota(jnp.int32, (L, L), 1)
    causal = col <= row
    upper = jnp.where(row <= col, 1.0, 0.0).astype(BF16)
    ones_blk = jnp.ones((L, LANES), BF16)
    lane8 = lax.broadcasted_iota(jnp.int32, (SUBLANES, L), 1)
    wrow = lax.broadcasted_iota(jnp.int32, (L, L + halo), 0)
    wcol = lax.broadcasted_iota(jnp.int32, (L, L + halo), 1)
    shifts = [jnp.where(wcol == wrow + (halo - (CONV_W - 1) + tap), 1.0, 0.0).astype(BF16)
              for tap in range(CONV_W - 1)]
    sub = lax.broadcasted_iota(jnp.int32, (SUBLANES, LANES), 0)

    def prep(b):
        window = cbuf[b]
        acc = cb_ref[...] + _dot(shifts[0], window) * cw_ref[0:1, :]
        for tap in range(1, CONV_W - 1):
            acc = acc + _dot(shifts[tap], window) * cw_ref[tap:tap + 1, :]
        acc = acc + qk_ref[b].astype(F32) * cw_ref[CONV_W - 1:CONV_W, :]
        qk = acc * jax.nn.sigmoid(acc)
        cbuf[b, 0:halo, :] = cbuf[b, L:L + halo, :]

        g_row = g_ref[b].T[0:SUBLANES, :]
        lf = _log_sigmoid(g_row)
        lf_hi = lf.astype(BF16)
        lf_lo = (lf - lf_hi.astype(F32)).astype(BF16)
        cs = _dot(jnp.concatenate([lf_hi, lf_lo], axis=0), upper)
        b_row = pltpu.roll(cs[0:SUBLANES, :] + cs[SUBLANES:2 * SUBLANES, :], M_HEADS, 0)
        a_row = g_row - b_row
        cm = a_row
        for sh in (1, 2, 4, 8, 16, 32, 64):
            cm = jnp.maximum(cm, jnp.where(lane8 >= sh, pltpu.roll(cm, sh, 1), -jnp.inf))
        m_prev = m_ref[b]
        mx = jnp.maximum(m_prev, cm)
        mx_last = jnp.maximum(m_prev, jnp.max(a_row, axis=1, keepdims=True))
        carry_in = jnp.exp(m_prev - mx)
        floor = jnp.exp(-(b_row + mx))
        wts = jnp.exp(a_row - mx_last)
        decay = jnp.exp(m_prev - mx_last)
        m_ref[b] = jnp.where(sub < M_HEADS, b_row[:, L - 1:L] + mx_last, 0.0)
        packed = jnp.concatenate(
            [jnp.where(sub < M_HEADS, mx, pltpu.roll(carry_in, M_HEADS, 0)),
             jnp.where(sub < M_HEADS, floor, pltpu.roll(wts, M_HEADS, 0)),
             jnp.zeros((L - 2 * SUBLANES, L), F32)], axis=0).T
        return qk, a_row, decay, packed

    def run_chains(batches):
        pre = {b: prep(b) for b in batches}
        chains = [(b, h) for b in batches for h in range(M_HEADS)]
        vcols = lambda h: slice(h * M_V_DIM, (h + 1) * M_V_DIM)
        col_of = lambda i, c: pre[c[0]][3][:, i * M_HEADS + c[1]:i * M_HEADS + c[1] + 1]
        q = {(b, h): pre[b][0][:, h * M_QK_DIM:(h + 1) * M_QK_DIM].astype(BF16) for b, h in chains}
        kf = {(b, h): pre[b][0][:, M_QK + h * M_QK_DIM:M_QK + (h + 1) * M_QK_DIM] * scale for b, h in chains}
        v_ext = {(b, h): jnp.concatenate([v_ref[b, :, vcols(h)], ones_blk], axis=1) for b, h in chains}
        ct = {(b, h): ct_refs[h][b] for b, h in chains}

        raw = {c: lax.dot_general(q[c], kf[c].astype(BF16), (((1,), (1,)), ((), ())),
                                  preferred_element_type=F32) for c in chains}
        prev = {c: _dot(q[c], ct[c].astype(BF16)) for c in chains}
        pmat = {(b, h): jnp.exp(jnp.where(causal, pre[b][1][h:h + 1, :] - col_of(0, (b, h)), -jnp.inf))
                for b, h in chains}
        s = {c: (raw[c] * pmat[c]).astype(BF16) for c in chains}
        numden = {c: _dot(s[c], v_ext[c]) + col_of(1, c) * prev[c] for c in chains}
        upd = {c: lax.dot_general((kf[c] * col_of(3, c)).astype(BF16), v_ext[c], (((0,), (0,)), ((), ())),
                                  preferred_element_type=F32) for c in chains}
        for b, h in chains:
            dec = jnp.concatenate([pre[b][2][h:h + 1, :]] * (ct[b, h].shape[1] // LANES), axis=1)
            ct_refs[h][b] = dec * ct[b, h] + upd[b, h]
        inv = {c: 1.0 / jnp.maximum(jnp.abs(numden[c][:, M_V_DIM:]), col_of(2, c)) for c in chains}
        hh = {c: numden[c][:, 0:M_V_DIM] * jnp.concatenate([inv[c]] * (M_V_DIM // LANES), axis=1) for c in chains}
        mu = {c: jnp.mean(hh[c], axis=-1, keepdims=True) for c in chains}
        dlt = {c: hh[c] - mu[c] for c in chains}
        var = {c: jnp.mean(dlt[c] * dlt[c], axis=-1, keepdims=True) for c in chains}
        for b, h in chains:
            hn = dlt[b, h] * lax.rsqrt(var[b, h] + LN_EPS) * mhg_ref[:, vcols(h)]
            og = jax.nn.sigmoid(o_ref[b, :, vcols(h)].astype(F32))
            out_ref[b, :, vcols(h)] = (hn * og).astype(BF16)

    for b0 in range(0, nb, MLSTM_GROUP):
        run_chains(range(b0, min(b0 + MLSTM_GROUP, nb)))


def _mlstm(z3, zif3, conv_w, conv_b, mh_g, side):
    nb, s, _ = z3.shape
    L = CHUNK
    blk = lambda colblk: pl.BlockSpec((nb, L, M_V), lambda c: (0, c, colblk))
    const = lambda shape: pl.BlockSpec(shape, lambda c: (0,) * len(shape))
    side_specs = _side_specs(side, s // L, lambda c: c)
    return pl.pallas_call(
        functools.partial(_mlstm_kernel, nb=nb, n_side=len(side)),
        grid=(s // L,),
        in_specs=[blk(0), blk(1), blk(2),
                  pl.BlockSpec((nb, L, LANES), lambda c: (0, c, 0)),
                  const((CONV_W, 2 * M_QK)), const((1, 2 * M_QK)), const((1, M_V))] + side_specs,
        out_specs=[pl.BlockSpec((nb, L, M_V), lambda c: (0, c, 0))] + side_specs,
        out_shape=[jax.ShapeDtypeStruct((nb, s, M_V), BF16)] + _side_shapes(side),
        scratch_shapes=([pltpu.VMEM((nb, L + CONV_HALO, 2 * M_QK), BF16)]
                        + [pltpu.VMEM((nb, M_QK_DIM, M_V_DIM + LANES), F32)] * M_HEADS
                        + [pltpu.VMEM((nb, SUBLANES, L), F32)]),
        compiler_params=_params(1),
        name="mlstm",
    )(z3, z3, z3, zif3, conv_w, conv_b, mh_g, *side)


def _branch_kernel(hg_ref, u_ref, uh_ref, gm_ref, gp_ref, wpool_ref, ps_ref, wm_ref, wp_ref,
                   out_ref, xa, xb, yp_ref, *, tm, seq):
    H = POOL_HALO
    G = POOL_GROUP_DIM
    t0 = lax.rem(pl.program_id(0) * tm, seq)
    u = u_ref[...].astype(F32)
    xa[H:H + tm, :] = u
    xa[0:H, :] = jnp.where(t0 == 0, 0.0, uh_ref[...].astype(F32))
    n = tm + H - 8
    xb[8:8 + n, :] = xa[8:8 + n, :] + xa[7:7 + n, :]
    n = tm + H - 16
    xa[16:16 + n, G:] = xb[16:16 + n, G:] + xb[14:14 + n, G:]
    n = tm + H - 24
    xb[24:24 + n, 2 * G:] = xa[24:24 + n, 2 * G:] + xa[20:20 + n, 2 * G:]
    xa[H:H + tm, 3 * G:] = xb[H:H + tm, 3 * G:] + xb[H - 8:H - 8 + tm, 3 * G:]
    tpos = t0 + lax.broadcasted_iota(jnp.int32, (tm, 1), 0)
    for g, win in enumerate(POOL_WINDOWS):
        src = (xb, xa, xb, xa)[g]
        cols = slice(g * G, (g + 1) * G)
        cnt = jnp.minimum(tpos + 1, win).astype(F32)
        y = src[H:H + tm, cols] / cnt - u[:, cols]
        yp = _dot(y.astype(BF16), wpool_ref[g]) * ps_ref[:, cols]
        yp_ref[:, cols] = yp.astype(BF16)
    hm = tm // ROW_SPLIT
    for r0 in range(0, tm, hm):
        rows = slice(r0, r0 + hm)
        pb = _dot(yp_ref[rows, :], wp_ref[...])
        a = _dot(hg_ref[rows, :], wm_ref[...])
        merged = (jax.nn.sigmoid(gm_ref[rows, :].astype(F32)) * a
                  + jax.nn.sigmoid(gp_ref[rows, :].astype(F32)) * pb)
        out_ref[rows, :] = merged.astype(BF16)


def _branch_side_kernel(*refs, n_side, tm, seq):
    side_in = refs[9:9 + n_side]
    side_out = refs[10 + n_side:10 + 2 * n_side]
    _branch_kernel(*refs[:9], refs[9 + n_side], *refs[10 + 2 * n_side:], tm=tm, seq=seq)
    _side_cast(side_in, side_out)


def _branch(hg2d, z_main, w_pool, pool_scale, w_m_br, w_p_br, seq, tm, side):
    t = hg2d.shape[0]
    d = w_m_br.shape[1]
    hb = tm // POOL_HALO
    const = lambda shape: pl.BlockSpec(shape, lambda i: (0,) * len(shape))
    side_specs = _side_specs(side, t // tm, lambda i: i)
    return pl.pallas_call(
        functools.partial(_branch_side_kernel, n_side=len(side), tm=tm, seq=seq),
        grid=(t // tm,),
        in_specs=[pl.BlockSpec((tm, M_V), lambda i: (i, 0)),
                  pl.BlockSpec((tm, POOL_W), lambda i: (i, 3)),
                  pl.BlockSpec((POOL_HALO, POOL_W), lambda i: (jnp.maximum(i * hb - 1, 0), 3)),
                  pl.BlockSpec((tm, d), lambda i: (i, 2)),
                  pl.BlockSpec((tm, d), lambda i: (i, 3)),
                  const(w_pool.shape), const((1, POOL_W)), const(w_m_br.shape), const(w_p_br.shape)] + side_specs,
        out_specs=[pl.BlockSpec((tm, d), lambda i: (i, 0))] + side_specs,
        out_shape=[jax.ShapeDtypeStruct((t, d), BF16)] + _side_shapes(side),
        scratch_shapes=[pltpu.VMEM((tm + POOL_HALO, POOL_W), F32),
                        pltpu.VMEM((tm + POOL_HALO, POOL_W), F32),
                        pltpu.VMEM((tm, POOL_W), BF16)],
        compiler_params=_params(1),
        name="branch",
    )(hg2d, z_main, z_main, z_main, z_main, w_pool, pool_scale, w_m_br, w_p_br, *side)


def _route_kernel(mg_ref, x_ref, wout_ref, g_ref, b_ref, wr_ref, br_ref, upper_ref,
                  x1_ref, route_ref, gate_ref, cnt_ref, carry, *, alpha, tm):
    @pl.when(pl.program_id(0) == 0)
    def _():
        carry[...] = jnp.zeros_like(carry)

    hm = tm // ROW_SPLIT
    pitch = _pitch(x_ref.shape[1])
    logits = []
    for r0 in range(0, tm, hm):
        r = alpha * x_ref[r0:r0 + hm, :] + _dot(mg_ref[r0:r0 + hm, :], wout_ref[...])
        x1 = _layer_norm(r, g_ref[...], b_ref[...])
        _store_token_major(x1_ref.at[r0 * pitch:(r0 + hm) * pitch], x1)
        logits.append(_dot(x1.astype(BF16), wr_ref[...]) + br_ref[...])
    lt = jnp.concatenate(logits, axis=0).T

    sub = lax.broadcasted_iota(jnp.int32, (SUBLANES, tm), 0).astype(F32)
    none = float(SUBLANES)

    def softmax(z):
        e = jnp.exp(z - jnp.max(z, axis=0, keepdims=True))
        return e / jnp.sum(e, axis=0, keepdims=True)

    def top1(vals):
        top = jnp.max(vals, axis=0, keepdims=True)
        return top, jnp.min(jnp.where(vals == top, sub, none), axis=0, keepdims=True)

    is_grp = sub < N_GROUPS
    pg = jnp.where(is_grp, softmax(jnp.where(is_grp, lt[0:SUBLANES, :], -jnp.inf)), -1.0)
    pg_top, g_idx = top1(pg)
    le_sel = lt[SUBLANES:2 * SUBLANES, :]
    for g in range(1, N_GROUPS):
        le_sel = jnp.where(g_idx == g, lt[(g + 1) * SUBLANES:(g + 2) * SUBLANES, :], le_sel)
    pe = softmax(le_sel)
    pe1, i1 = top1(pe)
    pe2, i2 = top1(jnp.where(sub == i1, -1.0, pe))
    den = pe1 + pe2
    gate1 = pg_top * pe1 / den
    gate2 = pg_top * pe2 / den
    e1 = g_idx * EXPERTS_PER_GROUP + i1
    e2 = g_idx * EXPERTS_PER_GROUP + i2

    eid = lax.broadcasted_iota(jnp.int32, (N_EXPERTS, tm), 0).astype(F32)
    hit1 = eid == e1
    hit2 = eid == e2
    onehot = jnp.where(hit1 | hit2, 1.0, 0.0)
    incl = _dot(onehot.astype(BF16), upper_ref[...])
    excl = incl - onehot + carry[:, 0:1]
    r1 = jnp.sum(jnp.where(hit1, excl, 0.0), axis=0, keepdims=True)
    r2 = jnp.sum(jnp.where(hit2, excl, 0.0), axis=0, keepdims=True)
    carry[...] = carry[...] + jnp.broadcast_to(incl[:, tm - 1:tm], carry.shape)
    cnt_ref[...] = carry[...]

    route = jnp.where(sub == 0, e1, jnp.where(sub == 1, e2, jnp.where(sub == 2, r1, jnp.where(sub == 3, r2, 0.0))))
    route_ref[...] = route.astype(jnp.int32)
    gate_ref[...] = jnp.where(sub == 0, gate1, jnp.where(sub == 1, gate2, 0.0))


def _route_side_kernel(*refs, n_side, alpha, tm):
    side_in = refs[8:8 + n_side]
    side_out = refs[12 + n_side:12 + 2 * n_side]
    _route_kernel(*refs[:8], *refs[8 + n_side:12 + n_side], *refs[12 + 2 * n_side:], alpha=alpha, tm=tm)
    _side_cast(side_in, side_out)


def _route(merged, x2d, w_out, ln_g, ln_b, w_r, b_r, upper, alpha, tm, side):
    t, d = x2d.shape
    const = lambda shape: pl.BlockSpec(shape, lambda i: (0,) * len(shape))
    resident = lambda shape: pl.BlockSpec(shape, lambda i: (0,) * len(shape), pipeline_mode=pl.Buffered(1))
    rows = lambda width: pl.BlockSpec((tm, width), lambda i: (i, 0))
    cols = pl.BlockSpec((SUBLANES, tm), lambda i: (0, i))
    side_specs = _side_specs(side, t // tm, lambda i: i)
    return pl.pallas_call(
        functools.partial(_route_side_kernel, n_side=len(side), alpha=alpha, tm=tm),
        grid=(t // tm,),
        in_specs=[rows(d), rows(d), resident(w_out.shape), const((1, d)), const((1, d)),
                  const(w_r.shape), const((1, LANES)), const((tm, tm))] + side_specs,
        out_specs=[pl.BlockSpec((tm * _pitch(d), LANES), lambda i: (i, 0)),
                   cols, cols, const((N_EXPERTS, LANES))] + side_specs,
        out_shape=[jax.ShapeDtypeStruct((t * _pitch(d), LANES), F32),
                   jax.ShapeDtypeStruct((SUBLANES, t), jnp.int32),
                   jax.ShapeDtypeStruct((SUBLANES, t), F32),
                   jax.ShapeDtypeStruct((N_EXPERTS, LANES), F32)] + _side_shapes(side),
        scratch_shapes=[pltpu.VMEM((N_EXPERTS, LANES), F32)],
        compiler_params=_params(1),
        name="route",
    )(merged, x2d, w_out, ln_g, ln_b, w_r, b_r, upper, *side)


def _dispatch_kernel(pad_start_ref, pad_len_ref, nu_ref, dest_ref, x_ref, xs_hbm, sem, *, tb, te, d):
    def copy(src_tok, dst_tok, n=1):
        return pltpu.make_async_copy(_tokens(x_ref, src_tok, d, n), _tokens(xs_hbm, dst_tok, d, n), sem)

    def wait_tokens(n):
        def body(r, c):
            copy(0, 0, n).wait()
            return c
        return body

    @pl.when(pl.program_id(0) == 0)
    def _():
        def per_expert(e, total):
            start = pad_start_ref[e]
            npad = pad_len_ref[e]

            def fill(r, c):
                copy(0, start + r).start()
                return c

            lax.fori_loop(0, npad, fill, 0)
            return total + npad

        total = lax.fori_loop(0, N_EXPERTS, per_expert, 0)
        lax.fori_loop(0, total // SUBLANES, wait_tokens(SUBLANES), 0)
        lax.fori_loop(0, total % SUBLANES, wait_tokens(1), 0)

        def fill_tile(j, c):
            copy(0, j * te, te).start()
            return c

        n_tiles = xs_hbm.shape[0] // (te * _pitch(d))
        lax.fori_loop(nu_ref[0], n_tiles, fill_tile, 0)
        lax.fori_loop(nu_ref[0], n_tiles, wait_tokens(te), 0)

    def scatter(t, c):
        for k in range(TOP_K):
            copy(t, dest_ref[TOP_K * t + k]).start(priority=k % 2)
        return c

    lax.fori_loop(0, tb, scatter, 0, unroll=DMA_UNROLL)
    for k in range(TOP_K):
        copy(0, 0, tb).wait()


def _dispatch(x1t, dest_flat, pad_start, pad_len, n_used, n_slots, tb, te, d):
    pitch = _pitch(d)
    t = x1t.shape[0] // pitch
    grid_spec = pltpu.PrefetchScalarGridSpec(
        num_scalar_prefetch=3,
        grid=(t // tb,),
        in_specs=[pl.BlockSpec((TOP_K * tb,), lambda i, ps, pn, nu: (i,), memory_space=pltpu.SMEM),
                  pl.BlockSpec((tb * pitch, LANES), lambda i, ps, pn, nu: (i, 0))],
        out_specs=pl.BlockSpec(memory_space=pl.ANY),
        scratch_shapes=[pltpu.SemaphoreType.DMA(())],
    )
    return pl.pallas_call(
        functools.partial(_dispatch_kernel, tb=tb, te=te, d=d),
        grid_spec=grid_spec,
        out_shape=jax.ShapeDtypeStruct((n_slots * pitch, LANES), x1t.dtype),
        compiler_params=_params(1),
        name="dispatch",
    )(pad_start, pad_len, n_used, dest_flat, x1t)


def _expert_kernel(be_ref, nu_ref, x_ref, wg_ref, wu_ref, wd_ref, y_ref, *, te, d):
    i = pl.program_id(0)

    @pl.when(i < nu_ref[0])
    def _():
        xb = _load_token_major(x_ref, te, d).astype(BF16)
        gt = _dot(xb, wg_ref[0])
        up = _dot(xb, wu_ref[0])
        hid = gt * jax.nn.sigmoid(gt) * up
        _store_token_major(y_ref, _dot(hid.astype(BF16), wd_ref[0]))

    @pl.when(i >= nu_ref[0])
    def _():
        y_ref[...] = jnp.zeros_like(y_ref)


def _experts(xs, block_e, n_used, w_gate, w_up, w_down, te):
    _, d, de = w_gate.shape
    pitch = _pitch(d)
    n_tiles = xs.shape[0] // (te * pitch)
    grid_spec = pltpu.PrefetchScalarGridSpec(
        num_scalar_prefetch=2,
        grid=(n_tiles,),
        in_specs=[pl.BlockSpec((te * pitch, LANES), lambda i, be, nu: (jnp.minimum(i, nu[0] - 1), 0)),
                  pl.BlockSpec((1, d, de), lambda i, be, nu: (be[i], 0, 0)),
                  pl.BlockSpec((1, d, de), lambda i, be, nu: (be[i], 0, 0)),
                  pl.BlockSpec((1, de, d), lambda i, be, nu: (be[i], 0, 0))],
        out_specs=pl.BlockSpec((te * pitch, LANES), lambda i, be, nu: (i, 0)),
    )
    return pl.pallas_call(
        functools.partial(_expert_kernel, te=te, d=d),
        grid_spec=grid_spec,
        out_shape=jax.ShapeDtypeStruct(xs.shape, F32),
        compiler_params=_params(1),
        name="experts",
    )(block_e, n_used, xs, w_gate, w_up, w_down)


def _final_kernel(dcur_ref, dnxt_ref, x1_ref, gate_ref, p_ref, ys_hbm, g_ref, b_ref, wg_ref, bg_ref, wp_ref,
                  out_ref, ybuf, sem, *, alpha, tm, d):
    i = pl.program_id(0)
    slot = lax.rem(i, 2)

    def issue(dest_ref, s):
        def body(t, c):
            for k in range(TOP_K):
                pltpu.make_async_copy(_tokens(ys_hbm, dest_ref[TOP_K * t + k], d),
                                      _tokens(ybuf.at[s, k], t, d), sem.at[s]).start(priority=k % 2)
            return c
        lax.fori_loop(0, tm, body, 0, unroll=DMA_UNROLL)

    @pl.when(i == 0)
    def _():
        issue(dcur_ref, 0)

    @pl.when(i + 1 < pl.num_programs(0))
    def _():
        issue(dnxt_ref, 1 - slot)

    for k in range(TOP_K):
        pltpu.make_async_copy(_tokens(ys_hbm, 0, d, tm), ybuf.at[slot, k], sem.at[slot]).wait()

    hm = tm // ROW_SPLIT
    for r0 in range(0, tm, hm):
        rows = slice(r0, r0 + hm)
        trows = pl.ds(r0 * _pitch(d), hm * _pitch(d))
        pp = _dot(p_ref[rows, :].astype(BF16), wp_ref[...])
        gate = gate_ref[rows, :]
        y = (gate[:, 0:1] * _load_token_major(ybuf.at[slot, 0, trows], hm, d)
             + gate[:, 1:2] * _load_token_major(ybuf.at[slot, 1, trows], hm, d))
        x2 = _layer_norm(alpha * _load_token_major(x1_ref.at[trows], hm, d) + y, g_ref[...], b_ref[...])
        gl = _dot(x2.astype(BF16), wg_ref[...]) + bg_ref[...]
        out_ref[rows, :] = x2 + jax.nn.sigmoid(gl) * pp


def _final(dest_flat, x1t, gate, p2d, ys, ln_g, ln_b, w_pg, b_pg, w_pp, alpha, tm):
    d = w_pg.shape[0]
    pitch = _pitch(d)
    t = x1t.shape[0] // pitch
    n = t // tm
    const = lambda shape: pl.BlockSpec(shape, lambda i: (0,) * len(shape))
    resident = lambda shape: pl.BlockSpec(shape, lambda i: (0,) * len(shape), pipeline_mode=pl.Buffered(1))
    rows = lambda width: pl.BlockSpec((tm, width), lambda i: (i, 0))
    return pl.pallas_call(
        functools.partial(_final_kernel, alpha=alpha, tm=tm, d=d),
        grid=(n,),
        in_specs=[pl.BlockSpec((TOP_K * tm,), lambda i: (i,), memory_space=pltpu.SMEM),
                  pl.BlockSpec((TOP_K * tm,), lambda i: (jnp.minimum(i + 1, n - 1),), memory_space=pltpu.SMEM),
                  pl.BlockSpec((tm * pitch, LANES), lambda i: (i, 0)),
                  rows(LANES), rows(p2d.shape[1]),
                  pl.BlockSpec(memory_space=pl.ANY),
                  const((1, d)), const((1, d)), resident(w_pg.shape), const((1, d)), resident(w_pp.shape)],
        out_specs=rows(d),
        out_shape=jax.ShapeDtypeStruct((t, d), F32),
        scratch_shapes=[pltpu.VMEM((2, TOP_K, tm * pitch, LANES), F32), pltpu.SemaphoreType.DMA((2,))],
        compiler_params=_params(1),
        name="final",
    )(dest_flat, dest_flat, x1t, gate, p2d, ys, ln_g, ln_b, w_pg, b_pg, w_pp)


def _pad_cols(a, width):
    return jnp.pad(a, ((0, 0), (0, width - a.shape[1])))


def _tri(n):
    return jnp.tril(jnp.ones((n, n), BF16))


def _layer(x, p, w_in_t, b_in, conv_w, conv_b, mh_g, w_pool, pool_scale, w_m_br, w_p_br, w_out,
           ln1_g, ln1_b, w_rg, b_rg, w_re, b_re, w_gate, w_up, w_down, ln2_g, ln2_b,
           w_ple_gate, b_ple_gate, w_ple_proj, alpha):
    nb, seq, d = x.shape
    t = nb * seq
    x2d = x.reshape(t, d)
    row = lambda a: a.reshape(1, -1)

    c_if = 2 * M_QK + 2 * M_V
    w_main, w_if = _regroup(w_in_t, c_if, 2 * M_HEADS, 1024, min(1024, d))
    b_main = row(jnp.concatenate([b_in[:c_if], b_in[c_if + 2 * M_HEADS:]]))
    b_if = _pad_cols(row(b_in[c_if:c_if + 2 * M_HEADS]), LANES)

    n_exp, _, d_exp = w_gate.shape
    tm_in = min(1024, t)
    z_main, z_if = _inproj(x2d, w_main, b_main, w_if, b_if, tm_in, 2048, [])

    dense_w = [w_pool, w_m_br, w_p_br, w_out, w_ple_gate, w_ple_proj, w_down.reshape(n_exp * d_exp, d)]
    hg, *dense_b = _mlstm(z_main.reshape(nb, seq, -1), z_if.reshape(nb, seq, LANES),
                          conv_w, row(conv_b), row(mh_g), [a.reshape(-1, a.shape[-1]) for a in dense_w])
    w_pool, w_m_br, w_p_br, w_out, w_ple_gate, w_ple_proj, wd_b = [
        b.reshape(a.shape) for a, b in zip(dense_w, dense_b)]

    tm = min(512, seq)
    merged, wu_b = _branch(hg.reshape(t, M_V), z_main, w_pool, row(pool_scale), w_m_br, w_p_br, seq, tm,
                           [w_up.reshape(n_exp * d, d_exp)])

    w_r = _pad_cols(jnp.concatenate([_pad_cols(w_rg, SUBLANES), w_re], axis=1), LANES).astype(BF16)
    b_r = _pad_cols(jnp.concatenate([_pad_cols(row(b_rg), SUBLANES), row(b_re)], axis=1), LANES)
    x1, route, gate_t, cnt, wg_b = _route(merged, x2d, w_out, row(ln1_g), row(ln1_b),
                                          w_r, b_r, _tri(tm).T, alpha, tm, [w_gate.reshape(n_exp * d, d_exp)])
    gate = _pad_cols(gate_t[0:TOP_K].T, LANES)

    te = EXPERT_TILE
    counts = cnt[:, 0].astype(jnp.int32)
    pcounts = (counts + te - 1) // te * te
    pends = jnp.cumsum(pcounts)
    pstarts = pends - pcounts
    e_sel = route[0:TOP_K, :, None] == jnp.arange(N_EXPERTS, dtype=jnp.int32)
    dest = (jnp.sum(jnp.where(e_sel, pstarts, 0), axis=-1) + route[TOP_K:2 * TOP_K]).T.reshape(-1)
    n_slots = t * TOP_K + N_EXPERTS * te
    n_tiles = n_slots // te
    n_used = (pends[-1] // te).reshape(1)
    tile_row = jnp.minimum(jnp.arange(n_tiles, dtype=jnp.int32), n_used - 1) * te
    block_e = jnp.minimum(jnp.sum((tile_row[:, None] >= pends[None, :]).astype(jnp.int32), axis=1), N_EXPERTS - 1)

    xs = _dispatch(x1, dest, pstarts + counts, pcounts - counts, n_used, n_slots, min(512, t), te, d)
    ys = _experts(xs, block_e, n_used, wg_b.reshape(n_exp, d, d_exp), wu_b.reshape(n_exp, d, d_exp),
                  wd_b.reshape(n_exp, d_exp, d), te)
    return _final(dest, x1, gate, p.reshape(t, -1), ys, row(ln2_g), row(ln2_b),
                  w_ple_gate, row(b_ple_gate), w_ple_proj, alpha,
                  min(512, t)).reshape(nb, seq, d)


def kernel(x, p, w_in, b_in, conv_w, conv_b, mh_g, w_pool, pool_scale, w_m_br, w_p_br, w_out, ln1_g, ln1_b, w_rg, b_rg, w_re, b_re, w_gate, w_up, w_down, ln2_g, ln2_b, w_ple_gate, b_ple_gate, w_ple_proj):
    depth = w_in.shape[0]
    alpha = (2 * depth) ** 0.25
    for i in range(depth):
        x = _layer(x, p[i], w_in[i].T, b_in[i], conv_w[i], conv_b[i], mh_g[i], w_pool[i], pool_scale[i],
                   w_m_br[i], w_p_br[i], w_out[i], ln1_g[i], ln1_b[i], w_rg[i], b_rg[i], w_re[i], b_re[i],
                   w_gate[i], w_up[i], w_down[i], ln2_g[i], ln2_b[i], w_ple_gate[i], b_ple_gate[i],
                   w_ple_proj[i], alpha)
    return x
```

```python
import functools

import jax
import jax.numpy as jnp
from jax import lax
from jax.experimental import pallas as pl
from jax.experimental.pallas import tpu as pltpu

F32 = jnp.float32
BF16 = jnp.bfloat16

M_HEADS = 4
M_QK_DIM = 128
M_V_DIM = 256
M_QK = M_HEADS * M_QK_DIM
M_V = M_HEADS * M_V_DIM
CONV_W = 4
CHUNK = 128
POOL_WINDOWS = (2, 4, 8, 16)
POOL_GROUP_DIM = 256
POOL_W = len(POOL_WINDOWS) * POOL_GROUP_DIM
N_GROUPS = 4
EXPERTS_PER_GROUP = 8
N_EXPERTS = N_GROUPS * EXPERTS_PER_GROUP
TOP_K = 2
LN_EPS = 1e-5

LANES = 128
SUBLANES = 8
VMEM_LIMIT = 56 * 1024 * 1024
BF16_SUBLANES = 16
CONV_HALO = BF16_SUBLANES
POOL_HALO = 32
EXPERT_TILE = 256
MLSTM_GROUP = 4
DMA_UNROLL = 8
ROW_SPLIT = 2


def _dot(a, b):
    return jnp.dot(a, b, preferred_element_type=F32)


def _params(n_grid):
    return pltpu.CompilerParams(dimension_semantics=("arbitrary",) * n_grid,
                                vmem_limit_bytes=VMEM_LIMIT)


def _log_sigmoid(x):
    return -(jnp.maximum(-x, 0.0) + jnp.log1p(jnp.exp(-jnp.abs(x))))


def _layer_norm(r, g, b):
    mu = jnp.mean(r, axis=-1, keepdims=True)
    d = r - mu
    var = jnp.mean(d * d, axis=-1, keepdims=True)
    return d * lax.rsqrt(var + LN_EPS) * g + b


def _pitch(d):
    return d // LANES + 1


def _store_token_major(ref, val):
    n, d = val.shape
    for c in range(d // LANES):
        ref[pl.ds(c, n, stride=_pitch(d)), :] = val[:, c * LANES:(c + 1) * LANES]
    ref[pl.ds(d // LANES, n, stride=_pitch(d)), :] = jnp.zeros((n, LANES), val.dtype)


def _load_token_major(ref, n, d):
    return jnp.concatenate([ref[pl.ds(c, n, stride=_pitch(d)), :] for c in range(d // LANES)], axis=1)


def _tokens(ref, tok, d, n=1):
    return ref.at[pl.ds(tok * _pitch(d), n * _pitch(d))]


def _token_data(ref, tok, d):
    return ref.at[pl.ds(tok * _pitch(d), d // LANES)]


def _side_specs(side, steps, step_of):
    def spec(a):
        rb = max(BF16_SUBLANES, a.shape[0] // steps)
        return pl.BlockSpec((rb, a.shape[1]), lambda *g: (jnp.minimum(step_of(*g), a.shape[0] // rb - 1), 0))
    return [spec(a) for a in side]


def _side_shapes(side):
    return [jax.ShapeDtypeStruct(a.shape, BF16) for a in side]


def _side_cast(side_in, side_out):
    for src, dst in zip(side_in, side_out):
        dst[...] = src[...].astype(BF16)


def _regroup_kernel(a_ref, c_ref, out_ref, cut_ref):
    out_ref[...] = a_ref[...].T.astype(BF16)
    skip, tk = c_ref.shape
    cut_ref[...] = jnp.concatenate([c_ref[...], jnp.zeros((LANES - skip, tk), F32)], axis=0).T


def _regroup(w_t, cut, skip, tn, tk):
    n, k = w_t.shape
    n_keep = cut // tn
    assert skip % SUBLANES == 0
    first_row = lambda j: pl.multiple_of(jnp.where(j < n_keep, j * tn, j * tn + skip), SUBLANES)
    return pl.pallas_call(
        _regroup_kernel,
        grid=(k // tk, (n - skip) // tn),
        in_specs=[pl.BlockSpec((pl.Element(tn), pl.Element(tk)), lambda kk, j: (first_row(j), kk * tk)),
                  pl.BlockSpec((pl.Element(skip), pl.Element(tk)), lambda kk, j: (cut, kk * tk))],
        out_specs=[pl.BlockSpec((tk, tn), lambda kk, j: (kk, j)),
                   pl.BlockSpec((tk, LANES), lambda kk, j: (kk, 0))],
        out_shape=[jax.ShapeDtypeStruct((k, n - skip), BF16), jax.ShapeDtypeStruct((k, LANES), F32)],
        compiler_params=_params(2),
        name="regroup",
    )(w_t, w_t)


def _inproj_kernel(*refs, n_side):
    x_ref, w_ref, b_ref, wif_ref, bif_ref = refs[:5]
    side_in = refs[5:5 + n_side]
    z_ref, zif_ref = refs[5 + n_side:7 + n_side]
    side_out = refs[7 + n_side:7 + 2 * n_side]
    xb_ref = refs[7 + 2 * n_side]

    @pl.when(pl.program_id(1) == 0)
    def _():
        xb = x_ref[...].astype(BF16)
        xb_ref[...] = xb
        zif_ref[...] = _dot(xb, wif_ref[...].astype(BF16)) + bif_ref[...]

    z_ref[...] = (_dot(xb_ref[...], w_ref[...]) + b_ref[...]).astype(BF16)
    _side_cast(side_in, side_out)


def _inproj(x2d, w_main, b_main, w_if, b_if, tm, tn, side):
    t, d = x2d.shape
    n = w_main.shape[1]
    nj = n // tn
    side_specs = _side_specs(side, (t // tm) * nj, lambda i, j: i * nj + j)
    return pl.pallas_call(
        functools.partial(_inproj_kernel, n_side=len(side)),
        grid=(t // tm, nj),
        in_specs=[pl.BlockSpec((tm, d), lambda i, j: (i, 0)),
                  pl.BlockSpec((d, tn), lambda i, j: (0, j)),
                  pl.BlockSpec((1, tn), lambda i, j: (0, j)),
                  pl.BlockSpec((d, LANES), lambda i, j: (0, 0)),
                  pl.BlockSpec((1, LANES), lambda i, j: (0, 0))] + side_specs,
        out_specs=[pl.BlockSpec((tm, tn), lambda i, j: (i, j)),
                   pl.BlockSpec((tm, LANES), lambda i, j: (i, 0))] + side_specs,
        out_shape=[jax.ShapeDtypeStruct((t, n), BF16),
                   jax.ShapeDtypeStruct((t, LANES), F32)] + _side_shapes(side),
        scratch_shapes=[pltpu.VMEM((tm, d), BF16)],
        compiler_params=_params(2),
        name="inproj",
    )(x2d, w_main, b_main, w_if, b_if, *side)


def _mlstm_kernel(*refs, nb, n_side):
    qk_ref, v_ref, o_ref, g_ref, cw_ref, cb_ref, mhg_ref = refs[:7]
    side_in = refs[7:7 + n_side]
    out_ref = refs[7 + n_side]
    side_out = refs[8 + n_side:8 + 2 * n_side]
    cbuf = refs[8 + 2 * n_side]
    state = refs[9 + 2 * n_side:]
    _mlstm_body(qk_ref, v_ref, o_ref, g_ref, cw_ref, cb_ref, mhg_ref, out_ref, cbuf, state, nb)
    _side_cast(side_in, side_out)


def _mlstm_body(qk_ref, v_ref, o_ref, g_ref, cw_ref, cb_ref, mhg_ref, out_ref, cbuf, state, nb):
    L = CHUNK
    halo = CONV_HALO
    ct_refs, m_ref = state[:M_HEADS], state[M_HEADS]

    @pl.when(pl.program_id(0) == 0)
    def _():
        cbuf[:, 0:halo, :] = jnp.zeros((nb, halo, 2 * M_QK), BF16)
        for ref in state:
            ref[...] = jnp.zeros_like(ref)

    cbuf[:, halo:halo + L, :] = qk_ref[...]
    scale = M_QK_DIM ** -0.5
    row = lax.broadcasted_iota(jnp.int32, (L, L), 0)
    col = lax.broadcasted_iota(jnp.int32, (L, L), 1)
    causal = col <= row
    upper = jnp.where(row <= col, 1.0, 0.0).astype(BF16)
    ones_blk = jnp.ones((L, LANES), BF16)
    lane8 = lax.broadcasted_iota(jnp.int32, (SUBLANES, L), 1)
    wrow = lax.broadcasted_iota(jnp.int32, (L, L + halo), 0)
    wcol = lax.broadcasted_iota(jnp.int32, (L, L + halo), 1)
    shifts = [jnp.where(wcol == wrow + (halo - (CONV_W - 1) + tap), 1.0, 0.0).astype(BF16)
              for tap in range(CONV_W - 1)]
    sub = lax.broadcasted_iota(jnp.int32, (SUBLANES, LANES), 0)

    def prep(b):
        window = cbuf[b]
        acc = cb_ref[...] + _dot(shifts[0], window) * cw_ref[0:1, :]
        for tap in range(1, CONV_W - 1):
            acc = acc + _dot(shifts[tap], window) * cw_ref[tap:tap + 1, :]
        acc = acc + qk_ref[b].astype(F32) * cw_ref[CONV_W - 1:CONV_W, :]
        qk = acc * jax.nn.sigmoid(acc)
        cbuf[b, 0:halo, :] = cbuf[b, L:L + halo, :]

        g_row = g_ref[b].T[0:SUBLANES, :]
        lf = _log_sigmoid(g_row)
        lf_hi = lf.astype(BF16)
        lf_lo = (lf - lf_hi.astype(F32)).astype(BF16)
        cs = _dot(jnp.concatenate([lf_hi, lf_lo], axis=0), upper)
        b_row = pltpu.roll(cs[0:SUBLANES, :] + cs[SUBLANES:2 * SUBLANES, :], M_HEADS, 0)
        a_row = g_row - b_row
        cm = a_row
        for sh in (1, 2, 4, 8, 16, 32, 64):
            cm = jnp.maximum(cm, jnp.where(lane8 >= sh, pltpu.roll(cm, sh, 1), -jnp.inf))
        m_prev = m_ref[b]
        mx = jnp.maximum(m_prev, cm)
        mx_last = jnp.maximum(m_prev, jnp.max(a_row, axis=1, keepdims=True))
        carry_in = jnp.exp(m_prev - mx)
        floor = jnp.exp(-(b_row + mx))
        wts = jnp.exp(a_row - mx_last)
        decay = jnp.exp(m_prev - mx_last)
        m_ref[b] = jnp.where(sub < M_HEADS, b_row[:, L - 1:L] + mx_last, 0.0)
        packed = jnp.concatenate(
            [jnp.where(sub < M_HEADS, mx, pltpu.roll(carry_in, M_HEADS, 0)),
             jnp.where(sub < M_HEADS, floor, pltpu.roll(wts, M_HEADS, 0)),
             jnp.zeros((L - 2 * SUBLANES, L), F32)], axis=0).T
        return qk, a_row, decay, packed

    def run_chains(batches):
        pre = {b: prep(b) for b in batches}
        chains = [(b, h) for b in batches for h in range(M_HEADS)]
        vcols = lambda h: slice(h * M_V_DIM, (h + 1) * M_V_DIM)
        col_of = lambda i, c: pre[c[0]][3][:, i * M_HEADS + c[1]:i * M_HEADS + c[1] + 1]
        q = {(b, h): pre[b][0][:, h * M_QK_DIM:(h + 1) * M_QK_DIM].astype(BF16) for b, h in chains}
        kf = {(b, h): pre[b][0][:, M_QK + h * M_QK_DIM:M_QK + (h + 1) * M_QK_DIM] * scale for b, h in chains}
        v_ext = {(b, h): jnp.concatenate([v_ref[b, :, vcols(h)], ones_blk], axis=1) for b, h in chains}
        ct = {(b, h): ct_refs[h][b] for b, h in chains}

        raw = {c: lax.dot_general(q[c], kf[c].astype(BF16), (((1,), (1,)), ((), ())),
                                  preferred_element_type=F32) for c in chains}
        prev = {c: _dot(q[c], ct[c].astype(BF16)) for c in chains}
        pmat = {(b, h): jnp.exp(jnp.where(causal, pre[b][1][h:h + 1, :] - col_of(0, (b, h)), -jnp.inf))
                for b, h in chains}
        s = {c: (raw[c] * pmat[c]).astype(BF16) for c in chains}
        numden = {c: _dot(s[c], v_ext[c]) + col_of(1, c) * prev[c] for c in chains}
        upd = {c: lax.dot_general((kf[c] * col_of(3, c)).astype(BF16), v_ext[c], (((0,), (0,)), ((), ())),
                                  preferred_element_type=F32) for c in chains}
        for b, h in chains:
            dec = jnp.concatenate([pre[b][2][h:h + 1, :]] * (ct[b, h].shape[1] // LANES), axis=1)
            ct_refs[h][b] = dec * ct[b, h] + upd[b, h]
        inv = {c: 1.0 / jnp.maximum(jnp.abs(numden[c][:, M_V_DIM:]), col_of(2, c)) for c in chains}
        hh = {c: numden[c][:, 0:M_V_DIM] * jnp.concatenate([inv[c]] * (M_V_DIM // LANES), axis=1) for c in chains}
        mu = {c: jnp.mean(hh[c], axis=-1, keepdims=True) for c in chains}
        dlt = {c: hh[c] - mu[c] for c in chains}
        var = {c: jnp.mean(dlt[c] * dlt[c], axis=-1, keepdims=True) for c in chains}
        for b, h in chains:
            hn = dlt[b, h] * lax.rsqrt(var[b, h] + LN_EPS) * mhg_ref[:, vcols(h)]
            og = jax.nn.sigmoid(o_ref[b, :, vcols(h)].astype(F32))
            out_ref[b, :, vcols(h)] = (hn * og).astype(BF16)

    for b0 in range(0, nb, MLSTM_GROUP):
        run_chains(range(b0, min(b0 + MLSTM_GROUP, nb)))


def _mlstm(z3, zif3, conv_w, conv_b, mh_g, side):
    nb, s, _ = z3.shape
    L = CHUNK
    blk = lambda colblk: pl.BlockSpec((nb, L, M_V), lambda c: (0, c, colblk))
    const = lambda shape: pl.BlockSpec(shape, lambda c: (0,) * len(shape))
    side_specs = _side_specs(side, s // L, lambda c: c)
    return pl.pallas_call(
        functools.partial(_mlstm_kernel, nb=nb, n_side=len(side)),
        grid=(s // L,),
        in_specs=[blk(0), blk(1), blk(2),
                  pl.BlockSpec((nb, L, LANES), lambda c: (0, c, 0)),
                  const((CONV_W, 2 * M_QK)), const((1, 2 * M_QK)), const((1, M_V))] + side_specs,
        out_specs=[pl.BlockSpec((nb, L, M_V), lambda c: (0, c, 0))] + side_specs,
        out_shape=[jax.ShapeDtypeStruct((nb, s, M_V), BF16)] + _side_shapes(side),
        scratch_shapes=([pltpu.VMEM((nb, L + CONV_HALO, 2 * M_QK), BF16)]
                        + [pltpu.VMEM((nb, M_QK_DIM, M_V_DIM + LANES), F32)] * M_HEADS
                        + [pltpu.VMEM((nb, SUBLANES, L), F32)]),
        compiler_params=_params(1),
        name="mlstm",
    )(z3, z3, z3, zif3, conv_w, conv_b, mh_g, *side)


def _branch_kernel(hg_ref, u_ref, uh_ref, gm_ref, gp_ref, wpool_ref, ps_ref, wm_ref, wp_ref,
                   out_ref, xa, xb, yp_ref, *, tm, seq):
    H = POOL_HALO
    G = POOL_GROUP_DIM
    t0 = lax.rem(pl.program_id(0) * tm, seq)
    u = u_ref[...].astype(F32)
    xa[H:H + tm, :] = u
    xa[0:H, :] = jnp.where(t0 == 0, 0.0, uh_ref[...].astype(F32))
    n = tm + H - 8
    xb[8:8 + n, :] = xa[8:8 + n, :] + xa[7:7 + n, :]
    n = tm + H - 16
    xa[16:16 + n, G:] = xb[16:16 + n, G:] + xb[14:14 + n, G:]
    n = tm + H - 24
    xb[24:24 + n, 2 * G:] = xa[24:24 + n, 2 * G:] + xa[20:20 + n, 2 * G:]
    xa[H:H + tm, 3 * G:] = xb[H:H + tm, 3 * G:] + xb[H - 8:H - 8 + tm, 3 * G:]
    tpos = t0 + lax.broadcasted_iota(jnp.int32, (tm, 1), 0)
    for g, win in enumerate(POOL_WINDOWS):
        src = (xb, xa, xb, xa)[g]
        cols = slice(g * G, (g + 1) * G)
        cnt = jnp.minimum(tpos + 1, win).astype(F32)
        y = src[H:H + tm, cols] / cnt - u[:, cols]
        yp = _dot(y.astype(BF16), wpool_ref[g]) * ps_ref[:, cols]
        yp_ref[:, cols] = yp.astype(BF16)
    hm = tm // ROW_SPLIT
    for r0 in range(0, tm, hm):
        rows = slice(r0, r0 + hm)
        pb = _dot(yp_ref[rows, :], wp_ref[...])
        a = _dot(hg_ref[rows, :], wm_ref[...])
        merged = (jax.nn.sigmoid(gm_ref[rows, :].astype(F32)) * a
                  + jax.nn.sigmoid(gp_ref[rows, :].astype(F32)) * pb)
        out_ref[rows, :] = merged.astype(BF16)


def _branch_side_kernel(*refs, n_side, tm, seq):
    side_in = refs[9:9 + n_side]
    side_out = refs[10 + n_side:10 + 2 * n_side]
    _branch_kernel(*refs[:9], refs[9 + n_side], *refs[10 + 2 * n_side:], tm=tm, seq=seq)
    _side_cast(side_in, side_out)


def _branch(hg2d, z_main, w_pool, pool_scale, w_m_br, w_p_br, seq, tm, side):
    t = hg2d.shape[0]
    d = w_m_br.shape[1]
    hb = tm // POOL_HALO
    const = lambda shape: pl.BlockSpec(shape, lambda i: (0,) * len(shape))
    side_specs = _side_specs(side, t // tm, lambda i: i)
    return pl.pallas_call(
        functools.partial(_branch_side_kernel, n_side=len(side), tm=tm, seq=seq),
        grid=(t // tm,),
        in_specs=[pl.BlockSpec((tm, M_V), lambda i: (i, 0)),
                  pl.BlockSpec((tm, POOL_W), lambda i: (i, 3)),
                  pl.BlockSpec((POOL_HALO, POOL_W), lambda i: (jnp.maximum(i * hb - 1, 0), 3)),
                  pl.BlockSpec((tm, d), lambda i: (i, 2)),
                  pl.BlockSpec((tm, d), lambda i: (i, 3)),
                  const(w_pool.shape), const((1, POOL_W)), const(w_m_br.shape), const(w_p_br.shape)] + side_specs,
        out_specs=[pl.BlockSpec((tm, d), lambda i: (i, 0))] + side_specs,
        out_shape=[jax.ShapeDtypeStruct((t, d), BF16)] + _side_shapes(side),
        scratch_shapes=[pltpu.VMEM((tm + POOL_HALO, POOL_W), F32),
                        pltpu.VMEM((tm + POOL_HALO, POOL_W), F32),
                        pltpu.VMEM((tm, POOL_W), BF16)],
        compiler_params=_params(1),
        name="branch",
    )(hg2d, z_main, z_main, z_main, z_main, w_pool, pool_scale, w_m_br, w_p_br, *side)


def _route_kernel(mg_ref, x_ref, wout_ref, g_ref, b_ref, wr_ref, br_ref, upper_ref,
                  x1_ref, route_ref, gate_ref, cnt_ref, carry, *, alpha, tm):
    @pl.when(pl.program_id(0) == 0)
    def _():
        carry[...] = jnp.zeros_like(carry)

    hm = tm // ROW_SPLIT
    pitch = _pitch(x_ref.shape[1])
    logits = []
    for r0 in range(0, tm, hm):
        r = alpha * x_ref[r0:r0 + hm, :] + _dot(mg_ref[r0:r0 + hm, :], wout_ref[...])
        x1 = _layer_norm(r, g_ref[...], b_ref[...])
        _store_token_major(x1_ref.at[r0 * pitch:(r0 + hm) * pitch], x1)
        logits.append(_dot(x1.astype(BF16), wr_ref[...]) + br_ref[...])
    lt = jnp.concatenate(logits, axis=0).T

    sub = lax.broadcasted_iota(jnp.int32, (SUBLANES, tm), 0).astype(F32)
    none = float(SUBLANES)

    def softmax(z):
        e = jnp.exp(z - jnp.max(z, axis=0, keepdims=True))
        return e / jnp.sum(e, axis=0, keepdims=True)

    def top1(vals):
        top = jnp.max(vals, axis=0, keepdims=True)
        return top, jnp.min(jnp.where(vals == top, sub, none), axis=0, keepdims=True)

    is_grp = sub < N_GROUPS
    pg = jnp.where(is_grp, softmax(jnp.where(is_grp, lt[0:SUBLANES, :], -jnp.inf)), -1.0)
    pg_top, g_idx = top1(pg)
    le_sel = lt[SUBLANES:2 * SUBLANES, :]
    for g in range(1, N_GROUPS):
        le_sel = jnp.where(g_idx == g, lt[(g + 1) * SUBLANES:(g + 2) * SUBLANES, :], le_sel)
    pe = softmax(le_sel)
    pe1, i1 = top1(pe)
    pe2, i2 = top1(jnp.where(sub == i1, -1.0, pe))
    den = pe1 + pe2
    gate1 = pg_top * pe1 / den
    gate2 = pg_top * pe2 / den
    e1 = g_idx * EXPERTS_PER_GROUP + i1
    e2 = g_idx * EXPERTS_PER_GROUP + i2

    eid = lax.broadcasted_iota(jnp.int32, (N_EXPERTS, tm), 0).astype(F32)
    hit1 = eid == e1
    hit2 = eid == e2
    onehot = jnp.where(hit1 | hit2, 1.0, 0.0)
    incl = _dot(onehot.astype(BF16), upper_ref[...])
    excl = incl - onehot + carry[:, 0:1]
    r1 = jnp.sum(jnp.where(hit1, excl, 0.0), axis=0, keepdims=True)
    r2 = jnp.sum(jnp.where(hit2, excl, 0.0), axis=0, keepdims=True)
    carry[...] = carry[...] + jnp.broadcast_to(incl[:, tm - 1:tm], carry.shape)
    cnt_ref[...] = carry[...]

    route = jnp.where(sub == 0, e1, jnp.where(sub == 1, e2, jnp.where(sub == 2, r1, jnp.where(sub == 3, r2, 0.0))))
    route_ref[...] = route.astype(jnp.int32)
    gates = jnp.where(sub == 0, gate1, jnp.where(sub == 1, gate2, 0.0))
    gate_ref[...] = jnp.concatenate([gates, jnp.zeros((LANES - SUBLANES, tm), F32)], axis=0).T


def _route_side_kernel(*refs, n_side, alpha, tm):
    side_in = refs[8:8 + n_side]
    side_out = refs[12 + n_side:12 + 2 * n_side]
    _route_kernel(*refs[:8], *refs[8 + n_side:12 + n_side], *refs[12 + 2 * n_side:], alpha=alpha, tm=tm)
    _side_cast(side_in, side_out)


def _route(merged, x2d, w_out, ln_g, ln_b, w_r, b_r, upper, alpha, tm, side):
    t, d = x2d.shape
    const = lambda shape: pl.BlockSpec(shape, lambda i: (0,) * len(shape))
    resident = lambda shape: pl.BlockSpec(shape, lambda i: (0,) * len(shape), pipeline_mode=pl.Buffered(1))
    rows = lambda width: pl.BlockSpec((tm, width), lambda i: (i, 0))
    cols = pl.BlockSpec((SUBLANES, tm), lambda i: (0, i))
    side_specs = _side_specs(side, t // tm, lambda i: i)
    return pl.pallas_call(
        functools.partial(_route_side_kernel, n_side=len(side), alpha=alpha, tm=tm),
        grid=(t // tm,),
        in_specs=[rows(d), rows(d), resident(w_out.shape), const((1, d)), const((1, d)),
                  const(w_r.shape), const((1, LANES)), const((tm, tm))] + side_specs,
        out_specs=[pl.BlockSpec((tm * _pitch(d), LANES), lambda i: (i, 0)),
                   cols, rows(LANES), const((N_EXPERTS, LANES))] + side_specs,
        out_shape=[jax.ShapeDtypeStruct((t * _pitch(d), LANES), F32),
                   jax.ShapeDtypeStruct((SUBLANES, t), jnp.int32),
                   jax.ShapeDtypeStruct((t, LANES), F32),
                   jax.ShapeDtypeStruct((N_EXPERTS, LANES), F32)] + _side_shapes(side),
        scratch_shapes=[pltpu.VMEM((N_EXPERTS, LANES), F32)],
        compiler_params=_params(1),
        name="route",
    )(merged, x2d, w_out, ln_g, ln_b, w_r, b_r, upper, *side)


def _dispatch_kernel(pad_start_ref, pad_len_ref, nu_ref, dest_ref, x_ref, xs_hbm, sem, *, tb, te, d):
    def copy(src_tok, dst_tok, n=1):
        return pltpu.make_async_copy(_tokens(x_ref, src_tok, d, n), _tokens(xs_hbm, dst_tok, d, n), sem)

    def wait_tokens(n):
        def body(r, c):
            copy(0, 0, n).wait()
            return c
        return body

    @pl.when(pl.program_id(0) == 0)
    def _():
        def per_expert(e, total):
            start = pad_start_ref[e]
            npad = pad_len_ref[e]

            def fill(r, c):
                copy(0, start + r).start()
                return c

            lax.fori_loop(0, npad, fill, 0)
            return total + npad

        total = lax.fori_loop(0, N_EXPERTS, per_expert, 0)
        lax.fori_loop(0, total // SUBLANES, wait_tokens(SUBLANES), 0)
        lax.fori_loop(0, total % SUBLANES, wait_tokens(1), 0)

        def fill_tile(j, c):
            copy(0, j * te, te).start()
            return c

        n_tiles = xs_hbm.shape[0] // (te * _pitch(d))
        lax.fori_loop(nu_ref[0], n_tiles, fill_tile, 0)
        lax.fori_loop(nu_ref[0], n_tiles, wait_tokens(te), 0)

    def scatter(t, c):
        for k in range(TOP_K):
            copy(t, dest_ref[k, t]).start(priority=k % 2)
        return c

    lax.fori_loop(0, tb, scatter, 0, unroll=DMA_UNROLL)
    for k in range(TOP_K):
        copy(0, 0, tb).wait()


def _dispatch(x1t, dest, pad_start, pad_len, n_used, n_slots, tb, te, d):
    pitch = _pitch(d)
    t = x1t.shape[0] // pitch
    grid_spec = pltpu.PrefetchScalarGridSpec(
        num_scalar_prefetch=3,
        grid=(t // tb,),
        in_specs=[pl.BlockSpec((TOP_K, tb), lambda i, ps, pn, nu: (0, i), memory_space=pltpu.SMEM),
                  pl.BlockSpec((tb * pitch, LANES), lambda i, ps, pn, nu: (i, 0))],
        out_specs=pl.BlockSpec(memory_space=pl.ANY),
        scratch_shapes=[pltpu.SemaphoreType.DMA(())],
    )
    return pl.pallas_call(
        functools.partial(_dispatch_kernel, tb=tb, te=te, d=d),
        grid_spec=grid_spec,
        out_shape=jax.ShapeDtypeStruct((n_slots * pitch, LANES), x1t.dtype),
        compiler_params=_params(1),
        name="dispatch",
    )(pad_start, pad_len, n_used, dest, x1t)


def _expert_kernel(be_ref, nu_ref, x_ref, wg_ref, wu_ref, wd_ref, y_ref, *, te, d):
    i = pl.program_id(0)

    @pl.when(i < nu_ref[0])
    def _():
        xb = _load_token_major(x_ref, te, d).astype(BF16)
        gt = _dot(xb, wg_ref[0])
        up = _dot(xb, wu_ref[0])
        hid = gt * jax.nn.sigmoid(gt) * up
        _store_token_major(y_ref, _dot(hid.astype(BF16), wd_ref[0]))

    @pl.when(i >= nu_ref[0])
    def _():
        y_ref[...] = jnp.zeros_like(y_ref)


def _experts(xs, block_e, n_used, w_gate, w_up, w_down, te):
    _, d, de = w_gate.shape
    pitch = _pitch(d)
    n_tiles = xs.shape[0] // (te * pitch)
    grid_spec = pltpu.PrefetchScalarGridSpec(
        num_scalar_prefetch=2,
        grid=(n_tiles,),
        in_specs=[pl.BlockSpec((te * pitch, LANES), lambda i, be, nu: (jnp.minimum(i, nu[0] - 1), 0)),
                  pl.BlockSpec((1, d, de), lambda i, be, nu: (be[i], 0, 0)),
                  pl.BlockSpec((1, d, de), lambda i, be, nu: (be[i], 0, 0)),
                  pl.BlockSpec((1, de, d), lambda i, be, nu: (be[i], 0, 0))],
        out_specs=pl.BlockSpec((te * pitch, LANES), lambda i, be, nu: (i, 0)),
    )
    return pl.pallas_call(
        functools.partial(_expert_kernel, te=te, d=d),
        grid_spec=grid_spec,
        out_shape=jax.ShapeDtypeStruct(xs.shape, F32),
        compiler_params=_params(1),
        name="experts",
    )(block_e, n_used, xs, w_gate, w_up, w_down)


def _final_kernel(dcur_ref, dnxt_ref, x1_ref, gate_ref, p_ref, ys_hbm, g_ref, b_ref, wg_ref, bg_ref, wp_ref,
                  out_ref, ybuf, sem, *, alpha, tm, d):
    i = pl.program_id(0)
    slot = lax.rem(i, 2)

    def issue(dest_ref, s):
        def body(t, c):
            for k in range(TOP_K):
                pltpu.make_async_copy(_token_data(ys_hbm, dest_ref[k, t], d),
                                      _token_data(ybuf.at[s, k], t, d), sem.at[s]).start(priority=k % 2)
            return c
        lax.fori_loop(0, tm, body, 0, unroll=DMA_UNROLL)

    @pl.when(i == 0)
    def _():
        issue(dcur_ref, 0)

    @pl.when(i + 1 < pl.num_programs(0))
    def _():
        issue(dnxt_ref, 1 - slot)

    for k in range(TOP_K):
        rows = tm * (d // LANES)
        pltpu.make_async_copy(ys_hbm.at[pl.ds(0, rows)], ybuf.at[slot, k, pl.ds(0, rows)], sem.at[slot]).wait()

    hm = tm // ROW_SPLIT
    for r0 in range(0, tm, hm):
        rows = slice(r0, r0 + hm)
        trows = pl.ds(r0 * _pitch(d), hm * _pitch(d))
        pp = _dot(p_ref[rows, :].astype(BF16), wp_ref[...])
        gate = gate_ref[rows, :]
        y = (gate[:, 0:1] * _load_token_major(ybuf.at[slot, 0, trows], hm, d)
             + gate[:, 1:2] * _load_token_major(ybuf.at[slot, 1, trows], hm, d))
        x2 = _layer_norm(alpha * _load_token_major(x1_ref.at[trows], hm, d) + y, g_ref[...], b_ref[...])
        gl = _dot(x2.astype(BF16), wg_ref[...]) + bg_ref[...]
        out_ref[rows, :] = x2 + jax.nn.sigmoid(gl) * pp


def _final(dest, x1t, gate, p2d, ys, ln_g, ln_b, w_pg, b_pg, w_pp, alpha, tm):
    d = w_pg.shape[0]
    pitch = _pitch(d)
    t = x1t.shape[0] // pitch
    n = t // tm
    const = lambda shape: pl.BlockSpec(shape, lambda i: (0,) * len(shape))
    resident = lambda shape: pl.BlockSpec(shape, lambda i: (0,) * len(shape), pipeline_mode=pl.Buffered(1))
    rows = lambda width: pl.BlockSpec((tm, width), lambda i: (i, 0))
    return pl.pallas_call(
        functools.partial(_final_kernel, alpha=alpha, tm=tm, d=d),
        grid=(n,),
        in_specs=[pl.BlockSpec((TOP_K, tm), lambda i: (0, i), memory_space=pltpu.SMEM),
                  pl.BlockSpec((TOP_K, tm), lambda i: (0, jnp.minimum(i + 1, n - 1)), memory_space=pltpu.SMEM),
                  pl.BlockSpec((tm * pitch, LANES), lambda i: (i, 0)),
                  rows(LANES), rows(p2d.shape[1]),
                  pl.BlockSpec(memory_space=pl.ANY),
                  const((1, d)), const((1, d)), resident(w_pg.shape), const((1, d)), resident(w_pp.shape)],
        out_specs=rows(d),
        out_shape=jax.ShapeDtypeStruct((t, d), F32),
        scratch_shapes=[pltpu.VMEM((2, TOP_K, tm * pitch, LANES), F32), pltpu.SemaphoreType.DMA((2,))],
        compiler_params=_params(1),
        name="final",
    )(dest, dest, x1t, gate, p2d, ys, ln_g, ln_b, w_pg, b_pg, w_pp)


def _pad_cols(a, width):
    return jnp.pad(a, ((0, 0), (0, width - a.shape[1])))


def _tri(n):
    return jnp.tril(jnp.ones((n, n), BF16))


def _layer(x, p, w_in_t, b_in, conv_w, conv_b, mh_g, w_pool, pool_scale, w_m_br, w_p_br, w_out,
           ln1_g, ln1_b, w_rg, b_rg, w_re, b_re, w_gate, w_up, w_down, ln2_g, ln2_b,
           w_ple_gate, b_ple_gate, w_ple_proj, alpha):
    nb, seq, d = x.shape
    t = nb * seq
    x2d = x.reshape(t, d)
    row = lambda a: a.reshape(1, -1)

    c_if = 2 * M_QK + 2 * M_V
    w_main, w_if = _regroup(w_in_t, c_if, 2 * M_HEADS, 1024, min(1024, d))
    b_main = row(jnp.concatenate([b_in[:c_if], b_in[c_if + 2 * M_HEADS:]]))
    b_if = _pad_cols(row(b_in[c_if:c_if + 2 * M_HEADS]), LANES)

    n_exp, _, d_exp = w_gate.shape
    tm_in = min(1024, t)
    z_main, z_if = _inproj(x2d, w_main, b_main, w_if, b_if, tm_in, 2048, [])

    dense_w = [w_pool, w_m_br, w_p_br, w_out, w_ple_gate, w_ple_proj, w_down.reshape(n_exp * d_exp, d)]
    hg, *dense_b = _mlstm(z_main.reshape(nb, seq, -1), z_if.reshape(nb, seq, LANES),
                          conv_w, row(conv_b), row(mh_g), [a.reshape(-1, a.shape[-1]) for a in dense_w])
    w_pool, w_m_br, w_p_br, w_out, w_ple_gate, w_ple_proj, wd_b = [
        b.reshape(a.shape) for a, b in zip(dense_w, dense_b)]

    tm = min(512, seq)
    merged, wu_b = _branch(hg.reshape(t, M_V), z_main, w_pool, row(pool_scale), w_m_br, w_p_br, seq, tm,
                           [w_up.reshape(n_exp * d, d_exp)])

    w_r = _pad_cols(jnp.concatenate([_pad_cols(w_rg, SUBLANES), w_re], axis=1), LANES).astype(BF16)
    b_r = _pad_cols(jnp.concatenate([_pad_cols(row(b_rg), SUBLANES), row(b_re)], axis=1), LANES)
    x1, route, gate, cnt, wg_b = _route(merged, x2d, w_out, row(ln1_g), row(ln1_b),
                                          w_r, b_r, _tri(tm).T, alpha, tm, [w_gate.reshape(n_exp * d, d_exp)])

    te = EXPERT_TILE
    counts = cnt[:, 0].astype(jnp.int32)
    pcounts = (counts + te - 1) // te * te
    pends = jnp.cumsum(pcounts)
    pstarts = pends - pcounts
    e_sel = route[0:TOP_K, :, None] == jnp.arange(N_EXPERTS, dtype=jnp.int32)
    dest = jnp.sum(jnp.where(e_sel, pstarts, 0), axis=-1) + route[TOP_K:2 * TOP_K]
    n_slots = t * TOP_K + N_EXPERTS * te
    n_tiles = n_slots // te
    n_used = (pends[-1] // te).reshape(1)
    tile_row = jnp.minimum(jnp.arange(n_tiles, dtype=jnp.int32), n_used - 1) * te
    block_e = jnp.minimum(jnp.sum((tile_row[:, None] >= pends[None, :]).astype(jnp.int32), axis=1), N_EXPERTS - 1)

    xs = _dispatch(x1, dest, pstarts + counts, pcounts - counts, n_used, n_slots, min(512, t), te, d)
    ys = _experts(xs, block_e, n_used, wg_b.reshape(n_exp, d, d_exp), wu_b.reshape(n_exp, d, d_exp),
                  wd_b.reshape(n_exp, d_exp, d), te)
    return _final(dest, x1, gate, p.reshape(t, -1), ys, row(ln2_g), row(ln2_b),
                  w_ple_gate, row(b_ple_gate), w_ple_proj, alpha,
                  min(512, t)).reshape(nb, seq, d)


def kernel(x, p, w_in, b_in, conv_w, conv_b, mh_g, w_pool, pool_scale, w_m_br, w_p_br, w_out, ln1_g, ln1_b, w_rg, b_rg, w_re, b_re, w_gate, w_up, w_down, ln2_g, ln2_b, w_ple_gate, b_ple_gate, w_ple_proj):
    depth = w_in.shape[0]
    alpha = (2 * depth) ** 0.25
    for i in range(depth):
        x = _layer(x, p[i], w_in[i].T, b_in[i], conv_w[i], conv_b[i], mh_g[i], w_pool[i], pool_scale[i],
                   w_m_br[i], w_p_br[i], w_out[i], ln1_g[i], ln1_b[i], w_rg[i], b_rg[i], w_re[i], b_re[i],
                   w_gate[i], w_up[i], w_down[i], ln2_g[i], ln2_b[i], w_ple_gate[i], b_ple_gate[i],
                   w_ple_proj[i], alpha)
    return x
```

```python
import functools

import jax
import jax.numpy as jnp
from jax import lax
from jax.experimental import pallas as pl
from jax.experimental.pallas import tpu as pltpu

F32 = jnp.float32
BF16 = jnp.bfloat16

M_HEADS = 4
M_QK_DIM = 128
M_V_DIM = 256
M_QK = M_HEADS * M_QK_DIM
M_V = M_HEADS * M_V_DIM
CONV_W = 4
CHUNK = 128
POOL_WINDOWS = (2, 4, 8, 16)
POOL_GROUP_DIM = 256
POOL_W = len(POOL_WINDOWS) * POOL_GROUP_DIM
N_GROUPS = 4
EXPERTS_PER_GROUP = 8
N_EXPERTS = N_GROUPS * EXPERTS_PER_GROUP
TOP_K = 2
LN_EPS = 1e-5

LANES = 128
SUBLANES = 8
VMEM_LIMIT = 56 * 1024 * 1024
BF16_SUBLANES = 16
CONV_HALO = BF16_SUBLANES
POOL_HALO = 32
EXPERT_TILE = 256
DISPATCH_TILE = 2048
MLSTM_GROUP = 4
DMA_UNROLL = 8
ROW_SPLIT = 2


def _dot(a, b):
    return jnp.dot(a, b, preferred_element_type=F32)


def _params(n_grid):
    return pltpu.CompilerParams(dimension_semantics=("arbitrary",) * n_grid,
                                vmem_limit_bytes=VMEM_LIMIT)


def _log_sigmoid(x):
    return -(jnp.maximum(-x, 0.0) + jnp.log1p(jnp.exp(-jnp.abs(x))))


def _layer_norm(r, g, b):
    mu = jnp.mean(r, axis=-1, keepdims=True)
    d = r - mu
    var = jnp.mean(d * d, axis=-1, keepdims=True)
    return d * lax.rsqrt(var + LN_EPS) * g + b


def _pitch(d):
    return d // LANES + 1


def _store_token_major(ref, val):
    n, d = val.shape
    for c in range(d // LANES):
        ref[pl.ds(c, n, stride=_pitch(d)), :] = val[:, c * LANES:(c + 1) * LANES]
    ref[pl.ds(d // LANES, n, stride=_pitch(d)), :] = jnp.zeros((n, LANES), val.dtype)


def _load_token_major(ref, n, d):
    return jnp.concatenate([ref[pl.ds(c, n, stride=_pitch(d)), :] for c in range(d // LANES)], axis=1)


def _tokens(ref, tok, d, n=1):
    return ref.at[pl.ds(tok * _pitch(d), n * _pitch(d))]


def _token_data(ref, tok, d):
    return ref.at[pl.ds(tok * _pitch(d), d // LANES)]


def _side_specs(side, steps, step_of):
    def spec(a):
        rb = max(BF16_SUBLANES, a.shape[0] // steps)
        return pl.BlockSpec((rb, a.shape[1]), lambda *g: (jnp.minimum(step_of(*g), a.shape[0] // rb - 1), 0))
    return [spec(a) for a in side]


def _side_shapes(side):
    return [jax.ShapeDtypeStruct(a.shape, BF16) for a in side]


def _side_cast(side_in, side_out):
    for src, dst in zip(side_in, side_out):
        dst[...] = src[...].astype(BF16)


def _regroup_kernel(a_ref, c_ref, out_ref, cut_ref):
    out_ref[...] = a_ref[...].T.astype(BF16)
    skip, tk = c_ref.shape
    cut_ref[...] = jnp.concatenate([c_ref[...], jnp.zeros((LANES - skip, tk), F32)], axis=0).T


def _regroup(w_t, cut, skip, tn, tk):
    n, k = w_t.shape
    n_keep = cut // tn
    assert skip % SUBLANES == 0
    first_row = lambda j: pl.multiple_of(jnp.where(j < n_keep, j * tn, j * tn + skip), SUBLANES)
    return pl.pallas_call(
        _regroup_kernel,
        grid=(k // tk, (n - skip) // tn),
        in_specs=[pl.BlockSpec((pl.Element(tn), pl.Element(tk)), lambda kk, j: (first_row(j), kk * tk)),
                  pl.BlockSpec((pl.Element(skip), pl.Element(tk)), lambda kk, j: (cut, kk * tk))],
        out_specs=[pl.BlockSpec((tk, tn), lambda kk, j: (kk, j)),
                   pl.BlockSpec((tk, LANES), lambda kk, j: (kk, 0))],
        out_shape=[jax.ShapeDtypeStruct((k, n - skip), BF16), jax.ShapeDtypeStruct((k, LANES), F32)],
        compiler_params=_params(2),
        name="regroup",
    )(w_t, w_t)


def _inproj_kernel(*refs, n_side):
    x_ref, w_ref, b_ref, wif_ref, bif_ref = refs[:5]
    side_in = refs[5:5 + n_side]
    z_ref, zif_ref = refs[5 + n_side:7 + n_side]
    side_out = refs[7 + n_side:7 + 2 * n_side]
    xb_ref = refs[7 + 2 * n_side]

    @pl.when(pl.program_id(1) == 0)
    def _():
        xb = x_ref[...].astype(BF16)
        xb_ref[...] = xb
        zif_ref[...] = _dot(xb, wif_ref[...].astype(BF16)) + bif_ref[...]

    z_ref[...] = (_dot(xb_ref[...], w_ref[...]) + b_ref[...]).astype(BF16)
    _side_cast(side_in, side_out)


def _inproj(x2d, w_main, b_main, w_if, b_if, tm, tn, side):
    t, d = x2d.shape
    n = w_main.shape[1]
    nj = n // tn
    side_specs = _side_specs(side, (t // tm) * nj, lambda i, j: i * nj + j)
    return pl.pallas_call(
        functools.partial(_inproj_kernel, n_side=len(side)),
        grid=(t // tm, nj),
        in_specs=[pl.BlockSpec((tm, d), lambda i, j: (i, 0)),
                  pl.BlockSpec((d, tn), lambda i, j: (0, j)),
                  pl.BlockSpec((1, tn), lambda i, j: (0, j)),
                  pl.BlockSpec((d, LANES), lambda i, j: (0, 0)),
                  pl.BlockSpec((1, LANES), lambda i, j: (0, 0))] + side_specs,
        out_specs=[pl.BlockSpec((tm, tn), lambda i, j: (i, j)),
                   pl.BlockSpec((tm, LANES), lambda i, j: (i, 0))] + side_specs,
        out_shape=[jax.ShapeDtypeStruct((t, n), BF16),
                   jax.ShapeDtypeStruct((t, LANES), F32)] + _side_shapes(side),
        scratch_shapes=[pltpu.VMEM((tm, d), BF16)],
        compiler_params=_params(2),
        name="inproj",
    )(x2d, w_main, b_main, w_if, b_if, *side)


def _mlstm_kernel(*refs, nb, n_side):
    qk_ref, v_ref, o_ref, g_ref, cw_ref, cb_ref, mhg_ref = refs[:7]
    side_in = refs[7:7 + n_side]
    out_ref = refs[7 + n_side]
    side_out = refs[8 + n_side:8 + 2 * n_side]
    cbuf = refs[8 + 2 * n_side]
    state = refs[9 + 2 * n_side:]
    _mlstm_body(qk_ref, v_ref, o_ref, g_ref, cw_ref, cb_ref, mhg_ref, out_ref, cbuf, state, nb)
    _side_cast(side_in, side_out)


def _mlstm_body(qk_ref, v_ref, o_ref, g_ref, cw_ref, cb_ref, mhg_ref, out_ref, cbuf, state, nb):
    L = CHUNK
    halo = CONV_HALO
    ct_refs, m_ref = state[:M_HEADS], state[M_HEADS]

    @pl.when(pl.program_id(0) == 0)
    def _():
        cbuf[:, 0:halo, :] = jnp.zeros((nb, halo, 2 * M_QK), BF16)
        for ref in state:
            ref[...] = jnp.zeros_like(ref)

    cbuf[:, halo:halo + L, :] = qk_ref[...]
    scale = M_QK_DIM ** -0.5
    row = lax.broadcasted_iota(jnp.int32, (L, L), 0)
    col = lax.broadcasted_iota(jnp.int32, (L, L), 1)
    causal = col <= row
    upper = jnp.where(row <= col, 1.0, 0.0).astype(BF16)
    ones_blk = jnp.ones((L, LANES), BF16)
    lane8 = lax.broadcasted_iota(jnp.int32, (SUBLANES, L), 1)
    wrow = lax.broadcasted_iota(jnp.int32, (L, L + halo), 0)
    wcol = lax.broadcasted_iota(jnp.int32, (L, L + halo), 1)
    shifts = [jnp.where(wcol == wrow + (halo - (CONV_W - 1) + tap), 1.0, 0.0).astype(BF16)
              for tap in range(CONV_W - 1)]
    sub = lax.broadcasted_iota(jnp.int32, (SUBLANES, LANES), 0)

    def prep(b):
        window = cbuf[b]
        acc = cb_ref[...] + _dot(shifts[0], window) * cw_ref[0:1, :]
        for tap in range(1, CONV_W - 1):
            acc = acc + _dot(shifts[tap], window) * cw_ref[tap:tap + 1, :]
        acc = acc + qk_ref[b].astype(F32) * cw_ref[CONV_W - 1:CONV_W, :]
        qk = acc * jax.nn.sigmoid(acc)
        cbuf[b, 0:halo, :] = cbuf[b, L:L + halo, :]

        g_row = g_ref[b].T[0:SUBLANES, :]
        lf = _log_sigmoid(g_row)
        lf_hi = lf.astype(BF16)
        lf_lo = (lf - lf_hi.astype(F32)).astype(BF16)
        cs = _dot(jnp.concatenate([lf_hi, lf_lo], axis=0), upper)
        b_row = pltpu.roll(cs[0:SUBLANES, :] + cs[SUBLANES:2 * SUBLANES, :], M_HEADS, 0)
        a_row = g_row - b_row
        cm = a_row
        for sh in (1, 2, 4, 8, 16, 32, 64):
            cm = jnp.maximum(cm, jnp.where(lane8 >= sh, pltpu.roll(cm, sh, 1), -jnp.inf))
        m_prev = m_ref[b]
        mx = jnp.maximum(m_prev, cm)
        mx_last = jnp.maximum(m_prev, jnp.max(a_row, axis=1, keepdims=True))
        carry_in = jnp.exp(m_prev - mx)
        floor = jnp.exp(-(b_row + mx))
        wts = jnp.exp(a_row - mx_last)
        decay = jnp.exp(m_prev - mx_last)
        m_ref[b] = jnp.where(sub < M_HEADS, b_row[:, L - 1:L] + mx_last, 0.0)
        packed = jnp.concatenate(
            [jnp.where(sub < M_HEADS, mx, pltpu.roll(carry_in, M_HEADS, 0)),
             jnp.where(sub < M_HEADS, floor, pltpu.roll(wts, M_HEADS, 0)),
             jnp.zeros((L - 2 * SUBLANES, L), F32)], axis=0).T
        return qk, a_row, decay, packed

    def run_chains(batches):
        pre = {b: prep(b) for b in batches}
        chains = [(b, h) for b in batches for h in range(M_HEADS)]
        vcols = lambda h: slice(h * M_V_DIM, (h + 1) * M_V_DIM)
        col_of = lambda i, c: pre[c[0]][3][:, i * M_HEADS + c[1]:i * M_HEADS + c[1] + 1]
        q = {(b, h): pre[b][0][:, h * M_QK_DIM:(h + 1) * M_QK_DIM].astype(BF16) for b, h in chains}
        kf = {(b, h): pre[b][0][:, M_QK + h * M_QK_DIM:M_QK + (h + 1) * M_QK_DIM] * scale for b, h in chains}
        v_ext = {(b, h): jnp.concatenate([v_ref[b, :, vcols(h)], ones_blk], axis=1) for b, h in chains}
        ct = {(b, h): ct_refs[h][b] for b, h in chains}

        raw = {c: lax.dot_general(q[c], kf[c].astype(BF16), (((1,), (1,)), ((), ())),
                                  preferred_element_type=F32) for c in chains}
        prev = {c: _dot(q[c], ct[c].astype(BF16)) for c in chains}
        pmat = {(b, h): jnp.exp(jnp.where(causal, pre[b][1][h:h + 1, :] - col_of(0, (b, h)), -jnp.inf))
                for b, h in chains}
        s = {c: (raw[c] * pmat[c]).astype(BF16) for c in chains}
        numden = {c: _dot(s[c], v_ext[c]) + col_of(1, c) * prev[c] for c in chains}
        upd = {c: lax.dot_general((kf[c] * col_of(3, c)).astype(BF16), v_ext[c], (((0,), (0,)), ((), ())),
                                  preferred_element_type=F32) for c in chains}
        for b, h in chains:
            dec = jnp.concatenate([pre[b][2][h:h + 1, :]] * (ct[b, h].shape[1] // LANES), axis=1)
            ct_refs[h][b] = dec * ct[b, h] + upd[b, h]
        inv = {c: 1.0 / jnp.maximum(jnp.abs(numden[c][:, M_V_DIM:]), col_of(2, c)) for c in chains}
        hh = {c: numden[c][:, 0:M_V_DIM] * jnp.concatenate([inv[c]] * (M_V_DIM // LANES), axis=1) for c in chains}
        mu = {c: jnp.mean(hh[c], axis=-1, keepdims=True) for c in chains}
        dlt = {c: hh[c] - mu[c] for c in chains}
        var = {c: jnp.mean(dlt[c] * dlt[c], axis=-1, keepdims=True) for c in chains}
        for b, h in chains:
            hn = dlt[b, h] * lax.rsqrt(var[b, h] + LN_EPS) * mhg_ref[:, vcols(h)]
            og = jax.nn.sigmoid(o_ref[b, :, vcols(h)].astype(F32))
            out_ref[b, :, vcols(h)] = (hn * og).astype(BF16)

    for b0 in range(0, nb, MLSTM_GROUP):
        run_chains(range(b0, min(b0 + MLSTM_GROUP, nb)))


def _mlstm(z3, zif3, conv_w, conv_b, mh_g, side):
    nb, s, _ = z3.shape
    L = CHUNK
    blk = lambda colblk: pl.BlockSpec((nb, L, M_V), lambda c: (0, c, colblk))
    const = lambda shape: pl.BlockSpec(shape, lambda c: (0,) * len(shape))
    side_specs = _side_specs(side, s // L, lambda c: c)
    return pl.pallas_call(
        functools.partial(_mlstm_kernel, nb=nb, n_side=len(side)),
        grid=(s // L,),
        in_specs=[blk(0), blk(1), blk(2),
                  pl.BlockSpec((nb, L, LANES), lambda c: (0, c, 0)),
                  const((CONV_W, 2 * M_QK)), const((1, 2 * M_QK)), const((1, M_V))] + side_specs,
        out_specs=[pl.BlockSpec((nb, L, M_V), lambda c: (0, c, 0))] + side_specs,
        out_shape=[jax.ShapeDtypeStruct((nb, s, M_V), BF16)] + _side_shapes(side),
        scratch_shapes=([pltpu.VMEM((nb, L + CONV_HALO, 2 * M_QK), BF16)]
                        + [pltpu.VMEM((nb, M_QK_DIM, M_V_DIM + LANES), F32)] * M_HEADS
                        + [pltpu.VMEM((nb, SUBLANES, L), F32)]),
        compiler_params=_params(1),
        name="mlstm",
    )(z3, z3, z3, zif3, conv_w, conv_b, mh_g, *side)


def _branch_kernel(hg_ref, u_ref, uh_ref, gm_ref, gp_ref, wpool_ref, ps_ref, wm_ref, wp_ref,
                   out_ref, xa, xb, yp_ref, *, tm, seq):
    H = POOL_HALO
    G = POOL_GROUP_DIM
    t0 = lax.rem(pl.program_id(0) * tm, seq)
    u = u_ref[...].astype(F32)
    xa[H:H + tm, :] = u
    xa[0:H, :] = jnp.where(t0 == 0, 0.0, uh_ref[...].astype(F32))
    n = tm + H - 8
    xb[8:8 + n, :] = xa[8:8 + n, :] + xa[7:7 + n, :]
    n = tm + H - 16
    xa[16:16 + n, G:] = xb[16:16 + n, G:] + xb[14:14 + n, G:]
    n = tm + H - 24
    xb[24:24 + n, 2 * G:] = xa[24:24 + n, 2 * G:] + xa[20:20 + n, 2 * G:]
    xa[H:H + tm, 3 * G:] = xb[H:H + tm, 3 * G:] + xb[H - 8:H - 8 + tm, 3 * G:]
    tpos = t0 + lax.broadcasted_iota(jnp.int32, (tm, 1), 0)
    for g, win in enumerate(POOL_WINDOWS):
        src = (xb, xa, xb, xa)[g]
        cols = slice(g * G, (g + 1) * G)
        cnt = jnp.minimum(tpos + 1, win).astype(F32)
        y = src[H:H + tm, cols] / cnt - u[:, cols]
        yp = _dot(y.astype(BF16), wpool_ref[g]) * ps_ref[:, cols]
        yp_ref[:, cols] = yp.astype(BF16)
    hm = tm // ROW_SPLIT
    for r0 in range(0, tm, hm):
        rows = slice(r0, r0 + hm)
        pb = _dot(yp_ref[rows, :], wp_ref[...])
        a = _dot(hg_ref[rows, :], wm_ref[...])
        merged = (jax.nn.sigmoid(gm_ref[rows, :].astype(F32)) * a
                  + jax.nn.sigmoid(gp_ref[rows, :].astype(F32)) * pb)
        out_ref[rows, :] = merged.astype(BF16)


def _branch_side_kernel(*refs, n_side, tm, seq):
    side_in = refs[9:9 + n_side]
    side_out = refs[10 + n_side:10 + 2 * n_side]
    _branch_kernel(*refs[:9], refs[9 + n_side], *refs[10 + 2 * n_side:], tm=tm, seq=seq)
    _side_cast(side_in, side_out)


def _branch(hg2d, z_main, w_pool, pool_scale, w_m_br, w_p_br, seq, tm, side):
    t = hg2d.shape[0]
    d = w_m_br.shape[1]
    hb = tm // POOL_HALO
    const = lambda shape: pl.BlockSpec(shape, lambda i: (0,) * len(shape))
    side_specs = _side_specs(side, t // tm, lambda i: i)
    return pl.pallas_call(
        functools.partial(_branch_side_kernel, n_side=len(side), tm=tm, seq=seq),
        grid=(t // tm,),
        in_specs=[pl.BlockSpec((tm, M_V), lambda i: (i, 0)),
                  pl.BlockSpec((tm, POOL_W), lambda i: (i, 3)),
                  pl.BlockSpec((POOL_HALO, POOL_W), lambda i: (jnp.maximum(i * hb - 1, 0), 3)),
                  pl.BlockSpec((tm, d), lambda i: (i, 2)),
                  pl.BlockSpec((tm, d), lambda i: (i, 3)),
                  const(w_pool.shape), const((1, POOL_W)), const(w_m_br.shape), const(w_p_br.shape)] + side_specs,
        out_specs=[pl.BlockSpec((tm, d), lambda i: (i, 0))] + side_specs,
        out_shape=[jax.ShapeDtypeStruct((t, d), BF16)] + _side_shapes(side),
        scratch_shapes=[pltpu.VMEM((tm + POOL_HALO, POOL_W), F32),
                        pltpu.VMEM((tm + POOL_HALO, POOL_W), F32),
                        pltpu.VMEM((tm, POOL_W), BF16)],
        compiler_params=_params(1),
        name="branch",
    )(hg2d, z_main, z_main, z_main, z_main, w_pool, pool_scale, w_m_br, w_p_br, *side)


def _route_kernel(mg_ref, x_ref, wout_ref, g_ref, b_ref, wr_ref, br_ref, upper_ref,
                  x1_ref, route_ref, gate_ref, cnt_ref, carry, *, alpha, tm):
    @pl.when(pl.program_id(0) == 0)
    def _():
        carry[...] = jnp.zeros_like(carry)

    hm = tm // ROW_SPLIT
    pitch = _pitch(x_ref.shape[1])
    logits = []
    for r0 in range(0, tm, hm):
        r = alpha * x_ref[r0:r0 + hm, :] + _dot(mg_ref[r0:r0 + hm, :], wout_ref[...])
        x1 = _layer_norm(r, g_ref[...], b_ref[...])
        _store_token_major(x1_ref.at[r0 * pitch:(r0 + hm) * pitch], x1)
        logits.append(_dot(x1.astype(BF16), wr_ref[...]) + br_ref[...])
    lt = jnp.concatenate(logits, axis=0).T

    sub = lax.broadcasted_iota(jnp.int32, (SUBLANES, tm), 0).astype(F32)
    none = float(SUBLANES)

    def softmax(z):
        e = jnp.exp(z - jnp.max(z, axis=0, keepdims=True))
        return e / jnp.sum(e, axis=0, keepdims=True)

    def top1(vals):
        top = jnp.max(vals, axis=0, keepdims=True)
        return top, jnp.min(jnp.where(vals == top, sub, none), axis=0, keepdims=True)

    is_grp = sub < N_GROUPS
    pg = jnp.where(is_grp, softmax(jnp.where(is_grp, lt[0:SUBLANES, :], -jnp.inf)), -1.0)
    pg_top, g_idx = top1(pg)
    le_sel = lt[SUBLANES:2 * SUBLANES, :]
    for g in range(1, N_GROUPS):
        le_sel = jnp.where(g_idx == g, lt[(g + 1) * SUBLANES:(g + 2) * SUBLANES, :], le_sel)
    pe = softmax(le_sel)
    pe1, i1 = top1(pe)
    pe2, i2 = top1(jnp.where(sub == i1, -1.0, pe))
    den = pe1 + pe2
    gate1 = pg_top * pe1 / den
    gate2 = pg_top * pe2 / den
    e1 = g_idx * EXPERTS_PER_GROUP + i1
    e2 = g_idx * EXPERTS_PER_GROUP + i2

    eid = lax.broadcasted_iota(jnp.int32, (N_EXPERTS, tm), 0).astype(F32)
    hit1 = eid == e1
    hit2 = eid == e2
    onehot = jnp.where(hit1 | hit2, 1.0, 0.0)
    incl = _dot(onehot.astype(BF16), upper_ref[...])
    excl = incl - onehot + carry[:, 0:1]
    r1 = jnp.sum(jnp.where(hit1, excl, 0.0), axis=0, keepdims=True)
    r2 = jnp.sum(jnp.where(hit2, excl, 0.0), axis=0, keepdims=True)
    carry[...] = carry[...] + jnp.broadcast_to(incl[:, tm - 1:tm], carry.shape)
    cnt_ref[...] = carry[...]

    route = jnp.where(sub == 0, e1, jnp.where(sub == 1, e2, jnp.where(sub == 2, r1, jnp.where(sub == 3, r2, 0.0))))
    route_ref[...] = route.astype(jnp.int32)
    gates = jnp.where(sub == 0, gate1, jnp.where(sub == 1, gate2, 0.0))
    gate_ref[...] = jnp.concatenate([gates, jnp.zeros((LANES - SUBLANES, tm), F32)], axis=0).T


def _route_side_kernel(*refs, n_side, alpha, tm):
    side_in = refs[8:8 + n_side]
    side_out = refs[12 + n_side:12 + 2 * n_side]
    _route_kernel(*refs[:8], *refs[8 + n_side:12 + n_side], *refs[12 + 2 * n_side:], alpha=alpha, tm=tm)
    _side_cast(side_in, side_out)


def _route(merged, x2d, w_out, ln_g, ln_b, w_r, b_r, upper, alpha, tm, side):
    t, d = x2d.shape
    const = lambda shape: pl.BlockSpec(shape, lambda i: (0,) * len(shape))
    resident = lambda shape: pl.BlockSpec(shape, lambda i: (0,) * len(shape), pipeline_mode=pl.Buffered(1))
    rows = lambda width: pl.BlockSpec((tm, width), lambda i: (i, 0))
    cols = pl.BlockSpec((SUBLANES, tm), lambda i: (0, i))
    side_specs = _side_specs(side, t // tm, lambda i: i)
    return pl.pallas_call(
        functools.partial(_route_side_kernel, n_side=len(side), alpha=alpha, tm=tm),
        grid=(t // tm,),
        in_specs=[rows(d), rows(d), resident(w_out.shape), const((1, d)), const((1, d)),
                  const(w_r.shape), const((1, LANES)), const((tm, tm))] + side_specs,
        out_specs=[pl.BlockSpec((tm * _pitch(d), LANES), lambda i: (i, 0)),
                   cols, rows(LANES), const((N_EXPERTS, LANES))] + side_specs,
        out_shape=[jax.ShapeDtypeStruct((t * _pitch(d), LANES), F32),
                   jax.ShapeDtypeStruct((SUBLANES, t), jnp.int32),
                   jax.ShapeDtypeStruct((t, LANES), F32),
                   jax.ShapeDtypeStruct((N_EXPERTS, LANES), F32)] + _side_shapes(side),
        scratch_shapes=[pltpu.VMEM((N_EXPERTS, LANES), F32)],
        compiler_params=_params(1),
        name="route",
    )(merged, x2d, w_out, ln_g, ln_b, w_r, b_r, upper, *side)


def _dispatch_kernel(pad_start_ref, pad_len_ref, nu_ref, dest_ref, x_ref, xs_hbm, sem, *, tb, te, d):
    def copy(src_tok, dst_tok, n=1):
        return pltpu.make_async_copy(_tokens(x_ref, src_tok, d, n), _tokens(xs_hbm, dst_tok, d, n), sem)

    def wait_tokens(n):
        def body(r, c):
            copy(0, 0, n).wait()
            return c
        return body

    @pl.when(pl.program_id(0) == 0)
    def _():
        def per_expert(e, total):
            start = pad_start_ref[e]
            npad = pad_len_ref[e]

            def fill(r, c):
                copy(0, start + r).start()
                return c

            lax.fori_loop(0, npad, fill, 0)
            return total + npad

        total = lax.fori_loop(0, N_EXPERTS, per_expert, 0)
        lax.fori_loop(0, total // SUBLANES, wait_tokens(SUBLANES), 0)
        lax.fori_loop(0, total % SUBLANES, wait_tokens(1), 0)

        def fill_tile(j, c):
            copy(0, j * te, te).start()
            return c

        n_tiles = xs_hbm.shape[0] // (te * _pitch(d))
        lax.fori_loop(nu_ref[0], n_tiles, fill_tile, 0)
        lax.fori_loop(nu_ref[0], n_tiles, wait_tokens(te), 0)

    def scatter(t, c):
        for k in range(TOP_K):
            copy(t, dest_ref[k, t]).start(priority=k % 2)
        return c

    lax.fori_loop(0, tb, scatter, 0, unroll=DMA_UNROLL)
    for k in range(TOP_K):
        copy(0, 0, tb).wait()


def _dispatch(x1t, dest, pad_start, pad_len, n_used, n_slots, tb, te, d):
    pitch = _pitch(d)
    t = x1t.shape[0] // pitch
    grid_spec = pltpu.PrefetchScalarGridSpec(
        num_scalar_prefetch=3,
        grid=(t // tb,),
        in_specs=[pl.BlockSpec((TOP_K, tb), lambda i, ps, pn, nu: (0, i), memory_space=pltpu.SMEM),
                  pl.BlockSpec((tb * pitch, LANES), lambda i, ps, pn, nu: (i, 0))],
        out_specs=pl.BlockSpec(memory_space=pl.ANY),
        scratch_shapes=[pltpu.SemaphoreType.DMA(())],
    )
    return pl.pallas_call(
        functools.partial(_dispatch_kernel, tb=tb, te=te, d=d),
        grid_spec=grid_spec,
        out_shape=jax.ShapeDtypeStruct((n_slots * pitch, LANES), x1t.dtype),
        compiler_params=_params(1),
        name="dispatch",
    )(pad_start, pad_len, n_used, dest, x1t)


def _expert_kernel(be_ref, nu_ref, x_ref, wg_ref, wu_ref, wd_ref, y_ref, *, te, d):
    i = pl.program_id(0)

    @pl.when(i < nu_ref[0])
    def _():
        xb = _load_token_major(x_ref, te, d).astype(BF16)
        gt = _dot(xb, wg_ref[0])
        up = _dot(xb, wu_ref[0])
        hid = gt * jax.nn.sigmoid(gt) * up
        _store_token_major(y_ref, _dot(hid.astype(BF16), wd_ref[0]))

    @pl.when(i >= nu_ref[0])
    def _():
        y_ref[...] = jnp.zeros_like(y_ref)


def _experts(xs, block_e, n_used, w_gate, w_up, w_down, te):
    _, d, de = w_gate.shape
    pitch = _pitch(d)
    n_tiles = xs.shape[0] // (te * pitch)
    grid_spec = pltpu.PrefetchScalarGridSpec(
        num_scalar_prefetch=2,
        grid=(n_tiles,),
        in_specs=[pl.BlockSpec((te * pitch, LANES), lambda i, be, nu: (jnp.minimum(i, nu[0] - 1), 0)),
                  pl.BlockSpec((1, d, de), lambda i, be, nu: (be[i], 0, 0)),
                  pl.BlockSpec((1, d, de), lambda i, be, nu: (be[i], 0, 0)),
                  pl.BlockSpec((1, de, d), lambda i, be, nu: (be[i], 0, 0))],
        out_specs=pl.BlockSpec((te * pitch, LANES), lambda i, be, nu: (i, 0)),
    )
    return pl.pallas_call(
        functools.partial(_expert_kernel, te=te, d=d),
        grid_spec=grid_spec,
        out_shape=jax.ShapeDtypeStruct(xs.shape, F32),
        compiler_params=_params(1),
        name="experts",
    )(block_e, n_used, xs, w_gate, w_up, w_down)


def _final_kernel(dcur_ref, dnxt_ref, x1_ref, gate_ref, p_ref, ys_hbm, g_ref, b_ref, wg_ref, bg_ref, wp_ref,
                  out_ref, ybuf, sem, *, alpha, tm, d):
    i = pl.program_id(0)
    slot = lax.rem(i, 2)

    def issue(dest_ref, s):
        def body(t, c):
            for k in range(TOP_K):
                pltpu.make_async_copy(_token_data(ys_hbm, dest_ref[k, t], d),
                                      _token_data(ybuf.at[s, k], t, d), sem.at[s]).start(priority=k % 2)
            return c
        lax.fori_loop(0, tm, body, 0, unroll=DMA_UNROLL)

    @pl.when(i == 0)
    def _():
        issue(dcur_ref, 0)

    @pl.when(i + 1 < pl.num_programs(0))
    def _():
        issue(dnxt_ref, 1 - slot)

    for k in range(TOP_K):
        rows = tm * (d // LANES)
        pltpu.make_async_copy(ys_hbm.at[pl.ds(0, rows)], ybuf.at[slot, k, pl.ds(0, rows)], sem.at[slot]).wait()

    hm = tm // ROW_SPLIT
    for r0 in range(0, tm, hm):
        rows = slice(r0, r0 + hm)
        trows = pl.ds(r0 * _pitch(d), hm * _pitch(d))
        pp = _dot(p_ref[rows, :].astype(BF16), wp_ref[...])
        gate = gate_ref[rows, :]
        y = (gate[:, 0:1] * _load_token_major(ybuf.at[slot, 0, trows], hm, d)
             + gate[:, 1:2] * _load_token_major(ybuf.at[slot, 1, trows], hm, d))
        x2 = _layer_norm(alpha * _load_token_major(x1_ref.at[trows], hm, d) + y, g_ref[...], b_ref[...])
        gl = _dot(x2.astype(BF16), wg_ref[...]) + bg_ref[...]
        out_ref[rows, :] = x2 + jax.nn.sigmoid(gl) * pp


def _final(dest, x1t, gate, p2d, ys, ln_g, ln_b, w_pg, b_pg, w_pp, alpha, tm):
    d = w_pg.shape[0]
    pitch = _pitch(d)
    t = x1t.shape[0] // pitch
    n = t // tm
    const = lambda shape: pl.BlockSpec(shape, lambda i: (0,) * len(shape))
    resident = lambda shape: pl.BlockSpec(shape, lambda i: (0,) * len(shape), pipeline_mode=pl.Buffered(1))
    rows = lambda width: pl.BlockSpec((tm, width), lambda i: (i, 0))
    return pl.pallas_call(
        functools.partial(_final_kernel, alpha=alpha, tm=tm, d=d),
        grid=(n,),
        in_specs=[pl.BlockSpec((TOP_K, tm), lambda i: (0, i), memory_space=pltpu.SMEM),
                  pl.BlockSpec((TOP_K, tm), lambda i: (0, jnp.minimum(i + 1, n - 1)), memory_space=pltpu.SMEM),
                  pl.BlockSpec((tm * pitch, LANES), lambda i: (i, 0)),
                  rows(LANES), rows(p2d.shape[1]),
                  pl.BlockSpec(memory_space=pl.ANY),
                  const((1, d)), const((1, d)), resident(w_pg.shape), const((1, d)), resident(w_pp.shape)],
        out_specs=rows(d),
        out_shape=jax.ShapeDtypeStruct((t, d), F32),
        scratch_shapes=[pltpu.VMEM((2, TOP_K, tm * pitch, LANES), F32), pltpu.SemaphoreType.DMA((2,))],
        compiler_params=_params(1),
        name="final",
    )(dest, dest, x1t, gate, p2d, ys, ln_g, ln_b, w_pg, b_pg, w_pp)


def _pad_cols(a, width):
    return jnp.pad(a, ((0, 0), (0, width - a.shape[1])))


def _tri(n):
    return jnp.tril(jnp.ones((n, n), BF16))


def _layer(x, p, w_in_t, b_in, conv_w, conv_b, mh_g, w_pool, pool_scale, w_m_br, w_p_br, w_out,
           ln1_g, ln1_b, w_rg, b_rg, w_re, b_re, w_gate, w_up, w_down, ln2_g, ln2_b,
           w_ple_gate, b_ple_gate, w_ple_proj, alpha):
    nb, seq, d = x.shape
    t = nb * seq
    x2d = x.reshape(t, d)
    row = lambda a: a.reshape(1, -1)

    c_if = 2 * M_QK + 2 * M_V
    w_main, w_if = _regroup(w_in_t, c_if, 2 * M_HEADS, 1024, min(1024, d))
    b_main = row(jnp.concatenate([b_in[:c_if], b_in[c_if + 2 * M_HEADS:]]))
    b_if = _pad_cols(row(b_in[c_if:c_if + 2 * M_HEADS]), LANES)

    n_exp, _, d_exp = w_gate.shape
    tm_in = min(1024, t)
    z_main, z_if = _inproj(x2d, w_main, b_main, w_if, b_if, tm_in, 2048, [])

    dense_w = [w_pool, w_m_br, w_p_br, w_out, w_ple_gate, w_ple_proj, w_down.reshape(n_exp * d_exp, d)]
    hg, *dense_b = _mlstm(z_main.reshape(nb, seq, -1), z_if.reshape(nb, seq, LANES),
                          conv_w, row(conv_b), row(mh_g), [a.reshape(-1, a.shape[-1]) for a in dense_w])
    w_pool, w_m_br, w_p_br, w_out, w_ple_gate, w_ple_proj, wd_b = [
        b.reshape(a.shape) for a, b in zip(dense_w, dense_b)]

    tm = min(512, seq)
    merged, wu_b = _branch(hg.reshape(t, M_V), z_main, w_pool, row(pool_scale), w_m_br, w_p_br, seq, tm,
                           [w_up.reshape(n_exp * d, d_exp)])

    w_r = _pad_cols(jnp.concatenate([_pad_cols(w_rg, SUBLANES), w_re], axis=1), LANES).astype(BF16)
    b_r = _pad_cols(jnp.concatenate([_pad_cols(row(b_rg), SUBLANES), row(b_re)], axis=1), LANES)
    x1, route, gate, cnt, wg_b = _route(merged, x2d, w_out, row(ln1_g), row(ln1_b),
                                          w_r, b_r, _tri(tm).T, alpha, tm, [w_gate.reshape(n_exp * d, d_exp)])

    te = EXPERT_TILE
    counts = cnt[:, 0].astype(jnp.int32)
    pcounts = (counts + te - 1) // te * te
    pends = jnp.cumsum(pcounts)
    pstarts = pends - pcounts
    e_sel = route[0:TOP_K, :, None] == jnp.arange(N_EXPERTS, dtype=jnp.int32)
    dest = jnp.sum(jnp.where(e_sel, pstarts, 0), axis=-1) + route[TOP_K:2 * TOP_K]
    n_slots = t * TOP_K + N_EXPERTS * te
    n_tiles = n_slots // te
    n_used = (pends[-1] // te).reshape(1)
    tile_row = jnp.minimum(jnp.arange(n_tiles, dtype=jnp.int32), n_used - 1) * te
    block_e = jnp.minimum(jnp.sum((tile_row[:, None] >= pends[None, :]).astype(jnp.int32), axis=1), N_EXPERTS - 1)

    xs = _dispatch(x1, dest, pstarts + counts, pcounts - counts, n_used, n_slots, min(DISPATCH_TILE, t), te, d)
    ys = _experts(xs, block_e, n_used, wg_b.reshape(n_exp, d, d_exp), wu_b.reshape(n_exp, d, d_exp),
                  wd_b.reshape(n_exp, d_exp, d), te)
    return _final(dest, x1, gate, p.reshape(t, -1), ys, row(ln2_g), row(ln2_b),
                  w_ple_gate, row(b_ple_gate), w_ple_proj, alpha,
                  min(512, t)).reshape(nb, seq, d)


def kernel(x, p, w_in, b_in, conv_w, conv_b, mh_g, w_pool, pool_scale, w_m_br, w_p_br, w_out, ln1_g, ln1_b, w_rg, b_rg, w_re, b_re, w_gate, w_up, w_down, ln2_g, ln2_b, w_ple_gate, b_ple_gate, w_ple_proj):
    depth = w_in.shape[0]
    alpha = (2 * depth) ** 0.25
    for i in range(depth):
        x = _layer(x, p[i], w_in[i].T, b_in[i], conv_w[i], conv_b[i], mh_g[i], w_pool[i], pool_scale[i],
                   w_m_br[i], w_p_br[i], w_out[i], ln1_g[i], ln1_b[i], w_rg[i], b_rg[i], w_re[i], b_re[i],
                   w_gate[i], w_up[i], w_down[i], ln2_g[i], ln2_b[i], w_ple_gate[i], b_ple_gate[i],
                   w_ple_proj[i], alpha)
    return x
```

```python
import functools

import jax
import jax.numpy as jnp
from jax import lax
from jax.experimental import pallas as pl
from jax.experimental.pallas import tpu as pltpu

F32 = jnp.float32
BF16 = jnp.bfloat16

M_HEADS = 4
M_QK_DIM = 128
M_V_DIM = 256
M_QK = M_HEADS * M_QK_DIM
M_V = M_HEADS * M_V_DIM
CONV_W = 4
CHUNK = 128
POOL_WINDOWS = (2, 4, 8, 16)
POOL_GROUP_DIM = 256
POOL_W = len(POOL_WINDOWS) * POOL_GROUP_DIM
N_GROUPS = 4
EXPERTS_PER_GROUP = 8
N_EXPERTS = N_GROUPS * EXPERTS_PER_GROUP
TOP_K = 2
LN_EPS = 1e-5

LANES = 128
SUBLANES = 8
VMEM_LIMIT = 56 * 1024 * 1024
BF16_SUBLANES = 16
CONV_HALO = BF16_SUBLANES
POOL_HALO = 32
EXPERT_TILE = 256
DISPATCH_TILE = 2048
MLSTM_GROUP = 4
DMA_UNROLL = 8
ROW_SPLIT = 2


def _dot(a, b):
    return jnp.dot(a, b, preferred_element_type=F32)


def _params(n_grid):
    return pltpu.CompilerParams(dimension_semantics=("arbitrary",) * n_grid,
                                vmem_limit_bytes=VMEM_LIMIT)


def _log_sigmoid(x):
    return -(jnp.maximum(-x, 0.0) + jnp.log1p(jnp.exp(-jnp.abs(x))))


def _layer_norm(r, g, b):
    mu = jnp.mean(r, axis=-1, keepdims=True)
    d = r - mu
    var = jnp.mean(d * d, axis=-1, keepdims=True)
    return d * lax.rsqrt(var + LN_EPS) * g + b


def _pitch(d):
    return d // LANES + 1


def _store_token_major(ref, val):
    n, d = val.shape
    for c in range(d // LANES):
        ref[pl.ds(c, n, stride=_pitch(d)), :] = val[:, c * LANES:(c + 1) * LANES]
    ref[pl.ds(d // LANES, n, stride=_pitch(d)), :] = jnp.zeros((n, LANES), val.dtype)


def _load_token_major(ref, n, d):
    return jnp.concatenate([ref[pl.ds(c, n, stride=_pitch(d)), :] for c in range(d // LANES)], axis=1)


def _tokens(ref, tok, d, n=1):
    return ref.at[pl.ds(tok * _pitch(d), n * _pitch(d))]


def _token_data(ref, tok, d):
    return ref.at[pl.ds(tok * _pitch(d), d // LANES)]


def _side_specs(side, steps, step_of):
    def spec(a):
        rb = max(BF16_SUBLANES, a.shape[0] // steps)
        return pl.BlockSpec((rb, a.shape[1]), lambda *g: (jnp.minimum(step_of(*g), a.shape[0] // rb - 1), 0))
    return [spec(a) for a in side]


def _side_shapes(side):
    return [jax.ShapeDtypeStruct(a.shape, BF16) for a in side]


def _side_cast(side_in, side_out):
    for src, dst in zip(side_in, side_out):
        dst[...] = src[...].astype(BF16)


def _regroup_kernel(a_ref, c_ref, out_ref, cut_ref):
    out_ref[...] = a_ref[...].T.astype(BF16)
    skip, tk = c_ref.shape
    cut_ref[...] = jnp.concatenate([c_ref[...], jnp.zeros((LANES - skip, tk), F32)], axis=0).T


def _regroup(w_t, cut, skip, tn, tk):
    n, k = w_t.shape
    n_keep = cut // tn
    assert skip % SUBLANES == 0
    first_row = lambda j: pl.multiple_of(jnp.where(j < n_keep, j * tn, j * tn + skip), SUBLANES)
    return pl.pallas_call(
        _regroup_kernel,
        grid=(k // tk, (n - skip) // tn),
        in_specs=[pl.BlockSpec((pl.Element(tn), pl.Element(tk)), lambda kk, j: (first_row(j), kk * tk)),
                  pl.BlockSpec((pl.Element(skip), pl.Element(tk)), lambda kk, j: (cut, kk * tk))],
        out_specs=[pl.BlockSpec((tk, tn), lambda kk, j: (kk, j)),
                   pl.BlockSpec((tk, LANES), lambda kk, j: (kk, 0))],
        out_shape=[jax.ShapeDtypeStruct((k, n - skip), BF16), jax.ShapeDtypeStruct((k, LANES), F32)],
        compiler_params=_params(2),
        name="regroup",
    )(w_t, w_t)


def _inproj_kernel(*refs, n_side):
    x_ref, w_ref, b_ref, wif_ref, bif_ref = refs[:5]
    side_in = refs[5:5 + n_side]
    z_ref, zif_ref = refs[5 + n_side:7 + n_side]
    side_out = refs[7 + n_side:7 + 2 * n_side]
    xb_ref = refs[7 + 2 * n_side]

    @pl.when(pl.program_id(1) == 0)
    def _():
        xb = x_ref[...].astype(BF16)
        xb_ref[...] = xb
        zif_ref[...] = _dot(xb, wif_ref[...].astype(BF16)) + bif_ref[...]

    z_ref[...] = (_dot(xb_ref[...], w_ref[...]) + b_ref[...]).astype(BF16)
    _side_cast(side_in, side_out)


def _inproj(x2d, w_main, b_main, w_if, b_if, tm, tn, side):
    t, d = x2d.shape
    n = w_main.shape[1]
    nj = n // tn
    side_specs = _side_specs(side, (t // tm) * nj, lambda i, j: i * nj + j)
    return pl.pallas_call(
        functools.partial(_inproj_kernel, n_side=len(side)),
        grid=(t // tm, nj),
        in_specs=[pl.BlockSpec((tm, d), lambda i, j: (i, 0)),
                  pl.BlockSpec((d, tn), lambda i, j: (0, j)),
                  pl.BlockSpec((1, tn), lambda i, j: (0, j)),
                  pl.BlockSpec((d, LANES), lambda i, j: (0, 0)),
                  pl.BlockSpec((1, LANES), lambda i, j: (0, 0))] + side_specs,
        out_specs=[pl.BlockSpec((tm, tn), lambda i, j: (i, j)),
                   pl.BlockSpec((tm, LANES), lambda i, j: (i, 0))] + side_specs,
        out_shape=[jax.ShapeDtypeStruct((t, n), BF16),
                   jax.ShapeDtypeStruct((t, LANES), F32)] + _side_shapes(side),
        scratch_shapes=[pltpu.VMEM((tm, d), BF16)],
        compiler_params=_params(2),
        name="inproj",
    )(x2d, w_main, b_main, w_if, b_if, *side)


def _mlstm_kernel(*refs, nb, n_side):
    qk_ref, v_ref, o_ref, g_ref, cw_ref, cb_ref, mhg_ref = refs[:7]
    side_in = refs[7:7 + n_side]
    out_ref = refs[7 + n_side]
    side_out = refs[8 + n_side:8 + 2 * n_side]
    cbuf = refs[8 + 2 * n_side]
    state = refs[9 + 2 * n_side:]
    _mlstm_body(qk_ref, v_ref, o_ref, g_ref, cw_ref, cb_ref, mhg_ref, out_ref, cbuf, state, nb)
    _side_cast(side_in, side_out)


def _mlstm_body(qk_ref, v_ref, o_ref, g_ref, cw_ref, cb_ref, mhg_ref, out_ref, cbuf, state, nb):
    L = CHUNK
    halo = CONV_HALO
    ct_refs, m_ref = state[:M_HEADS], state[M_HEADS]

    @pl.when(pl.program_id(0) == 0)
    def _():
        cbuf[:, 0:halo, :] = jnp.zeros((nb, halo, 2 * M_QK), BF16)
        for ref in state:
            ref[...] = jnp.zeros_like(ref)

    cbuf[:, halo:halo + L, :] = qk_ref[...]
    scale = M_QK_DIM ** -0.5
    row = lax.broadcasted_iota(jnp.int32, (L, L), 0)
    col = lax.broadcasted_iota(jnp.int32, (L, L), 1)
    causal = col <= row
    upper = jnp.where(row <= col, 1.0, 0.0).astype(BF16)
    ones_blk = jnp.ones((L, LANES), BF16)
    lane8 = lax.broadcasted_iota(jnp.int32, (SUBLANES, L), 1)
    wrow = lax.broadcasted_iota(jnp.int32, (L, L + halo), 0)
    wcol = lax.broadcasted_iota(jnp.int32, (L, L + halo), 1)
    shifts = [jnp.where(wcol == wrow + (halo - (CONV_W - 1) + tap), 1.0, 0.0).astype(BF16)
              for tap in range(CONV_W - 1)]
    sub = lax.broadcasted_iota(jnp.int32, (SUBLANES, LANES), 0)

    def prep(b):
        window = cbuf[b]
        acc = cb_ref[...] + _dot(shifts[0], window) * cw_ref[0:1, :]
        for tap in range(1, CONV_W - 1):
            acc = acc + _dot(shifts[tap], window) * cw_ref[tap:tap + 1, :]
        acc = acc + qk_ref[b].astype(F32) * cw_ref[CONV_W - 1:CONV_W, :]
        qk = acc * jax.nn.sigmoid(acc)
        cbuf[b, 0:halo, :] = cbuf[b, L:L + halo, :]

        g_row = g_ref[b].T[0:SUBLANES, :]
        lf = _log_sigmoid(g_row)
        lf_hi = lf.astype(BF16)
        lf_lo = (lf - lf_hi.astype(F32)).astype(BF16)
        cs = _dot(jnp.concatenate([lf_hi, lf_lo], axis=0), upper)
        b_row = pltpu.roll(cs[0:SUBLANES, :] + cs[SUBLANES:2 * SUBLANES, :], M_HEADS, 0)
        a_row = g_row - b_row
        cm = a_row
        for sh in (1, 2, 4, 8, 16, 32, 64):
            cm = jnp.maximum(cm, jnp.where(lane8 >= sh, pltpu.roll(cm, sh, 1), -jnp.inf))
        m_prev = m_ref[b]
        mx = jnp.maximum(m_prev, cm)
        mx_last = jnp.maximum(m_prev, jnp.max(a_row, axis=1, keepdims=True))
        carry_in = jnp.exp(m_prev - mx)
        floor = jnp.exp(-(b_row + mx))
        wts = jnp.exp(a_row - mx_last)
        decay = jnp.exp(m_prev - mx_last)
        m_ref[b] = jnp.where(sub < M_HEADS, b_row[:, L - 1:L] + mx_last, 0.0)
        packed = jnp.concatenate(
            [jnp.where(sub < M_HEADS, mx, pltpu.roll(carry_in, M_HEADS, 0)),
             jnp.where(sub < M_HEADS, floor, pltpu.roll(wts, M_HEADS, 0)),
             jnp.zeros((L - 2 * SUBLANES, L), F32)], axis=0).T
        return qk, a_row, decay, packed

    def run_chains(batches):
        pre = {b: prep(b) for b in batches}
        chains = [(b, h) for b in batches for h in range(M_HEADS)]
        vcols = lambda h: slice(h * M_V_DIM, (h + 1) * M_V_DIM)
        col_of = lambda i, c: pre[c[0]][3][:, i * M_HEADS + c[1]:i * M_HEADS + c[1] + 1]
        q = {(b, h): pre[b][0][:, h * M_QK_DIM:(h + 1) * M_QK_DIM].astype(BF16) for b, h in chains}
        kf = {(b, h): pre[b][0][:, M_QK + h * M_QK_DIM:M_QK + (h + 1) * M_QK_DIM] * scale for b, h in chains}
        v_ext = {(b, h): jnp.concatenate([v_ref[b, :, vcols(h)], ones_blk], axis=1) for b, h in chains}
        ct = {(b, h): ct_refs[h][b] for b, h in chains}

        raw = {c: lax.dot_general(q[c], kf[c].astype(BF16), (((1,), (1,)), ((), ())),
                                  preferred_element_type=F32) for c in chains}
        prev = {c: _dot(q[c], ct[c].astype(BF16)) for c in chains}
        pmat = {(b, h): jnp.exp(jnp.where(causal, pre[b][1][h:h + 1, :] - col_of(0, (b, h)), -jnp.inf))
                for b, h in chains}
        s = {c: (raw[c] * pmat[c]).astype(BF16) for c in chains}
        numden = {c: _dot(s[c], v_ext[c]) + col_of(1, c) * prev[c] for c in chains}
        upd = {c: lax.dot_general((kf[c] * col_of(3, c)).astype(BF16), v_ext[c], (((0,), (0,)), ((), ())),
                                  preferred_element_type=F32) for c in chains}
        for b, h in chains:
            dec = jnp.concatenate([pre[b][2][h:h + 1, :]] * (ct[b, h].shape[1] // LANES), axis=1)
            ct_refs[h][b] = dec * ct[b, h] + upd[b, h]
        inv = {c: 1.0 / jnp.maximum(jnp.abs(numden[c][:, M_V_DIM:]), col_of(2, c)) for c in chains}
        hh = {c: numden[c][:, 0:M_V_DIM] * jnp.concatenate([inv[c]] * (M_V_DIM // LANES), axis=1) for c in chains}
        mu = {c: jnp.mean(hh[c], axis=-1, keepdims=True) for c in chains}
        dlt = {c: hh[c] - mu[c] for c in chains}
        var = {c: jnp.mean(dlt[c] * dlt[c], axis=-1, keepdims=True) for c in chains}
        for b, h in chains:
            hn = dlt[b, h] * lax.rsqrt(var[b, h] + LN_EPS) * mhg_ref[:, vcols(h)]
            og = jax.nn.sigmoid(o_ref[b, :, vcols(h)].astype(F32))
            out_ref[b, :, vcols(h)] = (hn * og).astype(BF16)

    for b0 in range(0, nb, MLSTM_GROUP):
        run_chains(range(b0, min(b0 + MLSTM_GROUP, nb)))


def _mlstm(z3, zif3, conv_w, conv_b, mh_g, side):
    nb, s, _ = z3.shape
    L = CHUNK
    blk = lambda colblk: pl.BlockSpec((nb, L, M_V), lambda c: (0, c, colblk))
    const = lambda shape: pl.BlockSpec(shape, lambda c: (0,) * len(shape))
    side_specs = _side_specs(side, s // L, lambda c: c)
    return pl.pallas_call(
        functools.partial(_mlstm_kernel, nb=nb, n_side=len(side)),
        grid=(s // L,),
        in_specs=[blk(0), blk(1), blk(2),
                  pl.BlockSpec((nb, L, LANES), lambda c: (0, c, 0)),
                  const((CONV_W, 2 * M_QK)), const((1, 2 * M_QK)), const((1, M_V))] + side_specs,
        out_specs=[pl.BlockSpec((nb, L, M_V), lambda c: (0, c, 0))] + side_specs,
        out_shape=[jax.ShapeDtypeStruct((nb, s, M_V), BF16)] + _side_shapes(side),
        scratch_shapes=([pltpu.VMEM((nb, L + CONV_HALO, 2 * M_QK), BF16)]
                        + [pltpu.VMEM((nb, M_QK_DIM, M_V_DIM + LANES), F32)] * M_HEADS
                        + [pltpu.VMEM((nb, SUBLANES, L), F32)]),
        compiler_params=_params(1),
        name="mlstm",
    )(z3, z3, z3, zif3, conv_w, conv_b, mh_g, *side)


def _branch_kernel(hg_ref, u_ref, uh_ref, gm_ref, gp_ref, wpool_ref, ps_ref, wm_ref, wp_ref,
                   out_ref, xa, xb, yp_ref, *, tm, seq):
    H = POOL_HALO
    G = POOL_GROUP_DIM
    t0 = lax.rem(pl.program_id(0) * tm, seq)
    u = u_ref[...].astype(F32)
    xa[H:H + tm, :] = u
    xa[0:H, :] = jnp.where(t0 == 0, 0.0, uh_ref[...].astype(F32))
    n = tm + H - 8
    xb[8:8 + n, :] = xa[8:8 + n, :] + xa[7:7 + n, :]
    n = tm + H - 16
    xa[16:16 + n, G:] = xb[16:16 + n, G:] + xb[14:14 + n, G:]
    n = tm + H - 24
    xb[24:24 + n, 2 * G:] = xa[24:24 + n, 2 * G:] + xa[20:20 + n, 2 * G:]
    xa[H:H + tm, 3 * G:] = xb[H:H + tm, 3 * G:] + xb[H - 8:H - 8 + tm, 3 * G:]
    tpos = t0 + lax.broadcasted_iota(jnp.int32, (tm, 1), 0)
    for g, win in enumerate(POOL_WINDOWS):
        src = (xb, xa, xb, xa)[g]
        cols = slice(g * G, (g + 1) * G)
        cnt = jnp.minimum(tpos + 1, win).astype(F32)
        y = src[H:H + tm, cols] / cnt - u[:, cols]
        yp = _dot(y.astype(BF16), wpool_ref[g]) * ps_ref[:, cols]
        yp_ref[:, cols] = yp.astype(BF16)
    hm = tm // ROW_SPLIT
    for r0 in range(0, tm, hm):
        rows = slice(r0, r0 + hm)
        pb = _dot(yp_ref[rows, :], wp_ref[...])
        a = _dot(hg_ref[rows, :], wm_ref[...])
        merged = (jax.nn.sigmoid(gm_ref[rows, :].astype(F32)) * a
                  + jax.nn.sigmoid(gp_ref[rows, :].astype(F32)) * pb)
        out_ref[rows, :] = merged.astype(BF16)


def _branch_side_kernel(*refs, n_side, tm, seq):
    side_in = refs[9:9 + n_side]
    side_out = refs[10 + n_side:10 + 2 * n_side]
    _branch_kernel(*refs[:9], refs[9 + n_side], *refs[10 + 2 * n_side:], tm=tm, seq=seq)
    _side_cast(side_in, side_out)


def _branch(hg2d, z_main, w_pool, pool_scale, w_m_br, w_p_br, seq, tm, side):
    t = hg2d.shape[0]
    d = w_m_br.shape[1]
    hb = tm // POOL_HALO
    const = lambda shape: pl.BlockSpec(shape, lambda i: (0,) * len(shape))
    side_specs = _side_specs(side, t // tm, lambda i: i)
    return pl.pallas_call(
        functools.partial(_branch_side_kernel, n_side=len(side), tm=tm, seq=seq),
        grid=(t // tm,),
        in_specs=[pl.BlockSpec((tm, M_V), lambda i: (i, 0)),
                  pl.BlockSpec((tm, POOL_W), lambda i: (i, 3)),
                  pl.BlockSpec((POOL_HALO, POOL_W), lambda i: (jnp.maximum(i * hb - 1, 0), 3)),
                  pl.BlockSpec((tm, d), lambda i: (i, 2)),
                  pl.BlockSpec((tm, d), lambda i: (i, 3)),
                  const(w_pool.shape), const((1, POOL_W)), const(w_m_br.shape), const(w_p_br.shape)] + side_specs,
        out_specs=[pl.BlockSpec((tm, d), lambda i: (i, 0))] + side_specs,
        out_shape=[jax.ShapeDtypeStruct((t, d), BF16)] + _side_shapes(side),
        scratch_shapes=[pltpu.VMEM((tm + POOL_HALO, POOL_W), F32),
                        pltpu.VMEM((tm + POOL_HALO, POOL_W), F32),
                        pltpu.VMEM((tm, POOL_W), BF16)],
        compiler_params=_params(1),
        name="branch",
    )(hg2d, z_main, z_main, z_main, z_main, w_pool, pool_scale, w_m_br, w_p_br, *side)


def _route_kernel(mg_ref, x_ref, wout_ref, g_ref, b_ref, wr_ref, br_ref, upper_ref,
                  x1_ref, route_ref, gate_ref, cnt_ref, carry, *, alpha, tm):
    @pl.when(pl.program_id(0) == 0)
    def _():
        carry[...] = jnp.zeros_like(carry)

    hm = tm // ROW_SPLIT
    pitch = _pitch(x_ref.shape[1])
    logits = []
    for r0 in range(0, tm, hm):
        r = alpha * x_ref[r0:r0 + hm, :] + _dot(mg_ref[r0:r0 + hm, :], wout_ref[...])
        x1 = _layer_norm(r, g_ref[...], b_ref[...])
        _store_token_major(x1_ref.at[r0 * pitch:(r0 + hm) * pitch], x1)
        logits.append(_dot(x1.astype(BF16), wr_ref[...]) + br_ref[...])
    lt = jnp.concatenate(logits, axis=0).T

    sub = lax.broadcasted_iota(jnp.int32, (SUBLANES, tm), 0).astype(F32)
    none = float(SUBLANES)

    def softmax(z):
        e = jnp.exp(z - jnp.max(z, axis=0, keepdims=True))
        return e / jnp.sum(e, axis=0, keepdims=True)

    def top1(vals):
        top = jnp.max(vals, axis=0, keepdims=True)
        return top, jnp.min(jnp.where(vals == top, sub, none), axis=0, keepdims=True)

    is_grp = sub < N_GROUPS
    pg = jnp.where(is_grp, softmax(jnp.where(is_grp, lt[0:SUBLANES, :], -jnp.inf)), -1.0)
    pg_top, g_idx = top1(pg)
    le_sel = lt[SUBLANES:2 * SUBLANES, :]
    for g in range(1, N_GROUPS):
        le_sel = jnp.where(g_idx == g, lt[(g + 1) * SUBLANES:(g + 2) * SUBLANES, :], le_sel)
    pe = softmax(le_sel)
    pe1, i1 = top1(pe)
    pe2, i2 = top1(jnp.where(sub == i1, -1.0, pe))
    den = pe1 + pe2
    gate1 = pg_top * pe1 / den
    gate2 = pg_top * pe2 / den
    e1 = g_idx * EXPERTS_PER_GROUP + i1
    e2 = g_idx * EXPERTS_PER_GROUP + i2

    eid = lax.broadcasted_iota(jnp.int32, (N_EXPERTS, tm), 0).astype(F32)
    hit1 = eid == e1
    hit2 = eid == e2
    onehot = jnp.where(hit1 | hit2, 1.0, 0.0)
    incl = _dot(onehot.astype(BF16), upper_ref[...])
    excl = incl - onehot + carry[:, 0:1]
    r1 = jnp.sum(jnp.where(hit1, excl, 0.0), axis=0, keepdims=True)
    r2 = jnp.sum(jnp.where(hit2, excl, 0.0), axis=0, keepdims=True)
    carry[...] = carry[...] + jnp.broadcast_to(incl[:, tm - 1:tm], carry.shape)
    cnt_ref[...] = carry[...]

    route = jnp.where(sub == 0, e1, jnp.where(sub == 1, e2, jnp.where(sub == 2, r1, jnp.where(sub == 3, r2, 0.0))))
    route_ref[...] = route.astype(jnp.int32)
    gates = jnp.where(sub == 0, gate1, jnp.where(sub == 1, gate2, 0.0))
    gate_ref[...] = jnp.concatenate([gates, jnp.zeros((LANES - SUBLANES, tm), F32)], axis=0).T


def _route_side_kernel(*refs, n_side, alpha, tm):
    side_in = refs[8:8 + n_side]
    side_out = refs[12 + n_side:12 + 2 * n_side]
    _route_kernel(*refs[:8], *refs[8 + n_side:12 + n_side], *refs[12 + 2 * n_side:], alpha=alpha, tm=tm)
    _side_cast(side_in, side_out)


def _route(merged, x2d, w_out, ln_g, ln_b, w_r, b_r, upper, alpha, tm, side):
    t, d = x2d.shape
    const = lambda shape: pl.BlockSpec(shape, lambda i: (0,) * len(shape))
    resident = lambda shape: pl.BlockSpec(shape, lambda i: (0,) * len(shape), pipeline_mode=pl.Buffered(1))
    rows = lambda width: pl.BlockSpec((tm, width), lambda i: (i, 0))
    cols = pl.BlockSpec((SUBLANES, tm), lambda i: (0, i))
    side_specs = _side_specs(side, t // tm, lambda i: i)
    return pl.pallas_call(
        functools.partial(_route_side_kernel, n_side=len(side), alpha=alpha, tm=tm),
        grid=(t // tm,),
        in_specs=[rows(d), rows(d), resident(w_out.shape), const((1, d)), const((1, d)),
                  const(w_r.shape), const((1, LANES)), const((tm, tm))] + side_specs,
        out_specs=[pl.BlockSpec((tm * _pitch(d), LANES), lambda i: (i, 0)),
                   cols, rows(LANES), const((N_EXPERTS, LANES))] + side_specs,
        out_shape=[jax.ShapeDtypeStruct((t * _pitch(d), LANES), F32),
                   jax.ShapeDtypeStruct((SUBLANES, t), jnp.int32),
                   jax.ShapeDtypeStruct((t, LANES), F32),
                   jax.ShapeDtypeStruct((N_EXPERTS, LANES), F32)] + _side_shapes(side),
        scratch_shapes=[pltpu.VMEM((N_EXPERTS, LANES), F32)],
        compiler_params=_params(1),
        name="route",
    )(merged, x2d, w_out, ln_g, ln_b, w_r, b_r, upper, *side)


def _slot_map_kernel(pad_start_ref, pad_len_ref, nu_ref, dest_ref, inv_ref, *, tb, te):
    i = pl.program_id(0)

    @pl.when(i == 0)
    def _():
        def per_expert(e, c):
            start = pad_start_ref[e]

            def clear(r, cc):
                inv_ref[start + r] = 0
                return cc

            return lax.fori_loop(0, pad_len_ref[e], clear, c)

        lax.fori_loop(0, N_EXPERTS, per_expert, 0)

        def clear_tail(s, c):
            inv_ref[s] = 0
            return c

        lax.fori_loop(nu_ref[0] * te, inv_ref.shape[0], clear_tail, 0)

    def record(t, c):
        for k in range(TOP_K):
            inv_ref[dest_ref[k, t]] = i * tb + t
        return c

    lax.fori_loop(0, tb, record, 0, unroll=DMA_UNROLL)


def _slot_map(dest, pad_start, pad_len, n_used, n_slots, tb, te):
    t = dest.shape[1]
    grid_spec = pltpu.PrefetchScalarGridSpec(
        num_scalar_prefetch=3,
        grid=(t // tb,),
        in_specs=[pl.BlockSpec((TOP_K, tb), lambda i, ps, pn, nu: (0, i), memory_space=pltpu.SMEM)],
        out_specs=pl.BlockSpec(memory_space=pltpu.SMEM),
    )
    return pl.pallas_call(
        functools.partial(_slot_map_kernel, tb=tb, te=te),
        grid_spec=grid_spec,
        out_shape=jax.ShapeDtypeStruct((n_slots,), jnp.int32),
        compiler_params=_params(1),
        name="slot_map",
    )(pad_start, pad_len, n_used, dest)


def _expert_kernel(be_ref, valid_ref, inv_ref, inv_next_ref, x_hbm, wg_ref, wu_ref, wd_ref, y_ref, xbuf, sem,
                   *, te, d):
    i = pl.program_id(0)
    slot = lax.rem(i, 2)

    def copy(tok, s, r, n=1):
        return pltpu.make_async_copy(_tokens(x_hbm, tok, d, n), _tokens(xbuf.at[s], r, d, n), sem.at[s])

    def gather(rows_ref, s, n_tok):
        def chunk(c, cc):
            for u in range(DMA_UNROLL):
                copy(rows_ref[c * DMA_UNROLL + u], s, c * DMA_UNROLL + u).start(priority=u % 2)
            return cc

        def single(r, cc):
            copy(rows_ref[r], s, r).start()
            return cc

        full = n_tok // DMA_UNROLL
        lax.fori_loop(0, full, chunk, 0)
        lax.fori_loop(full * DMA_UNROLL, n_tok, single, 0)

    def drain(s, n_tok):
        def wait(n):
            def body(r, cc):
                copy(0, s, 0, n).wait()
                return cc
            return body
        lax.fori_loop(0, n_tok // SUBLANES, wait(SUBLANES), 0)
        lax.fori_loop(0, n_tok % SUBLANES, wait(1), 0)

    @pl.when(i == 0)
    def _():
        xbuf[...] = jnp.zeros_like(xbuf)
        gather(inv_ref, 0, valid_ref[0])

    @pl.when(i + 1 < pl.num_programs(0))
    def _():
        gather(inv_next_ref, 1 - slot, valid_ref[jnp.minimum(i + 1, pl.num_programs(0) - 1)])

    n_valid = valid_ref[i]
    drain(slot, n_valid)

    @pl.when(n_valid > 0)
    def _():
        xb = _load_token_major(xbuf.at[slot], te, d).astype(BF16)
        gt = _dot(xb, wg_ref[0])
        up = _dot(xb, wu_ref[0])
        hid = gt * jax.nn.sigmoid(gt) * up
        _store_token_major(y_ref, _dot(hid.astype(BF16), wd_ref[0]))

    @pl.when(n_valid == 0)
    def _():
        y_ref[...] = jnp.zeros_like(y_ref)


def _experts(x1t, inv, block_e, valid, w_gate, w_up, w_down, te):
    _, d, de = w_gate.shape
    pitch = _pitch(d)
    n_tiles = inv.shape[0] // te
    grid_spec = pltpu.PrefetchScalarGridSpec(
        num_scalar_prefetch=2,
        grid=(n_tiles,),
        in_specs=[pl.BlockSpec((te,), lambda i, be, va: (i,), memory_space=pltpu.SMEM),
                  pl.BlockSpec((te,), lambda i, be, va: (jnp.minimum(i + 1, n_tiles - 1),), memory_space=pltpu.SMEM),
                  pl.BlockSpec(memory_space=pl.ANY),
                  pl.BlockSpec((1, d, de), lambda i, be, va: (be[i], 0, 0)),
                  pl.BlockSpec((1, d, de), lambda i, be, va: (be[i], 0, 0)),
                  pl.BlockSpec((1, de, d), lambda i, be, va: (be[i], 0, 0))],
        out_specs=pl.BlockSpec((te * pitch, LANES), lambda i, be, va: (i, 0)),
        scratch_shapes=[pltpu.VMEM((2, te * pitch, LANES), F32), pltpu.SemaphoreType.DMA((2,))],
    )
    return pl.pallas_call(
        functools.partial(_expert_kernel, te=te, d=d),
        grid_spec=grid_spec,
        out_shape=jax.ShapeDtypeStruct((n_tiles * te * pitch, LANES), F32),
        compiler_params=_params(1),
        name="experts",
    )(block_e, valid, inv, inv, x1t, w_gate, w_up, w_down)


def _final_kernel(dcur_ref, dnxt_ref, x1_ref, gate_ref, p_ref, ys_hbm, g_ref, b_ref, wg_ref, bg_ref, wp_ref,
                  out_ref, ybuf, sem, *, alpha, tm, d):
    i = pl.program_id(0)
    slot = lax.rem(i, 2)

    def issue(dest_ref, s):
        def body(t, c):
            for k in range(TOP_K):
                pltpu.make_async_copy(_token_data(ys_hbm, dest_ref[k, t], d),
                                      _token_data(ybuf.at[s, k], t, d), sem.at[s]).start(priority=k % 2)
            return c
        lax.fori_loop(0, tm, body, 0, unroll=DMA_UNROLL)

    @pl.when(i == 0)
    def _():
        issue(dcur_ref, 0)

    @pl.when(i + 1 < pl.num_programs(0))
    def _():
        issue(dnxt_ref, 1 - slot)

    for k in range(TOP_K):
        rows = tm * (d // LANES)
        pltpu.make_async_copy(ys_hbm.at[pl.ds(0, rows)], ybuf.at[slot, k, pl.ds(0, rows)], sem.at[slot]).wait()

    hm = tm // ROW_SPLIT
    for r0 in range(0, tm, hm):
        rows = slice(r0, r0 + hm)
        trows = pl.ds(r0 * _pitch(d), hm * _pitch(d))
        pp = _dot(p_ref[rows, :].astype(BF16), wp_ref[...])
        gate = gate_ref[rows, :]
        y = (gate[:, 0:1] * _load_token_major(ybuf.at[slot, 0, trows], hm, d)
             + gate[:, 1:2] * _load_token_major(ybuf.at[slot, 1, trows], hm, d))
        x2 = _layer_norm(alpha * _load_token_major(x1_ref.at[trows], hm, d) + y, g_ref[...], b_ref[...])
        gl = _dot(x2.astype(BF16), wg_ref[...]) + bg_ref[...]
        out_ref[rows, :] = x2 + jax.nn.sigmoid(gl) * pp


def _final(dest, x1t, gate, p2d, ys, ln_g, ln_b, w_pg, b_pg, w_pp, alpha, tm):
    d = w_pg.shape[0]
    pitch = _pitch(d)
    t = x1t.shape[0] // pitch
    n = t // tm
    const = lambda shape: pl.BlockSpec(shape, lambda i: (0,) * len(shape))
    resident = lambda shape: pl.BlockSpec(shape, lambda i: (0,) * len(shape), pipeline_mode=pl.Buffered(1))
    rows = lambda width: pl.BlockSpec((tm, width), lambda i: (i, 0))
    return pl.pallas_call(
        functools.partial(_final_kernel, alpha=alpha, tm=tm, d=d),
        grid=(n,),
        in_specs=[pl.BlockSpec((TOP_K, tm), lambda i: (0, i), memory_space=pltpu.SMEM),
                  pl.BlockSpec((TOP_K, tm), lambda i: (0, jnp.minimum(i + 1, n - 1)), memory_space=pltpu.SMEM),
                  pl.BlockSpec((tm * pitch, LANES), lambda i: (i, 0)),
                  rows(LANES), rows(p2d.shape[1]),
                  pl.BlockSpec(memory_space=pl.ANY),
                  const((1, d)), const((1, d)), resident(w_pg.shape), const((1, d)), resident(w_pp.shape)],
        out_specs=rows(d),
        out_shape=jax.ShapeDtypeStruct((t, d), F32),
        scratch_shapes=[pltpu.VMEM((2, TOP_K, tm * pitch, LANES), F32), pltpu.SemaphoreType.DMA((2,))],
        compiler_params=_params(1),
        name="final",
    )(dest, dest, x1t, gate, p2d, ys, ln_g, ln_b, w_pg, b_pg, w_pp)


def _pad_cols(a, width):
    return jnp.pad(a, ((0, 0), (0, width - a.shape[1])))


def _tri(n):
    return jnp.tril(jnp.ones((n, n), BF16))


def _layer(x, p, w_in_t, b_in, conv_w, conv_b, mh_g, w_pool, pool_scale, w_m_br, w_p_br, w_out,
           ln1_g, ln1_b, w_rg, b_rg, w_re, b_re, w_gate, w_up, w_down, ln2_g, ln2_b,
           w_ple_gate, b_ple_gate, w_ple_proj, alpha):
    nb, seq, d = x.shape
    t = nb * seq
    x2d = x.reshape(t, d)
    row = lambda a: a.reshape(1, -1)

    c_if = 2 * M_QK + 2 * M_V
    w_main, w_if = _regroup(w_in_t, c_if, 2 * M_HEADS, 1024, min(1024, d))
    b_main = row(jnp.concatenate([b_in[:c_if], b_in[c_if + 2 * M_HEADS:]]))
    b_if = _pad_cols(row(b_in[c_if:c_if + 2 * M_HEADS]), LANES)

    n_exp, _, d_exp = w_gate.shape
    tm_in = min(1024, t)
    z_main, z_if = _inproj(x2d, w_main, b_main, w_if, b_if, tm_in, 2048, [])

    dense_w = [w_pool, w_m_br, w_p_br, w_out, w_ple_gate, w_ple_proj, w_down.reshape(n_exp * d_exp, d)]
    hg, *dense_b = _mlstm(z_main.reshape(nb, seq, -1), z_if.reshape(nb, seq, LANES),
                          conv_w, row(conv_b), row(mh_g), [a.reshape(-1, a.shape[-1]) for a in dense_w])
    w_pool, w_m_br, w_p_br, w_out, w_ple_gate, w_ple_proj, wd_b = [
        b.reshape(a.shape) for a, b in zip(dense_w, dense_b)]

    tm = min(512, seq)
    merged, wu_b = _branch(hg.reshape(t, M_V), z_main, w_pool, row(pool_scale), w_m_br, w_p_br, seq, tm,
                           [w_up.reshape(n_exp * d, d_exp)])

    w_r = _pad_cols(jnp.concatenate([_pad_cols(w_rg, SUBLANES), w_re], axis=1), LANES).astype(BF16)
    b_r = _pad_cols(jnp.concatenate([_pad_cols(row(b_rg), SUBLANES), row(b_re)], axis=1), LANES)
    x1, route, gate, cnt, wg_b = _route(merged, x2d, w_out, row(ln1_g), row(ln1_b),
                                          w_r, b_r, _tri(tm).T, alpha, tm, [w_gate.reshape(n_exp * d, d_exp)])

    te = EXPERT_TILE
    counts = cnt[:, 0].astype(jnp.int32)
    pcounts = (counts + te - 1) // te * te
    pends = jnp.cumsum(pcounts)
    pstarts = pends - pcounts
    e_sel = route[0:TOP_K, :, None] == jnp.arange(N_EXPERTS, dtype=jnp.int32)
    dest = jnp.sum(jnp.where(e_sel, pstarts, 0), axis=-1) + route[TOP_K:2 * TOP_K]
    n_slots = t * TOP_K + N_EXPERTS * te
    n_tiles = n_slots // te
    n_used = (pends[-1] // te).reshape(1)
    tile_row = jnp.minimum(jnp.arange(n_tiles, dtype=jnp.int32), n_used - 1) * te
    block_e = jnp.minimum(jnp.sum((tile_row[:, None] >= pends[None, :]).astype(jnp.int32), axis=1), N_EXPERTS - 1)

    seg_end = jnp.sum(jnp.where(block_e[:, None] == jnp.arange(N_EXPERTS), pstarts + counts, 0), axis=1)
    tile_first = jnp.arange(n_tiles, dtype=jnp.int32) * te
    valid = jnp.where(tile_first < pends[-1], jnp.clip(seg_end - tile_first, 0, te), 0).astype(jnp.int32)

    inv = _slot_map(dest, pstarts + counts, pcounts - counts, n_used, n_slots, min(DISPATCH_TILE, t), te)
    ys = _experts(x1, inv, block_e, valid, wg_b.reshape(n_exp, d, d_exp), wu_b.reshape(n_exp, d, d_exp),
                  wd_b.reshape(n_exp, d_exp, d), te)
    return _final(dest, x1, gate, p.reshape(t, -1), ys, row(ln2_g), row(ln2_b),
                  w_ple_gate, row(b_ple_gate), w_ple_proj, alpha,
                  min(512, t)).reshape(nb, seq, d)


def kernel(x, p, w_in, b_in, conv_w, conv_b, mh_g, w_pool, pool_scale, w_m_br, w_p_br, w_out, ln1_g, ln1_b, w_rg, b_rg, w_re, b_re, w_gate, w_up, w_down, ln2_g, ln2_b, w_ple_gate, b_ple_gate, w_ple_proj):
    depth = w_in.shape[0]
    alpha = (2 * depth) ** 0.25
    for i in range(depth):
        x = _layer(x, p[i], w_in[i].T, b_in[i], conv_w[i], conv_b[i], mh_g[i], w_pool[i], pool_scale[i],
                   w_m_br[i], w_p_br[i], w_out[i], ln1_g[i], ln1_b[i], w_rg[i], b_rg[i], w_re[i], b_re[i],
                   w_gate[i], w_up[i], w_down[i], ln2_g[i], ln2_b[i], w_ple_gate[i], b_ple_gate[i],
                   w_ple_proj[i], alpha)
    return x
```

```python
import functools

import jax
import jax.numpy as jnp
from jax import lax
from jax.experimental import pallas as pl
from jax.experimental.pallas import tpu as pltpu

F32 = jnp.float32
BF16 = jnp.bfloat16

M_HEADS = 4
M_QK_DIM = 128
M_V_DIM = 256
M_QK = M_HEADS * M_QK_DIM
M_V = M_HEADS * M_V_DIM
CONV_W = 4
CHUNK = 128
POOL_WINDOWS = (2, 4, 8, 16)
POOL_GROUP_DIM = 256
POOL_W = len(POOL_WINDOWS) * POOL_GROUP_DIM
N_GROUPS = 4
EXPERTS_PER_GROUP = 8
N_EXPERTS = N_GROUPS * EXPERTS_PER_GROUP
TOP_K = 2
LN_EPS = 1e-5

LANES = 128
SUBLANES = 8
VMEM_LIMIT = 56 * 1024 * 1024
BF16_SUBLANES = 16
CONV_HALO = BF16_SUBLANES
POOL_HALO = 32
EXPERT_TILE = 256
DISPATCH_TILE = 2048
MLSTM_GROUP = 4
ROW_DMA_PRIORITY = 1
DMA_UNROLL = 8
ROW_SPLIT = 2


def _dot(a, b):
    return jnp.dot(a, b, preferred_element_type=F32)


def _params(n_grid):
    return pltpu.CompilerParams(dimension_semantics=("arbitrary",) * n_grid,
                                vmem_limit_bytes=VMEM_LIMIT)


def _log_sigmoid(x):
    return -(jnp.maximum(-x, 0.0) + jnp.log1p(jnp.exp(-jnp.abs(x))))


def _layer_norm(r, g, b):
    mu = jnp.mean(r, axis=-1, keepdims=True)
    d = r - mu
    var = jnp.mean(d * d, axis=-1, keepdims=True)
    return d * lax.rsqrt(var + LN_EPS) * g + b


def _pitch(d):
    return d // LANES + 1


def _store_token_major(ref, val):
    n, d = val.shape
    for c in range(d // LANES):
        ref[pl.ds(c, n, stride=_pitch(d)), :] = val[:, c * LANES:(c + 1) * LANES]
    ref[pl.ds(d // LANES, n, stride=_pitch(d)), :] = jnp.zeros((n, LANES), val.dtype)


def _load_token_major(ref, n, d):
    return jnp.concatenate([ref[pl.ds(c, n, stride=_pitch(d)), :] for c in range(d // LANES)], axis=1)


def _tokens(ref, tok, d, n=1):
    return ref.at[pl.ds(tok * _pitch(d), n * _pitch(d))]


def _token_data(ref, tok, d):
    return ref.at[pl.ds(tok * _pitch(d), d // LANES)]


def _side_specs(side, steps, step_of):
    def spec(a):
        rb = max(BF16_SUBLANES, a.shape[0] // steps)
        return pl.BlockSpec((rb, a.shape[1]), lambda *g: (jnp.minimum(step_of(*g), a.shape[0] // rb - 1), 0))
    return [spec(a) for a in side]


def _side_shapes(side):
    return [jax.ShapeDtypeStruct(a.shape, BF16) for a in side]


def _side_cast(side_in, side_out):
    for src, dst in zip(side_in, side_out):
        dst[...] = src[...].astype(BF16)


def _regroup_kernel(a_ref, c_ref, out_ref, cut_ref):
    out_ref[...] = a_ref[...].T.astype(BF16)
    skip, tk = c_ref.shape
    cut_ref[...] = jnp.concatenate([c_ref[...], jnp.zeros((LANES - skip, tk), F32)], axis=0).T


def _regroup(w_t, cut, skip, tn, tk):
    n, k = w_t.shape
    n_keep = cut // tn
    assert skip % SUBLANES == 0
    first_row = lambda j: pl.multiple_of(jnp.where(j < n_keep, j * tn, j * tn + skip), SUBLANES)
    return pl.pallas_call(
        _regroup_kernel,
        grid=(k // tk, (n - skip) // tn),
        in_specs=[pl.BlockSpec((pl.Element(tn), pl.Element(tk)), lambda kk, j: (first_row(j), kk * tk)),
                  pl.BlockSpec((pl.Element(skip), pl.Element(tk)), lambda kk, j: (cut, kk * tk))],
        out_specs=[pl.BlockSpec((tk, tn), lambda kk, j: (kk, j)),
                   pl.BlockSpec((tk, LANES), lambda kk, j: (kk, 0))],
        out_shape=[jax.ShapeDtypeStruct((k, n - skip), BF16), jax.ShapeDtypeStruct((k, LANES), F32)],
        compiler_params=_params(2),
        name="regroup",
    )(w_t, w_t)


def _inproj_kernel(*refs, n_side):
    x_ref, w_ref, b_ref, wif_ref, bif_ref = refs[:5]
    side_in = refs[5:5 + n_side]
    z_ref, zif_ref = refs[5 + n_side:7 + n_side]
    side_out = refs[7 + n_side:7 + 2 * n_side]
    xb_ref = refs[7 + 2 * n_side]

    @pl.when(pl.program_id(1) == 0)
    def _():
        xb = x_ref[...].astype(BF16)
        xb_ref[...] = xb
        zif_ref[...] = _dot(xb, wif_ref[...].astype(BF16)) + bif_ref[...]

    z_ref[...] = (_dot(xb_ref[...], w_ref[...]) + b_ref[...]).astype(BF16)
    _side_cast(side_in, side_out)


def _inproj(x2d, w_main, b_main, w_if, b_if, tm, tn, side):
    t, d = x2d.shape
    n = w_main.shape[1]
    nj = n // tn
    side_specs = _side_specs(side, (t // tm) * nj, lambda i, j: i * nj + j)
    return pl.pallas_call(
        functools.partial(_inproj_kernel, n_side=len(side)),
        grid=(t // tm, nj),
        in_specs=[pl.BlockSpec((tm, d), lambda i, j: (i, 0)),
                  pl.BlockSpec((d, tn), lambda i, j: (0, j)),
                  pl.BlockSpec((1, tn), lambda i, j: (0, j)),
                  pl.BlockSpec((d, LANES), lambda i, j: (0, 0)),
                  pl.BlockSpec((1, LANES), lambda i, j: (0, 0))] + side_specs,
        out_specs=[pl.BlockSpec((tm, tn), lambda i, j: (i, j)),
                   pl.BlockSpec((tm, LANES), lambda i, j: (i, 0))] + side_specs,
        out_shape=[jax.ShapeDtypeStruct((t, n), BF16),
                   jax.ShapeDtypeStruct((t, LANES), F32)] + _side_shapes(side),
        scratch_shapes=[pltpu.VMEM((tm, d), BF16)],
        compiler_params=_params(2),
        name="inproj",
    )(x2d, w_main, b_main, w_if, b_if, *side)


def _mlstm_kernel(*refs, nb, n_side):
    qk_ref, v_ref, o_ref, g_ref, cw_ref, cb_ref, mhg_ref = refs[:7]
    side_in = refs[7:7 + n_side]
    out_ref = refs[7 + n_side]
    side_out = refs[8 + n_side:8 + 2 * n_side]
    cbuf = refs[8 + 2 * n_side]
    state = refs[9 + 2 * n_side:]
    _mlstm_body(qk_ref, v_ref, o_ref, g_ref, cw_ref, cb_ref, mhg_ref, out_ref, cbuf, state, nb)
    _side_cast(side_in, side_out)


def _mlstm_body(qk_ref, v_ref, o_ref, g_ref, cw_ref, cb_ref, mhg_ref, out_ref, cbuf, state, nb):
    L = CHUNK
    halo = CONV_HALO
    ct_refs, m_ref = state[:M_HEADS], state[M_HEADS]

    @pl.when(pl.program_id(0) == 0)
    def _():
        cbuf[:, 0:halo, :] = jnp.zeros((nb, halo, 2 * M_QK), BF16)
        for ref in state:
            ref[...] = jnp.zeros_like(ref)

    cbuf[:, halo:halo + L, :] = qk_ref[...]
    scale = M_QK_DIM ** -0.5
    row = lax.broadcasted_iota(jnp.int32, (L, L), 0)
    col = lax.broadcasted_iota(jnp.int32, (L, L), 1)
    causal = col <= row
    upper = jnp.where(row <= col, 1.0, 0.0).astype(BF16)
    ones_blk = jnp.ones((L, LANES), BF16)
    lane8 = lax.broadcasted_iota(jnp.int32, (SUBLANES, L), 1)
    wrow = lax.broadcasted_iota(jnp.int32, (L, L + halo), 0)
    wcol = lax.broadcasted_iota(jnp.int32, (L, L + halo), 1)
    shifts = [jnp.where(wcol == wrow + (halo - (CONV_W - 1) + tap), 1.0, 0.0).astype(BF16)
              for tap in range(CONV_W - 1)]
    sub = lax.broadcasted_iota(jnp.int32, (SUBLANES, LANES), 0)

    def prep(b):
        window = cbuf[b]
        acc = cb_ref[...] + _dot(shifts[0], window) * cw_ref[0:1, :]
        for tap in range(1, CONV_W - 1):
            acc = acc + _dot(shifts[tap], window) * cw_ref[tap:tap + 1, :]
        acc = acc + qk_ref[b].astype(F32) * cw_ref[CONV_W - 1:CONV_W, :]
        qk = acc * jax.nn.sigmoid(acc)
        cbuf[b, 0:halo, :] = cbuf[b, L:L + halo, :]

        g_row = g_ref[b].T[0:SUBLANES, :]
        lf = _log_sigmoid(g_row)
        lf_hi = lf.astype(BF16)
        lf_lo = (lf - lf_hi.astype(F32)).astype(BF16)
        cs = _dot(jnp.concatenate([lf_hi, lf_lo], axis=0), upper)
        b_row = pltpu.roll(cs[0:SUBLANES, :] + cs[SUBLANES:2 * SUBLANES, :], M_HEADS, 0)
        a_row = g_row - b_row
        cm = a_row
        for sh in (1, 2, 4, 8, 16, 32, 64):
            cm = jnp.maximum(cm, jnp.where(lane8 >= sh, pltpu.roll(cm, sh, 1), -jnp.inf))
        m_prev = m_ref[b]
        mx = jnp.maximum(m_prev, cm)
        mx_last = jnp.maximum(m_prev, jnp.max(a_row, axis=1, keepdims=True))
        carry_in = jnp.exp(m_prev - mx)
        floor = jnp.exp(-(b_row + mx))
        wts = jnp.exp(a_row - mx_last)
        decay = jnp.exp(m_prev - mx_last)
        m_ref[b] = jnp.where(sub < M_HEADS, b_row[:, L - 1:L] + mx_last, 0.0)
        packed = jnp.concatenate(
            [jnp.where(sub < M_HEADS, mx, pltpu.roll(carry_in, M_HEADS, 0)),
             jnp.where(sub < M_HEADS, floor, pltpu.roll(wts, M_HEADS, 0)),
             jnp.zeros((L - 2 * SUBLANES, L), F32)], axis=0).T
        return qk, a_row, decay, packed

    def run_chains(batches):
        pre = {b: prep(b) for b in batches}
        chains = [(b, h) for b in batches for h in range(M_HEADS)]
        vcols = lambda h: slice(h * M_V_DIM, (h + 1) * M_V_DIM)
        col_of = lambda i, c: pre[c[0]][3][:, i * M_HEADS + c[1]:i * M_HEADS + c[1] + 1]
        q = {(b, h): pre[b][0][:, h * M_QK_DIM:(h + 1) * M_QK_DIM].astype(BF16) for b, h in chains}
        kf = {(b, h): pre[b][0][:, M_QK + h * M_QK_DIM:M_QK + (h + 1) * M_QK_DIM] * scale for b, h in chains}
        v_ext = {(b, h): jnp.concatenate([v_ref[b, :, vcols(h)], ones_blk], axis=1) for b, h in chains}
        ct = {(b, h): ct_refs[h][b] for b, h in chains}

        raw = {c: lax.dot_general(q[c], kf[c].astype(BF16), (((1,), (1,)), ((), ())),
                                  preferred_element_type=F32) for c in chains}
        prev = {c: _dot(q[c], ct[c].astype(BF16)) for c in chains}
        pmat = {(b, h): jnp.exp(jnp.where(causal, pre[b][1][h:h + 1, :] - col_of(0, (b, h)), -jnp.inf))
                for b, h in chains}
        s = {c: (raw[c] * pmat[c]).astype(BF16) for c in chains}
        numden = {c: _dot(s[c], v_ext[c]) + col_of(1, c) * prev[c] for c in chains}
        upd = {c: lax.dot_general((kf[c] * col_of(3, c)).astype(BF16), v_ext[c], (((0,), (0,)), ((), ())),
                                  preferred_element_type=F32) for c in chains}
        for b, h in chains:
            dec = jnp.concatenate([pre[b][2][h:h + 1, :]] * (ct[b, h].shape[1] // LANES), axis=1)
            ct_refs[h][b] = dec * ct[b, h] + upd[b, h]
        inv = {c: 1.0 / jnp.maximum(jnp.abs(numden[c][:, M_V_DIM:]), col_of(2, c)) for c in chains}
        hh = {c: numden[c][:, 0:M_V_DIM] * jnp.concatenate([inv[c]] * (M_V_DIM // LANES), axis=1) for c in chains}
        mu = {c: jnp.mean(hh[c], axis=-1, keepdims=True) for c in chains}
        dlt = {c: hh[c] - mu[c] for c in chains}
        var = {c: jnp.mean(dlt[c] * dlt[c], axis=-1, keepdims=True) for c in chains}
        for b, h in chains:
            hn = dlt[b, h] * lax.rsqrt(var[b, h] + LN_EPS) * mhg_ref[:, vcols(h)]
            og = jax.nn.sigmoid(o_ref[b, :, vcols(h)].astype(F32))
            out_ref[b, :, vcols(h)] = (hn * og).astype(BF16)

    for b0 in range(0, nb, MLSTM_GROUP):
        run_chains(range(b0, min(b0 + MLSTM_GROUP, nb)))


def _mlstm(z3, zif3, conv_w, conv_b, mh_g, side):
    nb, s, _ = z3.shape
    L = CHUNK
    blk = lambda colblk: pl.BlockSpec((nb, L, M_V), lambda c: (0, c, colblk))
    const = lambda shape: pl.BlockSpec(shape, lambda c: (0,) * len(shape))
    side_specs = _side_specs(side, s // L, lambda c: c)
    return pl.pallas_call(
        functools.partial(_mlstm_kernel, nb=nb, n_side=len(side)),
        grid=(s // L,),
        in_specs=[blk(0), blk(1), blk(2),
                  pl.BlockSpec((nb, L, LANES), lambda c: (0, c, 0)),
                  const((CONV_W, 2 * M_QK)), const((1, 2 * M_QK)), const((1, M_V))] + side_specs,
        out_specs=[pl.BlockSpec((nb, L, M_V), lambda c: (0, c, 0))] + side_specs,
        out_shape=[jax.ShapeDtypeStruct((nb, s, M_V), BF16)] + _side_shapes(side),
        scratch_shapes=([pltpu.VMEM((nb, L + CONV_HALO, 2 * M_QK), BF16)]
                        + [pltpu.VMEM((nb, M_QK_DIM, M_V_DIM + LANES), F32)] * M_HEADS
                        + [pltpu.VMEM((nb, SUBLANES, L), F32)]),
        compiler_params=_params(1),
        name="mlstm",
    )(z3, z3, z3, zif3, conv_w, conv_b, mh_g, *side)


def _branch_kernel(hg_ref, u_ref, uh_ref, gm_ref, gp_ref, wpool_ref, ps_ref, wm_ref, wp_ref,
                   out_ref, xa, xb, yp_ref, *, tm, seq):
    H = POOL_HALO
    G = POOL_GROUP_DIM
    t0 = lax.rem(pl.program_id(0) * tm, seq)
    u = u_ref[...].astype(F32)
    xa[H:H + tm, :] = u
    xa[0:H, :] = jnp.where(t0 == 0, 0.0, uh_ref[...].astype(F32))
    n = tm + H - 8
    xb[8:8 + n, :] = xa[8:8 + n, :] + xa[7:7 + n, :]
    n = tm + H - 16
    xa[16:16 + n, G:] = xb[16:16 + n, G:] + xb[14:14 + n, G:]
    n = tm + H - 24
    xb[24:24 + n, 2 * G:] = xa[24:24 + n, 2 * G:] + xa[20:20 + n, 2 * G:]
    xa[H:H + tm, 3 * G:] = xb[H:H + tm, 3 * G:] + xb[H - 8:H - 8 + tm, 3 * G:]
    tpos = t0 + lax.broadcasted_iota(jnp.int32, (tm, 1), 0)
    for g, win in enumerate(POOL_WINDOWS):
        src = (xb, xa, xb, xa)[g]
        cols = slice(g * G, (g + 1) * G)
        cnt = jnp.minimum(tpos + 1, win).astype(F32)
        y = src[H:H + tm, cols] / cnt - u[:, cols]
        yp = _dot(y.astype(BF16), wpool_ref[g]) * ps_ref[:, cols]
        yp_ref[:, cols] = yp.astype(BF16)
    hm = tm // ROW_SPLIT
    for r0 in range(0, tm, hm):
        rows = slice(r0, r0 + hm)
        pb = _dot(yp_ref[rows, :], wp_ref[...])
        a = _dot(hg_ref[rows, :], wm_ref[...])
        merged = (jax.nn.sigmoid(gm_ref[rows, :].astype(F32)) * a
                  + jax.nn.sigmoid(gp_ref[rows, :].astype(F32)) * pb)
        out_ref[rows, :] = merged.astype(BF16)


def _branch_side_kernel(*refs, n_side, tm, seq):
    side_in = refs[9:9 + n_side]
    side_out = refs[10 + n_side:10 + 2 * n_side]
    _branch_kernel(*refs[:9], refs[9 + n_side], *refs[10 + 2 * n_side:], tm=tm, seq=seq)
    _side_cast(side_in, side_out)


def _branch(hg2d, z_main, w_pool, pool_scale, w_m_br, w_p_br, seq, tm, side):
    t = hg2d.shape[0]
    d = w_m_br.shape[1]
    hb = tm // POOL_HALO
    const = lambda shape: pl.BlockSpec(shape, lambda i: (0,) * len(shape))
    side_specs = _side_specs(side, t // tm, lambda i: i)
    return pl.pallas_call(
        functools.partial(_branch_side_kernel, n_side=len(side), tm=tm, seq=seq),
        grid=(t // tm,),
        in_specs=[pl.BlockSpec((tm, M_V), lambda i: (i, 0)),
                  pl.BlockSpec((tm, POOL_W), lambda i: (i, 3)),
                  pl.BlockSpec((POOL_HALO, POOL_W), lambda i: (jnp.maximum(i * hb - 1, 0), 3)),
                  pl.BlockSpec((tm, d), lambda i: (i, 2)),
                  pl.BlockSpec((tm, d), lambda i: (i, 3)),
                  const(w_pool.shape), const((1, POOL_W)), const(w_m_br.shape), const(w_p_br.shape)] + side_specs,
        out_specs=[pl.BlockSpec((tm, d), lambda i: (i, 0))] + side_specs,
        out_shape=[jax.ShapeDtypeStruct((t, d), BF16)] + _side_shapes(side),
        scratch_shapes=[pltpu.VMEM((tm + POOL_HALO, POOL_W), F32),
                        pltpu.VMEM((tm + POOL_HALO, POOL_W), F32),
                        pltpu.VMEM((tm, POOL_W), BF16)],
        compiler_params=_params(1),
        name="branch",
    )(hg2d, z_main, z_main, z_main, z_main, w_pool, pool_scale, w_m_br, w_p_br, *side)


def _route_kernel(mg_ref, x_ref, wout_ref, g_ref, b_ref, wr_ref, br_ref, upper_ref,
                  x1_ref, route_ref, gate_ref, cnt_ref, carry, *, alpha, tm):
    @pl.when(pl.program_id(0) == 0)
    def _():
        carry[...] = jnp.zeros_like(carry)

    hm = tm // ROW_SPLIT
    pitch = _pitch(x_ref.shape[1])
    logits = []
    for r0 in range(0, tm, hm):
        r = alpha * x_ref[r0:r0 + hm, :] + _dot(mg_ref[r0:r0 + hm, :], wout_ref[...])
        x1 = _layer_norm(r, g_ref[...], b_ref[...])
        _store_token_major(x1_ref.at[r0 * pitch:(r0 + hm) * pitch], x1)
        logits.append(_dot(x1.astype(BF16), wr_ref[...]) + br_ref[...])
    lt = jnp.concatenate(logits, axis=0).T

    sub = lax.broadcasted_iota(jnp.int32, (SUBLANES, tm), 0).astype(F32)
    none = float(SUBLANES)

    def softmax(z):
        e = jnp.exp(z - jnp.max(z, axis=0, keepdims=True))
        return e / jnp.sum(e, axis=0, keepdims=True)

    def top1(vals):
        top = jnp.max(vals, axis=0, keepdims=True)
        return top, jnp.min(jnp.where(vals == top, sub, none), axis=0, keepdims=True)

    is_grp = sub < N_GROUPS
    pg = jnp.where(is_grp, softmax(jnp.where(is_grp, lt[0:SUBLANES, :], -jnp.inf)), -1.0)
    pg_top, g_idx = top1(pg)
    le_sel = lt[SUBLANES:2 * SUBLANES, :]
    for g in range(1, N_GROUPS):
        le_sel = jnp.where(g_idx == g, lt[(g + 1) * SUBLANES:(g + 2) * SUBLANES, :], le_sel)
    pe = softmax(le_sel)
    pe1, i1 = top1(pe)
    pe2, i2 = top1(jnp.where(sub == i1, -1.0, pe))
    den = pe1 + pe2
    gate1 = pg_top * pe1 / den
    gate2 = pg_top * pe2 / den
    e1 = g_idx * EXPERTS_PER_GROUP + i1
    e2 = g_idx * EXPERTS_PER_GROUP + i2

    eid = lax.broadcasted_iota(jnp.int32, (N_EXPERTS, tm), 0).astype(F32)
    hit1 = eid == e1
    hit2 = eid == e2
    onehot = jnp.where(hit1 | hit2, 1.0, 0.0)
    incl = _dot(onehot.astype(BF16), upper_ref[...])
    excl = incl - onehot + carry[:, 0:1]
    r1 = jnp.sum(jnp.where(hit1, excl, 0.0), axis=0, keepdims=True)
    r2 = jnp.sum(jnp.where(hit2, excl, 0.0), axis=0, keepdims=True)
    carry[...] = carry[...] + jnp.broadcast_to(incl[:, tm - 1:tm], carry.shape)
    cnt_ref[...] = carry[...]

    route = jnp.where(sub == 0, e1, jnp.where(sub == 1, e2, jnp.where(sub == 2, r1, jnp.where(sub == 3, r2, 0.0))))
    route_ref[...] = route.astype(jnp.int32)
    gates = jnp.where(sub == 0, gate1, jnp.where(sub == 1, gate2, 0.0))
    gate_ref[...] = jnp.concatenate([gates, jnp.zeros((LANES - SUBLANES, tm), F32)], axis=0).T


def _route_side_kernel(*refs, n_side, alpha, tm):
    side_in = refs[8:8 + n_side]
    side_out = refs[12 + n_side:12 + 2 * n_side]
    _route_kernel(*refs[:8], *refs[8 + n_side:12 + n_side], *refs[12 + 2 * n_side:], alpha=alpha, tm=tm)
    _side_cast(side_in, side_out)


def _route(merged, x2d, w_out, ln_g, ln_b, w_r, b_r, upper, alpha, tm, side):
    t, d = x2d.shape
    const = lambda shape: pl.BlockSpec(shape, lambda i: (0,) * len(shape))
    resident = lambda shape: pl.BlockSpec(shape, lambda i: (0,) * len(shape), pipeline_mode=pl.Buffered(1))
    rows = lambda width: pl.BlockSpec((tm, width), lambda i: (i, 0))
    cols = pl.BlockSpec((SUBLANES, tm), lambda i: (0, i))
    side_specs = _side_specs(side, t // tm, lambda i: i)
    return pl.pallas_call(
        functools.partial(_route_side_kernel, n_side=len(side), alpha=alpha, tm=tm),
        grid=(t // tm,),
        in_specs=[rows(d), rows(d), resident(w_out.shape), const((1, d)), const((1, d)),
                  const(w_r.shape), const((1, LANES)), const((tm, tm))] + side_specs,
        out_specs=[pl.BlockSpec((tm * _pitch(d), LANES), lambda i: (i, 0)),
                   cols, rows(LANES), const((N_EXPERTS, LANES))] + side_specs,
        out_shape=[jax.ShapeDtypeStruct((t * _pitch(d), LANES), F32),
                   jax.ShapeDtypeStruct((SUBLANES, t), jnp.int32),
                   jax.ShapeDtypeStruct((t, LANES), F32),
                   jax.ShapeDtypeStruct((N_EXPERTS, LANES), F32)] + _side_shapes(side),
        scratch_shapes=[pltpu.VMEM((N_EXPERTS, LANES), F32)],
        compiler_params=_params(1),
        name="route",
    )(merged, x2d, w_out, ln_g, ln_b, w_r, b_r, upper, *side)


def _dispatch_kernel(pad_start_ref, pad_len_ref, nu_ref, dest_ref, x_ref, xs_hbm, sem, *, tb, te, d):
    def copy(src_tok, dst_tok, n=1):
        return pltpu.make_async_copy(_tokens(x_ref, src_tok, d, n), _tokens(xs_hbm, dst_tok, d, n), sem)

    def wait_tokens(n):
        def body(r, c):
            copy(0, 0, n).wait()
            return c
        return body

    @pl.when(pl.program_id(0) == 0)
    def _():
        def per_expert(e, total):
            start = pad_start_ref[e]
            npad = pad_len_ref[e]

            def fill(r, c):
                copy(0, start + r).start()
                return c

            lax.fori_loop(0, npad, fill, 0)
            return total + npad

        total = lax.fori_loop(0, N_EXPERTS, per_expert, 0)
        lax.fori_loop(0, total // SUBLANES, wait_tokens(SUBLANES), 0)
        lax.fori_loop(0, total % SUBLANES, wait_tokens(1), 0)

        def fill_tile(j, c):
            copy(0, j * te, te).start()
            return c

        n_tiles = xs_hbm.shape[0] // (te * _pitch(d))
        lax.fori_loop(nu_ref[0], n_tiles, fill_tile, 0)
        lax.fori_loop(nu_ref[0], n_tiles, wait_tokens(te), 0)

    def scatter(t, c):
        for k in range(TOP_K):
            copy(t, dest_ref[k, t]).start(priority=k % 2)
        return c

    lax.fori_loop(0, tb, scatter, 0, unroll=DMA_UNROLL)
    for k in range(TOP_K):
        copy(0, 0, tb).wait()


def _dispatch(x1t, dest, pad_start, pad_len, n_used, n_slots, tb, te, d):
    pitch = _pitch(d)
    t = x1t.shape[0] // pitch
    grid_spec = pltpu.PrefetchScalarGridSpec(
        num_scalar_prefetch=3,
        grid=(t // tb,),
        in_specs=[pl.BlockSpec((TOP_K, tb), lambda i, ps, pn, nu: (0, i), memory_space=pltpu.SMEM),
                  pl.BlockSpec((tb * pitch, LANES), lambda i, ps, pn, nu: (i, 0))],
        out_specs=pl.BlockSpec(memory_space=pl.ANY),
        scratch_shapes=[pltpu.SemaphoreType.DMA(())],
    )
    return pl.pallas_call(
        functools.partial(_dispatch_kernel, tb=tb, te=te, d=d),
        grid_spec=grid_spec,
        out_shape=jax.ShapeDtypeStruct((n_slots * pitch, LANES), x1t.dtype),
        compiler_params=_params(1),
        name="dispatch",
    )(pad_start, pad_len, n_used, dest, x1t)


def _expert_kernel(be_ref, nu_ref, x_ref, wg_ref, wu_ref, wd_ref, y_ref, *, te, d):
    i = pl.program_id(0)

    @pl.when(i < nu_ref[0])
    def _():
        xb = _load_token_major(x_ref, te, d).astype(BF16)
        gt = _dot(xb, wg_ref[0])
        up = _dot(xb, wu_ref[0])
        hid = gt * jax.nn.sigmoid(gt) * up
        _store_token_major(y_ref, _dot(hid.astype(BF16), wd_ref[0]))

    @pl.when(i >= nu_ref[0])
    def _():
        y_ref[...] = jnp.zeros_like(y_ref)


def _experts(xs, block_e, n_used, w_gate, w_up, w_down, te):
    _, d, de = w_gate.shape
    pitch = _pitch(d)
    n_tiles = xs.shape[0] // (te * pitch)
    grid_spec = pltpu.PrefetchScalarGridSpec(
        num_scalar_prefetch=2,
        grid=(n_tiles,),
        in_specs=[pl.BlockSpec((te * pitch, LANES), lambda i, be, nu: (jnp.minimum(i, nu[0] - 1), 0)),
                  pl.BlockSpec((1, d, de), lambda i, be, nu: (be[i], 0, 0)),
                  pl.BlockSpec((1, d, de), lambda i, be, nu: (be[i], 0, 0)),
                  pl.BlockSpec((1, de, d), lambda i, be, nu: (be[i], 0, 0))],
        out_specs=pl.BlockSpec((te * pitch, LANES), lambda i, be, nu: (i, 0)),
    )
    return pl.pallas_call(
        functools.partial(_expert_kernel, te=te, d=d),
        grid_spec=grid_spec,
        out_shape=jax.ShapeDtypeStruct(xs.shape, F32),
        compiler_params=_params(1),
        name="experts",
    )(block_e, n_used, xs, w_gate, w_up, w_down)


def _final_kernel(dcur_ref, dnxt_ref, x1_ref, gate_ref, p_ref, ys_hbm, g_ref, b_ref, wg_ref, bg_ref, wp_ref,
                  out_ref, ybuf, sem, *, alpha, tm, d):
    i = pl.program_id(0)
    slot = lax.rem(i, 2)

    def issue(dest_ref, s):
        def body(t, c):
            for k in range(TOP_K):
                pltpu.make_async_copy(_token_data(ys_hbm, dest_ref[k, t], d),
                                      _token_data(ybuf.at[s, k], t, d), sem.at[s]).start(priority=ROW_DMA_PRIORITY)
            return c
        lax.fori_loop(0, tm, body, 0, unroll=DMA_UNROLL)

    @pl.when(i == 0)
    def _():
        issue(dcur_ref, 0)

    @pl.when(i + 1 < pl.num_programs(0))
    def _():
        issue(dnxt_ref, 1 - slot)

    for k in range(TOP_K):
        rows = tm * (d // LANES)
        pltpu.make_async_copy(ys_hbm.at[pl.ds(0, rows)], ybuf.at[slot, k, pl.ds(0, rows)], sem.at[slot]).wait()

    hm = tm // ROW_SPLIT
    for r0 in range(0, tm, hm):
        rows = slice(r0, r0 + hm)
        trows = pl.ds(r0 * _pitch(d), hm * _pitch(d))
        pp = _dot(p_ref[rows, :].astype(BF16), wp_ref[...])
        gate = gate_ref[rows, :]
        y = (gate[:, 0:1] * _load_token_major(ybuf.at[slot, 0, trows], hm, d)
             + gate[:, 1:2] * _load_token_major(ybuf.at[slot, 1, trows], hm, d))
        x2 = _layer_norm(alpha * _load_token_major(x1_ref.at[trows], hm, d) + y, g_ref[...], b_ref[...])
        gl = _dot(x2.astype(BF16), wg_ref[...]) + bg_ref[...]
        out_ref[rows, :] = x2 + jax.nn.sigmoid(gl) * pp


def _final(dest, x1t, gate, p2d, ys, ln_g, ln_b, w_pg, b_pg, w_pp, alpha, tm):
    d = w_pg.shape[0]
    pitch = _pitch(d)
    t = x1t.shape[0] // pitch
    n = t // tm
    const = lambda shape: pl.BlockSpec(shape, lambda i: (0,) * len(shape))
    resident = lambda shape: pl.BlockSpec(shape, lambda i: (0,) * len(shape), pipeline_mode=pl.Buffered(1))
    rows = lambda width: pl.BlockSpec((tm, width), lambda i: (i, 0))
    return pl.pallas_call(
        functools.partial(_final_kernel, alpha=alpha, tm=tm, d=d),
        grid=(n,),
        in_specs=[pl.BlockSpec((TOP_K, tm), lambda i: (0, i), memory_space=pltpu.SMEM),
                  pl.BlockSpec((TOP_K, tm), lambda i: (0, jnp.minimum(i + 1, n - 1)), memory_space=pltpu.SMEM),
                  pl.BlockSpec((tm * pitch, LANES), lambda i: (i, 0)),
                  rows(LANES), rows(p2d.shape[1]),
                  pl.BlockSpec(memory_space=pl.ANY),
                  const((1, d)), const((1, d)), resident(w_pg.shape), const((1, d)), resident(w_pp.shape)],
        out_specs=rows(d),
        out_shape=jax.ShapeDtypeStruct((t, d), F32),
        scratch_shapes=[pltpu.VMEM((2, TOP_K, tm * pitch, LANES), F32), pltpu.SemaphoreType.DMA((2,))],
        compiler_params=_params(1),
        name="final",
    )(dest, dest, x1t, gate, p2d, ys, ln_g, ln_b, w_pg, b_pg, w_pp)


def _pad_cols(a, width):
    return jnp.pad(a, ((0, 0), (0, width - a.shape[1])))


def _tri(n):
    return jnp.tril(jnp.ones((n, n), BF16))


def _layer(x, p, w_in_t, b_in, conv_w, conv_b, mh_g, w_pool, pool_scale, w_m_br, w_p_br, w_out,
           ln1_g, ln1_b, w_rg, b_rg, w_re, b_re, w_gate, w_up, w_down, ln2_g, ln2_b,
           w_ple_gate, b_ple_gate, w_ple_proj, alpha):
    nb, seq, d = x.shape
    t = nb * seq
    x2d = x.reshape(t, d)
    row = lambda a: a.reshape(1, -1)

    c_if = 2 * M_QK + 2 * M_V
    w_main, w_if = _regroup(w_in_t, c_if, 2 * M_HEADS, 1024, min(1024, d))
    b_main = row(jnp.concatenate([b_in[:c_if], b_in[c_if + 2 * M_HEADS:]]))
    b_if = _pad_cols(row(b_in[c_if:c_if + 2 * M_HEADS]), LANES)

    n_exp, _, d_exp = w_gate.shape
    tm_in = min(1024, t)
    z_main, z_if = _inproj(x2d, w_main, b_main, w_if, b_if, tm_in, 2048, [])

    dense_w = [w_pool, w_m_br, w_p_br, w_out, w_ple_gate, w_ple_proj, w_down.reshape(n_exp * d_exp, d)]
    hg, *dense_b = _mlstm(z_main.reshape(nb, seq, -1), z_if.reshape(nb, seq, LANES),
                          conv_w, row(conv_b), row(mh_g), [a.reshape(-1, a.shape[-1]) for a in dense_w])
    w_pool, w_m_br, w_p_br, w_out, w_ple_gate, w_ple_proj, wd_b = [
        b.reshape(a.shape) for a, b in zip(dense_w, dense_b)]

    tm = min(512, seq)
    merged, wu_b = _branch(hg.reshape(t, M_V), z_main, w_pool, row(pool_scale), w_m_br, w_p_br, seq, tm,
                           [w_up.reshape(n_exp * d, d_exp)])

    w_r = _pad_cols(jnp.concatenate([_pad_cols(w_rg, SUBLANES), w_re], axis=1), LANES).astype(BF16)
    b_r = _pad_cols(jnp.concatenate([_pad_cols(row(b_rg), SUBLANES), row(b_re)], axis=1), LANES)
    x1, route, gate, cnt, wg_b = _route(merged, x2d, w_out, row(ln1_g), row(ln1_b),
                                          w_r, b_r, _tri(tm).T, alpha, tm, [w_gate.reshape(n_exp * d, d_exp)])

    te = EXPERT_TILE
    counts = cnt[:, 0].astype(jnp.int32)
    pcounts = (counts + te - 1) // te * te
    pends = jnp.cumsum(pcounts)
    pstarts = pends - pcounts
    e_sel = route[0:TOP_K, :, None] == jnp.arange(N_EXPERTS, dtype=jnp.int32)
    dest = jnp.sum(jnp.where(e_sel, pstarts, 0), axis=-1) + route[TOP_K:2 * TOP_K]
    n_slots = t * TOP_K + N_EXPERTS * te
    n_tiles = n_slots // te
    n_used = (pends[-1] // te).reshape(1)
    tile_row = jnp.minimum(jnp.arange(n_tiles, dtype=jnp.int32), n_used - 1) * te
    block_e = jnp.minimum(jnp.sum((tile_row[:, None] >= pends[None, :]).astype(jnp.int32), axis=1), N_EXPERTS - 1)

    xs = _dispatch(x1, dest, pstarts + counts, pcounts - counts, n_used, n_slots, min(DISPATCH_TILE, t), te, d)
    ys = _experts(xs, block_e, n_used, wg_b.reshape(n_exp, d, d_exp), wu_b.reshape(n_exp, d, d_exp),
                  wd_b.reshape(n_exp, d_exp, d), te)
    return _final(dest, x1, gate, p.reshape(t, -1), ys, row(ln2_g), row(ln2_b),
                  w_ple_gate, row(b_ple_gate), w_ple_proj, alpha,
                  min(512, t)).reshape(nb, seq, d)


def kernel(x, p, w_in, b_in, conv_w, conv_b, mh_g, w_pool, pool_scale, w_m_br, w_p_br, w_out, ln1_g, ln1_b, w_rg, b_rg, w_re, b_re, w_gate, w_up, w_down, ln2_g, ln2_b, w_ple_gate, b_ple_gate, w_ple_proj):
    depth = w_in.shape[0]
    alpha = (2 * depth) ** 0.25
    for i in range(depth):
        x = _layer(x, p[i], w_in[i].T, b_in[i], conv_w[i], conv_b[i], mh_g[i], w_pool[i], pool_scale[i],
                   w_m_br[i], w_p_br[i], w_out[i], ln1_g[i], ln1_b[i], w_rg[i], b_rg[i], w_re[i], b_re[i],
                   w_gate[i], w_up[i], w_down[i], ln2_g[i], ln2_b[i], w_ple_gate[i], b_ple_gate[i],
                   w_ple_proj[i], alpha)
    return x
```

```python
import functools

import jax
import jax.numpy as jnp
from jax import lax
from jax.experimental import pallas as pl
from jax.experimental.pallas import tpu as pltpu

F32 = jnp.float32
BF16 = jnp.bfloat16

M_HEADS = 4
M_QK_DIM = 128
M_V_DIM = 256
M_QK = M_HEADS * M_QK_DIM
M_V = M_HEADS * M_V_DIM
CONV_W = 4
CHUNK = 128
POOL_WINDOWS = (2, 4, 8, 16)
POOL_GROUP_DIM = 256
POOL_W = len(POOL_WINDOWS) * POOL_GROUP_DIM
N_GROUPS = 4
EXPERTS_PER_GROUP = 8
N_EXPERTS = N_GROUPS * EXPERTS_PER_GROUP
TOP_K = 2
LN_EPS = 1e-5

LANES = 128
SUBLANES = 8
VMEM_LIMIT = 56 * 1024 * 1024
BF16_SUBLANES = 16
CONV_HALO = BF16_SUBLANES
POOL_HALO = 32
EXPERT_TILE = 256
DISPATCH_TILE = 2048
MLSTM_GROUP = 4
DMA_UNROLL = 8
ROW_SPLIT = 2


def _dot(a, b):
    return jnp.dot(a, b, preferred_element_type=F32)


def _params(n_grid):
    return pltpu.CompilerParams(dimension_semantics=("arbitrary",) * n_grid,
                                vmem_limit_bytes=VMEM_LIMIT)


def _log_sigmoid(x):
    return -(jnp.maximum(-x, 0.0) + jnp.log1p(jnp.exp(-jnp.abs(x))))


def _layer_norm(r, g, b):
    mu = jnp.mean(r, axis=-1, keepdims=True)
    d = r - mu
    var = jnp.mean(d * d, axis=-1, keepdims=True)
    return d * lax.rsqrt(var + LN_EPS) * g + b


def _pitch(d):
    return d // LANES + 1


def _store_token_major(ref, val):
    n, d = val.shape
    for c in range(d // LANES):
        ref[pl.ds(c, n, stride=_pitch(d)), :] = val[:, c * LANES:(c + 1) * LANES]
    ref[pl.ds(d // LANES, n, stride=_pitch(d)), :] = jnp.zeros((n, LANES), val.dtype)


def _load_token_major(ref, n, d):
    return jnp.concatenate([ref[pl.ds(c, n, stride=_pitch(d)), :] for c in range(d // LANES)], axis=1)


def _tokens(ref, tok, d, n=1):
    return ref.at[pl.ds(tok * _pitch(d), n * _pitch(d))]


def _token_data(ref, tok, d):
    return ref.at[pl.ds(tok * _pitch(d), d // LANES)]


def _side_specs(side, steps, step_of):
    def spec(a):
        rb = max(BF16_SUBLANES, a.shape[0] // steps)
        return pl.BlockSpec((rb, a.shape[1]), lambda *g: (jnp.minimum(step_of(*g), a.shape[0] // rb - 1), 0))
    return [spec(a) for a in side]


def _side_shapes(side):
    return [jax.ShapeDtypeStruct(a.shape, BF16) for a in side]


def _side_cast(side_in, side_out):
    for src, dst in zip(side_in, side_out):
        dst[...] = src[...].astype(BF16)


def _regroup_kernel(a_ref, c_ref, out_ref, cut_ref):
    out_ref[...] = a_ref[...].T.astype(BF16)
    skip, tk = c_ref.shape
    cut_ref[...] = jnp.concatenate([c_ref[...], jnp.zeros((LANES - skip, tk), F32)], axis=0).T


def _regroup(w_t, cut, skip, tn, tk):
    n, k = w_t.shape
    n_keep = cut // tn
    assert skip % SUBLANES == 0
    first_row = lambda j: pl.multiple_of(jnp.where(j < n_keep, j * tn, j * tn + skip), SUBLANES)
    return pl.pallas_call(
        _regroup_kernel,
        grid=(k // tk, (n - skip) // tn),
        in_specs=[pl.BlockSpec((pl.Element(tn), pl.Element(tk)), lambda kk, j: (first_row(j), kk * tk)),
                  pl.BlockSpec((pl.Element(skip), pl.Element(tk)), lambda kk, j: (cut, kk * tk))],
        out_specs=[pl.BlockSpec((tk, tn), lambda kk, j: (kk, j)),
                   pl.BlockSpec((tk, LANES), lambda kk, j: (kk, 0))],
        out_shape=[jax.ShapeDtypeStruct((k, n - skip), BF16), jax.ShapeDtypeStruct((k, LANES), F32)],
        compiler_params=_params(2),
        name="regroup",
    )(w_t, w_t)


def _inproj_kernel(*refs, n_side):
    x_ref, w_ref, b_ref, wif_ref, bif_ref = refs[:5]
    side_in = refs[5:5 + n_side]
    z_ref, zif_ref = refs[5 + n_side:7 + n_side]
    side_out = refs[7 + n_side:7 + 2 * n_side]
    xb_ref = refs[7 + 2 * n_side]

    @pl.when(pl.program_id(1) == 0)
    def _():
        xb = x_ref[...].astype(BF16)
        xb_ref[...] = xb
        zif_ref[...] = _dot(xb, wif_ref[...].astype(BF16)) + bif_ref[...]

    z_ref[...] = (_dot(xb_ref[...], w_ref[...]) + b_ref[...]).astype(BF16)
    _side_cast(side_in, side_out)


def _inproj(x2d, w_main, b_main, w_if, b_if, tm, tn, side):
    t, d = x2d.shape
    n = w_main.shape[1]
    nj = n // tn
    side_specs = _side_specs(side, (t // tm) * nj, lambda i, j: i * nj + j)
    return pl.pallas_call(
        functools.partial(_inproj_kernel, n_side=len(side)),
        grid=(t // tm, nj),
        in_specs=[pl.BlockSpec((tm, d), lambda i, j: (i, 0)),
                  pl.BlockSpec((d, tn), lambda i, j: (0, j)),
                  pl.BlockSpec((1, tn), lambda i, j: (0, j)),
                  pl.BlockSpec((d, LANES), lambda i, j: (0, 0)),
                  pl.BlockSpec((1, LANES), lambda i, j: (0, 0))] + side_specs,
        out_specs=[pl.BlockSpec((tm, tn), lambda i, j: (i, j)),
                   pl.BlockSpec((tm, LANES), lambda i, j: (i, 0))] + side_specs,
        out_shape=[jax.ShapeDtypeStruct((t, n), BF16),
                   jax.ShapeDtypeStruct((t, LANES), F32)] + _side_shapes(side),
        scratch_shapes=[pltpu.VMEM((tm, d), BF16)],
        compiler_params=_params(2),
        name="inproj",
    )(x2d, w_main, b_main, w_if, b_if, *side)


def _mlstm_kernel(*refs, nb, n_side):
    qk_ref, v_ref, o_ref, g_ref, cw_ref, cb_ref, mhg_ref = refs[:7]
    side_in = refs[7:7 + n_side]
    out_ref = refs[7 + n_side]
    side_out = refs[8 + n_side:8 + 2 * n_side]
    cbuf = refs[8 + 2 * n_side]
    state = refs[9 + 2 * n_side:]
    _mlstm_body(qk_ref, v_ref, o_ref, g_ref, cw_ref, cb_ref, mhg_ref, out_ref, cbuf, state, nb)
    _side_cast(side_in, side_out)


def _mlstm_body(qk_ref, v_ref, o_ref, g_ref, cw_ref, cb_ref, mhg_ref, out_ref, cbuf, state, nb):
    L = CHUNK
    halo = CONV_HALO
    ct_refs, m_ref = state[:M_HEADS], state[M_HEADS]

    @pl.when(pl.program_id(0) == 0)
    def _():
        cbuf[:, 0:halo, :] = jnp.zeros((nb, halo, 2 * M_QK), BF16)
        for ref in state:
            ref[...] = jnp.zeros_like(ref)

    cbuf[:, halo:halo + L, :] = qk_ref[...]
    scale = M_QK_DIM ** -0.5
    row = lax.broadcasted_iota(jnp.int32, (L, L), 0)
    col = lax.broadcasted_iota(jnp.int32, (L, L), 1)
    causal = col <= row
    upper = jnp.where(row <= col, 1.0, 0.0).astype(BF16)
    ones_blk = jnp.ones((L, LANES), BF16)
    lane8 = lax.broadcasted_iota(jnp.int32, (SUBLANES, L), 1)
    wrow = lax.broadcasted_iota(jnp.int32, (L, L + halo), 0)
    wcol = lax.broadcasted_iota(jnp.int32, (L, L + halo), 1)
    shifts = [jnp.where(wcol == wrow + (halo - (CONV_W - 1) + tap), 1.0, 0.0).astype(BF16)
              for tap in range(CONV_W - 1)]
    sub = lax.broadcasted_iota(jnp.int32, (SUBLANES, LANES), 0)

    def prep(b):
        window = cbuf[b]
        acc = cb_ref[...] + _dot(shifts[0], window) * cw_ref[0:1, :]
        for tap in range(1, CONV_W - 1):
            acc = acc + _dot(shifts[tap], window) * cw_ref[tap:tap + 1, :]
        acc = acc + qk_ref[b].astype(F32) * cw_ref[CONV_W - 1:CONV_W, :]
        qk = acc * jax.nn.sigmoid(acc)
        cbuf[b, 0:halo, :] = cbuf[b, L:L + halo, :]

        g_row = g_ref[b].T[0:SUBLANES, :]
        lf = _log_sigmoid(g_row)
        lf_hi = lf.astype(BF16)
        lf_lo = (lf - lf_hi.astype(F32)).astype(BF16)
        cs = _dot(jnp.concatenate([lf_hi, lf_lo], axis=0), upper)
        b_row = pltpu.roll(cs[0:SUBLANES, :] + cs[SUBLANES:2 * SUBLANES, :], M_HEADS, 0)
        a_row = g_row - b_row
        cm = a_row
        for sh in (1, 2, 4, 8, 16, 32, 64):
            cm = jnp.maximum(cm, jnp.where(lane8 >= sh, pltpu.roll(cm, sh, 1), -jnp.inf))
        m_prev = m_ref[b]
        mx = jnp.maximum(m_prev, cm)
        mx_last = jnp.maximum(m_prev, jnp.max(a_row, axis=1, keepdims=True))
        carry_in = jnp.exp(m_prev - mx)
        floor = jnp.exp(-(b_row + mx))
        wts = jnp.exp(a_row - mx_last)
        decay = jnp.exp(m_prev - mx_last)
        m_ref[b] = jnp.where(sub < M_HEADS, b_row[:, L - 1:L] + mx_last, 0.0)
        packed = jnp.concatenate(
            [jnp.where(sub < M_HEADS, mx, pltpu.roll(carry_in, M_HEADS, 0)),
             jnp.where(sub < M_HEADS, floor, pltpu.roll(wts, M_HEADS, 0)),
             jnp.zeros((L - 2 * SUBLANES, L), F32)], axis=0).T
        return qk, a_row, decay, packed

    def run_chains(batches):
        pre = {b: prep(b) for b in batches}
        chains = [(b, h) for b in batches for h in range(M_HEADS)]
        vcols = lambda h: slice(h * M_V_DIM, (h + 1) * M_V_DIM)
        col_of = lambda i, c: pre[c[0]][3][:, i * M_HEADS + c[1]:i * M_HEADS + c[1] + 1]
        q = {(b, h): pre[b][0][:, h * M_QK_DIM:(h + 1) * M_QK_DIM].astype(BF16) for b, h in chains}
        kf = {(b, h): pre[b][0][:, M_QK + h * M_QK_DIM:M_QK + (h + 1) * M_QK_DIM] * scale for b, h in chains}
        v_ext = {(b, h): jnp.concatenate([v_ref[b, :, vcols(h)], ones_blk], axis=1) for b, h in chains}
        ct = {(b, h): ct_refs[h][b] for b, h in chains}

        raw = {c: lax.dot_general(q[c], kf[c].astype(BF16), (((1,), (1,)), ((), ())),
                                  preferred_element_type=F32) for c in chains}
        prev = {c: _dot(q[c], ct[c].astype(BF16)) for c in chains}
        pmat = {(b, h): jnp.exp(jnp.where(causal, pre[b][1][h:h + 1, :] - col_of(0, (b, h)), -jnp.inf))
                for b, h in chains}
        s = {c: (raw[c] * pmat[c]).astype(BF16) for c in chains}
        numden = {c: _dot(s[c], v_ext[c]) + col_of(1, c) * prev[c] for c in chains}
        upd = {c: lax.dot_general((kf[c] * col_of(3, c)).astype(BF16), v_ext[c], (((0,), (0,)), ((), ())),
                                  preferred_element_type=F32) for c in chains}
        for b, h in chains:
            dec = jnp.concatenate([pre[b][2][h:h + 1, :]] * (ct[b, h].shape[1] // LANES), axis=1)
            ct_refs[h][b] = dec * ct[b, h] + upd[b, h]
        inv = {c: 1.0 / jnp.maximum(jnp.abs(numden[c][:, M_V_DIM:]), col_of(2, c)) for c in chains}
        hh = {c: numden[c][:, 0:M_V_DIM] * jnp.concatenate([inv[c]] * (M_V_DIM // LANES), axis=1) for c in chains}
        mu = {c: jnp.mean(hh[c], axis=-1, keepdims=True) for c in chains}
        dlt = {c: hh[c] - mu[c] for c in chains}
        var = {c: jnp.mean(dlt[c] * dlt[c], axis=-1, keepdims=True) for c in chains}
        for b, h in chains:
            hn = dlt[b, h] * lax.rsqrt(var[b, h] + LN_EPS) * mhg_ref[:, vcols(h)]
            og = jax.nn.sigmoid(o_ref[b, :, vcols(h)].astype(F32))
            out_ref[b, :, vcols(h)] = (hn * og).astype(BF16)

    for b0 in range(0, nb, MLSTM_GROUP):
        run_chains(range(b0, min(b0 + MLSTM_GROUP, nb)))


def _mlstm(z3, zif3, conv_w, conv_b, mh_g, side):
    nb, s, _ = z3.shape
    L = CHUNK
    blk = lambda colblk: pl.BlockSpec((nb, L, M_V), lambda c: (0, c, colblk))
    const = lambda shape: pl.BlockSpec(shape, lambda c: (0,) * len(shape))
    side_specs = _side_specs(side, s // L, lambda c: c)
    return pl.pallas_call(
        functools.partial(_mlstm_kernel, nb=nb, n_side=len(side)),
        grid=(s // L,),
        in_specs=[blk(0), blk(1), blk(2),
                  pl.BlockSpec((nb, L, LANES), lambda c: (0, c, 0)),
                  const((CONV_W, 2 * M_QK)), const((1, 2 * M_QK)), const((1, M_V))] + side_specs,
        out_specs=[pl.BlockSpec((nb, L, M_V), lambda c: (0, c, 0))] + side_specs,
        out_shape=[jax.ShapeDtypeStruct((nb, s, M_V), BF16)] + _side_shapes(side),
        scratch_shapes=([pltpu.VMEM((nb, L + CONV_HALO, 2 * M_QK), BF16)]
                        + [pltpu.VMEM((nb, M_QK_DIM, M_V_DIM + LANES), F32)] * M_HEADS
                        + [pltpu.VMEM((nb, SUBLANES, L), F32)]),
        compiler_params=_params(1),
        name="mlstm",
    )(z3, z3, z3, zif3, conv_w, conv_b, mh_g, *side)


def _branch_kernel(hg_ref, u_ref, uh_ref, gm_ref, gp_ref, wpool_ref, ps_ref, wm_ref, wp_ref,
                   out_ref, xa, xb, yp_ref, *, tm, seq):
    H = POOL_HALO
    G = POOL_GROUP_DIM
    t0 = lax.rem(pl.program_id(0) * tm, seq)
    u = u_ref[...].astype(F32)
    xa[H:H + tm, :] = u
    xa[0:H, :] = jnp.where(t0 == 0, 0.0, uh_ref[...].astype(F32))
    n = tm + H - 8
    xb[8:8 + n, :] = xa[8:8 + n, :] + xa[7:7 + n, :]
    n = tm + H - 16
    xa[16:16 + n, G:] = xb[16:16 + n, G:] + xb[14:14 + n, G:]
    n = tm + H - 24
    xb[24:24 + n, 2 * G:] = xa[24:24 + n, 2 * G:] + xa[20:20 + n, 2 * G:]
    xa[H:H + tm, 3 * G:] = xb[H:H + tm, 3 * G:] + xb[H - 8:H - 8 + tm, 3 * G:]
    tpos = t0 + lax.broadcasted_iota(jnp.int32, (tm, 1), 0)
    for g, win in enumerate(POOL_WINDOWS):
        src = (xb, xa, xb, xa)[g]
        cols = slice(g * G, (g + 1) * G)
        cnt = jnp.minimum(tpos + 1, win).astype(F32)
        y = src[H:H + tm, cols] / cnt - u[:, cols]
        yp = _dot(y.astype(BF16), wpool_ref[g]) * ps_ref[:, cols]
        yp_ref[:, cols] = yp.astype(BF16)
    hm = tm // ROW_SPLIT
    for r0 in range(0, tm, hm):
        rows = slice(r0, r0 + hm)
        pb = _dot(yp_ref[rows, :], wp_ref[...])
        a = _dot(hg_ref[rows, :], wm_ref[...])
        merged = (jax.nn.sigmoid(gm_ref[rows, :].astype(F32)) * a
                  + jax.nn.sigmoid(gp_ref[rows, :].astype(F32)) * pb)
        out_ref[rows, :] = merged.astype(BF16)


def _branch_side_kernel(*refs, n_side, tm, seq):
    side_in = refs[9:9 + n_side]
    side_out = refs[10 + n_side:10 + 2 * n_side]
    _branch_kernel(*refs[:9], refs[9 + n_side], *refs[10 + 2 * n_side:], tm=tm, seq=seq)
    _side_cast(side_in, side_out)


def _branch(hg2d, z_main, w_pool, pool_scale, w_m_br, w_p_br, seq, tm, side):
    t = hg2d.shape[0]
    d = w_m_br.shape[1]
    hb = tm // POOL_HALO
    const = lambda shape: pl.BlockSpec(shape, lambda i: (0,) * len(shape))
    side_specs = _side_specs(side, t // tm, lambda i: i)
    return pl.pallas_call(
        functools.partial(_branch_side_kernel, n_side=len(side), tm=tm, seq=seq),
        grid=(t // tm,),
        in_specs=[pl.BlockSpec((tm, M_V), lambda i: (i, 0)),
                  pl.BlockSpec((tm, POOL_W), lambda i: (i, 3)),
                  pl.BlockSpec((POOL_HALO, POOL_W), lambda i: (jnp.maximum(i * hb - 1, 0), 3)),
                  pl.BlockSpec((tm, d), lambda i: (i, 2)),
                  pl.BlockSpec((tm, d), lambda i: (i, 3)),
                  const(w_pool.shape), const((1, POOL_W)), const(w_m_br.shape), const(w_p_br.shape)] + side_specs,
        out_specs=[pl.BlockSpec((tm, d), lambda i: (i, 0))] + side_specs,
        out_shape=[jax.ShapeDtypeStruct((t, d), BF16)] + _side_shapes(side),
        scratch_shapes=[pltpu.VMEM((tm + POOL_HALO, POOL_W), F32),
                        pltpu.VMEM((tm + POOL_HALO, POOL_W), F32),
                        pltpu.VMEM((tm, POOL_W), BF16)],
        compiler_params=_params(1),
        name="branch",
    )(hg2d, z_main, z_main, z_main, z_main, w_pool, pool_scale, w_m_br, w_p_br, *side)


def _route_kernel(mg_ref, x_ref, wout_ref, g_ref, b_ref, wr_ref, br_ref, upper_ref,
                  x1_ref, route_ref, gate_ref, cnt_ref, carry, *, alpha, tm):
    @pl.when(pl.program_id(0) == 0)
    def _():
        carry[...] = jnp.zeros_like(carry)

    hm = tm // ROW_SPLIT
    pitch = _pitch(x_ref.shape[1])
    logits = []
    for r0 in range(0, tm, hm):
        r = alpha * x_ref[r0:r0 + hm, :] + _dot(mg_ref[r0:r0 + hm, :], wout_ref[...])
        x1 = _layer_norm(r, g_ref[...], b_ref[...])
        _store_token_major(x1_ref.at[r0 * pitch:(r0 + hm) * pitch], x1)
        logits.append(_dot(x1.astype(BF16), wr_ref[...]) + br_ref[...])
    lt = jnp.concatenate(logits, axis=0).T

    sub = lax.broadcasted_iota(jnp.int32, (SUBLANES, tm), 0).astype(F32)
    none = float(SUBLANES)

    def softmax(z):
        e = jnp.exp(z - jnp.max(z, axis=0, keepdims=True))
        return e / jnp.sum(e, axis=0, keepdims=True)

    def top1(vals):
        top = jnp.max(vals, axis=0, keepdims=True)
        return top, jnp.min(jnp.where(vals == top, sub, none), axis=0, keepdims=True)

    is_grp = sub < N_GROUPS
    pg = jnp.where(is_grp, softmax(jnp.where(is_grp, lt[0:SUBLANES, :], -jnp.inf)), -1.0)
    pg_top, g_idx = top1(pg)
    le_sel = lt[SUBLANES:2 * SUBLANES, :]
    for g in range(1, N_GROUPS):
        le_sel = jnp.where(g_idx == g, lt[(g + 1) * SUBLANES:(g + 2) * SUBLANES, :], le_sel)
    pe = softmax(le_sel)
    pe1, i1 = top1(pe)
    pe2, i2 = top1(jnp.where(sub == i1, -1.0, pe))
    den = pe1 + pe2
    gate1 = pg_top * pe1 / den
    gate2 = pg_top * pe2 / den
    e1 = g_idx * EXPERTS_PER_GROUP + i1
    e2 = g_idx * EXPERTS_PER_GROUP + i2

    eid = lax.broadcasted_iota(jnp.int32, (N_EXPERTS, tm), 0).astype(F32)
    hit1 = eid == e1
    hit2 = eid == e2
    onehot = jnp.where(hit1 | hit2, 1.0, 0.0)
    incl = _dot(onehot.astype(BF16), upper_ref[...])
    excl = incl - onehot + carry[:, 0:1]
    r1 = jnp.sum(jnp.where(hit1, excl, 0.0), axis=0, keepdims=True)
    r2 = jnp.sum(jnp.where(hit2, excl, 0.0), axis=0, keepdims=True)
    carry[...] = carry[...] + jnp.broadcast_to(incl[:, tm - 1:tm], carry.shape)
    cnt_ref[...] = carry[...]

    route = jnp.where(sub == 0, e1, jnp.where(sub == 1, e2, jnp.where(sub == 2, r1, jnp.where(sub == 3, r2, 0.0))))
    route_ref[...] = route.astype(jnp.int32)
    gates = jnp.where(sub == 0, gate1, jnp.where(sub == 1, gate2, 0.0))
    gate_ref[...] = jnp.concatenate([gates, jnp.zeros((LANES - SUBLANES, tm), F32)], axis=0).T


def _route_side_kernel(*refs, n_side, alpha, tm):
    side_in = refs[8:8 + n_side]
    side_out = refs[12 + n_side:12 + 2 * n_side]
    _route_kernel(*refs[:8], *refs[8 + n_side:12 + n_side], *refs[12 + 2 * n_side:], alpha=alpha, tm=tm)
    _side_cast(side_in, side_out)


def _route(merged, x2d, w_out, ln_g, ln_b, w_r, b_r, upper, alpha, tm, side):
    t, d = x2d.shape
    const = lambda shape: pl.BlockSpec(shape, lambda i: (0,) * len(shape))
    resident = lambda shape: pl.BlockSpec(shape, lambda i: (0,) * len(shape), pipeline_mode=pl.Buffered(1))
    rows = lambda width: pl.BlockSpec((tm, width), lambda i: (i, 0))
    cols = pl.BlockSpec((SUBLANES, tm), lambda i: (0, i))
    side_specs = _side_specs(side, t // tm, lambda i: i)
    return pl.pallas_call(
        functools.partial(_route_side_kernel, n_side=len(side), alpha=alpha, tm=tm),
        grid=(t // tm,),
        in_specs=[rows(d), rows(d), resident(w_out.shape), const((1, d)), const((1, d)),
                  const(w_r.shape), const((1, LANES)), const((tm, tm))] + side_specs,
        out_specs=[pl.BlockSpec((tm * _pitch(d), LANES), lambda i: (i, 0)),
                   cols, rows(LANES), const((N_EXPERTS, LANES))] + side_specs,
        out_shape=[jax.ShapeDtypeStruct((t * _pitch(d), LANES), F32),
                   jax.ShapeDtypeStruct((SUBLANES, t), jnp.int32),
                   jax.ShapeDtypeStruct((t, LANES), F32),
                   jax.ShapeDtypeStruct((N_EXPERTS, LANES), F32)] + _side_shapes(side),
        scratch_shapes=[pltpu.VMEM((N_EXPERTS, LANES), F32)],
        compiler_params=_params(1),
        name="route",
    )(merged, x2d, w_out, ln_g, ln_b, w_r, b_r, upper, *side)


def _dispatch_kernel(pad_start_ref, pad_len_ref, nu_ref, dest_ref, x_ref, xs_hbm, sem, *, tb, te, d):
    def copy(src_tok, dst_tok, n=1):
        return pltpu.make_async_copy(_tokens(x_ref, src_tok, d, n), _tokens(xs_hbm, dst_tok, d, n), sem)

    def wait_tokens(n):
        def body(r, c):
            copy(0, 0, n).wait()
            return c
        return body

    @pl.when(pl.program_id(0) == 0)
    def _():
        def per_expert(e, total):
            start = pad_start_ref[e]
            npad = pad_len_ref[e]

            def fill(r, c):
                copy(0, start + r).start()
                return c

            lax.fori_loop(0, npad, fill, 0)
            return total + npad

        total = lax.fori_loop(0, N_EXPERTS, per_expert, 0)
        lax.fori_loop(0, total // SUBLANES, wait_tokens(SUBLANES), 0)
        lax.fori_loop(0, total % SUBLANES, wait_tokens(1), 0)

        def fill_tile(j, c):
            copy(0, j * te, te).start()
            return c

        n_tiles = xs_hbm.shape[0] // (te * _pitch(d))
        lax.fori_loop(nu_ref[0], n_tiles, fill_tile, 0)
        lax.fori_loop(nu_ref[0], n_tiles, wait_tokens(te), 0)

    def scatter(t, c):
        for k in range(TOP_K):
            copy(t, dest_ref[k, t]).start(priority=k % 2)
        return c

    lax.fori_loop(0, tb, scatter, 0, unroll=DMA_UNROLL)
    for k in range(TOP_K):
        copy(0, 0, tb).wait()


def _dispatch(x1t, dest, pad_start, pad_len, n_used, n_slots, tb, te, d):
    pitch = _pitch(d)
    t = x1t.shape[0] // pitch
    grid_spec = pltpu.PrefetchScalarGridSpec(
        num_scalar_prefetch=3,
        grid=(t // tb,),
        in_specs=[pl.BlockSpec((TOP_K, tb), lambda i, ps, pn, nu: (0, i), memory_space=pltpu.SMEM),
                  pl.BlockSpec((tb * pitch, LANES), lambda i, ps, pn, nu: (i, 0))],
        out_specs=pl.BlockSpec(memory_space=pl.ANY),
        scratch_shapes=[pltpu.SemaphoreType.DMA(())],
    )
    return pl.pallas_call(
        functools.partial(_dispatch_kernel, tb=tb, te=te, d=d),
        grid_spec=grid_spec,
        out_shape=jax.ShapeDtypeStruct((n_slots * pitch, LANES), x1t.dtype),
        compiler_params=_params(1),
        name="dispatch",
    )(pad_start, pad_len, n_used, dest, x1t)


def _expert_kernel(be_ref, nu_ref, x_ref, wg_ref, wu_ref, wd_ref, y_ref, *, te, d):
    i = pl.program_id(0)

    @pl.when(i < nu_ref[0])
    def _():
        xb = _load_token_major(x_ref, te, d).astype(BF16)
        gt = _dot(xb, wg_ref[0])
        up = _dot(xb, wu_ref[0])
        hid = gt * jax.nn.sigmoid(gt) * up
        _store_token_major(y_ref, _dot(hid.astype(BF16), wd_ref[0]))

    @pl.when(i >= nu_ref[0])
    def _():
        y_ref[...] = jnp.zeros_like(y_ref)


def _experts(xs, block_e, n_used, w_gate, w_up, w_down, te):
    _, d, de = w_gate.shape
    pitch = _pitch(d)
    n_tiles = xs.shape[0] // (te * pitch)
    grid_spec = pltpu.PrefetchScalarGridSpec(
        num_scalar_prefetch=2,
        grid=(n_tiles,),
        in_specs=[pl.BlockSpec((te * pitch, LANES), lambda i, be, nu: (jnp.minimum(i, nu[0] - 1), 0)),
                  pl.BlockSpec((1, d, de), lambda i, be, nu: (be[i], 0, 0)),
                  pl.BlockSpec((1, d, de), lambda i, be, nu: (be[i], 0, 0)),
                  pl.BlockSpec((1, de, d), lambda i, be, nu: (be[i], 0, 0))],
        out_specs=pl.BlockSpec((te * pitch, LANES), lambda i, be, nu: (i, 0)),
    )
    return pl.pallas_call(
        functools.partial(_expert_kernel, te=te, d=d),
        grid_spec=grid_spec,
        out_shape=jax.ShapeDtypeStruct(xs.shape, F32),
        compiler_params=_params(1),
        name="experts",
    )(block_e, n_used, xs, w_gate, w_up, w_down)


def _final_kernel(dcur_ref, dnxt_ref, x1_ref, gate_ref, p_ref, ys_hbm, g_ref, b_ref, wg_ref, bg_ref, wp_ref,
                  out_ref, ybuf, sem, *, alpha, tm, d):
    i = pl.program_id(0)
    slot = lax.rem(i, 2)

    def issue(dest_ref, s):
        def body(t, c):
            for k in range(TOP_K):
                pltpu.make_async_copy(_token_data(ys_hbm, dest_ref[k, t], d),
                                      _token_data(ybuf.at[s, k], t, d), sem.at[s]).start(priority=k % 2)
            return c
        lax.fori_loop(0, tm, body, 0, unroll=DMA_UNROLL)

    @pl.when(i == 0)
    def _():
        issue(dcur_ref, 0)

    @pl.when(i + 1 < pl.num_programs(0))
    def _():
        issue(dnxt_ref, 1 - slot)

    hm = tm // ROW_SPLIT
    for r0 in range(0, tm, hm):
        rows = slice(r0, r0 + hm)
        trows = pl.ds(r0 * _pitch(d), hm * _pitch(d))
        pp = _dot(p_ref[rows, :].astype(BF16), wp_ref[...])
        ax1 = alpha * _load_token_major(x1_ref.at[trows], hm, d)
        gate = gate_ref[rows, :]
        if r0 == 0:
            for k in range(TOP_K):
                n_rows = tm * (d // LANES)
                pltpu.make_async_copy(ys_hbm.at[pl.ds(0, n_rows)], ybuf.at[slot, k, pl.ds(0, n_rows)],
                                      sem.at[slot]).wait()
        y = (gate[:, 0:1] * _load_token_major(ybuf.at[slot, 0, trows], hm, d)
             + gate[:, 1:2] * _load_token_major(ybuf.at[slot, 1, trows], hm, d))
        x2 = _layer_norm(ax1 + y, g_ref[...], b_ref[...])
        gl = _dot(x2.astype(BF16), wg_ref[...]) + bg_ref[...]
        out_ref[rows, :] = x2 + jax.nn.sigmoid(gl) * pp


def _final(dest, x1t, gate, p2d, ys, ln_g, ln_b, w_pg, b_pg, w_pp, alpha, tm):
    d = w_pg.shape[0]
    pitch = _pitch(d)
    t = x1t.shape[0] // pitch
    n = t // tm
    const = lambda shape: pl.BlockSpec(shape, lambda i: (0,) * len(shape))
    resident = lambda shape: pl.BlockSpec(shape, lambda i: (0,) * len(shape), pipeline_mode=pl.Buffered(1))
    rows = lambda width: pl.BlockSpec((tm, width), lambda i: (i, 0))
    return pl.pallas_call(
        functools.partial(_final_kernel, alpha=alpha, tm=tm, d=d),
        grid=(n,),
        in_specs=[pl.BlockSpec((TOP_K, tm), lambda i: (0, i), memory_space=pltpu.SMEM),
                  pl.BlockSpec((TOP_K, tm), lambda i: (0, jnp.minimum(i + 1, n - 1)), memory_space=pltpu.SMEM),
                  pl.BlockSpec((tm * pitch, LANES), lambda i: (i, 0)),
                  rows(LANES), rows(p2d.shape[1]),
                  pl.BlockSpec(memory_space=pl.ANY),
                  const((1, d)), const((1, d)), resident(w_pg.shape), const((1, d)), resident(w_pp.shape)],
        out_specs=rows(d),
        out_shape=jax.ShapeDtypeStruct((t, d), F32),
        scratch_shapes=[pltpu.VMEM((2, TOP_K, tm * pitch, LANES), F32), pltpu.SemaphoreType.DMA((2,))],
        compiler_params=_params(1),
        name="final",
    )(dest, dest, x1t, gate, p2d, ys, ln_g, ln_b, w_pg, b_pg, w_pp)


def _pad_cols(a, width):
    return jnp.pad(a, ((0, 0), (0, width - a.shape[1])))


def _tri(n):
    return jnp.tril(jnp.ones((n, n), BF16))


def _layer(x, p, w_in_t, b_in, conv_w, conv_b, mh_g, w_pool, pool_scale, w_m_br, w_p_br, w_out,
           ln1_g, ln1_b, w_rg, b_rg, w_re, b_re, w_gate, w_up, w_down, ln2_g, ln2_b,
           w_ple_gate, b_ple_gate, w_ple_proj, alpha):
    nb, seq, d = x.shape
    t = nb * seq
    x2d = x.reshape(t, d)
    row = lambda a: a.reshape(1, -1)

    c_if = 2 * M_QK + 2 * M_V
    w_main, w_if = _regroup(w_in_t, c_if, 2 * M_HEADS, 1024, min(1024, d))
    b_main = row(jnp.concatenate([b_in[:c_if], b_in[c_if + 2 * M_HEADS:]]))
    b_if = _pad_cols(row(b_in[c_if:c_if + 2 * M_HEADS]), LANES)

    n_exp, _, d_exp = w_gate.shape
    tm_in = min(1024, t)
    z_main, z_if = _inproj(x2d, w_main, b_main, w_if, b_if, tm_in, 2048, [])

    dense_w = [w_pool, w_m_br, w_p_br, w_out, w_ple_gate, w_ple_proj, w_down.reshape(n_exp * d_exp, d)]
    hg, *dense_b = _mlstm(z_main.reshape(nb, seq, -1), z_if.reshape(nb, seq, LANES),
                          conv_w, row(conv_b), row(mh_g), [a.reshape(-1, a.shape[-1]) for a in dense_w])
    w_pool, w_m_br, w_p_br, w_out, w_ple_gate, w_ple_proj, wd_b = [
        b.reshape(a.shape) for a, b in zip(dense_w, dense_b)]

    tm = min(512, seq)
    merged, wu_b = _branch(hg.reshape(t, M_V), z_main, w_pool, row(pool_scale), w_m_br, w_p_br, seq, tm,
                           [w_up.reshape(n_exp * d, d_exp)])

    w_r = _pad_cols(jnp.concatenate([_pad_cols(w_rg, SUBLANES), w_re], axis=1), LANES).astype(BF16)
    b_r = _pad_cols(jnp.concatenate([_pad_cols(row(b_rg), SUBLANES), row(b_re)], axis=1), LANES)
    x1, route, gate, cnt, wg_b = _route(merged, x2d, w_out, row(ln1_g), row(ln1_b),
                                          w_r, b_r, _tri(tm).T, alpha, tm, [w_gate.reshape(n_exp * d, d_exp)])

    te = EXPERT_TILE
    counts = cnt[:, 0].astype(jnp.int32)
    pcounts = (counts + te - 1) // te * te
    pends = jnp.cumsum(pcounts)
    pstarts = pends - pcounts
    e_sel = route[0:TOP_K, :, None] == jnp.arange(N_EXPERTS, dtype=jnp.int32)
    dest = jnp.sum(jnp.where(e_sel, pstarts, 0), axis=-1) + route[TOP_K:2 * TOP_K]
    n_slots = t * TOP_K + N_EXPERTS * te
    n_tiles = n_slots // te
    n_used = (pends[-1] // te).reshape(1)
    tile_row = jnp.minimum(jnp.arange(n_tiles, dtype=jnp.int32), n_used - 1) * te
    block_e = jnp.minimum(jnp.sum((tile_row[:, None] >= pends[None, :]).astype(jnp.int32), axis=1), N_EXPERTS - 1)

    xs = _dispatch(x1, dest, pstarts + counts, pcounts - counts, n_used, n_slots, min(DISPATCH_TILE, t), te, d)
    ys = _experts(xs, block_e, n_used, wg_b.reshape(n_exp, d, d_exp), wu_b.reshape(n_exp, d, d_exp),
                  wd_b.reshape(n_exp, d_exp, d), te)
    return _final(dest, x1, gate, p.reshape(t, -1), ys, row(ln2_g), row(ln2_b),
                  w_ple_gate, row(b_ple_gate), w_ple_proj, alpha,
                  min(512, t)).reshape(nb, seq, d)


def kernel(x, p, w_in, b_in, conv_w, conv_b, mh_g, w_pool, pool_scale, w_m_br, w_p_br, w_out, ln1_g, ln1_b, w_rg, b_rg, w_re, b_re, w_gate, w_up, w_down, ln2_g, ln2_b, w_ple_gate, b_ple_gate, w_ple_proj):
    depth = w_in.shape[0]
    alpha = (2 * depth) ** 0.25
    for i in range(depth):
        x = _layer(x, p[i], w_in[i].T, b_in[i], conv_w[i], conv_b[i], mh_g[i], w_pool[i], pool_scale[i],
                   w_m_br[i], w_p_br[i], w_out[i], ln1_g[i], ln1_b[i], w_rg[i], b_rg[i], w_re[i], b_re[i],
                   w_gate[i], w_up[i], w_down[i], ln2_g[i], ln2_b[i], w_ple_gate[i], b_ple_gate[i],
                   w_ple_proj[i], alpha)
    return x
```
